```python
import jax, jax.numpy as jnp
from jax import lax
import numpy as np

D_MODEL = 2048
BATCH = 2
SEQ = 8192
DEPTH = 1

PLE_DIM = 256
ROPE_THETA = 10000.0
NORM_EPS = 1e-6
Q_BLOCK = 128
NEG = -1e30
A_HEADS = 16
A_KV_HEADS = 4
A_HEAD_DIM = 128
IDX_HEADS = 16
IDX_DIM = 64
TOPK_MAX = 256
B_HEADS = 16
B_Q_LORA = 512
B_KV_LORA = 256
B_NOPE = 128
B_ROPE = 64
B_V = 128
B_QK = B_NOPE + B_ROPE
N_BRANCH = 2
D_FF = ((8 * D_MODEL + 3 * 256 - 1) // (3 * 256)) * 256

IN_SIZES = (
    A_HEADS * A_HEAD_DIM,
    A_KV_HEADS * A_HEAD_DIM,
    A_KV_HEADS * A_HEAD_DIM,
    IDX_HEADS * IDX_DIM,
    IDX_DIM,
    IDX_HEADS,
    B_Q_LORA,
    B_KV_LORA,
    B_ROPE,
    N_BRANCH * D_MODEL,
)
N_IN = sum(IN_SIZES)
SPLIT_POINTS = [sum(IN_SIZES[:j]) for j in range(1, len(IN_SIZES))]

kernel_name = "hybrid_dsa_mla_gated_block"


def rms_norm(x, g):
    xf = x.astype(jnp.float32)
    y = xf * lax.rsqrt(jnp.mean(xf * xf, axis=-1, keepdims=True) + NORM_EPS)
    return (y * g.astype(jnp.float32)).astype(x.dtype)


def rope(x, pos):
    d = x.shape[-1]
    half = d // 2
    freqs = jnp.power(jnp.float32(ROPE_THETA), -jnp.arange(half, dtype=jnp.float32) * (2.0 / d))
    ang = pos.astype(jnp.float32)[..., None] * freqs
    cos = jnp.cos(ang)[:, :, None, :]
    sin = jnp.sin(ang)[:, :, None, :]
    xf = x.astype(jnp.float32)
    x1, x2 = xf[..., :half], xf[..., half:]
    return jnp.concatenate([x1 * cos - x2 * sin, x2 * cos + x1 * sin], axis=-1).astype(x.dtype)


def rope_tail(x, pos):
    return jnp.concatenate([x[..., :B_NOPE], rope(x[..., B_NOPE:], pos)], axis=-1)


def to_blocks(a):
    b, s = a.shape[0], a.shape[1]
    a = a.reshape((b, s // Q_BLOCK, Q_BLOCK) + a.shape[2:])
    return jnp.moveaxis(a, 1, 0)


def from_blocks(a):
    a = jnp.moveaxis(a, 0, 1)
    return a.reshape((a.shape[0], a.shape[1] * a.shape[2]) + a.shape[3:])


def dsa_attention(q, k, v, qi, ki, wi, pos, k_sel):
    b, s, ha, dh = q.shape
    hkv = k.shape[2]
    grp = ha // hkv
    scale = dh ** -0.5
    ki32 = ki.astype(jnp.float32)
    k32 = k.astype(jnp.float32)
    v32 = v.astype(jnp.float32)

    def block(args):
        qb, qib, wib, posb = args
        dots = jnp.einsum('bthd,bsd->bths', qib.astype(jnp.float32), ki32)
        idx_score = jnp.einsum('bth,bths->bts', wib.astype(jnp.float32), jax.nn.relu(dots))
        visible = pos[:, None, :] <= posb[:, :, None]
        idx_score = jnp.where(visible, idx_score, NEG)
        _, sel = lax.top_k(idx_score, k_sel)
        k_g = jax.vmap(lambda kb, ib: kb[ib])(k32, sel)
        v_g = jax.vmap(lambda vb, ib: vb[ib])(v32, sel)
        pos_g = jax.vmap(lambda pb, ib: pb[ib])(pos, sel)
        valid = pos_g <= posb[:, :, None]
        qg = qb.astype(jnp.float32).reshape(b, Q_BLOCK, hkv, grp, dh)
        logits = jnp.einsum('btngd,btjnd->btngj', qg, k_g) * scale
        logits = jnp.where(valid[:, :, None, None, :], logits, NEG)
        probs = jax.nn.softmax(logits, axis=-1)
        out = jnp.einsum('btngj,btjnd->btngd', probs, v_g)
        return out.reshape(b, Q_BLOCK, ha * dh).astype(q.dtype)

    outs = lax.map(block, (to_blocks(q), to_blocks(qi), to_blocks(wi), to_blocks(pos)))
    return from_blocks(outs)


def mla_attention(q, k, v, pos):
    b, s, h, dqk = q.shape
    scale = dqk ** -0.5
    k32 = k.astype(jnp.float32)
    v32 = v.astype(jnp.float32)

    def block(args):
        qb, posb = args
        logits = jnp.einsum('bthd,bshd->bhts', qb.astype(jnp.float32), k32) * scale
        mask = pos[:, None, None, :] <= posb[:, None, :, None]
        logits = jnp.where(mask, logits, NEG)
        probs = jax.nn.softmax(logits, axis=-1)
        out = jnp.einsum('bhts,bshd->bthd', probs, v32)
        return out.reshape(b, Q_BLOCK, h * v.shape[-1]).astype(q.dtype)

    outs = lax.map(block, (to_blocks(q), to_blocks(pos)))
    return from_blocks(outs)


def setup_inputs(seed: int = 0) -> dict:
    key = jax.random.key(seed)
    ks = iter(jax.random.split(key, 32))

    def w(shape, fan_in):
        return jax.random.normal(next(ks), shape, jnp.float32) * (fan_in ** -0.5)

    def gain(shape):
        return 1.0 + 0.05 * jax.random.normal(next(ks), shape, jnp.float32)

    x = jax.random.normal(next(ks), (BATCH, SEQ, D_MODEL), jnp.float32)
    p = jax.random.normal(next(ks), (DEPTH, BATCH, SEQ, PLE_DIM), jnp.float32)
    positions = jnp.broadcast_to(jnp.arange(SEQ, dtype=jnp.int32), (BATCH, SEQ))
    return {
        "x": x,
        "p": p,
        "positions": positions,
        "g_mix_norm": gain((DEPTH, D_MODEL)),
        "w_in": w((DEPTH, D_MODEL, N_IN), D_MODEL),
        "g_qa": gain((DEPTH, A_HEAD_DIM)),
        "g_ka": gain((DEPTH, A_HEAD_DIM)),
        "g_cq": gain((DEPTH, B_Q_LORA)),
        "w_uq": w((DEPTH, B_Q_LORA, B_HEADS * B_QK), B_Q_LORA),
        "g_ckv": gain((DEPTH, B_KV_LORA)),
        "w_ukv": w((DEPTH, B_KV_LORA, B_HEADS * (B_NOPE + B_V)), B_KV_LORA),
        "g_qb": gain((DEPTH, B_QK)),
        "g_kb": gain((DEPTH, B_QK)),
        "w_out_a": w((DEPTH, A_HEADS * A_HEAD_DIM, D_MODEL), A_HEADS * A_HEAD_DIM),
        "w_out_b": w((DEPTH, B_HEADS * B_V, D_MODEL), B_HEADS * B_V),
        "w_o": w((DEPTH, D_MODEL, D_MODEL), D_MODEL),
        "g_ffn_norm": gain((DEPTH, D_MODEL)),
        "w_ffn_gate": w((DEPTH, D_MODEL, D_FF), D_MODEL),
        "w_ffn_up": w((DEPTH, D_MODEL, D_FF), D_MODEL),
        "w_ffn_down": w((DEPTH, D_FF, D_MODEL), D_FF),
        "g_ple_norm": gain((DEPTH, D_MODEL)),
        "w_ple_gate": w((DEPTH, D_MODEL, D_MODEL), D_MODEL),
        "w_ple_proj": w((DEPTH, PLE_DIM, D_MODEL), PLE_DIM),
    }


def reference(x, p, positions, g_mix_norm, w_in, g_qa, g_ka, g_cq, w_uq, g_ckv, w_ukv,
              g_qb, g_kb, w_out_a, w_out_b, w_o, g_ffn_norm, w_ffn_gate, w_ffn_up,
              w_ffn_down, g_ple_norm, w_ple_gate, w_ple_proj):
    b, s, _ = x.shape
    k_sel = min(TOPK_MAX, s // 4)
    for i in range(DEPTH):
        h = rms_norm(x, g_mix_norm[i])
        proj = h @ w_in[i]
        qa, ka, va, qi, ki, wi, cq, ckv, kr, gates = jnp.split(proj, SPLIT_POINTS, axis=-1)

        qa = rope(rms_norm(qa.reshape(b, s, A_HEADS, A_HEAD_DIM), g_qa[i]), positions)
        ka = rope(rms_norm(ka.reshape(b, s, A_KV_HEADS, A_HEAD_DIM), g_ka[i]), positions)
        va = va.reshape(b, s, A_KV_HEADS, A_HEAD_DIM)
        qi = rope(qi.reshape(b, s, IDX_HEADS, IDX_DIM), positions)
        ki = rope(ki[:, :, None, :], positions)[:, :, 0, :]
        wi = wi * ((IDX_HEADS ** -0.5) * (IDX_DIM ** -0.5))
        y_a = dsa_attention(qa, ka, va, qi, ki, wi, positions, k_sel)

        qb = (rms_norm(cq, g_cq[i]) @ w_uq[i]).reshape(b, s, B_HEADS, B_QK)
        kv = (rms_norm(ckv, g_ckv[i]) @ w_ukv[i]).reshape(b, s, B_HEADS, B_NOPE + B_V)
        k_nope, vb = kv[..., :B_NOPE], kv[..., B_NOPE:]
        k_pe = jnp.broadcast_to(kr[:, :, None, :], (b, s, B_HEADS, B_ROPE))
        kb = jnp.concatenate([k_nope, k_pe], axis=-1)
        qb = rope_tail(rms_norm(qb, g_qb[i]), positions)
        kb = rope_tail(rms_norm(kb, g_kb[i]), positions)
        y_b = mla_attention(qb, kb, vb, positions)

        gate_a, gate_b = jnp.split(jax.nn.sigmoid(gates), N_BRANCH, axis=-1)
        mixed = gate_a * (y_a @ w_out_a[i]) + gate_b * (y_b @ w_out_b[i])
        x = x + mixed @ w_o[i]

        h2 = rms_norm(x, g_ffn_norm[i])
        x = x + (jax.nn.silu(h2 @ w_ffn_gate[i]) * (h2 @ w_ffn_up[i])) @ w_ffn_down[i]

        ple_gate = jax.nn.sigmoid(rms_norm(x, g_ple_norm[i]) @ w_ple_gate[i])
        x = x + ple_gate * (p[i] @ w_ple_proj[i])
    return x
```

```python
import functools

import numpy as np
import jax
import jax.numpy as jnp
from jax import lax
from jax.experimental import pallas as pl
from jax.experimental.pallas import tpu as pltpu

F32 = jnp.float32
BF16 = jnp.bfloat16

ROPE_THETA = 10000.0
NORM_EPS = 1e-6
NEG = -1e30
M_INIT = -1e29
POS_BIG = 3e38
A_HEADS = 16
A_KV_HEADS = 4
A_HEAD_DIM = 128
IDX_HEADS = 16
IDX_DIM = 64
TOPK_MAX = 256
B_HEADS = 16
B_Q_LORA = 512
B_KV_LORA = 256
B_NOPE = 128
B_ROPE = 64
B_V = 128
B_QK = B_NOPE + B_ROPE
LANES = 128
VMEM_LIMIT = 56 * 1024 * 1024

COL_QA = 0
COL_GATE_A = 2048
COL_GATE_B = 4096
COL_QI = 6144
COL_KA = 7168
COL_VA = 7680
COL_CQ = 8192
COL_CKV = 8704
COL_KIW = 8960
COL_KR = 9088
N_PROJ = 9216


def _dot(a, b):
    return jnp.dot(a, b, preferred_element_type=F32)


def _dot_nt(a, b):
    return lax.dot_general(a, b, (((1,), (1,)), ((), ())), preferred_element_type=F32)


def _cparams(sem):
    return pltpu.CompilerParams(dimension_semantics=sem, vmem_limit_bytes=VMEM_LIMIT)


def _rms_to_bf16(x, g):
    ms = jnp.mean(x * x, axis=-1, keepdims=True)
    return (x * lax.rsqrt(ms + NORM_EPS) * g).astype(BF16)


def _norm_matmul_kernel(x_ref, g_ref, w_ref, o_ref, hn_ref):
    @pl.when(pl.program_id(1) == 0)
    def _():
        hn_ref[...] = _rms_to_bf16(x_ref[...], g_ref[...])

    o_ref[...] = _dot(hn_ref[...], w_ref[...]).astype(o_ref.dtype)


def _norm_matmul(x, x_col_block, k, g, w, out_dtype, tm, tn):
    m = x.shape[0]
    n = w.shape[1]
    return pl.pallas_call(
        _norm_matmul_kernel,
        grid=(m // tm, n // tn),
        in_specs=[
            pl.BlockSpec((tm, k), lambda i, j: (i, x_col_block)),
            pl.BlockSpec((1, k), lambda i, j: (0, 0)),
            pl.BlockSpec((k, tn), lambda i, j: (0, j)),
        ],
        out_specs=pl.BlockSpec((tm, tn), lambda i, j: (i, j)),
        out_shape=jax.ShapeDtypeStruct((m, n), out_dtype),
        scratch_shapes=[pltpu.VMEM((tm, k), BF16)],
        compiler_params=_cparams(("parallel", "arbitrary")),
    )(x, g, w)


def _ffn_up_kernel(x_ref, g_ref, wg_ref, wu_ref, o_ref, hn_ref):
    @pl.when(pl.program_id(1) == 0)
    def _():
        hn_ref[...] = _rms_to_bf16(x_ref[...], g_ref[...])

    h = hn_ref[...]
    a = _dot(h, wg_ref[...])
    b = _dot(h, wu_ref[...])
    o_ref[...] = (a * jax.nn.sigmoid(a) * b).astype(o_ref.dtype)


def _ffn_up(x, g, wg, wu, tm, tn):
    m, k = x.shape
    n = wg.shape[1]
    return pl.pallas_call(
        _ffn_up_kernel,
        grid=(m // tm, n // tn),
        in_specs=[
            pl.BlockSpec((tm, k), lambda i, j: (i, 0)),
            pl.BlockSpec((1, k), lambda i, j: (0, 0)),
            pl.BlockSpec((k, tn), lambda i, j: (0, j)),
            pl.BlockSpec((k, tn), lambda i, j: (0, j)),
        ],
        out_specs=pl.BlockSpec((tm, tn), lambda i, j: (i, j)),
        out_shape=jax.ShapeDtypeStruct((m, n), BF16),
        scratch_shapes=[pltpu.VMEM((tm, k), BF16)],
        compiler_params=_cparams(("parallel", "arbitrary")),
    )(x, g, wg, wu)


def _matmul_res_kernel(a_ref, w_ref, r_ref, o_ref):
    o_ref[...] = r_ref[...] + _dot(a_ref[...], w_ref[...])


def _matmul_res(a, w, res, tm, tn):
    m, k = a.shape
    n = w.shape[1]
    return pl.pallas_call(
        _matmul_res_kernel,
        grid=(n // tn, m // tm),
        in_specs=[
            pl.BlockSpec((tm, k), lambda j, i: (i, 0)),
            pl.BlockSpec((k, tn), lambda j, i: (0, j)),
            pl.BlockSpec((tm, tn), lambda j, i: (i, j)),
        ],
        out_specs=pl.BlockSpec((tm, tn), lambda j, i: (i, j)),
        out_shape=jax.ShapeDtypeStruct((m, n), F32),
        compiler_params=_cparams(("parallel", "parallel")),
    )(a, w, res)


def _merge_kernel(ya_ref, yb_ref, wa_ref, wb_ref, ga_ref, gb_ref, o_ref):
    a = _dot(ya_ref[...], wa_ref[...])
    b = _dot(yb_ref[...], wb_ref[...])
    o_ref[...] = (jax.nn.sigmoid(ga_ref[...]) * a + jax.nn.sigmoid(gb_ref[...]) * b).astype(o_ref.dtype)


def _merge(ya, yb, wa, wb, proj, tm, tn):
    m, k = ya.shape
    n = wa.shape[1]
    ga0 = COL_GATE_A // tn
    gb0 = COL_GATE_B // tn
    return pl.pallas_call(
        _merge_kernel,
        grid=(m // tm, n // tn),
        in_specs=[
            pl.BlockSpec((tm, k), lambda i, j: (i, 0)),
            pl.BlockSpec((tm, k), lambda i, j: (i, 0)),
            pl.BlockSpec((k, tn), lambda i, j: (0, j)),
            pl.BlockSpec((k, tn), lambda i, j: (0, j)),
            pl.BlockSpec((tm, tn), lambda i, j: (i, ga0 + j)),
            pl.BlockSpec((tm, tn), lambda i, j: (i, gb0 + j)),
        ],
        out_specs=pl.BlockSpec((tm, tn), lambda i, j: (i, j)),
        out_shape=jax.ShapeDtypeStruct((m, n), BF16),
        compiler_params=_cparams(("parallel", "parallel")),
    )(ya, yb, wa, wb, proj, proj)


def _ple_kernel(x_ref, g_ref, wg_ref, p_ref, wp_ref, xr_ref, o_ref, hn_ref):
    @pl.when(pl.program_id(1) == 0)
    def _():
        hn_ref[...] = _rms_to_bf16(x_ref[...], g_ref[...])

    gate = jax.nn.sigmoid(_dot(hn_ref[...], wg_ref[...]))
    emb = _dot(p_ref[...].astype(BF16), wp_ref[...])
    o_ref[...] = xr_ref[...] + gate * emb


def _ple(x, g, wg, p, wp, tm, tn):
    m, k = x.shape
    n = wg.shape[1]
    kp = p.shape[1]
    return pl.pallas_call(
        _ple_kernel,
        grid=(m // tm, n // tn),
        in_specs=[
            pl.BlockSpec((tm, k), lambda i, j: (i, 0)),
            pl.BlockSpec((1, k), lambda i, j: (0, 0)),
            pl.BlockSpec((k, tn), lambda i, j: (0, j)),
            pl.BlockSpec((tm, kp), lambda i, j: (i, 0)),
            pl.BlockSpec((kp, tn), lambda i, j: (0, j)),
            pl.BlockSpec((tm, tn), lambda i, j: (i, j)),
        ],
        out_specs=pl.BlockSpec((tm, tn), lambda i, j: (i, j)),
        out_shape=jax.ShapeDtypeStruct((m, n), F32),
        scratch_shapes=[pltpu.VMEM((tm, k), BF16)],
        compiler_params=_cparams(("parallel", "arbitrary")),
    )(x, g, wg, p, wp, x)


def _rope64_pairs(x, c64, s64, first_half):
    partner = jnp.where(first_half, pltpu.roll(x, 96, 1), pltpu.roll(x, 32, 1))
    return x * c64 + partner * s64


def _prep_a_kernel(qa_ref, ka_ref, va_ref, qi_ref, kiw_ref, c128_ref, s128_ref, c64_ref, s64_ref,
                   gq_ref, gk_ref, qa_o, ka_o, va_o, qi_o, klo_o, khi_o, wi_o, *, wi_scale):
    c128 = c128_ref[...]
    s128 = s128_ref[...]

    def norm_rope(x, g):
        ms = jnp.mean(x * x, axis=-1, keepdims=True)
        y = x * lax.rsqrt(ms + NORM_EPS) * g
        return y * c128 + pltpu.roll(y, 64, 1) * s128

    for h in range(A_HEADS):
        hs = slice(h * LANES, (h + 1) * LANES)
        qa_o[:, hs] = norm_rope(qa_ref[:, hs], gq_ref[...]).astype(BF16)
    for h in range(A_KV_HEADS):
        hs = slice(h * LANES, (h + 1) * LANES)
        ka_o[:, hs] = norm_rope(ka_ref[:, hs], gk_ref[...]).astype(BF16)
    va_o[...] = va_ref[...].astype(BF16)

    c64 = c64_ref[...]
    s64 = s64_ref[...]
    lane = lax.broadcasted_iota(jnp.int32, c64.shape, 1)
    first_half = (lane % 64) < 32
    for p in range(IDX_HEADS * IDX_DIM // LANES):
        ps = slice(p * LANES, (p + 1) * LANES)
        qi_o[:, ps] = _rope64_pairs(qi_ref[:, ps], c64, s64, first_half).astype(BF16)
    kiw = kiw_ref[...]
    ki = jnp.where(lane < 64, _rope64_pairs(kiw, c64, s64, first_half), 0.0)
    klo_o[...] = ki.astype(BF16)
    khi_o[...] = pltpu.roll(ki, 64, 1).astype(BF16)
    wi_o[...] = kiw * wi_scale


def _prep_a(proj, c128, s128, c64, s64, gq, gk, tp):
    m = proj.shape[0]
    row = lambda w, cb: pl.BlockSpec((tp, w), lambda i: (i, cb))
    vec = pl.BlockSpec((1, LANES), lambda i: (0, 0))
    out = lambda w: pl.BlockSpec((tp, w), lambda i: (i, 0))
    wi_scale = (IDX_HEADS ** -0.5) * (IDX_DIM ** -0.5)
    return pl.pallas_call(
        functools.partial(_prep_a_kernel, wi_scale=wi_scale),
        grid=(m // tp,),
        in_specs=[
            row(2048, COL_QA // 2048), row(512, COL_KA // 512), row(512, COL_VA // 512),
            row(1024, COL_QI // 1024), row(LANES, COL_KIW // LANES),
            out(LANES), out(LANES), out(LANES), out(LANES), vec, vec,
        ],
        out_specs=[out(2048), out(512), out(512), out(1024), out(LANES), out(LANES), out(LANES)],
        out_shape=[
            jax.ShapeDtypeStruct((m, 2048), BF16), jax.ShapeDtypeStruct((m, 512), BF16),
            jax.ShapeDtypeStruct((m, 512), BF16), jax.ShapeDtypeStruct((m, 1024), BF16),
            jax.ShapeDtypeStruct((m, LANES), BF16), jax.ShapeDtypeStruct((m, LANES), BF16),
            jax.ShapeDtypeStruct((m, LANES), F32),
        ],
        compiler_params=_cparams(("parallel",)),
    )(proj, proj, proj, proj, proj, c128, s128, c64, s64, gq, gk)


def _prep_b_kernel(qb_ref, kv_ref, kr_ref, c64_ref, s64_ref, gqn_ref, gqp_ref, gkn_ref, gkp_ref,
                   q_o, k_o, v_o, *, q_scale):
    c64 = c64_ref[...]
    s64 = s64_ref[...]
    lane = lax.broadcasted_iota(jnp.int32, c64.shape, 1)
    first_half = (lane % 64) < 32
    lo = lane < 64
    kr = jnp.where(lo, kr_ref[...], 0.0)
    kr_ss = jnp.sum(kr * kr, axis=-1, keepdims=True)
    kpe_lo = jnp.where(lo, _rope64_pairs(kr * gkp_ref[...], c64, s64, first_half), 0.0)
    kpe_hi = pltpu.roll(kpe_lo, 64, 1)
    nope_w = B_HEADS * B_NOPE
    for h in range(B_HEADS):
        hs = slice(h * LANES, (h + 1) * LANES)
        pair = h // 2
        mine = lo if h % 2 == 0 else jnp.logical_not(lo)
        qn = qb_ref[:, hs]
        qp = jnp.where(mine, qb_ref[:, nope_w + pair * LANES:nope_w + (pair + 1) * LANES], 0.0)
        ss = jnp.sum(qn * qn, axis=-1, keepdims=True) + jnp.sum(qp * qp, axis=-1, keepdims=True)
        r = lax.rsqrt(ss * (1.0 / B_QK) + NORM_EPS) * q_scale
        q_o[:, 2 * h * LANES:(2 * h + 1) * LANES] = (qn * r * gqn_ref[...]).astype(BF16)
        qpe = _rope64_pairs(qp * gqp_ref[...], c64, s64, first_half) * r
        q_o[:, (2 * h + 1) * LANES:(2 * h + 2) * LANES] = jnp.where(mine, qpe, 0.0).astype(BF16)
        kn = kv_ref[:, hs]
        rk = lax.rsqrt((jnp.sum(kn * kn, axis=-1, keepdims=True) + kr_ss) * (1.0 / B_QK) + NORM_EPS)
        k_o[:, 2 * h * LANES:(2 * h + 1) * LANES] = (kn * rk * gkn_ref[...]).astype(BF16)
        kpe = kpe_lo if h % 2 == 0 else kpe_hi
        k_o[:, (2 * h + 1) * LANES:(2 * h + 2) * LANES] = (kpe * rk).astype(BF16)
        v_o[:, hs] = kv_ref[:, nope_w + h * LANES:nope_w + (h + 1) * LANES].astype(BF16)


def _prep_b(qb, kv, proj, c64, s64, gqn, gqp, gkn, gkp, tp):
    m = qb.shape[0]
    vec = pl.BlockSpec((1, LANES), lambda i: (0, 0))
    out = lambda w: pl.BlockSpec((tp, w), lambda i: (i, 0))
    return pl.pallas_call(
        functools.partial(_prep_b_kernel, q_scale=B_QK ** -0.5),
        grid=(m // tp,),
        in_specs=[
            out(qb.shape[1]), out(kv.shape[1]),
            pl.BlockSpec((tp, LANES), lambda i: (i, COL_KR // LANES)),
            out(LANES), out(LANES), vec, vec, vec, vec,
        ],
        out_specs=[out(B_HEADS * 256), out(B_HEADS * 256), out(B_HEADS * B_V)],
        out_shape=[
            jax.ShapeDtypeStruct((m, B_HEADS * 256), BF16),
            jax.ShapeDtypeStruct((m, B_HEADS * 256), BF16),
            jax.ShapeDtypeStruct((m, B_HEADS * B_V), BF16),
        ],
        compiler_params=_cparams(("parallel",)),
    )(qb, kv, proj, c64, s64, gqn, gqp, gkn, gkp)


def _indexer_kernel(q_ref, klo_ref, khi_ref, w_ref, sc_ref, th_ref, *, tq, tk, seq, k_sel):
    qi = pl.program_id(1)
    n_vis = (qi * tq) // tk + 1
    n_all = seq // tk
    row = qi * tq + lax.broadcasted_iota(jnp.int32, (tq, 1), 0)
    wcols = [w_ref[0, :, IDX_DIM + h:IDX_DIM + h + 1] for h in range(IDX_HEADS)]

    def gen(c, carry):
        rmax, rmin = carry
        off = pl.multiple_of(c * tk, tk)
        kl = klo_ref[0, pl.ds(off, tk), :]
        kh = khi_ref[0, pl.ds(off, tk), :]
        acc = jnp.zeros((tq, tk), F32)
        for p in range(IDX_HEADS // 2):
            qp = q_ref[0, :, p * LANES:(p + 1) * LANES]
            acc = acc + wcols[2 * p] * jnp.maximum(_dot_nt(qp, kl), 0.0)
            acc = acc + wcols[2 * p + 1] * jnp.maximum(_dot_nt(qp, kh), 0.0)
        col = off + lax.broadcasted_iota(jnp.int32, (tq, tk), 1)
        vis = col <= row
        sc_ref[0, :, pl.ds(off, tk)] = jnp.where(vis, acc, NEG)
        rmax = jnp.maximum(rmax, jnp.max(jnp.where(vis, acc, NEG), axis=-1, keepdims=True))
        rmin = jnp.minimum(rmin, jnp.min(jnp.where(vis, acc, POS_BIG), axis=-1, keepdims=True))
        return rmax, rmin

    rmax, rmin = lax.fori_loop(0, n_vis, gen,
                               (jnp.full((tq, 1), NEG, F32), jnp.full((tq, 1), POS_BIG, F32)))

    def fill(c, carry):
        sc_ref[0, :, pl.ds(pl.multiple_of(c * tk, tk), tk)] = jnp.full((tq, tk), NEG, F32)
        return carry

    lax.fori_loop(n_vis, n_all, fill, 0)

    def count_ge(mid):
        midb = jnp.broadcast_to(mid, (tq, LANES))

        def body(c, acc):
            off = pl.multiple_of(c * tk, tk)
            blk = sc_ref[0, :, pl.ds(off, tk)]
            for j in range(tk // LANES):
                acc = acc + jnp.where(blk[:, j * LANES:(j + 1) * LANES] >= midb, 1.0, 0.0)
            return acc

        acc = lax.fori_loop(0, n_vis, body, jnp.zeros((tq, LANES), F32))
        return jnp.sum(acc, axis=-1, keepdims=True)

    active0 = (row + 1) > k_sel
    lo0 = jnp.where(active0, rmin, NEG)
    hi0 = jnp.where(active0, rmax, NEG)
    act0 = jnp.where(active0, 1.0, 0.0)

    def cond(st):
        it, nact = st[0], st[1]
        return jnp.logical_and(nact > 0.0, it < 64)

    def body(st):
        it, _, lo, hi, act = st
        mid = lo + (hi - lo) * 0.5
        cnt = count_ge(mid)
        is_act = act > 0.0
        ge = cnt >= k_sel
        stuck = jnp.logical_or(mid <= lo, mid >= hi)
        done = jnp.logical_or(cnt == k_sel, stuck)
        lo = jnp.where(jnp.logical_and(is_act, ge), mid, lo)
        hi = jnp.where(jnp.logical_and(is_act, jnp.logical_not(ge)), mid, hi)
        act = jnp.where(jnp.logical_and(is_act, jnp.logical_not(done)), 1.0, 0.0)
        return it + 1, jnp.sum(act), lo, hi, act

    st = lax.while_loop(cond, body, (jnp.int32(0), jnp.sum(act0), lo0, hi0, act0))
    th_ref[0] = jnp.broadcast_to(st[2], (tq, LANES))


def _indexer(qi_r, klo, khi, wi, tq, tk, k_sel):
    b, s, _ = qi_r.shape
    return pl.pallas_call(
        functools.partial(_indexer_kernel, tq=tq, tk=tk, seq=s, k_sel=float(k_sel)),
        grid=(b, s // tq),
        in_specs=[
            pl.BlockSpec((1, tq, IDX_HEADS * IDX_DIM), lambda bi, qi: (bi, qi, 0)),
            pl.BlockSpec((1, s, LANES), lambda bi, qi: (bi, 0, 0)),
            pl.BlockSpec((1, s, LANES), lambda bi, qi: (bi, 0, 0)),
            pl.BlockSpec((1, tq, LANES), lambda bi, qi: (bi, qi, 0)),
        ],
        out_specs=[
            pl.BlockSpec((1, tq, s), lambda bi, qi: (bi, qi, 0)),
            pl.BlockSpec((1, tq, LANES), lambda bi, qi: (bi, qi, 0)),
        ],
        out_shape=[jax.ShapeDtypeStruct((b, s, s), F32), jax.ShapeDtypeStruct((b, s, LANES), F32)],
        compiler_params=_cparams(("parallel", "parallel")),
    )(qi_r, klo, khi, wi)


def _causal_pairs(s, tq, tk):
    qs, ks = [], []
    for qi in range(s // tq):
        for ki in range((qi * tq + tq - 1) // tk + 1):
            qs.append(qi)
            ks.append(ki)
    return jnp.asarray(np.array(qs, np.int32)), jnp.asarray(np.array(ks, np.int32))


def _online_softmax_step(s, v, h, m_ref, l_ref, acc_ref, dv):
    m_prev = m_ref[h]
    m_new = jnp.maximum(m_prev, jnp.max(s, axis=-1, keepdims=True))
    alpha = jnp.exp(m_prev - m_new)
    p = jnp.exp(s - m_new[:, :1])
    l_ref[h] = alpha * l_ref[h] + jnp.sum(p, axis=-1, keepdims=True)
    hs = slice(h * dv, (h + 1) * dv)
    acc_ref[:, hs] = alpha * acc_ref[:, hs] + _dot(p.astype(BF16), v)
    m_ref[h] = m_new


def _dsa_kernel(qt_ref, kt_ref, q_ref, k_ref, v_ref, sc_ref, th_ref, o_ref, m_ref, l_ref, acc_ref,
                *, tq, tk):
    step = pl.program_id(1)
    qi = qt_ref[step]
    ki = kt_ref[step]

    @pl.when(ki == 0)
    def _():
        m_ref[...] = jnp.full(m_ref.shape, M_INIT, F32)
        l_ref[...] = jnp.zeros(l_ref.shape, F32)
        acc_ref[...] = jnp.zeros(acc_ref.shape, F32)

    row = qi * tq + lax.broadcasted_iota(jnp.int32, (tq, tk), 0)
    col = ki * tk + lax.broadcasted_iota(jnp.int32, (tq, tk), 1)
    keep = jnp.logical_and(sc_ref[0] >= th_ref[0][:, :1], col <= row)
    grp = A_HEADS // A_KV_HEADS
    for h in range(A_HEADS):
        g = h // grp
        q = q_ref[0, :, h * LANES:(h + 1) * LANES]
        k = k_ref[0, :, g * LANES:(g + 1) * LANES]
        v = v_ref[0, :, g * LANES:(g + 1) * LANES]
        s = jnp.where(keep, _dot_nt(q, k), NEG)
        _online_softmax_step(s, v, h, m_ref, l_ref, acc_ref, A_HEAD_DIM)

    @pl.when(ki == (qi * tq + tq - 1) // tk)
    def _():
        for h in range(A_HEADS):
            hs = slice(h * LANES, (h + 1) * LANES)
            o_ref[0, :, hs] = (acc_ref[:, hs] / l_ref[h]).astype(o_ref.dtype)


def _dsa_attn(q, k, v, scores, th, tq, tk):
    b, s, dq = q.shape
    qt, kt = _causal_pairs(s, tq, tk)
    grid_spec = pltpu.PrefetchScalarGridSpec(
        num_scalar_prefetch=2,
        grid=(b, qt.shape[0]),
        in_specs=[
            pl.BlockSpec((1, tq, dq), lambda bi, p, qt, kt: (bi, qt[p], 0)),
            pl.BlockSpec((1, tk, k.shape[2]), lambda bi, p, qt, kt: (bi, kt[p], 0)),
            pl.BlockSpec((1, tk, v.shape[2]), lambda bi, p, qt, kt: (bi, kt[p], 0)),
            pl.BlockSpec((1, tq, tk), lambda bi, p, qt, kt: (bi, qt[p], kt[p])),
            pl.BlockSpec((1, tq, LANES), lambda bi, p, qt, kt: (bi, qt[p], 0)),
        ],
        out_specs=pl.BlockSpec((1, tq, dq), lambda bi, p, qt, kt: (bi, qt[p], 0)),
        scratch_shapes=[
            pltpu.VMEM((A_HEADS, tq, LANES), F32),
            pltpu.VMEM((A_HEADS, tq, LANES), F32),
            pltpu.VMEM((tq, dq), F32),
        ],
    )
    return pl.pallas_call(
        functools.partial(_dsa_kernel, tq=tq, tk=tk),
        grid_spec=grid_spec,
        out_shape=jax.ShapeDtypeStruct((b, s, dq), BF16),
        compiler_params=_cparams(("parallel", "arbitrary")),
    )(qt, kt, q, k, v, scores, th)


def _mla_kernel(qt_ref, kt_ref, q_ref, k_ref, v_ref, o_ref, m_ref, l_ref, acc_ref, *, tq, tk, hg):
    step = pl.program_id(2)
    qi = qt_ref[step]
    ki = kt_ref[step]

    @pl.when(ki == 0)
    def _():
        m_ref[...] = jnp.full(m_ref.shape, M_INIT, F32)
        l_ref[...] = jnp.zeros(l_ref.shape, F32)
        acc_ref[...] = jnp.zeros(acc_ref.shape, F32)

    row = qi * tq + lax.broadcasted_iota(jnp.int32, (tq, tk), 0)
    col = ki * tk + lax.broadcasted_iota(jnp.int32, (tq, tk), 1)
    keep = col <= row
    for h in range(hg):
        q = q_ref[0, :, h * 256:(h + 1) * 256]
        k = k_ref[0, :, h * 256:(h + 1) * 256]
        v = v_ref[0, :, h * B_V:(h + 1) * B_V]
        s = jnp.where(keep, _dot_nt(q, k), NEG)
        _online_softmax_step(s, v, h, m_ref, l_ref, acc_ref, B_V)

    @pl.when(ki == (qi * tq + tq - 1) // tk)
    def _():
        for h in range(hg):
            hs = slice(h * B_V, (h + 1) * B_V)
            o_ref[0, :, hs] = (acc_ref[:, hs] / l_ref[h]).astype(o_ref.dtype)


def _mla_attn(q, k, v, tq, tk, hg):
    b, s, _ = q.shape
    qt, kt = _causal_pairs(s, tq, tk)
    grid_spec = pltpu.PrefetchScalarGridSpec(
        num_scalar_prefetch=2,
        grid=(b, B_HEADS // hg, qt.shape[0]),
        in_specs=[
            pl.BlockSpec((1, tq, hg * 256), lambda bi, g, p, qt, kt: (bi, qt[p], g)),
            pl.BlockSpec((1, tk, hg * 256), lambda bi, g, p, qt, kt: (bi, kt[p], g)),
            pl.BlockSpec((1, tk, hg * B_V), lambda bi, g, p, qt, kt: (bi, kt[p], g)),
        ],
        out_specs=pl.BlockSpec((1, tq, hg * B_V), lambda bi, g, p, qt, kt: (bi, qt[p], g)),
        scratch_shapes=[
            pltpu.VMEM((hg, tq, LANES), F32),
            pltpu.VMEM((hg, tq, LANES), F32),
            pltpu.VMEM((tq, hg * B_V), F32),
        ],
    )
    return pl.pallas_call(
        functools.partial(_mla_kernel, tq=tq, tk=tk, hg=hg),
        grid_spec=grid_spec,
        out_shape=jax.ShapeDtypeStruct((b, s, B_HEADS * B_V), BF16),
        compiler_params=_cparams(("parallel", "parallel", "arbitrary")),
    )(qt, kt, q, k, v)


def _regroup_w_in(w):
    sizes = (A_HEADS * A_HEAD_DIM, A_KV_HEADS * A_HEAD_DIM, A_KV_HEADS * A_HEAD_DIM,
             IDX_HEADS * IDX_DIM, IDX_DIM, IDX_HEADS, B_Q_LORA, B_KV_LORA, B_ROPE)
    offs = np.cumsum((0,) + sizes)
    qa, ka, va, qi, ki, wi, cq, ckv, kr = (w[:, offs[j]:offs[j + 1]] for j in range(len(sizes)))
    gates = w[:, offs[-1]:]
    d = w.shape[0]
    kiw = jnp.concatenate([ki, wi, jnp.zeros((d, LANES - IDX_DIM - IDX_HEADS), w.dtype)], axis=1)
    krp = jnp.concatenate([kr, jnp.zeros((d, LANES - B_ROPE), w.dtype)], axis=1)
    return jnp.concatenate([qa, gates, qi, ka, va, cq, ckv, kiw, krp], axis=1).astype(BF16)


def _rope_tables(pos, d):
    half = d // 2
    freqs = jnp.power(jnp.float32(ROPE_THETA), -jnp.arange(half, dtype=F32) * (2.0 / d))
    ang = pos.astype(F32)[..., None] * freqs
    cos, sin = jnp.cos(ang), jnp.sin(ang)
    reps = LANES // d
    c = jnp.tile(jnp.concatenate([cos, cos], axis=-1), (1, 1, reps))
    s = jnp.tile(jnp.concatenate([-sin, sin], axis=-1), (1, 1, reps))
    return c.reshape(-1, LANES), s.reshape(-1, LANES)


def _layer(x, p, positions, g_mix_norm, w_in, g_qa, g_ka, g_cq, w_uq, g_ckv, w_ukv, g_qb, g_kb,
           w_out_a, w_out_b, w_o, g_ffn_norm, w_ffn_gate, w_ffn_up, w_ffn_down, g_ple_norm,
           w_ple_gate, w_ple_proj):
    b, s, d = x.shape
    m = b * s
    k_sel = min(TOPK_MAX, s // 4)
    xf = x.reshape(m, d)
    row = lambda g: g.reshape(1, -1).astype(F32)

    c128, s128 = _rope_tables(positions, A_HEAD_DIM)
    c64, s64 = _rope_tables(positions, IDX_DIM)

    proj = _norm_matmul(xf, 0, d, row(g_mix_norm), _regroup_w_in(w_in), F32, tm=1024, tn=512)

    qa_r, ka_r, va_b, qi_r, klo, khi, wi = _prep_a(
        proj, c128, s128, c64, s64, row(g_qa) * (A_HEAD_DIM ** -0.5), row(g_ka), tp=512)
    r3 = lambda a: a.reshape(b, s, a.shape[-1])
    scores, th = _indexer(r3(qi_r), r3(klo), r3(khi), r3(wi), tq=128, tk=512, k_sel=k_sel)
    y_a = _dsa_attn(r3(qa_r), r3(ka_r), r3(va_b), scores, th, tq=256, tk=512)

    w_uq_r = w_uq.reshape(B_Q_LORA, B_HEADS, B_QK)
    w_uq_r = jnp.concatenate([w_uq_r[:, :, :B_NOPE].reshape(B_Q_LORA, -1),
                              w_uq_r[:, :, B_NOPE:].reshape(B_Q_LORA, -1)], axis=1).astype(BF16)
    w_ukv_r = w_ukv.reshape(B_KV_LORA, B_HEADS, 2, B_NOPE)
    w_ukv_r = jnp.concatenate([w_ukv_r[:, :, 0, :].reshape(B_KV_LORA, -1),
                               w_ukv_r[:, :, 1, :].reshape(B_KV_LORA, -1)], axis=1).astype(BF16)
    qb = _norm_matmul(proj, COL_CQ // B_Q_LORA, B_Q_LORA, row(g_cq), w_uq_r, F32, tm=1024, tn=1024)
    kv = _norm_matmul(proj, COL_CKV // B_KV_LORA, B_KV_LORA, row(g_ckv), w_ukv_r, F32, tm=1024, tn=1024)
    pe2 = lambda g: jnp.tile(g[B_NOPE:], 2).reshape(1, LANES).astype(F32)
    q_m, k_m, v_m = _prep_b(qb, kv, proj, c64, s64, row(g_qb[:B_NOPE]), pe2(g_qb),
                            row(g_kb[:B_NOPE]), pe2(g_kb), tp=256)
    y_b = _mla_attn(r3(q_m), r3(k_m), r3(v_m), tq=512, tk=512, hg=4)

    mixed = _merge(y_a.reshape(m, -1), y_b.reshape(m, -1), w_out_a.astype(BF16), w_out_b.astype(BF16),
                   proj, tm=1024, tn=512)
    x1 = _matmul_res(mixed, w_o.astype(BF16), xf, tm=1024, tn=1024)

    act = _ffn_up(x1, row(g_ffn_norm), w_ffn_gate.astype(BF16), w_ffn_up.astype(BF16), tm=1024, tn=512)
    x2 = _matmul_res(act, w_ffn_down.astype(BF16), x1, tm=512, tn=1024)

    out = _ple(x2, row(g_ple_norm), w_ple_gate.astype(BF16), p.reshape(m, -1),
               w_ple_proj.astype(BF16), tm=1024, tn=512)
    return out.reshape(b, s, d)


def kernel(x, p, positions, g_mix_norm, w_in, g_qa, g_ka, g_cq, w_uq, g_ckv, w_ukv, g_qb, g_kb,
           w_out_a, w_out_b, w_o, g_ffn_norm, w_ffn_gate, w_ffn_up, w_ffn_down, g_ple_norm,
           w_ple_gate, w_ple_proj):
    for i in range(p.shape[0]):
        x = _layer(x, p[i], positions, g_mix_norm[i], w_in[i], g_qa[i], g_ka[i], g_cq[i], w_uq[i],
                   g_ckv[i], w_ukv[i], g_qb[i], g_kb[i], w_out_a[i], w_out_b[i], w_o[i],
                   g_ffn_norm[i], w_ffn_gate[i], w_ffn_up[i], w_ffn_down[i], g_ple_norm[i],
                   w_ple_gate[i], w_ple_proj[i])
    return x
```

```python
import functools

import numpy as np
import jax
import jax.numpy as jnp
from jax import lax
from jax.experimental import pallas as pl
from jax.experimental.pallas import tpu as pltpu

F32 = jnp.float32
BF16 = jnp.bfloat16

ROPE_THETA = 10000.0
NORM_EPS = 1e-6
NEG = -1e30
M_INIT = -1e29
POS_BIG = 3e38
TH_ALL = 0.5 * NEG
LOG2E = 1.4426950408889634
MAX_FIXED_SHIFT = 60.0
A_HEADS = 16
A_KV_HEADS = 4
A_HEAD_DIM = 128
IDX_HEADS = 16
IDX_DIM = 64
TOPK_MAX = 256
B_HEADS = 16
B_Q_LORA = 512
B_KV_LORA = 256
B_NOPE = 128
B_ROPE = 64
B_V = 128
B_QK = B_NOPE + B_ROPE
LANES = 128
VMEM_LIMIT = 56 * 1024 * 1024

COL_QA = 0
COL_GATE_A = 2048
COL_GATE_B = 4096
COL_QI = 6144
COL_KA = 7168
COL_VA = 7680
COL_CQ = 8192
COL_CKV = 8704
COL_KIW = 8960
COL_KR = 9088
N_PROJ = 9216


def _dot(a, b):
    return jnp.dot(a, b, preferred_element_type=F32)


def _dot_nt(a, b):
    return lax.dot_general(a, b, (((1,), (1,)), ((), ())), preferred_element_type=F32)


def _cparams(sem):
    return pltpu.CompilerParams(dimension_semantics=sem, vmem_limit_bytes=VMEM_LIMIT)


def _rms_to_bf16(x, g):
    ms = jnp.mean(x * x, axis=-1, keepdims=True)
    return (x * lax.rsqrt(ms + NORM_EPS) * g).astype(BF16)


def _norm_matmul_kernel(x_ref, g_ref, w_ref, o_ref, hn_ref):
    @pl.when(pl.program_id(1) == 0)
    def _():
        hn_ref[...] = _rms_to_bf16(x_ref[...], g_ref[...])

    o_ref[...] = _dot(hn_ref[...], w_ref[...]).astype(o_ref.dtype)


def _norm_matmul(name, x, x_col_block, k, g, w, out_dtype, tm, tn):
    m = x.shape[0]
    n = w.shape[1]
    return pl.pallas_call(
        _norm_matmul_kernel,
        name=name,
        grid=(m // tm, n // tn),
        in_specs=[
            pl.BlockSpec((tm, k), lambda i, j: (i, x_col_block)),
            pl.BlockSpec((1, k), lambda i, j: (0, 0)),
            pl.BlockSpec((k, tn), lambda i, j: (0, j)),
        ],
        out_specs=pl.BlockSpec((tm, tn), lambda i, j: (i, j)),
        out_shape=jax.ShapeDtypeStruct((m, n), out_dtype),
        scratch_shapes=[pltpu.VMEM((tm, k), BF16)],
        compiler_params=_cparams(("parallel", "arbitrary")),
    )(x, g, w)


def _ffn_up_kernel(x_ref, g_ref, wg_ref, wu_ref, o_ref, hn_ref):
    @pl.when(pl.program_id(1) == 0)
    def _():
        hn_ref[...] = _rms_to_bf16(x_ref[...], g_ref[...])

    h = hn_ref[...]
    a = _dot(h, wg_ref[...])
    b = _dot(h, wu_ref[...])
    o_ref[...] = (a * jax.nn.sigmoid(a) * b).astype(o_ref.dtype)


def _ffn_up(x, g, wg, wu, tm, tn):
    m, k = x.shape
    n = wg.shape[1]
    return pl.pallas_call(
        _ffn_up_kernel,
        name="ffn_up",
        grid=(m // tm, n // tn),
        in_specs=[
            pl.BlockSpec((tm, k), lambda i, j: (i, 0)),
            pl.BlockSpec((1, k), lambda i, j: (0, 0)),
            pl.BlockSpec((k, tn), lambda i, j: (0, j)),
            pl.BlockSpec((k, tn), lambda i, j: (0, j)),
        ],
        out_specs=pl.BlockSpec((tm, tn), lambda i, j: (i, j)),
        out_shape=jax.ShapeDtypeStruct((m, n), BF16),
        scratch_shapes=[pltpu.VMEM((tm, k), BF16)],
        compiler_params=_cparams(("parallel", "arbitrary")),
    )(x, g, wg, wu)


def _matmul_res_kernel(a_ref, w_ref, r_ref, o_ref):
    o_ref[...] = r_ref[...] + _dot(a_ref[...], w_ref[...])


def _matmul_res(name, a, w, res, tm, tn):
    m, k = a.shape
    n = w.shape[1]
    return pl.pallas_call(
        _matmul_res_kernel,
        name=name,
        grid=(n // tn, m // tm),
        in_specs=[
            pl.BlockSpec((tm, k), lambda j, i: (i, 0)),
            pl.BlockSpec((k, tn), lambda j, i: (0, j)),
            pl.BlockSpec((tm, tn), lambda j, i: (i, j)),
        ],
        out_specs=pl.BlockSpec((tm, tn), lambda j, i: (i, j)),
        out_shape=jax.ShapeDtypeStruct((m, n), F32),
        compiler_params=_cparams(("parallel", "parallel")),
    )(a, w, res)


def _merge_kernel(ya_ref, yb_ref, wa_ref, wb_ref, ga_ref, gb_ref, o_ref):
    a = _dot(ya_ref[...], wa_ref[...])
    b = _dot(yb_ref[...], wb_ref[...])
    o_ref[...] = (jax.nn.sigmoid(ga_ref[...]) * a + jax.nn.sigmoid(gb_ref[...]) * b).astype(o_ref.dtype)


def _merge(ya, yb, wa, wb, proj, tm, tn):
    m, k = ya.shape
    n = wa.shape[1]
    ga0 = COL_GATE_A // tn
    gb0 = COL_GATE_B // tn
    return pl.pallas_call(
        _merge_kernel,
        name="merge",
        grid=(m // tm, n // tn),
        in_specs=[
            pl.BlockSpec((tm, k), lambda i, j: (i, 0)),
            pl.BlockSpec((tm, k), lambda i, j: (i, 0)),
            pl.BlockSpec((k, tn), lambda i, j: (0, j)),
            pl.BlockSpec((k, tn), lambda i, j: (0, j)),
            pl.BlockSpec((tm, tn), lambda i, j: (i, ga0 + j)),
            pl.BlockSpec((tm, tn), lambda i, j: (i, gb0 + j)),
        ],
        out_specs=pl.BlockSpec((tm, tn), lambda i, j: (i, j)),
        out_shape=jax.ShapeDtypeStruct((m, n), BF16),
        compiler_params=_cparams(("parallel", "parallel")),
    )(ya, yb, wa, wb, proj, proj)


def _ple_kernel(x_ref, g_ref, wg_ref, p_ref, wp_ref, xr_ref, o_ref, hn_ref):
    @pl.when(pl.program_id(1) == 0)
    def _():
        hn_ref[...] = _rms_to_bf16(x_ref[...], g_ref[...])

    gate = jax.nn.sigmoid(_dot(hn_ref[...], wg_ref[...]))
    emb = _dot(p_ref[...].astype(BF16), wp_ref[...])
    o_ref[...] = xr_ref[...] + gate * emb


def _ple(x, g, wg, p, wp, tm, tn):
    m, k = x.shape
    n = wg.shape[1]
    kp = p.shape[1]
    return pl.pallas_call(
        _ple_kernel,
        name="ple",
        grid=(m // tm, n // tn),
        in_specs=[
            pl.BlockSpec((tm, k), lambda i, j: (i, 0)),
            pl.BlockSpec((1, k), lambda i, j: (0, 0)),
            pl.BlockSpec((k, tn), lambda i, j: (0, j)),
            pl.BlockSpec((tm, kp), lambda i, j: (i, 0)),
            pl.BlockSpec((kp, tn), lambda i, j: (0, j)),
            pl.BlockSpec((tm, tn), lambda i, j: (i, j)),
        ],
        out_specs=pl.BlockSpec((tm, tn), lambda i, j: (i, j)),
        out_shape=jax.ShapeDtypeStruct((m, n), F32),
        scratch_shapes=[pltpu.VMEM((tm, k), BF16)],
        compiler_params=_cparams(("parallel", "arbitrary")),
    )(x, g, wg, p, wp, x)


def _rope64_pairs(x, c64, s64, first_half):
    partner = jnp.where(first_half, pltpu.roll(x, 96, 1), pltpu.roll(x, 32, 1))
    return x * c64 + partner * s64


def _prep_a_kernel(qa_ref, ka_ref, va_ref, qi_ref, kiw_ref, c128_ref, s128_ref, c64_ref, s64_ref,
                   gq_ref, gk_ref, qa_o, ka_o, va_o, qi_o, klo_o, khi_o, wi_o, *, wi_scale):
    c128 = c128_ref[...]
    s128 = s128_ref[...]

    def norm_rope(x, g):
        ms = jnp.mean(x * x, axis=-1, keepdims=True)
        y = x * lax.rsqrt(ms + NORM_EPS) * g
        return y * c128 + pltpu.roll(y, 64, 1) * s128

    for h in range(A_HEADS):
        hs = slice(h * LANES, (h + 1) * LANES)
        qa_o[:, hs] = norm_rope(qa_ref[:, hs], gq_ref[...]).astype(BF16)
    for h in range(A_KV_HEADS):
        hs = slice(h * LANES, (h + 1) * LANES)
        ka_o[:, hs] = norm_rope(ka_ref[:, hs], gk_ref[...]).astype(BF16)
    va_o[...] = va_ref[...].astype(BF16)

    c64 = c64_ref[...]
    s64 = s64_ref[...]
    lane = lax.broadcasted_iota(jnp.int32, c64.shape, 1)
    first_half = (lane % 64) < 32
    for p in range(IDX_HEADS * IDX_DIM // LANES):
        ps = slice(p * LANES, (p + 1) * LANES)
        qi_o[:, ps] = _rope64_pairs(qi_ref[:, ps], c64, s64, first_half).astype(BF16)
    kiw = kiw_ref[...]
    ki = jnp.where(lane < 64, _rope64_pairs(kiw, c64, s64, first_half), 0.0)
    klo_o[...] = ki.astype(BF16)
    khi_o[...] = pltpu.roll(ki, 64, 1).astype(BF16)
    wi_o[...] = kiw * wi_scale


def _prep_a(proj, c128, s128, c64, s64, gq, gk, tp):
    m = proj.shape[0]
    row = lambda w, cb: pl.BlockSpec((tp, w), lambda i: (i, cb))
    vec = pl.BlockSpec((1, LANES), lambda i: (0, 0))
    out = lambda w: pl.BlockSpec((tp, w), lambda i: (i, 0))
    wi_scale = (IDX_HEADS ** -0.5) * (IDX_DIM ** -0.5)
    return pl.pallas_call(
        functools.partial(_prep_a_kernel, wi_scale=wi_scale),
        name="prep_a",
        grid=(m // tp,),
        in_specs=[
            row(2048, COL_QA // 2048), row(512, COL_KA // 512), row(512, COL_VA // 512),
            row(1024, COL_QI // 1024), row(LANES, COL_KIW // LANES),
            out(LANES), out(LANES), out(LANES), out(LANES), vec, vec,
        ],
        out_specs=[out(2048), out(512), out(512), out(1024), out(LANES), out(LANES), out(LANES)],
        out_shape=[
            jax.ShapeDtypeStruct((m, 2048), BF16), jax.ShapeDtypeStruct((m, 512), BF16),
            jax.ShapeDtypeStruct((m, 512), BF16), jax.ShapeDtypeStruct((m, 1024), BF16),
            jax.ShapeDtypeStruct((m, LANES), BF16), jax.ShapeDtypeStruct((m, LANES), BF16),
            jax.ShapeDtypeStruct((m, LANES), F32),
        ],
        compiler_params=_cparams(("parallel",)),
    )(proj, proj, proj, proj, proj, c128, s128, c64, s64, gq, gk)


def _prep_b_kernel(qb_ref, kv_ref, kr_ref, c64_ref, s64_ref, gqn_ref, gqp_ref, gkn_ref, gkp_ref, shift_ref,
                   q_o, k_o, v_o, *, q_scale):
    c64 = c64_ref[...]
    s64 = s64_ref[...]
    lane = lax.broadcasted_iota(jnp.int32, c64.shape, 1)
    first_half = (lane % 64) < 32
    lo = lane < 64
    kr = jnp.where(lo, kr_ref[...], 0.0)
    kr_ss = jnp.sum(kr * kr, axis=-1, keepdims=True)
    kpe_lo = jnp.where(lo, _rope64_pairs(kr * gkp_ref[...], c64, s64, first_half), 0.0)
    kpe_hi = pltpu.roll(kpe_lo, 64, 1)
    nope_w = B_HEADS * B_NOPE
    for h in range(B_HEADS):
        hs = slice(h * LANES, (h + 1) * LANES)
        pair = h // 2
        mine = lo if h % 2 == 0 else jnp.logical_not(lo)
        shift_lane = lane == (64 if h % 2 == 0 else 0)
        qn = qb_ref[:, hs]
        qp = jnp.where(mine, qb_ref[:, nope_w + pair * LANES:nope_w + (pair + 1) * LANES], 0.0)
        ss = jnp.sum(qn * qn, axis=-1, keepdims=True) + jnp.sum(qp * qp, axis=-1, keepdims=True)
        r = lax.rsqrt(ss * (1.0 / B_QK) + NORM_EPS) * q_scale
        q_o[:, 2 * h * LANES:(2 * h + 1) * LANES] = (qn * r * gqn_ref[...]).astype(BF16)
        qpe = _rope64_pairs(qp * gqp_ref[...], c64, s64, first_half) * r
        qpe = jnp.where(mine, qpe, jnp.where(shift_lane, 1.0, 0.0))
        q_o[:, (2 * h + 1) * LANES:(2 * h + 2) * LANES] = qpe.astype(BF16)
        kn = kv_ref[:, hs]
        rk = lax.rsqrt((jnp.sum(kn * kn, axis=-1, keepdims=True) + kr_ss) * (1.0 / B_QK) + NORM_EPS)
        k_o[:, 2 * h * LANES:(2 * h + 1) * LANES] = (kn * rk * gkn_ref[...]).astype(BF16)
        kpe = kpe_lo if h % 2 == 0 else kpe_hi
        k_o[:, (2 * h + 1) * LANES:(2 * h + 2) * LANES] = jnp.where(
            shift_lane, -shift_ref[...], kpe * rk).astype(BF16)
        v_o[:, hs] = kv_ref[:, nope_w + h * LANES:nope_w + (h + 1) * LANES].astype(BF16)


def _prep_b(qb, kv, proj, c64, s64, gqn, gqp, gkn, gkp, shift, tp):
    m = qb.shape[0]
    vec = pl.BlockSpec((1, LANES), lambda i: (0, 0))
    out = lambda w: pl.BlockSpec((tp, w), lambda i: (i, 0))
    return pl.pallas_call(
        functools.partial(_prep_b_kernel, q_scale=B_QK ** -0.5 * LOG2E),
        name="prep_b",
        grid=(m // tp,),
        in_specs=[
            out(qb.shape[1]), out(kv.shape[1]),
            pl.BlockSpec((tp, LANES), lambda i: (i, COL_KR // LANES)),
            out(LANES), out(LANES), vec, vec, vec, vec, vec,
        ],
        out_specs=[out(B_HEADS * 256), out(B_HEADS * 256), out(B_HEADS * B_V)],
        out_shape=[
            jax.ShapeDtypeStruct((m, B_HEADS * 256), BF16),
            jax.ShapeDtypeStruct((m, B_HEADS * 256), BF16),
            jax.ShapeDtypeStruct((m, B_HEADS * B_V), BF16),
        ],
        compiler_params=_cparams(("parallel",)),
    )(qb, kv, proj, c64, s64, gqn, gqp, gkn, gkp, shift)


def _indexer_kernel(q_ref, klo_ref, khi_ref, w_ref, sc_ref, th_ref, *, tq, tk, seq, k_sel):
    qi = pl.program_id(1)
    n_vis = (qi * tq) // tk + 1
    n_all = seq // tk
    row = qi * tq + lax.broadcasted_iota(jnp.int32, (tq, 1), 0)
    wcols = [w_ref[0, :, IDX_DIM + h:IDX_DIM + h + 1] for h in range(IDX_HEADS)]

    def gen(c, carry):
        rmax, rmin = carry
        off = pl.multiple_of(c * tk, tk)
        kl = klo_ref[0, pl.ds(off, tk), :]
        kh = khi_ref[0, pl.ds(off, tk), :]
        acc = jnp.zeros((tq, tk), F32)
        for p in range(IDX_HEADS // 2):
            qp = q_ref[0, :, p * LANES:(p + 1) * LANES]
            acc = acc + wcols[2 * p] * jnp.maximum(_dot_nt(qp, kl), 0.0)
            acc = acc + wcols[2 * p + 1] * jnp.maximum(_dot_nt(qp, kh), 0.0)
        col = off + lax.broadcasted_iota(jnp.int32, (tq, tk), 1)
        vis = col <= row
        sc_ref[0, :, pl.ds(off, tk)] = jnp.where(vis, acc, NEG)
        rmax = jnp.maximum(rmax, jnp.max(jnp.where(vis, acc, NEG), axis=-1, keepdims=True))
        rmin = jnp.minimum(rmin, jnp.min(jnp.where(vis, acc, POS_BIG), axis=-1, keepdims=True))
        return rmax, rmin

    rmax, rmin = lax.fori_loop(0, n_vis, gen,
                               (jnp.full((tq, 1), NEG, F32), jnp.full((tq, 1), POS_BIG, F32)))

    def fill(c, carry):
        sc_ref[0, :, pl.ds(pl.multiple_of(c * tk, tk), tk)] = jnp.full((tq, tk), NEG, F32)
        return carry

    lax.fori_loop(n_vis, n_all, fill, 0)

    def count_ge(mid):
        midb = jnp.broadcast_to(mid, (tq, LANES))

        def body(c, acc):
            off = pl.multiple_of(c * tk, tk)
            blk = sc_ref[0, :, pl.ds(off, tk)]
            for j in range(tk // LANES):
                acc = acc + jnp.where(blk[:, j * LANES:(j + 1) * LANES] >= midb, 1.0, 0.0)
            return acc

        acc = lax.fori_loop(0, n_vis, body, jnp.zeros((tq, LANES), F32))
        return jnp.sum(acc, axis=-1, keepdims=True)

    active0 = (row + 1) > k_sel
    lo0 = jnp.where(active0, rmin, TH_ALL)
    hi0 = jnp.where(active0, rmax, TH_ALL)
    act0 = jnp.where(active0, 1.0, 0.0)

    def cond(st):
        it, nact = st[0], st[1]
        return jnp.logical_and(nact > 0.0, it < 64)

    def body(st):
        it, _, lo, hi, act = st
        mid = lo + (hi - lo) * 0.5
        cnt = count_ge(mid)
        is_act = act > 0.0
        ge = cnt >= k_sel
        stuck = jnp.logical_or(mid <= lo, mid >= hi)
        done = jnp.logical_or(cnt == k_sel, stuck)
        lo = jnp.where(jnp.logical_and(is_act, ge), mid, lo)
        hi = jnp.where(jnp.logical_and(is_act, jnp.logical_not(ge)), mid, hi)
        act = jnp.where(jnp.logical_and(is_act, jnp.logical_not(done)), 1.0, 0.0)
        return it + 1, jnp.sum(act), lo, hi, act

    st = lax.while_loop(cond, body, (jnp.int32(0), jnp.sum(act0), lo0, hi0, act0))
    th_ref[0] = jnp.broadcast_to(st[2], (tq, LANES))


def _indexer(qi_r, klo, khi, wi, tq, tk, k_sel):
    b, s, _ = qi_r.shape
    return pl.pallas_call(
        functools.partial(_indexer_kernel, tq=tq, tk=tk, seq=s, k_sel=float(k_sel)),
        name="indexer",
        grid=(b, s // tq),
        in_specs=[
            pl.BlockSpec((1, tq, IDX_HEADS * IDX_DIM), lambda bi, qi: (bi, qi, 0)),
            pl.BlockSpec((1, s, LANES), lambda bi, qi: (bi, 0, 0)),
            pl.BlockSpec((1, s, LANES), lambda bi, qi: (bi, 0, 0)),
            pl.BlockSpec((1, tq, LANES), lambda bi, qi: (bi, qi, 0)),
        ],
        out_specs=[
            pl.BlockSpec((1, tq, s), lambda bi, qi: (bi, qi, 0)),
            pl.BlockSpec((1, tq, LANES), lambda bi, qi: (bi, qi, 0)),
        ],
        out_shape=[jax.ShapeDtypeStruct((b, s, s), F32), jax.ShapeDtypeStruct((b, s, LANES), F32)],
        compiler_params=_cparams(("parallel", "parallel")),
    )(qi_r, klo, khi, wi)


def _causal_pairs(s, tq, tk):
    qs, ks = [], []
    for qi in range(s // tq):
        for ki in range((qi * tq + tq - 1) // tk + 1):
            qs.append(qi)
            ks.append(ki)
    return jnp.asarray(np.array(qs, np.int32)), jnp.asarray(np.array(ks, np.int32))


def _softmax_step(s, v, rows, m_ref, l_ref, acc_ref, fixed_shift):
    blocks = [s[:, j * LANES:(j + 1) * LANES] for j in range(s.shape[1] // LANES)]
    if fixed_shift:
        ps = [jnp.exp2(blk) for blk in blocks]
        l_ref[rows] = l_ref[rows] + functools.reduce(jnp.add, ps)
        p = jnp.concatenate([blk.astype(BF16) for blk in ps], axis=1)
        acc_ref[rows] = acc_ref[rows] + _dot(p, v)
        return
    mx = functools.reduce(jnp.maximum, blocks)
    m_prev = m_ref[rows]
    m_new = jnp.maximum(m_prev, jnp.max(mx, axis=-1, keepdims=True))
    alpha = jnp.exp2(m_prev - m_new)
    ps = [jnp.exp2(blk - m_new) for blk in blocks]
    l_ref[rows] = alpha * l_ref[rows] + jnp.sum(functools.reduce(jnp.add, ps), axis=-1, keepdims=True)
    p = jnp.concatenate([blk.astype(BF16) for blk in ps], axis=1)
    acc_ref[rows] = alpha * acc_ref[rows] + _dot(p, v)
    m_ref[rows] = m_new


def _softmax_finish(rows, l_ref, acc_ref, fixed_shift):
    l = l_ref[rows]
    if fixed_shift:
        l = jnp.sum(l, axis=-1, keepdims=True)
    return acc_ref[rows] / l


def _dsa_kernel(qt_ref, kt_ref, q_ref, k_ref, v_ref, sc_ref, th_ref, shift_ref, o_ref, m_ref, l_ref,
                acc_ref, bias_ref, qs_ref, *, tq, tk, fixed_shift):
    step = pl.program_id(1)
    qi = qt_ref[step]
    ki = kt_ref[step]

    @pl.when(ki == 0)
    def _():
        m_ref[...] = jnp.full(m_ref.shape, M_INIT, F32)
        l_ref[...] = jnp.zeros(l_ref.shape, F32)
        acc_ref[...] = jnp.zeros(acc_ref.shape, F32)
        for h in range(A_HEADS):
            qs_ref[h * tq:(h + 1) * tq, :] = q_ref[0, :, h * LANES:(h + 1) * LANES]

    bias_ref[...] = jnp.where(sc_ref[0] >= th_ref[0][:, :1], -shift_ref[:, :1], NEG)
    grp = A_HEADS // A_KV_HEADS
    for g in range(A_KV_HEADS):
        rows = slice(g * grp * tq, (g + 1) * grp * tq)
        k = k_ref[0, :, g * LANES:(g + 1) * LANES]
        v = v_ref[0, :, g * LANES:(g + 1) * LANES]
        s = _dot_nt(qs_ref[rows, :], k)
        s = jnp.concatenate([s[j * tq:(j + 1) * tq] + bias_ref[...] for j in range(grp)], axis=0)
        _softmax_step(s, v, rows, m_ref, l_ref, acc_ref, fixed_shift)

    @pl.when(ki == (qi * tq + tq - 1) // tk)
    def _():
        for h in range(A_HEADS):
            out = _softmax_finish(slice(h * tq, (h + 1) * tq), l_ref, acc_ref, fixed_shift)
            o_ref[0, :, h * LANES:(h + 1) * LANES] = out.astype(o_ref.dtype)


def _dsa_attn(q, k, v, scores, th, shift, tq, tk, fixed_shift):
    b, s, dq = q.shape
    qt, kt = _causal_pairs(s, tq, tk)
    grid_spec = pltpu.PrefetchScalarGridSpec(
        num_scalar_prefetch=2,
        grid=(b, qt.shape[0]),
        in_specs=[
            pl.BlockSpec((1, tq, dq), lambda bi, p, qt, kt: (bi, qt[p], 0)),
            pl.BlockSpec((1, tk, k.shape[2]), lambda bi, p, qt, kt: (bi, kt[p], 0)),
            pl.BlockSpec((1, tk, v.shape[2]), lambda bi, p, qt, kt: (bi, kt[p], 0)),
            pl.BlockSpec((1, tq, tk), lambda bi, p, qt, kt: (bi, qt[p], kt[p])),
            pl.BlockSpec((1, tq, LANES), lambda bi, p, qt, kt: (bi, qt[p], 0)),
            pl.BlockSpec((1, LANES), lambda bi, p, qt, kt: (0, 0)),
        ],
        out_specs=pl.BlockSpec((1, tq, dq), lambda bi, p, qt, kt: (bi, qt[p], 0)),
        scratch_shapes=[
            pltpu.VMEM((A_HEADS * tq, LANES), F32),
            pltpu.VMEM((A_HEADS * tq, LANES), F32),
            pltpu.VMEM((A_HEADS * tq, A_HEAD_DIM), F32),
            pltpu.VMEM((tq, tk), F32),
            pltpu.VMEM((A_HEADS * tq, A_HEAD_DIM), BF16),
        ],
    )
    return pl.pallas_call(
        functools.partial(_dsa_kernel, tq=tq, tk=tk, fixed_shift=fixed_shift),
        name="dsa_attn_fixed" if fixed_shift else "dsa_attn_online",
        grid_spec=grid_spec,
        out_shape=jax.ShapeDtypeStruct((b, s, dq), BF16),
        compiler_params=_cparams(("parallel", "arbitrary")),
    )(qt, kt, q, k, v, scores, th, shift)


def _mla_kernel(qt_ref, kt_ref, q_ref, k_ref, v_ref, o_ref, m_ref, l_ref, acc_ref, *, tq, tk, hg,
                fixed_shift):
    step = pl.program_id(2)
    qi = qt_ref[step]
    ki = kt_ref[step]

    @pl.when(ki == 0)
    def _():
        m_ref[...] = jnp.full(m_ref.shape, M_INIT, F32)
        l_ref[...] = jnp.zeros(l_ref.shape, F32)
        acc_ref[...] = jnp.zeros(acc_ref.shape, F32)

    def heads(masked):
        if masked:
            keep = (lax.broadcasted_iota(jnp.int32, (tq, tk), 1)
                    <= lax.broadcasted_iota(jnp.int32, (tq, tk), 0))
        for h in range(hg):
            q = q_ref[0, :, h * 256:(h + 1) * 256]
            k = k_ref[0, :, h * 256:(h + 1) * 256]
            v = v_ref[0, :, h * B_V:(h + 1) * B_V]
            s = _dot_nt(q, k)
            if masked:
                s = jnp.where(keep, s, NEG)
            _softmax_step(s, v, slice(h * tq, (h + 1) * tq), m_ref, l_ref, acc_ref, fixed_shift)

    @pl.when(ki < qi)
    def _():
        heads(False)

    @pl.when(ki == qi)
    def _():
        heads(True)
        for h in range(hg):
            out = _softmax_finish(slice(h * tq, (h + 1) * tq), l_ref, acc_ref, fixed_shift)
            o_ref[0, :, h * B_V:(h + 1) * B_V] = out.astype(o_ref.dtype)


def _mla_attn(q, k, v, tq, tk, hg, fixed_shift):
    b, s, _ = q.shape
    assert tq == tk
    qt, kt = _causal_pairs(s, tq, tk)
    grid_spec = pltpu.PrefetchScalarGridSpec(
        num_scalar_prefetch=2,
        grid=(b, B_HEADS // hg, qt.shape[0]),
        in_specs=[
            pl.BlockSpec((1, tq, hg * 256), lambda bi, g, p, qt, kt: (bi, qt[p], g)),
            pl.BlockSpec((1, tk, hg * 256), lambda bi, g, p, qt, kt: (bi, kt[p], g)),
            pl.BlockSpec((1, tk, hg * B_V), lambda bi, g, p, qt, kt: (bi, kt[p], g)),
        ],
        out_specs=pl.BlockSpec((1, tq, hg * B_V), lambda bi, g, p, qt, kt: (bi, qt[p], g)),
        scratch_shapes=[
            pltpu.VMEM((hg * tq, LANES), F32),
            pltpu.VMEM((hg * tq, LANES), F32),
            pltpu.VMEM((hg * tq, B_V), F32),
        ],
    )
    return pl.pallas_call(
        functools.partial(_mla_kernel, tq=tq, tk=tk, hg=hg, fixed_shift=fixed_shift),
        name="mla_attn_fixed" if fixed_shift else "mla_attn_online",
        grid_spec=grid_spec,
        out_shape=jax.ShapeDtypeStruct((b, s, B_HEADS * B_V), BF16),
        compiler_params=_cparams(("parallel", "parallel", "arbitrary")),
    )(qt, kt, q, k, v)


def _regroup_w_in(w):
    sizes = (A_HEADS * A_HEAD_DIM, A_KV_HEADS * A_HEAD_DIM, A_KV_HEADS * A_HEAD_DIM,
             IDX_HEADS * IDX_DIM, IDX_DIM, IDX_HEADS, B_Q_LORA, B_KV_LORA, B_ROPE)
    offs = np.cumsum((0,) + sizes)
    qa, ka, va, qi, ki, wi, cq, ckv, kr = (w[:, offs[j]:offs[j + 1]] for j in range(len(sizes)))
    gates = w[:, offs[-1]:]
    d = w.shape[0]
    kiw = jnp.concatenate([ki, wi, jnp.zeros((d, LANES - IDX_DIM - IDX_HEADS), w.dtype)], axis=1)
    krp = jnp.concatenate([kr, jnp.zeros((d, LANES - B_ROPE), w.dtype)], axis=1)
    return jnp.concatenate([qa, gates, qi, ka, va, cq, ckv, kiw, krp], axis=1).astype(BF16)


def _softmax_shift(d, g_q, g_k):
    bound = (d ** 0.5) * LOG2E * 1.02 * jnp.max(jnp.abs(g_q)) * jnp.max(jnp.abs(g_k))
    c = jnp.ceil(bound.astype(F32))
    ok = c <= MAX_FIXED_SHIFT
    return jnp.full((1, LANES), jnp.where(ok, c, 0.0), F32), ok


def _rope_tables(pos, d):
    half = d // 2
    freqs = jnp.power(jnp.float32(ROPE_THETA), -jnp.arange(half, dtype=F32) * (2.0 / d))
    ang = pos.astype(F32)[..., None] * freqs
    cos, sin = jnp.cos(ang), jnp.sin(ang)
    reps = LANES // d
    c = jnp.tile(jnp.concatenate([cos, cos], axis=-1), (1, 1, reps))
    s = jnp.tile(jnp.concatenate([-sin, sin], axis=-1), (1, 1, reps))
    return c.reshape(-1, LANES), s.reshape(-1, LANES)


def _layer(x, p, positions, g_mix_norm, w_in, g_qa, g_ka, g_cq, w_uq, g_ckv, w_ukv, g_qb, g_kb,
           w_out_a, w_out_b, w_o, g_ffn_norm, w_ffn_gate, w_ffn_up, w_ffn_down, g_ple_norm,
           w_ple_gate, w_ple_proj):
    b, s, d = x.shape
    m = b * s
    k_sel = min(TOPK_MAX, s // 4)
    xf = x.reshape(m, d)
    row = lambda g: g.reshape(1, -1).astype(F32)

    c128, s128 = _rope_tables(positions, A_HEAD_DIM)
    c64, s64 = _rope_tables(positions, IDX_DIM)

    proj = _norm_matmul("in_proj", xf, 0, d, row(g_mix_norm), _regroup_w_in(w_in), F32, tm=1024, tn=512)

    qa_r, ka_r, va_b, qi_r, klo, khi, wi = _prep_a(
        proj, c128, s128, c64, s64, row(g_qa) * (A_HEAD_DIM ** -0.5 * LOG2E), row(g_ka), tp=512)
    r3 = lambda a: a.reshape(b, s, a.shape[-1])
    scores, th = _indexer(r3(qi_r), r3(klo), r3(khi), r3(wi), tq=256, tk=512, k_sel=k_sel)
    shift_a, fixed_a = _softmax_shift(A_HEAD_DIM, g_qa, g_ka)
    y_a = lax.cond(
        fixed_a,
        functools.partial(_dsa_attn, tq=512, tk=512, fixed_shift=True),
        functools.partial(_dsa_attn, tq=512, tk=512, fixed_shift=False),
        r3(qa_r), r3(ka_r), r3(va_b), scores, th, shift_a)

    w_uq_r = w_uq.reshape(B_Q_LORA, B_HEADS, B_QK)
    w_uq_r = jnp.concatenate([w_uq_r[:, :, :B_NOPE].reshape(B_Q_LORA, -1),
                              w_uq_r[:, :, B_NOPE:].reshape(B_Q_LORA, -1)], axis=1).astype(BF16)
    w_ukv_r = w_ukv.reshape(B_KV_LORA, B_HEADS, 2, B_NOPE)
    w_ukv_r = jnp.concatenate([w_ukv_r[:, :, 0, :].reshape(B_KV_LORA, -1),
                               w_ukv_r[:, :, 1, :].reshape(B_KV_LORA, -1)], axis=1).astype(BF16)
    qb = _norm_matmul("q_up", proj, COL_CQ // B_Q_LORA, B_Q_LORA, row(g_cq), w_uq_r, F32, tm=1024, tn=1024)
    kv = _norm_matmul("kv_up", proj, COL_CKV // B_KV_LORA, B_KV_LORA, row(g_ckv), w_ukv_r, F32, tm=1024, tn=1024)
    pe2 = lambda g: jnp.tile(g[B_NOPE:], 2).reshape(1, LANES).astype(F32)
    shift_b, fixed_b = _softmax_shift(B_QK, g_qb, g_kb)
    q_m, k_m, v_m = _prep_b(qb, kv, proj, c64, s64, row(g_qb[:B_NOPE]), pe2(g_qb),
                            row(g_kb[:B_NOPE]), pe2(g_kb), shift_b, tp=256)
    y_b = lax.cond(
        fixed_b,
        functools.partial(_mla_attn, tq=512, tk=512, hg=4, fixed_shift=True),
        functools.partial(_mla_attn, tq=512, tk=512, hg=4, fixed_shift=False),
        r3(q_m), r3(k_m), r3(v_m))

    mixed = _merge(y_a.reshape(m, -1), y_b.reshape(m, -1), w_out_a.astype(BF16), w_out_b.astype(BF16),
                   proj, tm=1024, tn=512)
    x1 = _matmul_res("w_o", mixed, w_o.astype(BF16), xf, tm=1024, tn=1024)

    act = _ffn_up(x1, row(g_ffn_norm), w_ffn_gate.astype(BF16), w_ffn_up.astype(BF16), tm=1024, tn=512)
    x2 = _matmul_res("ffn_down", act, w_ffn_down.astype(BF16), x1, tm=512, tn=1024)

    out = _ple(x2, row(g_ple_norm), w_ple_gate.astype(BF16), p.reshape(m, -1),
               w_ple_proj.astype(BF16), tm=1024, tn=512)
    return out.reshape(b, s, d)


def kernel(x, p, positions, g_mix_norm, w_in, g_qa, g_ka, g_cq, w_uq, g_ckv, w_ukv, g_qb, g_kb,
           w_out_a, w_out_b, w_o, g_ffn_norm, w_ffn_gate, w_ffn_up, w_ffn_down, g_ple_norm,
           w_ple_gate, w_ple_proj):
    for i in range(p.shape[0]):
        x = _layer(x, p[i], positions, g_mix_norm[i], w_in[i], g_qa[i], g_ka[i], g_cq[i], w_uq[i],
                   g_ckv[i], w_ukv[i], g_qb[i], g_kb[i], w_out_a[i], w_out_b[i], w_o[i],
                   g_ffn_norm[i], w_ffn_gate[i], w_ffn_up[i], w_ffn_down[i], g_ple_norm[i],
                   w_ple_gate[i], w_ple_proj[i])
    return x
```

```python
import functools

import numpy as np
import jax
import jax.numpy as jnp
from jax import lax
from jax.experimental import pallas as pl
from jax.experimental.pallas import tpu as pltpu

F32 = jnp.float32
BF16 = jnp.bfloat16

ROPE_THETA = 10000.0
NORM_EPS = 1e-6
NEG = -1e30
M_INIT = -1e29
POS_BIG = 3e38
TH_ALL = 0.5 * NEG
LOG2E = 1.4426950408889634
CNT_ROWS = 128
MAX_FIXED_SHIFT = 60.0
A_HEADS = 16
A_KV_HEADS = 4
A_HEAD_DIM = 128
IDX_HEADS = 16
IDX_DIM = 64
TOPK_MAX = 256
B_HEADS = 16
B_Q_LORA = 512
B_KV_LORA = 256
B_NOPE = 128
B_ROPE = 64
B_V = 128
B_QK = B_NOPE + B_ROPE
LANES = 128
VMEM_LIMIT = 56 * 1024 * 1024

COL_QA = 0
COL_GATE_A = 2048
COL_GATE_B = 4096
COL_QI = 6144
COL_KA = 7168
COL_VA = 7680
COL_CQ = 8192
COL_CKV = 8704
COL_KIW = 8960
COL_KR = 9088
N_PROJ = 9216


def _dot(a, b):
    return jnp.dot(a, b, preferred_element_type=F32)


def _dot_nt(a, b):
    return lax.dot_general(a, b, (((1,), (1,)), ((), ())), preferred_element_type=F32)


def _cparams(sem):
    return pltpu.CompilerParams(dimension_semantics=sem, vmem_limit_bytes=VMEM_LIMIT)


def _rms_to_bf16(x, g):
    ms = jnp.mean(x * x, axis=-1, keepdims=True)
    return (x * lax.rsqrt(ms + NORM_EPS) * g).astype(BF16)


def _norm_matmul_kernel(x_ref, g_ref, w_ref, o_ref, hn_ref):
    @pl.when(pl.program_id(1) == 0)
    def _():
        hn_ref[...] = _rms_to_bf16(x_ref[...], g_ref[...])

    o_ref[...] = _dot(hn_ref[...], w_ref[...]).astype(o_ref.dtype)


def _norm_matmul(name, x, x_col_block, k, g, w, out_dtype, tm, tn):
    m = x.shape[0]
    n = w.shape[1]
    return pl.pallas_call(
        _norm_matmul_kernel,
        name=name,
        grid=(m // tm, n // tn),
        in_specs=[
            pl.BlockSpec((tm, k), lambda i, j: (i, x_col_block)),
            pl.BlockSpec((1, k), lambda i, j: (0, 0)),
            pl.BlockSpec((k, tn), lambda i, j: (0, j)),
        ],
        out_specs=pl.BlockSpec((tm, tn), lambda i, j: (i, j)),
        out_shape=jax.ShapeDtypeStruct((m, n), out_dtype),
        scratch_shapes=[pltpu.VMEM((tm, k), BF16)],
        compiler_params=_cparams(("parallel", "arbitrary")),
    )(x, g, w)


def _ffn_up_kernel(x_ref, g_ref, wg_ref, wu_ref, o_ref, hn_ref):
    @pl.when(pl.program_id(1) == 0)
    def _():
        hn_ref[...] = _rms_to_bf16(x_ref[...], g_ref[...])

    h = hn_ref[...]
    a = _dot(h, wg_ref[...])
    b = _dot(h, wu_ref[...])
    o_ref[...] = (a * jax.nn.sigmoid(a) * b).astype(o_ref.dtype)


def _ffn_up(x, g, wg, wu, tm, tn):
    m, k = x.shape
    n = wg.shape[1]
    return pl.pallas_call(
        _ffn_up_kernel,
        name="ffn_up",
        grid=(m // tm, n // tn),
        in_specs=[
            pl.BlockSpec((tm, k), lambda i, j: (i, 0)),
            pl.BlockSpec((1, k), lambda i, j: (0, 0)),
            pl.BlockSpec((k, tn), lambda i, j: (0, j)),
            pl.BlockSpec((k, tn), lambda i, j: (0, j)),
        ],
        out_specs=pl.BlockSpec((tm, tn), lambda i, j: (i, j)),
        out_shape=jax.ShapeDtypeStruct((m, n), BF16),
        scratch_shapes=[pltpu.VMEM((tm, k), BF16)],
        compiler_params=_cparams(("parallel", "arbitrary")),
    )(x, g, wg, wu)


def _matmul_res_kernel(a_ref, w_ref, r_ref, o_ref):
    o_ref[...] = r_ref[...] + _dot(a_ref[...], w_ref[...])


def _matmul_res(name, a, w, res, tm, tn):
    m, k = a.shape
    n = w.shape[1]
    return pl.pallas_call(
        _matmul_res_kernel,
        name=name,
        grid=(n // tn, m // tm),
        in_specs=[
            pl.BlockSpec((tm, k), lambda j, i: (i, 0)),
            pl.BlockSpec((k, tn), lambda j, i: (0, j)),
            pl.BlockSpec((tm, tn), lambda j, i: (i, j)),
        ],
        out_specs=pl.BlockSpec((tm, tn), lambda j, i: (i, j)),
        out_shape=jax.ShapeDtypeStruct((m, n), F32),
        compiler_params=_cparams(("parallel", "parallel")),
    )(a, w, res)


def _merge_kernel(ya_ref, yb_ref, wa_ref, wb_ref, ga_ref, gb_ref, o_ref):
    a = _dot(ya_ref[...], wa_ref[...])
    b = _dot(yb_ref[...], wb_ref[...])
    o_ref[...] = (jax.nn.sigmoid(ga_ref[...]) * a + jax.nn.sigmoid(gb_ref[...]) * b).astype(o_ref.dtype)


def _merge(ya, yb, wa, wb, proj, tm, tn):
    m, k = ya.shape
    n = wa.shape[1]
    ga0 = COL_GATE_A // tn
    gb0 = COL_GATE_B // tn
    return pl.pallas_call(
        _merge_kernel,
        name="merge",
        grid=(m // tm, n // tn),
        in_specs=[
            pl.BlockSpec((tm, k), lambda i, j: (i, 0)),
            pl.BlockSpec((tm, k), lambda i, j: (i, 0)),
            pl.BlockSpec((k, tn), lambda i, j: (0, j)),
            pl.BlockSpec((k, tn), lambda i, j: (0, j)),
            pl.BlockSpec((tm, tn), lambda i, j: (i, ga0 + j)),
            pl.BlockSpec((tm, tn), lambda i, j: (i, gb0 + j)),
        ],
        out_specs=pl.BlockSpec((tm, tn), lambda i, j: (i, j)),
        out_shape=jax.ShapeDtypeStruct((m, n), BF16),
        compiler_params=_cparams(("parallel", "parallel")),
    )(ya, yb, wa, wb, proj, proj)


def _ple_kernel(x_ref, g_ref, wg_ref, p_ref, wp_ref, xr_ref, o_ref, hn_ref):
    @pl.when(pl.program_id(1) == 0)
    def _():
        hn_ref[...] = _rms_to_bf16(x_ref[...], g_ref[...])

    gate = jax.nn.sigmoid(_dot(hn_ref[...], wg_ref[...]))
    emb = _dot(p_ref[...].astype(BF16), wp_ref[...])
    o_ref[...] = xr_ref[...] + gate * emb


def _ple(x, g, wg, p, wp, tm, tn):
    m, k = x.shape
    n = wg.shape[1]
    kp = p.shape[1]
    return pl.pallas_call(
        _ple_kernel,
        name="ple",
        grid=(m // tm, n // tn),
        in_specs=[
            pl.BlockSpec((tm, k), lambda i, j: (i, 0)),
            pl.BlockSpec((1, k), lambda i, j: (0, 0)),
            pl.BlockSpec((k, tn), lambda i, j: (0, j)),
            pl.BlockSpec((tm, kp), lambda i, j: (i, 0)),
            pl.BlockSpec((kp, tn), lambda i, j: (0, j)),
            pl.BlockSpec((tm, tn), lambda i, j: (i, j)),
        ],
        out_specs=pl.BlockSpec((tm, tn), lambda i, j: (i, j)),
        out_shape=jax.ShapeDtypeStruct((m, n), F32),
        scratch_shapes=[pltpu.VMEM((tm, k), BF16)],
        compiler_params=_cparams(("parallel", "arbitrary")),
    )(x, g, wg, p, wp, x)


def _rope64_pairs(x, c64, s64, first_half):
    partner = jnp.where(first_half, pltpu.roll(x, 96, 1), pltpu.roll(x, 32, 1))
    return x * c64 + partner * s64


def _prep_a_kernel(qa_ref, ka_ref, va_ref, qi_ref, kiw_ref, c128_ref, s128_ref, c64_ref, s64_ref,
                   gq_ref, gk_ref, qa_o, ka_o, va_o, qi_o, klo_o, khi_o, wi_o, *, wi_scale):
    c128 = c128_ref[...]
    s128 = s128_ref[...]

    def norm_rope(x, g):
        ms = jnp.mean(x * x, axis=-1, keepdims=True)
        y = x * lax.rsqrt(ms + NORM_EPS) * g
        return y * c128 + pltpu.roll(y, 64, 1) * s128

    for h in range(A_HEADS):
        hs = slice(h * LANES, (h + 1) * LANES)
        qa_o[:, hs] = norm_rope(qa_ref[:, hs], gq_ref[...]).astype(BF16)
    for h in range(A_KV_HEADS):
        hs = slice(h * LANES, (h + 1) * LANES)
        ka_o[:, hs] = norm_rope(ka_ref[:, hs], gk_ref[...]).astype(BF16)
    va_o[...] = va_ref[...].astype(BF16)

    c64 = c64_ref[...]
    s64 = s64_ref[...]
    lane = lax.broadcasted_iota(jnp.int32, c64.shape, 1)
    first_half = (lane % 64) < 32
    for p in range(IDX_HEADS * IDX_DIM // LANES):
        ps = slice(p * LANES, (p + 1) * LANES)
        qi_o[:, ps] = _rope64_pairs(qi_ref[:, ps], c64, s64, first_half).astype(BF16)
    kiw = kiw_ref[...]
    ki = jnp.where(lane < 64, _rope64_pairs(kiw, c64, s64, first_half), 0.0)
    klo_o[...] = ki.astype(BF16)
    khi_o[...] = pltpu.roll(ki, 64, 1).astype(BF16)
    wi_o[...] = kiw * wi_scale


def _prep_a(proj, c128, s128, c64, s64, gq, gk, tp):
    m = proj.shape[0]
    row = lambda w, cb: pl.BlockSpec((tp, w), lambda i: (i, cb))
    vec = pl.BlockSpec((1, LANES), lambda i: (0, 0))
    out = lambda w: pl.BlockSpec((tp, w), lambda i: (i, 0))
    wi_scale = (IDX_HEADS ** -0.5) * (IDX_DIM ** -0.5)
    return pl.pallas_call(
        functools.partial(_prep_a_kernel, wi_scale=wi_scale),
        name="prep_a",
        grid=(m // tp,),
        in_specs=[
            row(2048, COL_QA // 2048), row(512, COL_KA // 512), row(512, COL_VA // 512),
            row(1024, COL_QI // 1024), row(LANES, COL_KIW // LANES),
            out(LANES), out(LANES), out(LANES), out(LANES), vec, vec,
        ],
        out_specs=[out(2048), out(512), out(512), out(1024), out(LANES), out(LANES), out(LANES)],
        out_shape=[
            jax.ShapeDtypeStruct((m, 2048), BF16), jax.ShapeDtypeStruct((m, 512), BF16),
            jax.ShapeDtypeStruct((m, 512), BF16), jax.ShapeDtypeStruct((m, 1024), BF16),
            jax.ShapeDtypeStruct((m, LANES), BF16), jax.ShapeDtypeStruct((m, LANES), BF16),
            jax.ShapeDtypeStruct((m, LANES), F32),
        ],
        compiler_params=_cparams(("parallel",)),
    )(proj, proj, proj, proj, proj, c128, s128, c64, s64, gq, gk)


def _prep_b_kernel(qb_ref, kv_ref, kr_ref, c64_ref, s64_ref, gqn_ref, gqp_ref, gkn_ref, gkp_ref, shift_ref,
                   q_o, k_o, v_o, *, q_scale):
    c64 = c64_ref[...]
    s64 = s64_ref[...]
    lane = lax.broadcasted_iota(jnp.int32, c64.shape, 1)
    first_half = (lane % 64) < 32
    lo = lane < 64
    kr = jnp.where(lo, kr_ref[...], 0.0)
    kr_ss = jnp.sum(kr * kr, axis=-1, keepdims=True)
    kpe_lo = jnp.where(lo, _rope64_pairs(kr * gkp_ref[...], c64, s64, first_half), 0.0)
    kpe_hi = pltpu.roll(kpe_lo, 64, 1)
    nope_w = B_HEADS * B_NOPE
    for h in range(B_HEADS):
        hs = slice(h * LANES, (h + 1) * LANES)
        pair = h // 2
        mine = lo if h % 2 == 0 else jnp.logical_not(lo)
        shift_lane = lane == (64 if h % 2 == 0 else 0)
        qn = qb_ref[:, hs]
        qp = jnp.where(mine, qb_ref[:, nope_w + pair * LANES:nope_w + (pair + 1) * LANES], 0.0)
        ss = jnp.sum(qn * qn, axis=-1, keepdims=True) + jnp.sum(qp * qp, axis=-1, keepdims=True)
        r = lax.rsqrt(ss * (1.0 / B_QK) + NORM_EPS) * q_scale
        q_o[:, 2 * h * LANES:(2 * h + 1) * LANES] = (qn * r * gqn_ref[...]).astype(BF16)
        qpe = _rope64_pairs(qp * gqp_ref[...], c64, s64, first_half) * r
        qpe = jnp.where(mine, qpe, jnp.where(shift_lane, 1.0, 0.0))
        q_o[:, (2 * h + 1) * LANES:(2 * h + 2) * LANES] = qpe.astype(BF16)
        kn = kv_ref[:, hs]
        rk = lax.rsqrt((jnp.sum(kn * kn, axis=-1, keepdims=True) + kr_ss) * (1.0 / B_QK) + NORM_EPS)
        k_o[:, 2 * h * LANES:(2 * h + 1) * LANES] = (kn * rk * gkn_ref[...]).astype(BF16)
        kpe = kpe_lo if h % 2 == 0 else kpe_hi
        k_o[:, (2 * h + 1) * LANES:(2 * h + 2) * LANES] = jnp.where(
            shift_lane, -shift_ref[...], kpe * rk).astype(BF16)
        v_o[:, hs] = kv_ref[:, nope_w + h * LANES:nope_w + (h + 1) * LANES].astype(BF16)


def _prep_b(qb, kv, proj, c64, s64, gqn, gqp, gkn, gkp, shift, tp):
    m = qb.shape[0]
    vec = pl.BlockSpec((1, LANES), lambda i: (0, 0))
    out = lambda w: pl.BlockSpec((tp, w), lambda i: (i, 0))
    return pl.pallas_call(
        functools.partial(_prep_b_kernel, q_scale=B_QK ** -0.5 * LOG2E),
        name="prep_b",
        grid=(m // tp,),
        in_specs=[
            out(qb.shape[1]), out(kv.shape[1]),
            pl.BlockSpec((tp, LANES), lambda i: (i, COL_KR // LANES)),
            out(LANES), out(LANES), vec, vec, vec, vec, vec,
        ],
        out_specs=[out(B_HEADS * 256), out(B_HEADS * 256), out(B_HEADS * B_V)],
        out_shape=[
            jax.ShapeDtypeStruct((m, B_HEADS * 256), BF16),
            jax.ShapeDtypeStruct((m, B_HEADS * 256), BF16),
            jax.ShapeDtypeStruct((m, B_HEADS * B_V), BF16),
        ],
        compiler_params=_cparams(("parallel",)),
    )(qb, kv, proj, c64, s64, gqn, gqp, gkn, gkp, shift)


def _indexer_kernel(q_ref, klo_ref, khi_ref, w_ref, sc_ref, th_ref, *, tq, tk, seq, k_sel):
    qi = pl.program_id(1)
    n_vis = (qi * tq) // tk + 1
    n_all = seq // tk
    row = qi * tq + lax.broadcasted_iota(jnp.int32, (tq, 1), 0)
    wcols = [w_ref[0, :, IDX_DIM + h:IDX_DIM + h + 1] for h in range(IDX_HEADS)]

    def gen(c, carry):
        rmax, rmin = carry
        off = pl.multiple_of(c * tk, tk)
        kl = klo_ref[0, pl.ds(off, tk), :]
        kh = khi_ref[0, pl.ds(off, tk), :]
        acc = jnp.zeros((tq, tk), F32)
        for p in range(IDX_HEADS // 2):
            qp = q_ref[0, :, p * LANES:(p + 1) * LANES]
            acc = acc + wcols[2 * p] * jnp.maximum(_dot_nt(qp, kl), 0.0)
            acc = acc + wcols[2 * p + 1] * jnp.maximum(_dot_nt(qp, kh), 0.0)
        col = off + lax.broadcasted_iota(jnp.int32, (tq, tk), 1)
        vis = col <= row
        sc_ref[0, :, pl.ds(off, tk)] = jnp.where(vis, acc, NEG)
        rmax = jnp.maximum(rmax, jnp.max(jnp.where(vis, acc, NEG), axis=-1, keepdims=True))
        rmin = jnp.minimum(rmin, jnp.min(jnp.where(vis, acc, POS_BIG), axis=-1, keepdims=True))
        return rmax, rmin

    rmax, rmin = lax.fori_loop(0, n_vis, gen,
                               (jnp.full((tq, 1), NEG, F32), jnp.full((tq, 1), POS_BIG, F32)))

    def fill(c, carry):
        sc_ref[0, :, pl.ds(pl.multiple_of(c * tk, tk), tk)] = jnp.full((tq, tk), NEG, F32)
        return carry

    lax.fori_loop(n_vis, n_all, fill, 0)

    cw = min(2 * tk, seq)
    n_cnt = (n_vis * tk + cw - 1) // cw

    def count_ge(mid):
        accs = []
        for r in range(tq // CNT_ROWS):
            rs = slice(r * CNT_ROWS, (r + 1) * CNT_ROWS)
            midb = mid[rs]

            def body(c, acc, rs=rs, midb=midb):
                off = pl.multiple_of(c * cw, cw)
                for j in range(cw // LANES):
                    blk = sc_ref[0, rs, pl.ds(off + j * LANES, LANES)]
                    acc = acc + jnp.where(blk >= midb, 1.0, 0.0)
                return acc

            accs.append(lax.fori_loop(0, n_cnt, body, jnp.zeros((CNT_ROWS, LANES), F32)))
        return jnp.concatenate(
            [jnp.broadcast_to(jnp.sum(a, axis=-1, keepdims=True), (CNT_ROWS, LANES)) for a in accs], axis=0)

    rowb = qi * tq + lax.broadcasted_iota(jnp.int32, (tq, LANES), 0)
    active0 = (rowb + 1) > k_sel
    lo0 = jnp.where(active0, jnp.broadcast_to(rmin, (tq, LANES)), TH_ALL)
    hi0 = jnp.where(active0, jnp.broadcast_to(rmax, (tq, LANES)), TH_ALL)

    def n_open(lo, hi):
        return jnp.sum(jnp.where(lo < hi, 1.0, 0.0))

    def cond(st):
        it, nopen = st[0], st[1]
        return jnp.logical_and(nopen > 0.0, it < 64)

    def body(st):
        it, _, lo, hi = st
        mid = lo + (hi - lo) * 0.5
        cnt = count_ge(mid)
        nopen = n_open(lo, hi)
        ge = cnt >= k_sel
        lo_n = jnp.where(ge, mid, lo)
        hi_n = jnp.where(ge, hi, mid)
        hi_n = jnp.where(cnt == k_sel, lo_n, hi_n)
        hi_n = jnp.where(mid >= hi, lo_n, hi_n)
        hi_n = jnp.where(mid <= lo, lo_n, hi_n)
        return it + 1, nopen, lo_n, hi_n

    st = lax.while_loop(cond, body, (jnp.int32(0), n_open(lo0, hi0), lo0, hi0))
    th_ref[0] = st[2]


def _indexer(qi_r, klo, khi, wi, tq, tk, k_sel):
    b, s, _ = qi_r.shape
    return pl.pallas_call(
        functools.partial(_indexer_kernel, tq=tq, tk=tk, seq=s, k_sel=float(k_sel)),
        name="indexer",
        grid=(b, s // tq),
        in_specs=[
            pl.BlockSpec((1, tq, IDX_HEADS * IDX_DIM), lambda bi, qi: (bi, qi, 0)),
            pl.BlockSpec((1, s, LANES), lambda bi, qi: (bi, 0, 0)),
            pl.BlockSpec((1, s, LANES), lambda bi, qi: (bi, 0, 0)),
            pl.BlockSpec((1, tq, LANES), lambda bi, qi: (bi, qi, 0)),
        ],
        out_specs=[
            pl.BlockSpec((1, tq, s), lambda bi, qi: (bi, qi, 0)),
            pl.BlockSpec((1, tq, LANES), lambda bi, qi: (bi, qi, 0)),
        ],
        out_shape=[jax.ShapeDtypeStruct((b, s, s), F32), jax.ShapeDtypeStruct((b, s, LANES), F32)],
        compiler_params=_cparams(("parallel", "parallel")),
    )(qi_r, klo, khi, wi)


def _causal_pairs(s, tq, tk):
    qs, ks = [], []
    for qi in range(s // tq):
        for ki in range((qi * tq + tq - 1) // tk + 1):
            qs.append(qi)
            ks.append(ki)
    return jnp.asarray(np.array(qs, np.int32)), jnp.asarray(np.array(ks, np.int32))


def _softmax_step(s, v, rows, m_ref, l_ref, acc_ref, fixed_shift):
    blocks = [s[:, j * LANES:(j + 1) * LANES] for j in range(s.shape[1] // LANES)]
    if fixed_shift:
        ps = [jnp.exp2(blk) for blk in blocks]
        l_ref[rows] = l_ref[rows] + functools.reduce(jnp.add, ps)
        p = jnp.concatenate([blk.astype(BF16) for blk in ps], axis=1)
        acc_ref[rows] = acc_ref[rows] + _dot(p, v)
        return
    mx = functools.reduce(jnp.maximum, blocks)
    m_prev = m_ref[rows]
    m_new = jnp.maximum(m_prev, jnp.max(mx, axis=-1, keepdims=True))
    alpha = jnp.exp2(m_prev - m_new)
    ps = [jnp.exp2(blk - m_new) for blk in blocks]
    l_ref[rows] = alpha * l_ref[rows] + jnp.sum(functools.reduce(jnp.add, ps), axis=-1, keepdims=True)
    p = jnp.concatenate([blk.astype(BF16) for blk in ps], axis=1)
    acc_ref[rows] = alpha * acc_ref[rows] + _dot(p, v)
    m_ref[rows] = m_new


def _softmax_finish(rows, l_ref, acc_ref, fixed_shift):
    l = l_ref[rows]
    if fixed_shift:
        l = jnp.sum(l, axis=-1, keepdims=True)
    return acc_ref[rows] / l


def _dsa_kernel(qt_ref, kt_ref, q_ref, k_ref, v_ref, sc_ref, th_ref, shift_ref, o_ref, m_ref, l_ref,
                acc_ref, bias_ref, qs_ref, *, tq, tk, fixed_shift):
    step = pl.program_id(1)
    qi = qt_ref[step]
    ki = kt_ref[step]

    @pl.when(ki == 0)
    def _():
        m_ref[...] = jnp.full(m_ref.shape, M_INIT, F32)
        l_ref[...] = jnp.zeros(l_ref.shape, F32)
        acc_ref[...] = jnp.zeros(acc_ref.shape, F32)
        for h in range(A_HEADS):
            qs_ref[h * tq:(h + 1) * tq, :] = q_ref[0, :, h * LANES:(h + 1) * LANES]

    bias_ref[...] = jnp.where(sc_ref[0] >= th_ref[0][:, :1], -shift_ref[:, :1], NEG)
    grp = A_HEADS // A_KV_HEADS
    for g in range(A_KV_HEADS):
        rows = slice(g * grp * tq, (g + 1) * grp * tq)
        k = k_ref[0, :, g * LANES:(g + 1) * LANES]
        v = v_ref[0, :, g * LANES:(g + 1) * LANES]
        s = _dot_nt(qs_ref[rows, :], k)
        s = jnp.concatenate([s[j * tq:(j + 1) * tq] + bias_ref[...] for j in range(grp)], axis=0)
        _softmax_step(s, v, rows, m_ref, l_ref, acc_ref, fixed_shift)

    @pl.when(ki == (qi * tq + tq - 1) // tk)
    def _():
        for h in range(A_HEADS):
            out = _softmax_finish(slice(h * tq, (h + 1) * tq), l_ref, acc_ref, fixed_shift)
            o_ref[0, :, h * LANES:(h + 1) * LANES] = out.astype(o_ref.dtype)


def _dsa_attn(q, k, v, scores, th, shift, tq, tk, fixed_shift):
    b, s, dq = q.shape
    qt, kt = _causal_pairs(s, tq, tk)
    grid_spec = pltpu.PrefetchScalarGridSpec(
        num_scalar_prefetch=2,
        grid=(b, qt.shape[0]),
        in_specs=[
            pl.BlockSpec((1, tq, dq), lambda bi, p, qt, kt: (bi, qt[p], 0)),
            pl.BlockSpec((1, tk, k.shape[2]), lambda bi, p, qt, kt: (bi, kt[p], 0)),
            pl.BlockSpec((1, tk, v.shape[2]), lambda bi, p, qt, kt: (bi, kt[p], 0)),
            pl.BlockSpec((1, tq, tk), lambda bi, p, qt, kt: (bi, qt[p], kt[p])),
            pl.BlockSpec((1, tq, LANES), lambda bi, p, qt, kt: (bi, qt[p], 0)),
            pl.BlockSpec((1, LANES), lambda bi, p, qt, kt: (0, 0)),
        ],
        out_specs=pl.BlockSpec((1, tq, dq), lambda bi, p, qt, kt: (bi, qt[p], 0)),
        scratch_shapes=[
            pltpu.VMEM((A_HEADS * tq, LANES), F32),
            pltpu.VMEM((A_HEADS * tq, LANES), F32),
            pltpu.VMEM((A_HEADS * tq, A_HEAD_DIM), F32),
            pltpu.VMEM((tq, tk), F32),
            pltpu.VMEM((A_HEADS * tq, A_HEAD_DIM), BF16),
        ],
    )
    return pl.pallas_call(
        functools.partial(_dsa_kernel, tq=tq, tk=tk, fixed_shift=fixed_shift),
        name="dsa_attn_fixed" if fixed_shift else "dsa_attn_online",
        grid_spec=grid_spec,
        out_shape=jax.ShapeDtypeStruct((b, s, dq), BF16),
        compiler_params=_cparams(("parallel", "arbitrary")),
    )(qt, kt, q, k, v, scores, th, shift)


def _mla_kernel(qt_ref, kt_ref, q_ref, k_ref, v_ref, o_ref, m_ref, l_ref, acc_ref, *, tq, tk, hg,
                fixed_shift):
    step = pl.program_id(2)
    qi = qt_ref[step]
    ki = kt_ref[step]

    @pl.when(ki == 0)
    def _():
        m_ref[...] = jnp.full(m_ref.shape, M_INIT, F32)
        l_ref[...] = jnp.zeros(l_ref.shape, F32)
        acc_ref[...] = jnp.zeros(acc_ref.shape, F32)

    def heads(masked):
        if masked:
            keep = (lax.broadcasted_iota(jnp.int32, (tq, tk), 1)
                    <= lax.broadcasted_iota(jnp.int32, (tq, tk), 0))
        for h in range(hg):
            q = q_ref[0, :, h * 256:(h + 1) * 256]
            k = k_ref[0, :, h * 256:(h + 1) * 256]
            v = v_ref[0, :, h * B_V:(h + 1) * B_V]
            s = _dot_nt(q, k)
            if masked:
                s = jnp.where(keep, s, NEG)
            _softmax_step(s, v, slice(h * tq, (h + 1) * tq), m_ref, l_ref, acc_ref, fixed_shift)

    @pl.when(ki < qi)
    def _():
        heads(False)

    @pl.when(ki == qi)
    def _():
        heads(True)
        for h in range(hg):
            out = _softmax_finish(slice(h * tq, (h + 1) * tq), l_ref, acc_ref, fixed_shift)
            o_ref[0, :, h * B_V:(h + 1) * B_V] = out.astype(o_ref.dtype)


def _mla_attn(q, k, v, tq, tk, hg, fixed_shift):
    b, s, _ = q.shape
    assert tq == tk
    qt, kt = _causal_pairs(s, tq, tk)
    grid_spec = pltpu.PrefetchScalarGridSpec(
        num_scalar_prefetch=2,
        grid=(b, B_HEADS // hg, qt.shape[0]),
        in_specs=[
            pl.BlockSpec((1, tq, hg * 256), lambda bi, g, p, qt, kt: (bi, qt[p], g)),
            pl.BlockSpec((1, tk, hg * 256), lambda bi, g, p, qt, kt: (bi, kt[p], g)),
            pl.BlockSpec((1, tk, hg * B_V), lambda bi, g, p, qt, kt: (bi, kt[p], g)),
        ],
        out_specs=pl.BlockSpec((1, tq, hg * B_V), lambda bi, g, p, qt, kt: (bi, qt[p], g)),
        scratch_shapes=[
            pltpu.VMEM((hg * tq, LANES), F32),
            pltpu.VMEM((hg * tq, LANES), F32),
            pltpu.VMEM((hg * tq, B_V), F32),
        ],
    )
    return pl.pallas_call(
        functools.partial(_mla_kernel, tq=tq, tk=tk, hg=hg, fixed_shift=fixed_shift),
        name="mla_attn_fixed" if fixed_shift else "mla_attn_online",
        grid_spec=grid_spec,
        out_shape=jax.ShapeDtypeStruct((b, s, B_HEADS * B_V), BF16),
        compiler_params=_cparams(("parallel", "parallel", "arbitrary")),
    )(qt, kt, q, k, v)


def _regroup_w_in(w):
    sizes = (A_HEADS * A_HEAD_DIM, A_KV_HEADS * A_HEAD_DIM, A_KV_HEADS * A_HEAD_DIM,
             IDX_HEADS * IDX_DIM, IDX_DIM, IDX_HEADS, B_Q_LORA, B_KV_LORA, B_ROPE)
    offs = np.cumsum((0,) + sizes)
    w = w.astype(BF16)
    qa, ka, va, qi, ki, wi, cq, ckv, kr = (w[:, offs[j]:offs[j + 1]] for j in range(len(sizes)))
    gates = w[:, offs[-1]:]
    d = w.shape[0]
    kiw = jnp.concatenate([ki, wi, jnp.zeros((d, LANES - IDX_DIM - IDX_HEADS), w.dtype)], axis=1)
    krp = jnp.concatenate([kr, jnp.zeros((d, LANES - B_ROPE), w.dtype)], axis=1)
    return jnp.concatenate([qa, gates, qi, ka, va, cq, ckv, kiw, krp], axis=1).astype(BF16)


def _softmax_shift(d, g_q, g_k):
    bound = (d ** 0.5) * LOG2E * 1.02 * jnp.max(jnp.abs(g_q)) * jnp.max(jnp.abs(g_k))
    c = jnp.ceil(bound.astype(F32))
    ok = c <= MAX_FIXED_SHIFT
    return jnp.full((1, LANES), jnp.where(ok, c, 0.0), F32), ok


def _rope_tables(pos, d):
    half = d // 2
    freqs = jnp.power(jnp.float32(ROPE_THETA), -jnp.arange(half, dtype=F32) * (2.0 / d))
    ang = pos.astype(F32)[..., None] * freqs
    cos, sin = jnp.cos(ang), jnp.sin(ang)
    reps = LANES // d
    c = jnp.tile(jnp.concatenate([cos, cos], axis=-1), (1, 1, reps))
    s = jnp.tile(jnp.concatenate([-sin, sin], axis=-1), (1, 1, reps))
    return c.reshape(-1, LANES), s.reshape(-1, LANES)


def _layer(x, p, positions, g_mix_norm, w_in, g_qa, g_ka, g_cq, w_uq, g_ckv, w_ukv, g_qb, g_kb,
           w_out_a, w_out_b, w_o, g_ffn_norm, w_ffn_gate, w_ffn_up, w_ffn_down, g_ple_norm,
           w_ple_gate, w_ple_proj):
    b, s, d = x.shape
    m = b * s
    k_sel = min(TOPK_MAX, s // 4)
    xf = x.reshape(m, d)
    row = lambda g: g.reshape(1, -1).astype(F32)

    c128, s128 = _rope_tables(positions, A_HEAD_DIM)
    c64, s64 = _rope_tables(positions, IDX_DIM)

    proj = _norm_matmul("in_proj", xf, 0, d, row(g_mix_norm), _regroup_w_in(w_in), F32, tm=1024, tn=512)

    qa_r, ka_r, va_b, qi_r, klo, khi, wi = _prep_a(
        proj, c128, s128, c64, s64, row(g_qa) * (A_HEAD_DIM ** -0.5 * LOG2E), row(g_ka), tp=512)
    r3 = lambda a: a.reshape(b, s, a.shape[-1])
    scores, th = _indexer(r3(qi_r), r3(klo), r3(khi), r3(wi), tq=512, tk=512, k_sel=k_sel)
    shift_a, fixed_a = _softmax_shift(A_HEAD_DIM, g_qa, g_ka)
    y_a = lax.cond(
        fixed_a,
        functools.partial(_dsa_attn, tq=512, tk=512, fixed_shift=True),
        functools.partial(_dsa_attn, tq=512, tk=512, fixed_shift=False),
        r3(qa_r), r3(ka_r), r3(va_b), scores, th, shift_a)

    w_uq_r = w_uq.reshape(B_Q_LORA, B_HEADS, B_QK)
    w_uq_r = jnp.concatenate([w_uq_r[:, :, :B_NOPE].reshape(B_Q_LORA, -1),
                              w_uq_r[:, :, B_NOPE:].reshape(B_Q_LORA, -1)], axis=1).astype(BF16)
    w_ukv_r = w_ukv.reshape(B_KV_LORA, B_HEADS, 2, B_NOPE)
    w_ukv_r = jnp.concatenate([w_ukv_r[:, :, 0, :].reshape(B_KV_LORA, -1),
                               w_ukv_r[:, :, 1, :].reshape(B_KV_LORA, -1)], axis=1).astype(BF16)
    qb = _norm_matmul("q_up", proj, COL_CQ // B_Q_LORA, B_Q_LORA, row(g_cq), w_uq_r, F32, tm=1024, tn=1024)
    kv = _norm_matmul("kv_up", proj, COL_CKV // B_KV_LORA, B_KV_LORA, row(g_ckv), w_ukv_r, F32, tm=1024, tn=1024)
    pe2 = lambda g: jnp.tile(g[B_NOPE:], 2).reshape(1, LANES).astype(F32)
    shift_b, fixed_b = _softmax_shift(B_QK, g_qb, g_kb)
    q_m, k_m, v_m = _prep_b(qb, kv, proj, c64, s64, row(g_qb[:B_NOPE]), pe2(g_qb),
                            row(g_kb[:B_NOPE]), pe2(g_kb), shift_b, tp=256)
    y_b = lax.cond(
        fixed_b,
        functools.partial(_mla_attn, tq=512, tk=512, hg=8, fixed_shift=True),
        functools.partial(_mla_attn, tq=512, tk=512, hg=8, fixed_shift=False),
        r3(q_m), r3(k_m), r3(v_m))

    mixed = _merge(y_a.reshape(m, -1), y_b.reshape(m, -1), w_out_a.astype(BF16), w_out_b.astype(BF16),
                   proj, tm=1024, tn=512)
    x1 = _matmul_res("w_o", mixed, w_o.astype(BF16), xf, tm=1024, tn=1024)

    act = _ffn_up(x1, row(g_ffn_norm), w_ffn_gate.astype(BF16), w_ffn_up.astype(BF16), tm=1024, tn=512)
    x2 = _matmul_res("ffn_down", act, w_ffn_down.astype(BF16), x1, tm=512, tn=1024)

    out = _ple(x2, row(g_ple_norm), w_ple_gate.astype(BF16), p.reshape(m, -1),
               w_ple_proj.astype(BF16), tm=1024, tn=512)
    return out.reshape(b, s, d)


def kernel(x, p, positions, g_mix_norm, w_in, g_qa, g_ka, g_cq, w_uq, g_ckv, w_ukv, g_qb, g_kb,
           w_out_a, w_out_b, w_o, g_ffn_norm, w_ffn_gate, w_ffn_up, w_ffn_down, g_ple_norm,
           w_ple_gate, w_ple_proj):
    for i in range(p.shape[0]):
        x = _layer(x, p[i], positions, g_mix_norm[i], w_in[i], g_qa[i], g_ka[i], g_cq[i], w_uq[i],
                   g_ckv[i], w_ukv[i], g_qb[i], g_kb[i], w_out_a[i], w_out_b[i], w_o[i],
                   g_ffn_norm[i], w_ffn_gate[i], w_ffn_up[i], w_ffn_down[i], g_ple_norm[i],
                   w_ple_gate[i], w_ple_proj[i])
    return x
```

```python
import functools

import numpy as np
import jax
import jax.numpy as jnp
from jax import lax
from jax.experimental import pallas as pl
from jax.experimental.pallas import tpu as pltpu

F32 = jnp.float32
BF16 = jnp.bfloat16

ROPE_THETA = 10000.0
NORM_EPS = 1e-6
NEG = -1e30
M_INIT = -1e29
POS_BIG = 3e38
TH_ALL = 0.5 * NEG
LOG2E = 1.4426950408889634
MAX_SEARCH_STEPS = 512
CNT_ROWS = 128
MAX_FIXED_SHIFT = 60.0
A_HEADS = 16
A_KV_HEADS = 4
A_HEAD_DIM = 128
IDX_HEADS = 16
IDX_DIM = 64
TOPK_MAX = 256
B_HEADS = 16
B_Q_LORA = 512
B_KV_LORA = 256
B_NOPE = 128
B_ROPE = 64
B_V = 128
B_QK = B_NOPE + B_ROPE
LANES = 128
VMEM_LIMIT = 56 * 1024 * 1024

COL_QA = 0
COL_GATE_A = 2048
COL_GATE_B = 4096
COL_QI = 6144
COL_KA = 7168
COL_VA = 7680
COL_CQ = 8192
COL_CKV = 8704
COL_KIW = 8960
COL_KR = 9088
N_PROJ = 9216


def _dot(a, b):
    return jnp.dot(a, b, preferred_element_type=F32)


def _dot_nt(a, b):
    return lax.dot_general(a, b, (((1,), (1,)), ((), ())), preferred_element_type=F32)


def _cparams(sem):
    return pltpu.CompilerParams(dimension_semantics=sem, vmem_limit_bytes=VMEM_LIMIT)


def _rms_to_bf16(x, g):
    ms = jnp.mean(x * x, axis=-1, keepdims=True)
    return (x * lax.rsqrt(ms + NORM_EPS) * g).astype(BF16)


def _norm_matmul_kernel(x_ref, g_ref, w_ref, o_ref, hn_ref):
    @pl.when(pl.program_id(1) == 0)
    def _():
        hn_ref[...] = _rms_to_bf16(x_ref[...], g_ref[...])

    o_ref[...] = _dot(hn_ref[...], w_ref[...]).astype(o_ref.dtype)


def _norm_matmul(name, x, x_col_block, k, g, w, out_dtype, tm, tn):
    m = x.shape[0]
    n = w.shape[1]
    return pl.pallas_call(
        _norm_matmul_kernel,
        name=name,
        grid=(m // tm, n // tn),
        in_specs=[
            pl.BlockSpec((tm, k), lambda i, j: (i, x_col_block)),
            pl.BlockSpec((1, k), lambda i, j: (0, 0)),
            pl.BlockSpec((k, tn), lambda i, j: (0, j)),
        ],
        out_specs=pl.BlockSpec((tm, tn), lambda i, j: (i, j)),
        out_shape=jax.ShapeDtypeStruct((m, n), out_dtype),
        scratch_shapes=[pltpu.VMEM((tm, k), BF16)],
        compiler_params=_cparams(("parallel", "arbitrary")),
    )(x, g, w)


def _ffn_up_kernel(x_ref, g_ref, wg_ref, wu_ref, o_ref, hn_ref):
    @pl.when(pl.program_id(1) == 0)
    def _():
        hn_ref[...] = _rms_to_bf16(x_ref[...], g_ref[...])

    h = hn_ref[...]
    a = _dot(h, wg_ref[...])
    b = _dot(h, wu_ref[...])
    o_ref[...] = (a * jax.nn.sigmoid(a) * b).astype(o_ref.dtype)


def _ffn_up(x, g, wg, wu, tm, tn):
    m, k = x.shape
    n = wg.shape[1]
    return pl.pallas_call(
        _ffn_up_kernel,
        name="ffn_up",
        grid=(m // tm, n // tn),
        in_specs=[
            pl.BlockSpec((tm, k), lambda i, j: (i, 0)),
            pl.BlockSpec((1, k), lambda i, j: (0, 0)),
            pl.BlockSpec((k, tn), lambda i, j: (0, j)),
            pl.BlockSpec((k, tn), lambda i, j: (0, j)),
        ],
        out_specs=pl.BlockSpec((tm, tn), lambda i, j: (i, j)),
        out_shape=jax.ShapeDtypeStruct((m, n), BF16),
        scratch_shapes=[pltpu.VMEM((tm, k), BF16)],
        compiler_params=_cparams(("parallel", "arbitrary")),
    )(x, g, wg, wu)


def _matmul_res_kernel(a_ref, w_ref, r_ref, o_ref):
    o_ref[...] = r_ref[...] + _dot(a_ref[...], w_ref[...])


def _matmul_res(name, a, w, res, tm, tn):
    m, k = a.shape
    n = w.shape[1]
    return pl.pallas_call(
        _matmul_res_kernel,
        name=name,
        grid=(n // tn, m // tm),
        in_specs=[
            pl.BlockSpec((tm, k), lambda j, i: (i, 0)),
            pl.BlockSpec((k, tn), lambda j, i: (0, j)),
            pl.BlockSpec((tm, tn), lambda j, i: (i, j)),
        ],
        out_specs=pl.BlockSpec((tm, tn), lambda j, i: (i, j)),
        out_shape=jax.ShapeDtypeStruct((m, n), F32),
        compiler_params=_cparams(("parallel", "parallel")),
    )(a, w, res)


def _merge_kernel(ya_ref, yb_ref, wa_ref, wb_ref, ga_ref, gb_ref, o_ref):
    a = _dot(ya_ref[...], wa_ref[...])
    b = _dot(yb_ref[...], wb_ref[...])
    o_ref[...] = (jax.nn.sigmoid(ga_ref[...]) * a + jax.nn.sigmoid(gb_ref[...]) * b).astype(o_ref.dtype)


def _merge(ya, yb, wa, wb, proj, tm, tn):
    m, k = ya.shape
    n = wa.shape[1]
    ga0 = COL_GATE_A // tn
    gb0 = COL_GATE_B // tn
    return pl.pallas_call(
        _merge_kernel,
        name="merge",
        grid=(m // tm, n // tn),
        in_specs=[
            pl.BlockSpec((tm, k), lambda i, j: (i, 0)),
            pl.BlockSpec((tm, k), lambda i, j: (i, 0)),
            pl.BlockSpec((k, tn), lambda i, j: (0, j)),
            pl.BlockSpec((k, tn), lambda i, j: (0, j)),
            pl.BlockSpec((tm, tn), lambda i, j: (i, ga0 + j)),
            pl.BlockSpec((tm, tn), lambda i, j: (i, gb0 + j)),
        ],
        out_specs=pl.BlockSpec((tm, tn), lambda i, j: (i, j)),
        out_shape=jax.ShapeDtypeStruct((m, n), BF16),
        compiler_params=_cparams(("parallel", "parallel")),
    )(ya, yb, wa, wb, proj, proj)


def _ple_kernel(x_ref, g_ref, wg_ref, p_ref, wp_ref, xr_ref, o_ref, hn_ref):
    @pl.when(pl.program_id(1) == 0)
    def _():
        hn_ref[...] = _rms_to_bf16(x_ref[...], g_ref[...])

    gate = jax.nn.sigmoid(_dot(hn_ref[...], wg_ref[...]))
    emb = _dot(p_ref[...].astype(BF16), wp_ref[...])
    o_ref[...] = xr_ref[...] + gate * emb


def _ple(x, g, wg, p, wp, tm, tn):
    m, k = x.shape
    n = wg.shape[1]
    kp = p.shape[1]
    return pl.pallas_call(
        _ple_kernel,
        name="ple",
        grid=(m // tm, n // tn),
        in_specs=[
            pl.BlockSpec((tm, k), lambda i, j: (i, 0)),
            pl.BlockSpec((1, k), lambda i, j: (0, 0)),
            pl.BlockSpec((k, tn), lambda i, j: (0, j)),
            pl.BlockSpec((tm, kp), lambda i, j: (i, 0)),
            pl.BlockSpec((kp, tn), lambda i, j: (0, j)),
            pl.BlockSpec((tm, tn), lambda i, j: (i, j)),
        ],
        out_specs=pl.BlockSpec((tm, tn), lambda i, j: (i, j)),
        out_shape=jax.ShapeDtypeStruct((m, n), F32),
        scratch_shapes=[pltpu.VMEM((tm, k), BF16)],
        compiler_params=_cparams(("parallel", "arbitrary")),
    )(x, g, wg, p, wp, x)


def _rope64_pairs(x, c64, s64, first_half):
    partner = jnp.where(first_half, pltpu.roll(x, 96, 1), pltpu.roll(x, 32, 1))
    return x * c64 + partner * s64


def _prep_a_kernel(qa_ref, ka_ref, va_ref, qi_ref, kiw_ref, c128_ref, s128_ref, c64_ref, s64_ref,
                   gq_ref, gk_ref, qa_o, ka_o, va_o, qi_o, klo_o, khi_o, wi_o, *, wi_scale):
    c128 = c128_ref[...]
    s128 = s128_ref[...]

    def norm_rope(x, g):
        ms = jnp.mean(x * x, axis=-1, keepdims=True)
        y = x * lax.rsqrt(ms + NORM_EPS) * g
        return y * c128 + pltpu.roll(y, 64, 1) * s128

    for h in range(A_HEADS):
        hs = slice(h * LANES, (h + 1) * LANES)
        qa_o[:, hs] = norm_rope(qa_ref[:, hs], gq_ref[...]).astype(BF16)
    for h in range(A_KV_HEADS):
        hs = slice(h * LANES, (h + 1) * LANES)
        ka_o[:, hs] = norm_rope(ka_ref[:, hs], gk_ref[...]).astype(BF16)
    va_o[...] = va_ref[...].astype(BF16)

    c64 = c64_ref[...]
    s64 = s64_ref[...]
    lane = lax.broadcasted_iota(jnp.int32, c64.shape, 1)
    first_half = (lane % 64) < 32
    for p in range(IDX_HEADS * IDX_DIM // LANES):
        ps = slice(p * LANES, (p + 1) * LANES)
        qi_o[:, ps] = _rope64_pairs(qi_ref[:, ps], c64, s64, first_half).astype(BF16)
    kiw = kiw_ref[...]
    ki = jnp.where(lane < 64, _rope64_pairs(kiw, c64, s64, first_half), 0.0)
    klo_o[...] = ki.astype(BF16)
    khi_o[...] = pltpu.roll(ki, 64, 1).astype(BF16)
    wi_o[...] = kiw * wi_scale


def _prep_a(proj, c128, s128, c64, s64, gq, gk, tp):
    m = proj.shape[0]
    row = lambda w, cb: pl.BlockSpec((tp, w), lambda i: (i, cb))
    vec = pl.BlockSpec((1, LANES), lambda i: (0, 0))
    out = lambda w: pl.BlockSpec((tp, w), lambda i: (i, 0))
    wi_scale = (IDX_HEADS ** -0.5) * (IDX_DIM ** -0.5)
    return pl.pallas_call(
        functools.partial(_prep_a_kernel, wi_scale=wi_scale),
        name="prep_a",
        grid=(m // tp,),
        in_specs=[
            row(2048, COL_QA // 2048), row(512, COL_KA // 512), row(512, COL_VA // 512),
            row(1024, COL_QI // 1024), row(LANES, COL_KIW // LANES),
            out(LANES), out(LANES), out(LANES), out(LANES), vec, vec,
        ],
        out_specs=[out(2048), out(512), out(512), out(1024), out(LANES), out(LANES), out(LANES)],
        out_shape=[
            jax.ShapeDtypeStruct((m, 2048), BF16), jax.ShapeDtypeStruct((m, 512), BF16),
            jax.ShapeDtypeStruct((m, 512), BF16), jax.ShapeDtypeStruct((m, 1024), BF16),
            jax.ShapeDtypeStruct((m, LANES), BF16), jax.ShapeDtypeStruct((m, LANES), BF16),
            jax.ShapeDtypeStruct((m, LANES), F32),
        ],
        compiler_params=_cparams(("parallel",)),
    )(proj, proj, proj, proj, proj, c128, s128, c64, s64, gq, gk)


def _prep_b_kernel(qb_ref, kv_ref, kr_ref, c64_ref, s64_ref, gqn_ref, gqp_ref, gkn_ref, gkp_ref, shift_ref,
                   q_o, k_o, v_o, *, q_scale):
    c64 = c64_ref[...]
    s64 = s64_ref[...]
    lane = lax.broadcasted_iota(jnp.int32, c64.shape, 1)
    first_half = (lane % 64) < 32
    lo = lane < 64
    kr = jnp.where(lo, kr_ref[...], 0.0)
    kr_ss = jnp.sum(kr * kr, axis=-1, keepdims=True)
    kpe_lo = jnp.where(lo, _rope64_pairs(kr * gkp_ref[...], c64, s64, first_half), 0.0)
    kpe_hi = pltpu.roll(kpe_lo, 64, 1)
    nope_w = B_HEADS * B_NOPE
    for h in range(B_HEADS):
        hs = slice(h * LANES, (h + 1) * LANES)
        pair = h // 2
        mine = lo if h % 2 == 0 else jnp.logical_not(lo)
        shift_lane = lane == (64 if h % 2 == 0 else 0)
        qn = qb_ref[:, hs]
        qp = jnp.where(mine, qb_ref[:, nope_w + pair * LANES:nope_w + (pair + 1) * LANES], 0.0)
        ss = jnp.sum(qn * qn, axis=-1, keepdims=True) + jnp.sum(qp * qp, axis=-1, keepdims=True)
        r = lax.rsqrt(ss * (1.0 / B_QK) + NORM_EPS) * q_scale
        q_o[:, 2 * h * LANES:(2 * h + 1) * LANES] = (qn * r * gqn_ref[...]).astype(BF16)
        qpe = _rope64_pairs(qp * gqp_ref[...], c64, s64, first_half) * r
        qpe = jnp.where(mine, qpe, jnp.where(shift_lane, 1.0, 0.0))
        q_o[:, (2 * h + 1) * LANES:(2 * h + 2) * LANES] = qpe.astype(BF16)
        kn = kv_ref[:, hs]
        rk = lax.rsqrt((jnp.sum(kn * kn, axis=-1, keepdims=True) + kr_ss) * (1.0 / B_QK) + NORM_EPS)
        k_o[:, 2 * h * LANES:(2 * h + 1) * LANES] = (kn * rk * gkn_ref[...]).astype(BF16)
        kpe = kpe_lo if h % 2 == 0 else kpe_hi
        k_o[:, (2 * h + 1) * LANES:(2 * h + 2) * LANES] = jnp.where(
            shift_lane, -shift_ref[...], kpe * rk).astype(BF16)
        v_o[:, hs] = kv_ref[:, nope_w + h * LANES:nope_w + (h + 1) * LANES].astype(BF16)


def _prep_b(qb, kv, proj, c64, s64, gqn, gqp, gkn, gkp, shift, tp):
    m = qb.shape[0]
    vec = pl.BlockSpec((1, LANES), lambda i: (0, 0))
    out = lambda w: pl.BlockSpec((tp, w), lambda i: (i, 0))
    return pl.pallas_call(
        functools.partial(_prep_b_kernel, q_scale=B_QK ** -0.5 * LOG2E),
        name="prep_b",
        grid=(m // tp,),
        in_specs=[
            out(qb.shape[1]), out(kv.shape[1]),
            pl.BlockSpec((tp, LANES), lambda i: (i, COL_KR // LANES)),
            out(LANES), out(LANES), vec, vec, vec, vec, vec,
        ],
        out_specs=[out(B_HEADS * 256), out(B_HEADS * 256), out(B_HEADS * B_V)],
        out_shape=[
            jax.ShapeDtypeStruct((m, B_HEADS * 256), BF16),
            jax.ShapeDtypeStruct((m, B_HEADS * 256), BF16),
            jax.ShapeDtypeStruct((m, B_HEADS * B_V), BF16),
        ],
        compiler_params=_cparams(("parallel",)),
    )(qb, kv, proj, c64, s64, gqn, gqp, gkn, gkp, shift)


def _indexer_kernel(q_ref, klo_ref, khi_ref, w_ref, sc_ref, th_ref, *, tq, tk, seq, k_sel):
    qi = pl.program_id(1)
    n_vis = (qi * tq) // tk + 1
    n_all = seq // tk
    row = qi * tq + lax.broadcasted_iota(jnp.int32, (tq, 1), 0)
    wcols = [w_ref[0, :, IDX_DIM + h:IDX_DIM + h + 1] for h in range(IDX_HEADS)]

    def gen(c, carry):
        rmax, rmin = carry
        off = pl.multiple_of(c * tk, tk)
        kl = klo_ref[0, pl.ds(off, tk), :]
        kh = khi_ref[0, pl.ds(off, tk), :]
        acc = jnp.zeros((tq, tk), F32)
        for p in range(IDX_HEADS // 2):
            qp = q_ref[0, :, p * LANES:(p + 1) * LANES]
            acc = acc + wcols[2 * p] * jnp.maximum(_dot_nt(qp, kl), 0.0)
            acc = acc + wcols[2 * p + 1] * jnp.maximum(_dot_nt(qp, kh), 0.0)
        col = off + lax.broadcasted_iota(jnp.int32, (tq, tk), 1)
        vis = col <= row
        sc_ref[0, :, pl.ds(off, tk)] = jnp.where(vis, acc, NEG)
        rmax = jnp.maximum(rmax, jnp.max(jnp.where(vis, acc, NEG), axis=-1, keepdims=True))
        rmin = jnp.minimum(rmin, jnp.min(jnp.where(vis, acc, POS_BIG), axis=-1, keepdims=True))
        return rmax, rmin

    rmax, rmin = lax.fori_loop(0, n_vis, gen,
                               (jnp.full((tq, 1), NEG, F32), jnp.full((tq, 1), POS_BIG, F32)))

    def fill(c, carry):
        sc_ref[0, :, pl.ds(pl.multiple_of(c * tk, tk), tk)] = jnp.full((tq, tk), NEG, F32)
        return carry

    lax.fori_loop(n_vis, n_all, fill, 0)

    cw = min(2 * tk, seq)
    n_cnt = (n_vis * tk + cw - 1) // cw

    def count_ge(mid, strict=False):
        accs = []
        for r in range(tq // CNT_ROWS):
            rs = slice(r * CNT_ROWS, (r + 1) * CNT_ROWS)
            midb = mid[rs]

            def body(c, acc, rs=rs, midb=midb):
                off = pl.multiple_of(c * cw, cw)
                for j in range(cw // LANES):
                    blk = sc_ref[0, rs, pl.ds(off + j * LANES, LANES)]
                    acc = acc + jnp.where(blk > midb if strict else blk >= midb, 1.0, 0.0)
                return acc

            accs.append(lax.fori_loop(0, n_cnt, body, jnp.zeros((CNT_ROWS, LANES), F32)))
        return jnp.concatenate(
            [jnp.broadcast_to(jnp.sum(a, axis=-1, keepdims=True), (CNT_ROWS, LANES)) for a in accs], axis=0)

    rowb = qi * tq + lax.broadcasted_iota(jnp.int32, (tq, LANES), 0)
    active0 = (rowb + 1) > k_sel
    lo0 = jnp.where(active0, jnp.broadcast_to(rmin, (tq, LANES)), TH_ALL)
    hi0 = jnp.where(active0, jnp.broadcast_to(rmax, (tq, LANES)), TH_ALL)

    def n_open(lo, hi):
        return jnp.sum(jnp.where(lo < hi, 1.0, 0.0))

    def cond(st):
        it, nopen = st[0], st[1]
        return jnp.logical_and(nopen > 0.0, it < MAX_SEARCH_STEPS)

    def body(st):
        it, _, lo, hi, cnt_lo = st
        mid = lo + (hi - lo) * 0.5
        cnt = count_ge(mid)
        nopen = n_open(lo, hi)
        ge = cnt >= k_sel
        lo_n = jnp.where(ge, mid, lo)
        hi_n = jnp.where(ge, hi, mid)
        hi_n = jnp.where(cnt == k_sel, lo_n, hi_n)
        hi_n = jnp.where(mid >= hi, lo_n, hi_n)
        hi_n = jnp.where(mid <= lo, lo_n, hi_n)
        return it + 1, nopen, lo_n, hi_n, jnp.where(ge, cnt, cnt_lo)

    st = lax.while_loop(cond, body, (jnp.int32(0), n_open(lo0, hi0), lo0, hi0,
                                     (rowb + 1).astype(F32)))
    th = st[2]
    th_ref[0] = th

    tied = jnp.logical_and(active0, st[4] > k_sel)

    @pl.when(jnp.sum(jnp.where(tied, 1.0, 0.0)) > 0.0)
    def _():
        keep_ties = k_sel - count_ge(th, strict=True)
        reps = tk // LANES
        th_w = jnp.concatenate([th] * reps, axis=1)
        keep_w = jnp.concatenate([keep_ties] * reps, axis=1)
        earlier = (lax.broadcasted_iota(jnp.int32, (tk, tk), 0)
                   < lax.broadcasted_iota(jnp.int32, (tk, tk), 1))
        tri = jnp.where(earlier, 1.0, 0.0).astype(BF16)

        def demote(c, seen):
            off = pl.multiple_of(c * tk, tk)
            blk = sc_ref[0, :, pl.ds(off, tk)]
            eq = blk == th_w
            eqf = jnp.where(eq, 1.0, 0.0)
            rank = _dot(eqf.astype(BF16), tri) + jnp.concatenate([seen] * reps, axis=1)
            sc_ref[0, :, pl.ds(off, tk)] = jnp.where(jnp.logical_and(eq, rank >= keep_w), NEG, blk)
            return seen + jnp.sum(eqf, axis=-1, keepdims=True)

        lax.fori_loop(0, n_vis, demote, jnp.zeros((tq, LANES), F32))


def _indexer(qi_r, klo, khi, wi, tq, tk, k_sel):
    b, s, _ = qi_r.shape
    return pl.pallas_call(
        functools.partial(_indexer_kernel, tq=tq, tk=tk, seq=s, k_sel=float(k_sel)),
        name="indexer",
        grid=(b, s // tq),
        in_specs=[
            pl.BlockSpec((1, tq, IDX_HEADS * IDX_DIM), lambda bi, qi: (bi, qi, 0)),
            pl.BlockSpec((1, s, LANES), lambda bi, qi: (bi, 0, 0)),
            pl.BlockSpec((1, s, LANES), lambda bi, qi: (bi, 0, 0)),
            pl.BlockSpec((1, tq, LANES), lambda bi, qi: (bi, qi, 0)),
        ],
        out_specs=[
            pl.BlockSpec((1, tq, s), lambda bi, qi: (bi, qi, 0)),
            pl.BlockSpec((1, tq, LANES), lambda bi, qi: (bi, qi, 0)),
        ],
        out_shape=[jax.ShapeDtypeStruct((b, s, s), F32), jax.ShapeDtypeStruct((b, s, LANES), F32)],
        compiler_params=_cparams(("parallel", "parallel")),
    )(qi_r, klo, khi, wi)


def _causal_pairs(s, tq, tk):
    qs, ks = [], []
    for qi in range(s // tq):
        for ki in range((qi * tq + tq - 1) // tk + 1):
            qs.append(qi)
            ks.append(ki)
    return jnp.asarray(np.array(qs, np.int32)), jnp.asarray(np.array(ks, np.int32))


def _softmax_step(s, v, rows, m_ref, l_ref, acc_ref, fixed_shift):
    blocks = [s[:, j * LANES:(j + 1) * LANES] for j in range(s.shape[1] // LANES)]
    if fixed_shift:
        ps = [jnp.exp2(blk) for blk in blocks]
        l_ref[rows] = l_ref[rows] + functools.reduce(jnp.add, ps)
        p = jnp.concatenate([blk.astype(BF16) for blk in ps], axis=1)
        acc_ref[rows] = acc_ref[rows] + _dot(p, v)
        return
    mx = functools.reduce(jnp.maximum, blocks)
    m_prev = m_ref[rows]
    m_new = jnp.maximum(m_prev, jnp.max(mx, axis=-1, keepdims=True))
    alpha = jnp.exp2(m_prev - m_new)
    ps = [jnp.exp2(blk - m_new) for blk in blocks]
    l_ref[rows] = alpha * l_ref[rows] + jnp.sum(functools.reduce(jnp.add, ps), axis=-1, keepdims=True)
    p = jnp.concatenate([blk.astype(BF16) for blk in ps], axis=1)
    acc_ref[rows] = alpha * acc_ref[rows] + _dot(p, v)
    m_ref[rows] = m_new


def _softmax_finish(rows, l_ref, acc_ref, fixed_shift):
    l = l_ref[rows]
    if fixed_shift:
        l = jnp.sum(l, axis=-1, keepdims=True)
    return acc_ref[rows] / l


def _dsa_kernel(qt_ref, kt_ref, q_ref, k_ref, v_ref, sc_ref, th_ref, shift_ref, o_ref, m_ref, l_ref,
                acc_ref, bias_ref, qs_ref, *, tq, tk, fixed_shift):
    step = pl.program_id(1)
    qi = qt_ref[step]
    ki = kt_ref[step]

    @pl.when(ki == 0)
    def _():
        m_ref[...] = jnp.full(m_ref.shape, M_INIT, F32)
        l_ref[...] = jnp.zeros(l_ref.shape, F32)
        acc_ref[...] = jnp.zeros(acc_ref.shape, F32)
        for h in range(A_HEADS):
            qs_ref[h * tq:(h + 1) * tq, :] = q_ref[0, :, h * LANES:(h + 1) * LANES]

    bias_ref[...] = jnp.where(sc_ref[0] >= th_ref[0][:, :1], -shift_ref[:, :1], NEG)
    grp = A_HEADS // A_KV_HEADS
    for g in range(A_KV_HEADS):
        rows = slice(g * grp * tq, (g + 1) * grp * tq)
        k = k_ref[0, :, g * LANES:(g + 1) * LANES]
        v = v_ref[0, :, g * LANES:(g + 1) * LANES]
        s = _dot_nt(qs_ref[rows, :], k)
        s = jnp.concatenate([s[j * tq:(j + 1) * tq] + bias_ref[...] for j in range(grp)], axis=0)
        _softmax_step(s, v, rows, m_ref, l_ref, acc_ref, fixed_shift)

    @pl.when(ki == (qi * tq + tq - 1) // tk)
    def _():
        for h in range(A_HEADS):
            out = _softmax_finish(slice(h * tq, (h + 1) * tq), l_ref, acc_ref, fixed_shift)
            o_ref[0, :, h * LANES:(h + 1) * LANES] = out.astype(o_ref.dtype)


def _dsa_attn(q, k, v, scores, th, shift, tq, tk, fixed_shift):
    b, s, dq = q.shape
    qt, kt = _causal_pairs(s, tq, tk)
    grid_spec = pltpu.PrefetchScalarGridSpec(
        num_scalar_prefetch=2,
        grid=(b, qt.shape[0]),
        in_specs=[
            pl.BlockSpec((1, tq, dq), lambda bi, p, qt, kt: (bi, qt[p], 0)),
            pl.BlockSpec((1, tk, k.shape[2]), lambda bi, p, qt, kt: (bi, kt[p], 0)),
            pl.BlockSpec((1, tk, v.shape[2]), lambda bi, p, qt, kt: (bi, kt[p], 0)),
            pl.BlockSpec((1, tq, tk), lambda bi, p, qt, kt: (bi, qt[p], kt[p])),
            pl.BlockSpec((1, tq, LANES), lambda bi, p, qt, kt: (bi, qt[p], 0)),
            pl.BlockSpec((1, LANES), lambda bi, p, qt, kt: (0, 0)),
        ],
        out_specs=pl.BlockSpec((1, tq, dq), lambda bi, p, qt, kt: (bi, qt[p], 0)),
        scratch_shapes=[
            pltpu.VMEM((A_HEADS * tq, LANES), F32),
            pltpu.VMEM((A_HEADS * tq, LANES), F32),
            pltpu.VMEM((A_HEADS * tq, A_HEAD_DIM), F32),
            pltpu.VMEM((tq, tk), F32),
            pltpu.VMEM((A_HEADS * tq, A_HEAD_DIM), BF16),
        ],
    )
    return pl.pallas_call(
        functools.partial(_dsa_kernel, tq=tq, tk=tk, fixed_shift=fixed_shift),
        name="dsa_attn_fixed" if fixed_shift else "dsa_attn_online",
        grid_spec=grid_spec,
        out_shape=jax.ShapeDtypeStruct((b, s, dq), BF16),
        compiler_params=_cparams(("parallel", "arbitrary")),
    )(qt, kt, q, k, v, scores, th, shift)


def _mla_kernel(qt_ref, kt_ref, q_ref, k_ref, v_ref, o_ref, m_ref, l_ref, acc_ref, *, tq, tk, hg,
                fixed_shift):
    step = pl.program_id(2)
    qi = qt_ref[step]
    ki = kt_ref[step]

    @pl.when(ki == 0)
    def _():
        m_ref[...] = jnp.full(m_ref.shape, M_INIT, F32)
        l_ref[...] = jnp.zeros(l_ref.shape, F32)
        acc_ref[...] = jnp.zeros(acc_ref.shape, F32)

    def heads(masked):
        if masked:
            keep = (lax.broadcasted_iota(jnp.int32, (tq, tk), 1)
                    <= lax.broadcasted_iota(jnp.int32, (tq, tk), 0))
        for h in range(hg):
            q = q_ref[0, :, h * 256:(h + 1) * 256]
            k = k_ref[0, :, h * 256:(h + 1) * 256]
            v = v_ref[0, :, h * B_V:(h + 1) * B_V]
            s = _dot_nt(q, k)
            if masked:
                s = jnp.where(keep, s, NEG)
            _softmax_step(s, v, slice(h * tq, (h + 1) * tq), m_ref, l_ref, acc_ref, fixed_shift)

    @pl.when(ki < qi)
    def _():
        heads(False)

    @pl.when(ki == qi)
    def _():
        heads(True)
        for h in range(hg):
            out = _softmax_finish(slice(h * tq, (h + 1) * tq), l_ref, acc_ref, fixed_shift)
            o_ref[0, :, h * B_V:(h + 1) * B_V] = out.astype(o_ref.dtype)


def _mla_attn(q, k, v, tq, tk, hg, fixed_shift):
    b, s, _ = q.shape
    assert tq == tk
    qt, kt = _causal_pairs(s, tq, tk)
    grid_spec = pltpu.PrefetchScalarGridSpec(
        num_scalar_prefetch=2,
        grid=(b, B_HEADS // hg, qt.shape[0]),
        in_specs=[
            pl.BlockSpec((1, tq, hg * 256), lambda bi, g, p, qt, kt: (bi, qt[p], g)),
            pl.BlockSpec((1, tk, hg * 256), lambda bi, g, p, qt, kt: (bi, kt[p], g)),
            pl.BlockSpec((1, tk, hg * B_V), lambda bi, g, p, qt, kt: (bi, kt[p], g)),
        ],
        out_specs=pl.BlockSpec((1, tq, hg * B_V), lambda bi, g, p, qt, kt: (bi, qt[p], g)),
        scratch_shapes=[
            pltpu.VMEM((hg * tq, LANES), F32),
            pltpu.VMEM((hg * tq, LANES), F32),
            pltpu.VMEM((hg * tq, B_V), F32),
        ],
    )
    return pl.pallas_call(
        functools.partial(_mla_kernel, tq=tq, tk=tk, hg=hg, fixed_shift=fixed_shift),
        name="mla_attn_fixed" if fixed_shift else "mla_attn_online",
        grid_spec=grid_spec,
        out_shape=jax.ShapeDtypeStruct((b, s, B_HEADS * B_V), BF16),
        compiler_params=_cparams(("parallel", "parallel", "arbitrary")),
    )(qt, kt, q, k, v)


def _regroup_w_in(w):
    sizes = (A_HEADS * A_HEAD_DIM, A_KV_HEADS * A_HEAD_DIM, A_KV_HEADS * A_HEAD_DIM,
             IDX_HEADS * IDX_DIM, IDX_DIM, IDX_HEADS, B_Q_LORA, B_KV_LORA, B_ROPE)
    offs = np.cumsum((0,) + sizes)
    w = w.astype(BF16)
    qa, ka, va, qi, ki, wi, cq, ckv, kr = (w[:, offs[j]:offs[j + 1]] for j in range(len(sizes)))
    gates = w[:, offs[-1]:]
    d = w.shape[0]
    kiw = jnp.concatenate([ki, wi, jnp.zeros((d, LANES - IDX_DIM - IDX_HEADS), w.dtype)], axis=1)
    krp = jnp.concatenate([kr, jnp.zeros((d, LANES - B_ROPE), w.dtype)], axis=1)
    return jnp.concatenate([qa, gates, qi, ka, va, cq, ckv, kiw, krp], axis=1).astype(BF16)


def _softmax_shift(d, g_q, g_k):
    bound = (d ** 0.5) * LOG2E * 1.02 * jnp.max(jnp.abs(g_q)) * jnp.max(jnp.abs(g_k))
    c = jnp.ceil(bound.astype(F32))
    ok = c <= MAX_FIXED_SHIFT
    return jnp.full((1, LANES), jnp.where(ok, c, 0.0), F32), ok


def _rope_tables(pos, d):
    half = d // 2
    freqs = jnp.power(jnp.float32(ROPE_THETA), -jnp.arange(half, dtype=F32) * (2.0 / d))
    ang = pos.astype(F32)[..., None] * freqs
    cos, sin = jnp.cos(ang), jnp.sin(ang)
    reps = LANES // d
    c = jnp.tile(jnp.concatenate([cos, cos], axis=-1), (1, 1, reps))
    s = jnp.tile(jnp.concatenate([-sin, sin], axis=-1), (1, 1, reps))
    return c.reshape(-1, LANES), s.reshape(-1, LANES)


def _layer(x, p, positions, g_mix_norm, w_in, g_qa, g_ka, g_cq, w_uq, g_ckv, w_ukv, g_qb, g_kb,
           w_out_a, w_out_b, w_o, g_ffn_norm, w_ffn_gate, w_ffn_up, w_ffn_down, g_ple_norm,
           w_ple_gate, w_ple_proj):
    b, s, d = x.shape
    m = b * s
    k_sel = min(TOPK_MAX, s // 4)
    xf = x.reshape(m, d)
    row = lambda g: g.reshape(1, -1).astype(F32)

    c128, s128 = _rope_tables(positions, A_HEAD_DIM)
    c64, s64 = _rope_tables(positions, IDX_DIM)

    proj = _norm_matmul("in_proj", xf, 0, d, row(g_mix_norm), _regroup_w_in(w_in), F32, tm=1024, tn=512)

    qa_r, ka_r, va_b, qi_r, klo, khi, wi = _prep_a(
        proj, c128, s128, c64, s64, row(g_qa) * (A_HEAD_DIM ** -0.5 * LOG2E), row(g_ka), tp=512)
    r3 = lambda a: a.reshape(b, s, a.shape[-1])
    scores, th = _indexer(r3(qi_r), r3(klo), r3(khi), r3(wi), tq=512, tk=512, k_sel=k_sel)
    shift_a, fixed_a = _softmax_shift(A_HEAD_DIM, g_qa, g_ka)
    y_a = lax.cond(
        fixed_a,
        functools.partial(_dsa_attn, tq=512, tk=512, fixed_shift=True),
        functools.partial(_dsa_attn, tq=512, tk=512, fixed_shift=False),
        r3(qa_r), r3(ka_r), r3(va_b), scores, th, shift_a)

    w_uq_r = w_uq.reshape(B_Q_LORA, B_HEADS, B_QK)
    w_uq_r = jnp.concatenate([w_uq_r[:, :, :B_NOPE].reshape(B_Q_LORA, -1),
                              w_uq_r[:, :, B_NOPE:].reshape(B_Q_LORA, -1)], axis=1).astype(BF16)
    w_ukv_r = w_ukv.reshape(B_KV_LORA, B_HEADS, 2, B_NOPE)
    w_ukv_r = jnp.concatenate([w_ukv_r[:, :, 0, :].reshape(B_KV_LORA, -1),
                               w_ukv_r[:, :, 1, :].reshape(B_KV_LORA, -1)], axis=1).astype(BF16)
    qb = _norm_matmul("q_up", proj, COL_CQ // B_Q_LORA, B_Q_LORA, row(g_cq), w_uq_r, F32, tm=1024, tn=1024)
    kv = _norm_matmul("kv_up", proj, COL_CKV // B_KV_LORA, B_KV_LORA, row(g_ckv), w_ukv_r, F32, tm=1024, tn=1024)
    pe2 = lambda g: jnp.tile(g[B_NOPE:], 2).reshape(1, LANES).astype(F32)
    shift_b, fixed_b = _softmax_shift(B_QK, g_qb, g_kb)
    q_m, k_m, v_m = _prep_b(qb, kv, proj, c64, s64, row(g_qb[:B_NOPE]), pe2(g_qb),
                            row(g_kb[:B_NOPE]), pe2(g_kb), shift_b, tp=256)
    y_b = lax.cond(
        fixed_b,
        functools.partial(_mla_attn, tq=512, tk=512, hg=8, fixed_shift=True),
        functools.partial(_mla_attn, tq=512, tk=512, hg=8, fixed_shift=False),
        r3(q_m), r3(k_m), r3(v_m))

    mixed = _merge(y_a.reshape(m, -1), y_b.reshape(m, -1), w_out_a.astype(BF16), w_out_b.astype(BF16),
                   proj, tm=1024, tn=512)
    x1 = _matmul_res("w_o", mixed, w_o.astype(BF16), xf, tm=1024, tn=1024)

    act = _ffn_up(x1, row(g_ffn_norm), w_ffn_gate.astype(BF16), w_ffn_up.astype(BF16), tm=1024, tn=512)
    x2 = _matmul_res("ffn_down", act, w_ffn_down.astype(BF16), x1, tm=512, tn=1024)

    out = _ple(x2, row(g_ple_norm), w_ple_gate.astype(BF16), p.reshape(m, -1),
               w_ple_proj.astype(BF16), tm=1024, tn=512)
    return out.reshape(b, s, d)


def kernel(x, p, positions, g_mix_norm, w_in, g_qa, g_ka, g_cq, w_uq, g_ckv, w_ukv, g_qb, g_kb,
           w_out_a, w_out_b, w_o, g_ffn_norm, w_ffn_gate, w_ffn_up, w_ffn_down, g_ple_norm,
           w_ple_gate, w_ple_proj):
    for i in range(p.shape[0]):
        x = _layer(x, p[i], positions, g_mix_norm[i], w_in[i], g_qa[i], g_ka[i], g_cq[i], w_uq[i],
                   g_ckv[i], w_ukv[i], g_qb[i], g_kb[i], w_out_a[i], w_out_b[i], w_o[i],
                   g_ffn_norm[i], w_ffn_gate[i], w_ffn_up[i], w_ffn_down[i], g_ple_norm[i],
                   w_ple_gate[i], w_ple_proj[i])
    return x
```

```python
import functools

import numpy as np
import jax
import jax.numpy as jnp
from jax import lax
from jax.experimental import pallas as pl
from jax.experimental.pallas import tpu as pltpu

F32 = jnp.float32
BF16 = jnp.bfloat16

ROPE_THETA = 10000.0
NORM_EPS = 1e-6
NEG = -1e30
M_INIT = -1e29
POS_BIG = 3e38
TH_ALL = 0.5 * NEG
LOG2E = 1.4426950408889634
MAX_SEARCH_STEPS = 512
CNT_ROWS = 128
MAX_FIXED_SHIFT = 60.0
A_HEADS = 16
A_KV_HEADS = 4
A_HEAD_DIM = 128
IDX_HEADS = 16
IDX_DIM = 64
TOPK_MAX = 256
B_HEADS = 16
B_Q_LORA = 512
B_KV_LORA = 256
B_NOPE = 128
B_ROPE = 64
B_V = 128
B_QK = B_NOPE + B_ROPE
LANES = 128
VMEM_LIMIT = 56 * 1024 * 1024

COL_QA = 0
COL_GATE_A = 2048
COL_GATE_B = 4096
COL_QI = 6144
COL_KA = 7168
COL_VA = 7680
COL_CQ = 8192
COL_CKV = 8704
COL_KIW = 8960
COL_KR = 9088
N_PROJ = 9216


def _dot(a, b):
    return jnp.dot(a, b, preferred_element_type=F32)


def _dot_nt(a, b):
    return lax.dot_general(a, b, (((1,), (1,)), ((), ())), preferred_element_type=F32)


def _cparams(sem):
    return pltpu.CompilerParams(dimension_semantics=sem, vmem_limit_bytes=VMEM_LIMIT)


def _rms_to_bf16(x, g):
    ms = jnp.mean(x * x, axis=-1, keepdims=True)
    return (x * lax.rsqrt(ms + NORM_EPS) * g).astype(BF16)


def _norm_matmul_kernel(x_ref, g_ref, w_ref, o_ref, hn_ref):
    @pl.when(pl.program_id(1) == 0)
    def _():
        hn_ref[...] = _rms_to_bf16(x_ref[...], g_ref[...])

    o_ref[...] = _dot(hn_ref[...], w_ref[...]).astype(o_ref.dtype)


def _norm_matmul(name, x, x_col_block, k, g, w, out_dtype, tm, tn):
    m = x.shape[0]
    n = w.shape[1]
    return pl.pallas_call(
        _norm_matmul_kernel,
        name=name,
        grid=(m // tm, n // tn),
        in_specs=[
            pl.BlockSpec((tm, k), lambda i, j: (i, x_col_block)),
            pl.BlockSpec((1, k), lambda i, j: (0, 0)),
            pl.BlockSpec((k, tn), lambda i, j: (0, j)),
        ],
        out_specs=pl.BlockSpec((tm, tn), lambda i, j: (i, j)),
        out_shape=jax.ShapeDtypeStruct((m, n), out_dtype),
        scratch_shapes=[pltpu.VMEM((tm, k), BF16)],
        compiler_params=_cparams(("parallel", "arbitrary")),
    )(x, g, w)


def _ffn_up_kernel(x_ref, g_ref, wg_ref, wu_ref, o_ref, hn_ref):
    @pl.when(pl.program_id(1) == 0)
    def _():
        hn_ref[...] = _rms_to_bf16(x_ref[...], g_ref[...])

    h = hn_ref[...]
    a = _dot(h, wg_ref[...])
    b = _dot(h, wu_ref[...])
    o_ref[...] = (a * jax.nn.sigmoid(a) * b).astype(o_ref.dtype)


def _ffn_up(x, g, wg, wu, tm, tn):
    m, k = x.shape
    n = wg.shape[1]
    return pl.pallas_call(
        _ffn_up_kernel,
        name="ffn_up",
        grid=(m // tm, n // tn),
        in_specs=[
            pl.BlockSpec((tm, k), lambda i, j: (i, 0)),
            pl.BlockSpec((1, k), lambda i, j: (0, 0)),
            pl.BlockSpec((k, tn), lambda i, j: (0, j)),
            pl.BlockSpec((k, tn), lambda i, j: (0, j)),
        ],
        out_specs=pl.BlockSpec((tm, tn), lambda i, j: (i, j)),
        out_shape=jax.ShapeDtypeStruct((m, n), BF16),
        scratch_shapes=[pltpu.VMEM((tm, k), BF16)],
        compiler_params=_cparams(("parallel", "arbitrary")),
    )(x, g, wg, wu)


def _matmul_res_kernel(a_ref, w_ref, r_ref, o_ref):
    o_ref[...] = r_ref[...] + _dot(a_ref[...], w_ref[...])


def _matmul_res(name, a, w, res, tm, tn):
    m, k = a.shape
    n = w.shape[1]
    return pl.pallas_call(
        _matmul_res_kernel,
        name=name,
        grid=(n // tn, m // tm),
        in_specs=[
            pl.BlockSpec((tm, k), lambda j, i: (i, 0)),
            pl.BlockSpec((k, tn), lambda j, i: (0, j)),
            pl.BlockSpec((tm, tn), lambda j, i: (i, j)),
        ],
        out_specs=pl.BlockSpec((tm, tn), lambda j, i: (i, j)),
        out_shape=jax.ShapeDtypeStruct((m, n), F32),
        compiler_params=_cparams(("parallel", "parallel")),
    )(a, w, res)


def _merge_kernel(ya_ref, yb_ref, wa_ref, wb_ref, ga_ref, gb_ref, o_ref):
    a = _dot(ya_ref[...], wa_ref[...])
    b = _dot(yb_ref[...], wb_ref[...])
    o_ref[...] = (jax.nn.sigmoid(ga_ref[...]) * a + jax.nn.sigmoid(gb_ref[...]) * b).astype(o_ref.dtype)


def _merge(ya, yb, wa, wb, proj, tm, tn):
    m, k = ya.shape
    n = wa.shape[1]
    ga0 = COL_GATE_A // tn
    gb0 = COL_GATE_B // tn
    return pl.pallas_call(
        _merge_kernel,
        name="merge",
        grid=(m // tm, n // tn),
        in_specs=[
            pl.BlockSpec((tm, k), lambda i, j: (i, 0)),
            pl.BlockSpec((tm, k), lambda i, j: (i, 0)),
            pl.BlockSpec((k, tn), lambda i, j: (0, j)),
            pl.BlockSpec((k, tn), lambda i, j: (0, j)),
            pl.BlockSpec((tm, tn), lambda i, j: (i, ga0 + j)),
            pl.BlockSpec((tm, tn), lambda i, j: (i, gb0 + j)),
        ],
        out_specs=pl.BlockSpec((tm, tn), lambda i, j: (i, j)),
        out_shape=jax.ShapeDtypeStruct((m, n), BF16),
        compiler_params=_cparams(("parallel", "parallel")),
    )(ya, yb, wa, wb, proj, proj)


def _ple_kernel(x_ref, g_ref, wg_ref, p_ref, wp_ref, xr_ref, o_ref, hn_ref):
    @pl.when(pl.program_id(1) == 0)
    def _():
        hn_ref[...] = _rms_to_bf16(x_ref[...], g_ref[...])

    gate = jax.nn.sigmoid(_dot(hn_ref[...], wg_ref[...]))
    emb = _dot(p_ref[...].astype(BF16), wp_ref[...])
    o_ref[...] = xr_ref[...] + gate * emb


def _ple(x, g, wg, p, wp, tm, tn):
    m, k = x.shape
    n = wg.shape[1]
    kp = p.shape[1]
    return pl.pallas_call(
        _ple_kernel,
        name="ple",
        grid=(m // tm, n // tn),
        in_specs=[
            pl.BlockSpec((tm, k), lambda i, j: (i, 0)),
            pl.BlockSpec((1, k), lambda i, j: (0, 0)),
            pl.BlockSpec((k, tn), lambda i, j: (0, j)),
            pl.BlockSpec((tm, kp), lambda i, j: (i, 0)),
            pl.BlockSpec((kp, tn), lambda i, j: (0, j)),
            pl.BlockSpec((tm, tn), lambda i, j: (i, j)),
        ],
        out_specs=pl.BlockSpec((tm, tn), lambda i, j: (i, j)),
        out_shape=jax.ShapeDtypeStruct((m, n), F32),
        scratch_shapes=[pltpu.VMEM((tm, k), BF16)],
        compiler_params=_cparams(("parallel", "arbitrary")),
    )(x, g, wg, p, wp, x)


def _rope64_pairs(x, c64, s64, first_half):
    partner = jnp.where(first_half, pltpu.roll(x, 96, 1), pltpu.roll(x, 32, 1))
    return x * c64 + partner * s64


def _prep_a_kernel(qa_ref, ka_ref, va_ref, qi_ref, kiw_ref, c128_ref, s128_ref, c64_ref, s64_ref,
                   gq_ref, gk_ref, qa_o, ka_o, va_o, qi_o, klo_o, khi_o, wi_o, *, wi_scale):
    c128 = c128_ref[...]
    s128 = s128_ref[...]

    def norm_rope(x, g):
        ms = jnp.mean(x * x, axis=-1, keepdims=True)
        y = x * lax.rsqrt(ms + NORM_EPS) * g
        return y * c128 + pltpu.roll(y, 64, 1) * s128

    for h in range(A_HEADS):
        hs = slice(h * LANES, (h + 1) * LANES)
        qa_o[:, hs] = norm_rope(qa_ref[:, hs], gq_ref[...]).astype(BF16)
    for h in range(A_KV_HEADS):
        hs = slice(h * LANES, (h + 1) * LANES)
        ka_o[:, hs] = norm_rope(ka_ref[:, hs], gk_ref[...]).astype(BF16)
    va_o[...] = va_ref[...].astype(BF16)

    c64 = c64_ref[...]
    s64 = s64_ref[...]
    lane = lax.broadcasted_iota(jnp.int32, c64.shape, 1)
    first_half = (lane % 64) < 32
    for p in range(IDX_HEADS * IDX_DIM // LANES):
        ps = slice(p * LANES, (p + 1) * LANES)
        qi_o[:, ps] = _rope64_pairs(qi_ref[:, ps], c64, s64, first_half).astype(BF16)
    kiw = kiw_ref[...]
    ki = jnp.where(lane < 64, _rope64_pairs(kiw, c64, s64, first_half), 0.0)
    klo_o[...] = ki.astype(BF16)
    khi_o[...] = pltpu.roll(ki, 64, 1).astype(BF16)
    wi_o[...] = kiw * wi_scale


def _prep_a(proj, c128, s128, c64, s64, gq, gk, tp):
    m = proj.shape[0]
    row = lambda w, cb: pl.BlockSpec((tp, w), lambda i: (i, cb))
    vec = pl.BlockSpec((1, LANES), lambda i: (0, 0))
    out = lambda w: pl.BlockSpec((tp, w), lambda i: (i, 0))
    wi_scale = (IDX_HEADS ** -0.5) * (IDX_DIM ** -0.5)
    return pl.pallas_call(
        functools.partial(_prep_a_kernel, wi_scale=wi_scale),
        name="prep_a",
        grid=(m // tp,),
        in_specs=[
            row(2048, COL_QA // 2048), row(512, COL_KA // 512), row(512, COL_VA // 512),
            row(1024, COL_QI // 1024), row(LANES, COL_KIW // LANES),
            out(LANES), out(LANES), out(LANES), out(LANES), vec, vec,
        ],
        out_specs=[out(2048), out(512), out(512), out(1024), out(LANES), out(LANES), out(LANES)],
        out_shape=[
            jax.ShapeDtypeStruct((m, 2048), BF16), jax.ShapeDtypeStruct((m, 512), BF16),
            jax.ShapeDtypeStruct((m, 512), BF16), jax.ShapeDtypeStruct((m, 1024), BF16),
            jax.ShapeDtypeStruct((m, LANES), BF16), jax.ShapeDtypeStruct((m, LANES), BF16),
            jax.ShapeDtypeStruct((m, LANES), F32),
        ],
        compiler_params=_cparams(("parallel",)),
    )(proj, proj, proj, proj, proj, c128, s128, c64, s64, gq, gk)


def _mla_q_kernel(cq_ref, g_ref, w_ref, c64_ref, s64_ref, gqn_ref, gqp_ref, q_o, *, q_scale):
    qb = _dot(_rms_to_bf16(cq_ref[...], g_ref[...]), w_ref[...])
    c64 = c64_ref[...]
    s64 = s64_ref[...]
    lane = lax.broadcasted_iota(jnp.int32, c64.shape, 1)
    first_half = (lane % 64) < 32
    lo = lane < 64
    nope_w = B_HEADS * B_NOPE
    for pair in range(B_HEADS // 2):
        qp2 = qb[:, nope_w + pair * LANES:nope_w + (pair + 1) * LANES]
        qp2_sq = qp2 * qp2
        rot2 = _rope64_pairs(qp2 * gqp_ref[...], c64, s64, first_half)
        for h in (2 * pair, 2 * pair + 1):
            mine = lo if h % 2 == 0 else jnp.logical_not(lo)
            shift_lane = lane == (64 if h % 2 == 0 else 0)
            qn = qb[:, h * LANES:(h + 1) * LANES]
            ss = jnp.sum(qn * qn + jnp.where(mine, qp2_sq, 0.0), axis=-1, keepdims=True)
            r = lax.rsqrt(ss * (1.0 / B_QK) + NORM_EPS) * q_scale
            q_o[:, 2 * h * LANES:(2 * h + 1) * LANES] = (qn * r * gqn_ref[...]).astype(BF16)
            qpe = jnp.where(mine, rot2 * r, jnp.where(shift_lane, 1.0, 0.0))
            q_o[:, (2 * h + 1) * LANES:(2 * h + 2) * LANES] = qpe.astype(BF16)


def _mla_kv_kernel(ckv_ref, g_ref, w_ref, kr_ref, c64_ref, s64_ref, gkn_ref, gkp_ref, shift_ref, k_o, v_o):
    kv = _dot(_rms_to_bf16(ckv_ref[...], g_ref[...]), w_ref[...])
    c64 = c64_ref[...]
    s64 = s64_ref[...]
    lane = lax.broadcasted_iota(jnp.int32, c64.shape, 1)
    first_half = (lane % 64) < 32
    lo = lane < 64
    kr = jnp.where(lo, kr_ref[...], 0.0)
    kr_ss = jnp.sum(kr * kr, axis=-1, keepdims=True)
    kpe_lo = jnp.where(lo, _rope64_pairs(kr * gkp_ref[...], c64, s64, first_half), 0.0)
    kpe_hi = pltpu.roll(kpe_lo, 64, 1)
    nope_w = B_HEADS * B_NOPE
    for h in range(B_HEADS):
        hs = slice(h * LANES, (h + 1) * LANES)
        shift_lane = lane == (64 if h % 2 == 0 else 0)
        kn = kv[:, hs]
        rk = lax.rsqrt((jnp.sum(kn * kn, axis=-1, keepdims=True) + kr_ss) * (1.0 / B_QK) + NORM_EPS)
        k_o[:, 2 * h * LANES:(2 * h + 1) * LANES] = (kn * rk * gkn_ref[...]).astype(BF16)
        kpe = kpe_lo if h % 2 == 0 else kpe_hi
        k_o[:, (2 * h + 1) * LANES:(2 * h + 2) * LANES] = jnp.where(
            shift_lane, -shift_ref[...], kpe * rk).astype(BF16)
        v_o[:, hs] = kv[:, nope_w + h * LANES:nope_w + (h + 1) * LANES].astype(BF16)


def _mla_q(proj, g_cq, w_uq, c64, s64, gqn, gqp, tm):
    m = proj.shape[0]
    vec = lambda w: pl.BlockSpec((1, w), lambda i: (0, 0))
    rows = lambda w: pl.BlockSpec((tm, w), lambda i: (i, 0))
    return pl.pallas_call(
        functools.partial(_mla_q_kernel, q_scale=B_QK ** -0.5 * LOG2E),
        name="mla_q",
        grid=(m // tm,),
        in_specs=[
            pl.BlockSpec((tm, B_Q_LORA), lambda i: (i, COL_CQ // B_Q_LORA)),
            vec(B_Q_LORA),
            pl.BlockSpec(w_uq.shape, lambda i: (0, 0)),
            rows(LANES), rows(LANES), vec(LANES), vec(LANES),
        ],
        out_specs=rows(B_HEADS * 256),
        out_shape=jax.ShapeDtypeStruct((m, B_HEADS * 256), BF16),
        compiler_params=_cparams(("parallel",)),
    )(proj, g_cq, w_uq, c64, s64, gqn, gqp)


def _mla_kv(proj, g_ckv, w_ukv, c64, s64, gkn, gkp, shift, tm):
    m = proj.shape[0]
    vec = lambda w: pl.BlockSpec((1, w), lambda i: (0, 0))
    rows = lambda w: pl.BlockSpec((tm, w), lambda i: (i, 0))
    return pl.pallas_call(
        _mla_kv_kernel,
        name="mla_kv",
        grid=(m // tm,),
        in_specs=[
            pl.BlockSpec((tm, B_KV_LORA), lambda i: (i, COL_CKV // B_KV_LORA)),
            vec(B_KV_LORA),
            pl.BlockSpec(w_ukv.shape, lambda i: (0, 0)),
            pl.BlockSpec((tm, LANES), lambda i: (i, COL_KR // LANES)),
            rows(LANES), rows(LANES), vec(LANES), vec(LANES), vec(LANES),
        ],
        out_specs=[rows(B_HEADS * 256), rows(B_HEADS * B_V)],
        out_shape=[
            jax.ShapeDtypeStruct((m, B_HEADS * 256), BF16),
            jax.ShapeDtypeStruct((m, B_HEADS * B_V), BF16),
        ],
        compiler_params=_cparams(("parallel",)),
    )(proj, g_ckv, w_ukv, proj, c64, s64, gkn, gkp, shift)


def _indexer_kernel(q_ref, klo_ref, khi_ref, w_ref, sc_ref, th_ref, *, tq, tk, seq, k_sel):
    qi = pl.program_id(1)
    n_vis = (qi * tq) // tk + 1
    n_all = seq // tk
    row = qi * tq + lax.broadcasted_iota(jnp.int32, (tq, 1), 0)
    wcols = [w_ref[0, :, IDX_DIM + h:IDX_DIM + h + 1] for h in range(IDX_HEADS)]

    def gen(c, carry):
        rmax, rmin = carry
        off = pl.multiple_of(c * tk, tk)
        kl = klo_ref[0, pl.ds(off, tk), :]
        kh = khi_ref[0, pl.ds(off, tk), :]
        acc = jnp.zeros((tq, tk), F32)
        for p in range(IDX_HEADS // 2):
            qp = q_ref[0, :, p * LANES:(p + 1) * LANES]
            acc = acc + wcols[2 * p] * jnp.maximum(_dot_nt(qp, kl), 0.0)
            acc = acc + wcols[2 * p + 1] * jnp.maximum(_dot_nt(qp, kh), 0.0)
        col = off + lax.broadcasted_iota(jnp.int32, (tq, tk), 1)
        vis = col <= row
        sc_ref[0, :, pl.ds(off, tk)] = jnp.where(vis, acc, NEG)
        rmax = jnp.maximum(rmax, jnp.max(jnp.where(vis, acc, NEG), axis=-1, keepdims=True))
        rmin = jnp.minimum(rmin, jnp.min(jnp.where(vis, acc, POS_BIG), axis=-1, keepdims=True))
        return rmax, rmin

    rmax, rmin = lax.fori_loop(0, n_vis, gen,
                               (jnp.full((tq, 1), NEG, F32), jnp.full((tq, 1), POS_BIG, F32)))

    def fill(c, carry):
        sc_ref[0, :, pl.ds(pl.multiple_of(c * tk, tk), tk)] = jnp.full((tq, tk), NEG, F32)
        return carry

    lax.fori_loop(n_vis, n_all, fill, 0)

    cw = min(2 * tk, seq)
    n_cnt = (n_vis * tk + cw - 1) // cw

    def count_ge(mid, strict=False):
        accs = []
        for r in range(tq // CNT_ROWS):
            rs = slice(r * CNT_ROWS, (r + 1) * CNT_ROWS)
            midb = mid[rs]

            def body(c, acc, rs=rs, midb=midb):
                off = pl.multiple_of(c * cw, cw)
                for j in range(cw // LANES):
                    blk = sc_ref[0, rs, pl.ds(off + j * LANES, LANES)]
                    acc = acc + jnp.where(blk > midb if strict else blk >= midb, 1.0, 0.0)
                return acc

            accs.append(lax.fori_loop(0, n_cnt, body, jnp.zeros((CNT_ROWS, LANES), F32)))
        return jnp.concatenate(
            [jnp.broadcast_to(jnp.sum(a, axis=-1, keepdims=True), (CNT_ROWS, LANES)) for a in accs], axis=0)

    rowb = qi * tq + lax.broadcasted_iota(jnp.int32, (tq, LANES), 0)
    active0 = (rowb + 1) > k_sel
    lo0 = jnp.where(active0, jnp.broadcast_to(rmin, (tq, LANES)), TH_ALL)
    hi0 = jnp.where(active0, jnp.broadcast_to(rmax, (tq, LANES)), TH_ALL)

    def n_open(lo, hi):
        return jnp.sum(jnp.where(lo < hi, 1.0, 0.0))

    def cond(st):
        it, nopen = st[0], st[1]
        return jnp.logical_and(nopen > 0.0, it < MAX_SEARCH_STEPS)

    def body(st):
        it, _, lo, hi, cnt_lo = st
        mid = lo + (hi - lo) * 0.5
        cnt = count_ge(mid)
        nopen = n_open(lo, hi)
        ge = cnt >= k_sel
        lo_n = jnp.where(ge, mid, lo)
        hi_n = jnp.where(ge, hi, mid)
        hi_n = jnp.where(cnt == k_sel, lo_n, hi_n)
        hi_n = jnp.where(mid >= hi, lo_n, hi_n)
        hi_n = jnp.where(mid <= lo, lo_n, hi_n)
        return it + 1, nopen, lo_n, hi_n, jnp.where(ge, cnt, cnt_lo)

    st = lax.while_loop(cond, body, (jnp.int32(0), n_open(lo0, hi0), lo0, hi0,
                                     (rowb + 1).astype(F32)))
    th = st[2]
    th_ref[0] = th

    tied = jnp.logical_and(active0, st[4] > k_sel)

    @pl.when(jnp.sum(jnp.where(tied, 1.0, 0.0)) > 0.0)
    def _():
        keep_ties = k_sel - count_ge(th, strict=True)
        reps = tk // LANES
        th_w = jnp.concatenate([th] * reps, axis=1)
        keep_w = jnp.concatenate([keep_ties] * reps, axis=1)
        earlier = (lax.broadcasted_iota(jnp.int32, (tk, tk), 0)
                   < lax.broadcasted_iota(jnp.int32, (tk, tk), 1))
        tri = jnp.where(earlier, 1.0, 0.0).astype(BF16)

        def demote(c, seen):
            off = pl.multiple_of(c * tk, tk)
            blk = sc_ref[0, :, pl.ds(off, tk)]
            eq = blk == th_w
            eqf = jnp.where(eq, 1.0, 0.0)
            rank = _dot(eqf.astype(BF16), tri) + jnp.concatenate([seen] * reps, axis=1)
            sc_ref[0, :, pl.ds(off, tk)] = jnp.where(jnp.logical_and(eq, rank >= keep_w), NEG, blk)
            return seen + jnp.sum(eqf, axis=-1, keepdims=True)

        lax.fori_loop(0, n_vis, demote, jnp.zeros((tq, LANES), F32))


def _indexer(qi_r, klo, khi, wi, tq, tk, k_sel):
    b, s, _ = qi_r.shape
    return pl.pallas_call(
        functools.partial(_indexer_kernel, tq=tq, tk=tk, seq=s, k_sel=float(k_sel)),
        name="indexer",
        grid=(b, s // tq),
        in_specs=[
            pl.BlockSpec((1, tq, IDX_HEADS * IDX_DIM), lambda bi, qi: (bi, qi, 0)),
            pl.BlockSpec((1, s, LANES), lambda bi, qi: (bi, 0, 0)),
            pl.BlockSpec((1, s, LANES), lambda bi, qi: (bi, 0, 0)),
            pl.BlockSpec((1, tq, LANES), lambda bi, qi: (bi, qi, 0)),
        ],
        out_specs=[
            pl.BlockSpec((1, tq, s), lambda bi, qi: (bi, qi, 0)),
            pl.BlockSpec((1, tq, LANES), lambda bi, qi: (bi, qi, 0)),
        ],
        out_shape=[jax.ShapeDtypeStruct((b, s, s), F32), jax.ShapeDtypeStruct((b, s, LANES), F32)],
        compiler_params=_cparams(("parallel", "parallel")),
    )(qi_r, klo, khi, wi)


def _causal_pairs(s, tq, tk):
    qs, ks = [], []
    for qi in range(s // tq):
        for ki in range((qi * tq + tq - 1) // tk + 1):
            qs.append(qi)
            ks.append(ki)
    return jnp.asarray(np.array(qs, np.int32)), jnp.asarray(np.array(ks, np.int32))


def _softmax_step(s, v, rows, m_ref, l_ref, acc_ref, fixed_shift):
    blocks = [s[:, j * LANES:(j + 1) * LANES] for j in range(s.shape[1] // LANES)]
    if fixed_shift:
        ps = [jnp.exp2(blk) for blk in blocks]
        l_ref[rows] = l_ref[rows] + functools.reduce(jnp.add, ps)
        p = jnp.concatenate([blk.astype(BF16) for blk in ps], axis=1)
        acc_ref[rows] = acc_ref[rows] + _dot(p, v)
        return
    mx = functools.reduce(jnp.maximum, blocks)
    m_prev = m_ref[rows]
    m_new = jnp.maximum(m_prev, jnp.max(mx, axis=-1, keepdims=True))
    alpha = jnp.exp2(m_prev - m_new)
    ps = [jnp.exp2(blk - m_new) for blk in blocks]
    l_ref[rows] = alpha * l_ref[rows] + jnp.sum(functools.reduce(jnp.add, ps), axis=-1, keepdims=True)
    p = jnp.concatenate([blk.astype(BF16) for blk in ps], axis=1)
    acc_ref[rows] = alpha * acc_ref[rows] + _dot(p, v)
    m_ref[rows] = m_new


def _softmax_finish(rows, l_ref, acc_ref, fixed_shift):
    l = l_ref[rows]
    if fixed_shift:
        l = jnp.sum(l, axis=-1, keepdims=True)
    return acc_ref[rows] / l


def _dsa_kernel(qt_ref, kt_ref, q_ref, k_ref, v_ref, sc_ref, th_ref, shift_ref, o_ref, m_ref, l_ref,
                acc_ref, bias_ref, qs_ref, *, tq, tk, fixed_shift):
    step = pl.program_id(1)
    qi = qt_ref[step]
    ki = kt_ref[step]

    @pl.when(ki == 0)
    def _():
        m_ref[...] = jnp.full(m_ref.shape, M_INIT, F32)
        l_ref[...] = jnp.zeros(l_ref.shape, F32)
        acc_ref[...] = jnp.zeros(acc_ref.shape, F32)
        for h in range(A_HEADS):
            qs_ref[h * tq:(h + 1) * tq, :] = q_ref[0, :, h * LANES:(h + 1) * LANES]

    bias_ref[...] = jnp.where(sc_ref[0] >= th_ref[0][:, :1], -shift_ref[:, :1], NEG)
    grp = A_HEADS // A_KV_HEADS
    for g in range(A_KV_HEADS):
        rows = slice(g * grp * tq, (g + 1) * grp * tq)
        k = k_ref[0, :, g * LANES:(g + 1) * LANES]
        v = v_ref[0, :, g * LANES:(g + 1) * LANES]
        s = _dot_nt(qs_ref[rows, :], k)
        s = jnp.concatenate([s[j * tq:(j + 1) * tq] + bias_ref[...] for j in range(grp)], axis=0)
        _softmax_step(s, v, rows, m_ref, l_ref, acc_ref, fixed_shift)

    @pl.when(ki == (qi * tq + tq - 1) // tk)
    def _():
        for h in range(A_HEADS):
            out = _softmax_finish(slice(h * tq, (h + 1) * tq), l_ref, acc_ref, fixed_shift)
            o_ref[0, :, h * LANES:(h + 1) * LANES] = out.astype(o_ref.dtype)


def _dsa_attn(q, k, v, scores, th, shift, tq, tk, fixed_shift):
    b, s, dq = q.shape
    qt, kt = _causal_pairs(s, tq, tk)
    grid_spec = pltpu.PrefetchScalarGridSpec(
        num_scalar_prefetch=2,
        grid=(b, qt.shape[0]),
        in_specs=[
            pl.BlockSpec((1, tq, dq), lambda bi, p, qt, kt: (bi, qt[p], 0)),
            pl.BlockSpec((1, tk, k.shape[2]), lambda bi, p, qt, kt: (bi, kt[p], 0)),
            pl.BlockSpec((1, tk, v.shape[2]), lambda bi, p, qt, kt: (bi, kt[p], 0)),
            pl.BlockSpec((1, tq, tk), lambda bi, p, qt, kt: (bi, qt[p], kt[p])),
            pl.BlockSpec((1, tq, LANES), lambda bi, p, qt, kt: (bi, qt[p], 0)),
            pl.BlockSpec((1, LANES), lambda bi, p, qt, kt: (0, 0)),
        ],
        out_specs=pl.BlockSpec((1, tq, dq), lambda bi, p, qt, kt: (bi, qt[p], 0)),
        scratch_shapes=[
            pltpu.VMEM((A_HEADS * tq, LANES), F32),
            pltpu.VMEM((A_HEADS * tq, LANES), F32),
            pltpu.VMEM((A_HEADS * tq, A_HEAD_DIM), F32),
            pltpu.VMEM((tq, tk), F32),
            pltpu.VMEM((A_HEADS * tq, A_HEAD_DIM), BF16),
        ],
    )
    return pl.pallas_call(
        functools.partial(_dsa_kernel, tq=tq, tk=tk, fixed_shift=fixed_shift),
        name="dsa_attn_fixed" if fixed_shift else "dsa_attn_online",
        grid_spec=grid_spec,
        out_shape=jax.ShapeDtypeStruct((b, s, dq), BF16),
        compiler_params=_cparams(("parallel", "arbitrary")),
    )(qt, kt, q, k, v, scores, th, shift)


def _mla_kernel(qt_ref, kt_ref, q_ref, k_ref, v_ref, o_ref, m_ref, l_ref, acc_ref, *, tq, tk, hg,
                fixed_shift):
    step = pl.program_id(2)
    qi = qt_ref[step]
    ki = kt_ref[step]

    @pl.when(ki == 0)
    def _():
        m_ref[...] = jnp.full(m_ref.shape, M_INIT, F32)
        l_ref[...] = jnp.zeros(l_ref.shape, F32)
        acc_ref[...] = jnp.zeros(acc_ref.shape, F32)

    def heads(masked):
        if masked:
            keep = (lax.broadcasted_iota(jnp.int32, (tq, tk), 1)
                    <= lax.broadcasted_iota(jnp.int32, (tq, tk), 0))
        for h in range(hg):
            q = q_ref[0, :, h * 256:(h + 1) * 256]
            k = k_ref[0, :, h * 256:(h + 1) * 256]
            v = v_ref[0, :, h * B_V:(h + 1) * B_V]
            s = _dot_nt(q, k)
            if masked:
                s = jnp.where(keep, s, NEG)
            _softmax_step(s, v, slice(h * tq, (h + 1) * tq), m_ref, l_ref, acc_ref, fixed_shift)

    @pl.when(ki < qi)
    def _():
        heads(False)

    @pl.when(ki == qi)
    def _():
        heads(True)
        for h in range(hg):
            out = _softmax_finish(slice(h * tq, (h + 1) * tq), l_ref, acc_ref, fixed_shift)
            o_ref[0, :, h * B_V:(h + 1) * B_V] = out.astype(o_ref.dtype)


def _mla_attn(q, k, v, tq, tk, hg, fixed_shift):
    b, s, _ = q.shape
    assert tq == tk
    qt, kt = _causal_pairs(s, tq, tk)
    grid_spec = pltpu.PrefetchScalarGridSpec(
        num_scalar_prefetch=2,
        grid=(b, B_HEADS // hg, qt.shape[0]),
        in_specs=[
            pl.BlockSpec((1, tq, hg * 256), lambda bi, g, p, qt, kt: (bi, qt[p], g)),
            pl.BlockSpec((1, tk, hg * 256), lambda bi, g, p, qt, kt: (bi, kt[p], g)),
            pl.BlockSpec((1, tk, hg * B_V), lambda bi, g, p, qt, kt: (bi, kt[p], g)),
        ],
        out_specs=pl.BlockSpec((1, tq, hg * B_V), lambda bi, g, p, qt, kt: (bi, qt[p], g)),
        scratch_shapes=[
            pltpu.VMEM((hg * tq, LANES), F32),
            pltpu.VMEM((hg * tq, LANES), F32),
            pltpu.VMEM((hg * tq, B_V), F32),
        ],
    )
    return pl.pallas_call(
        functools.partial(_mla_kernel, tq=tq, tk=tk, hg=hg, fixed_shift=fixed_shift),
        name="mla_attn_fixed" if fixed_shift else "mla_attn_online",
        grid_spec=grid_spec,
        out_shape=jax.ShapeDtypeStruct((b, s, B_HEADS * B_V), BF16),
        compiler_params=_cparams(("parallel", "parallel", "arbitrary")),
    )(qt, kt, q, k, v)


def _regroup_w_in(w):
    sizes = (A_HEADS * A_HEAD_DIM, A_KV_HEADS * A_HEAD_DIM, A_KV_HEADS * A_HEAD_DIM,
             IDX_HEADS * IDX_DIM, IDX_DIM, IDX_HEADS, B_Q_LORA, B_KV_LORA, B_ROPE)
    offs = np.cumsum((0,) + sizes)
    w = w.astype(BF16)
    qa, ka, va, qi, ki, wi, cq, ckv, kr = (w[:, offs[j]:offs[j + 1]] for j in range(len(sizes)))
    gates = w[:, offs[-1]:]
    d = w.shape[0]
    kiw = jnp.concatenate([ki, wi, jnp.zeros((d, LANES - IDX_DIM - IDX_HEADS), w.dtype)], axis=1)
    krp = jnp.concatenate([kr, jnp.zeros((d, LANES - B_ROPE), w.dtype)], axis=1)
    return jnp.concatenate([qa, gates, qi, ka, va, cq, ckv, kiw, krp], axis=1).astype(BF16)


def _softmax_shift(d, g_q, g_k):
    bound = (d ** 0.5) * LOG2E * 1.02 * jnp.max(jnp.abs(g_q)) * jnp.max(jnp.abs(g_k))
    c = jnp.ceil(bound.astype(F32))
    ok = c <= MAX_FIXED_SHIFT
    return jnp.full((1, LANES), jnp.where(ok, c, 0.0), F32), ok


def _rope_tables(pos, d):
    half = d // 2
    freqs = jnp.power(jnp.float32(ROPE_THETA), -jnp.arange(half, dtype=F32) * (2.0 / d))
    ang = pos.astype(F32)[..., None] * freqs
    cos, sin = jnp.cos(ang), jnp.sin(ang)
    reps = LANES // d
    c = jnp.tile(jnp.concatenate([cos, cos], axis=-1), (1, 1, reps))
    s = jnp.tile(jnp.concatenate([-sin, sin], axis=-1), (1, 1, reps))
    return c.reshape(-1, LANES), s.reshape(-1, LANES)


def _layer(x, p, positions, g_mix_norm, w_in, g_qa, g_ka, g_cq, w_uq, g_ckv, w_ukv, g_qb, g_kb,
           w_out_a, w_out_b, w_o, g_ffn_norm, w_ffn_gate, w_ffn_up, w_ffn_down, g_ple_norm,
           w_ple_gate, w_ple_proj):
    b, s, d = x.shape
    m = b * s
    k_sel = min(TOPK_MAX, s // 4)
    xf = x.reshape(m, d)
    row = lambda g: g.reshape(1, -1).astype(F32)

    c128, s128 = _rope_tables(positions, A_HEAD_DIM)
    c64, s64 = _rope_tables(positions, IDX_DIM)

    proj = _norm_matmul("in_proj", xf, 0, d, row(g_mix_norm), _regroup_w_in(w_in), F32, tm=1024, tn=1024)

    qa_r, ka_r, va_b, qi_r, klo, khi, wi = _prep_a(
        proj, c128, s128, c64, s64, row(g_qa) * (A_HEAD_DIM ** -0.5 * LOG2E), row(g_ka), tp=512)
    r3 = lambda a: a.reshape(b, s, a.shape[-1])
    scores, th = _indexer(r3(qi_r), r3(klo), r3(khi), r3(wi), tq=512, tk=512, k_sel=k_sel)
    shift_a, fixed_a = _softmax_shift(A_HEAD_DIM, g_qa, g_ka)
    y_a = lax.cond(
        fixed_a,
        functools.partial(_dsa_attn, tq=512, tk=512, fixed_shift=True),
        functools.partial(_dsa_attn, tq=512, tk=512, fixed_shift=False),
        r3(qa_r), r3(ka_r), r3(va_b), scores, th, shift_a)

    w_uq_r = w_uq.reshape(B_Q_LORA, B_HEADS, B_QK)
    w_uq_r = jnp.concatenate([w_uq_r[:, :, :B_NOPE].reshape(B_Q_LORA, -1),
                              w_uq_r[:, :, B_NOPE:].reshape(B_Q_LORA, -1)], axis=1).astype(BF16)
    w_ukv_r = w_ukv.reshape(B_KV_LORA, B_HEADS, 2, B_NOPE)
    w_ukv_r = jnp.concatenate([w_ukv_r[:, :, 0, :].reshape(B_KV_LORA, -1),
                               w_ukv_r[:, :, 1, :].reshape(B_KV_LORA, -1)], axis=1).astype(BF16)
    pe2 = lambda g: jnp.tile(g[B_NOPE:], 2).reshape(1, LANES).astype(F32)
    shift_b, fixed_b = _softmax_shift(B_QK, g_qb, g_kb)
    q_m = _mla_q(proj, row(g_cq), w_uq_r, c64, s64, row(g_qb[:B_NOPE]), pe2(g_qb), tm=512)
    k_m, v_m = _mla_kv(proj, row(g_ckv), w_ukv_r, c64, s64, row(g_kb[:B_NOPE]), pe2(g_kb), shift_b, tm=512)
    y_b = lax.cond(
        fixed_b,
        functools.partial(_mla_attn, tq=512, tk=512, hg=8, fixed_shift=True),
        functools.partial(_mla_attn, tq=512, tk=512, hg=8, fixed_shift=False),
        r3(q_m), r3(k_m), r3(v_m))

    mixed = _merge(y_a.reshape(m, -1), y_b.reshape(m, -1), w_out_a.astype(BF16), w_out_b.astype(BF16),
                   proj, tm=1024, tn=512)
    x1 = _matmul_res("w_o", mixed, w_o.astype(BF16), xf, tm=1024, tn=1024)

    act = _ffn_up(x1, row(g_ffn_norm), w_ffn_gate.astype(BF16), w_ffn_up.astype(BF16), tm=1024, tn=512)
    x2 = _matmul_res("ffn_down", act, w_ffn_down.astype(BF16), x1, tm=512, tn=1024)

    out = _ple(x2, row(g_ple_norm), w_ple_gate.astype(BF16), p.reshape(m, -1),
               w_ple_proj.astype(BF16), tm=1024, tn=512)
    return out.reshape(b, s, d)


def kernel(x, p, positions, g_mix_norm, w_in, g_qa, g_ka, g_cq, w_uq, g_ckv, w_ukv, g_qb, g_kb,
           w_out_a, w_out_b, w_o, g_ffn_norm, w_ffn_gate, w_ffn_up, w_ffn_down, g_ple_norm,
           w_ple_gate, w_ple_proj):
    for i in range(p.shape[0]):
        x = _layer(x, p[i], positions, g_mix_norm[i], w_in[i], g_qa[i], g_ka[i], g_cq[i], w_uq[i],
                   g_ckv[i], w_ukv[i], g_qb[i], g_kb[i], w_out_a[i], w_out_b[i], w_o[i],
                   g_ffn_norm[i], w_ffn_gate[i], w_ffn_up[i], w_ffn_down[i], g_ple_norm[i],
                   w_ple_gate[i], w_ple_proj[i])
    return x
```

```python
import functools

import numpy as np
import jax
import jax.numpy as jnp
from jax import lax
from jax.experimental import pallas as pl
from jax.experimental.pallas import tpu as pltpu

F32 = jnp.float32
BF16 = jnp.bfloat16

ROPE_THETA = 10000.0
NORM_EPS = 1e-6
NEG = -1e30
M_INIT = -1e29
POS_BIG = 3e38
TH_ALL = 0.5 * NEG
LOG2E = 1.4426950408889634
MAX_SEARCH_STEPS = 512
CNT_ROWS = 128
MAX_FIXED_SHIFT = 60.0
A_HEADS = 16
A_KV_HEADS = 4
A_HEAD_DIM = 128
IDX_HEADS = 16
IDX_DIM = 64
TOPK_MAX = 256
B_HEADS = 16
B_Q_LORA = 512
B_KV_LORA = 256
B_NOPE = 128
B_ROPE = 64
B_V = 128
B_QK = B_NOPE + B_ROPE
LANES = 128
VMEM_LIMIT = 56 * 1024 * 1024

COL_QA = 0
COL_GATE_A = 2048
COL_GATE_B = 4096
COL_QI = 6144
COL_KA = 7168
COL_VA = 7680
COL_CQ = 8192
COL_CKV = 8704
COL_KIW = 8960
COL_KR = 9088
N_PROJ = 9216


def _dot(a, b):
    return jnp.dot(a, b, preferred_element_type=F32)


def _dot_nt(a, b):
    return lax.dot_general(a, b, (((1,), (1,)), ((), ())), preferred_element_type=F32)


def _cparams(sem):
    return pltpu.CompilerParams(dimension_semantics=sem, vmem_limit_bytes=VMEM_LIMIT)


def _rms_to_bf16(x, g):
    ms = jnp.mean(x * x, axis=-1, keepdims=True)
    return (x * lax.rsqrt(ms + NORM_EPS) * g).astype(BF16)


def _norm_matmul_kernel(x_ref, g_ref, w_ref, o_ref, hn_ref):
    @pl.when(pl.program_id(1) == 0)
    def _():
        hn_ref[...] = _rms_to_bf16(x_ref[...], g_ref[...])

    o_ref[...] = _dot(hn_ref[...], w_ref[...]).astype(o_ref.dtype)


def _norm_matmul(name, x, x_col_block, k, g, w, out_dtype, tm, tn):
    m = x.shape[0]
    n = w.shape[1]
    return pl.pallas_call(
        _norm_matmul_kernel,
        name=name,
        grid=(m // tm, n // tn),
        in_specs=[
            pl.BlockSpec((tm, k), lambda i, j: (i, x_col_block)),
            pl.BlockSpec((1, k), lambda i, j: (0, 0)),
            pl.BlockSpec((k, tn), lambda i, j: (0, j)),
        ],
        out_specs=pl.BlockSpec((tm, tn), lambda i, j: (i, j)),
        out_shape=jax.ShapeDtypeStruct((m, n), out_dtype),
        scratch_shapes=[pltpu.VMEM((tm, k), BF16)],
        compiler_params=_cparams(("parallel", "arbitrary")),
    )(x, g, w)


def _ffn_up_kernel(x_ref, g_ref, wg_ref, wu_ref, o_ref, hn_ref):
    @pl.when(pl.program_id(1) == 0)
    def _():
        hn_ref[...] = _rms_to_bf16(x_ref[...], g_ref[...])

    h = hn_ref[...]
    a = _dot(h, wg_ref[...])
    b = _dot(h, wu_ref[...])
    o_ref[...] = (a * jax.nn.sigmoid(a) * b).astype(o_ref.dtype)


def _ffn_up(x, g, wg, wu, tm, tn):
    m, k = x.shape
    n = wg.shape[1]
    return pl.pallas_call(
        _ffn_up_kernel,
        name="ffn_up",
        grid=(m // tm, n // tn),
        in_specs=[
            pl.BlockSpec((tm, k), lambda i, j: (i, 0)),
            pl.BlockSpec((1, k), lambda i, j: (0, 0)),
            pl.BlockSpec((k, tn), lambda i, j: (0, j)),
            pl.BlockSpec((k, tn), lambda i, j: (0, j)),
        ],
        out_specs=pl.BlockSpec((tm, tn), lambda i, j: (i, j)),
        out_shape=jax.ShapeDtypeStruct((m, n), BF16),
        scratch_shapes=[pltpu.VMEM((tm, k), BF16)],
        compiler_params=_cparams(("parallel", "arbitrary")),
    )(x, g, wg, wu)


def _matmul_res_kernel(a_ref, w_ref, r_ref, o_ref):
    o_ref[...] = r_ref[...] + _dot(a_ref[...], w_ref[...])


def _matmul_res(name, a, w, res, tm, tn):
    m, k = a.shape
    n = w.shape[1]
    return pl.pallas_call(
        _matmul_res_kernel,
        name=name,
        grid=(n // tn, m // tm),
        in_specs=[
            pl.BlockSpec((tm, k), lambda j, i: (i, 0)),
            pl.BlockSpec((k, tn), lambda j, i: (0, j)),
            pl.BlockSpec((tm, tn), lambda j, i: (i, j)),
        ],
        out_specs=pl.BlockSpec((tm, tn), lambda j, i: (i, j)),
        out_shape=jax.ShapeDtypeStruct((m, n), F32),
        compiler_params=_cparams(("parallel", "parallel")),
    )(a, w, res)


def _merge_kernel(ya_ref, yb_ref, wa_ref, wb_ref, ga_ref, gb_ref, o_ref):
    a = _dot(ya_ref[...], wa_ref[...])
    b = _dot(yb_ref[...], wb_ref[...])
    o_ref[...] = (jax.nn.sigmoid(ga_ref[...]) * a + jax.nn.sigmoid(gb_ref[...]) * b).astype(o_ref.dtype)


def _merge(ya, yb, wa, wb, proj, tm, tn):
    m, k = ya.shape
    n = wa.shape[1]
    ga0 = COL_GATE_A // tn
    gb0 = COL_GATE_B // tn
    return pl.pallas_call(
        _merge_kernel,
        name="merge",
        grid=(m // tm, n // tn),
        in_specs=[
            pl.BlockSpec((tm, k), lambda i, j: (i, 0)),
            pl.BlockSpec((tm, k), lambda i, j: (i, 0)),
            pl.BlockSpec((k, tn), lambda i, j: (0, j)),
            pl.BlockSpec((k, tn), lambda i, j: (0, j)),
            pl.BlockSpec((tm, tn), lambda i, j: (i, ga0 + j)),
            pl.BlockSpec((tm, tn), lambda i, j: (i, gb0 + j)),
        ],
        out_specs=pl.BlockSpec((tm, tn), lambda i, j: (i, j)),
        out_shape=jax.ShapeDtypeStruct((m, n), BF16),
        compiler_params=_cparams(("parallel", "parallel")),
    )(ya, yb, wa, wb, proj, proj)


def _ple_kernel(x_ref, g_ref, wg_ref, p_ref, wp_ref, xr_ref, o_ref, hn_ref):
    @pl.when(pl.program_id(1) == 0)
    def _():
        hn_ref[...] = _rms_to_bf16(x_ref[...], g_ref[...])

    gate = jax.nn.sigmoid(_dot(hn_ref[...], wg_ref[...]))
    emb = _dot(p_ref[...].astype(BF16), wp_ref[...])
    o_ref[...] = xr_ref[...] + gate * emb


def _ple(x, g, wg, p, wp, tm, tn):
    m, k = x.shape
    n = wg.shape[1]
    kp = p.shape[1]
    return pl.pallas_call(
        _ple_kernel,
        name="ple",
        grid=(m // tm, n // tn),
        in_specs=[
            pl.BlockSpec((tm, k), lambda i, j: (i, 0)),
            pl.BlockSpec((1, k), lambda i, j: (0, 0)),
            pl.BlockSpec((k, tn), lambda i, j: (0, j)),
            pl.BlockSpec((tm, kp), lambda i, j: (i, 0)),
            pl.BlockSpec((kp, tn), lambda i, j: (0, j)),
            pl.BlockSpec((tm, tn), lambda i, j: (i, j)),
        ],
        out_specs=pl.BlockSpec((tm, tn), lambda i, j: (i, j)),
        out_shape=jax.ShapeDtypeStruct((m, n), F32),
        scratch_shapes=[pltpu.VMEM((tm, k), BF16)],
        compiler_params=_cparams(("parallel", "arbitrary")),
    )(x, g, wg, p, wp, x)


def _rope64_pairs(x, c64, s64, first_half):
    partner = jnp.where(first_half, pltpu.roll(x, 96, 1), pltpu.roll(x, 32, 1))
    return x * c64 + partner * s64


def _prep_a_kernel(qa_ref, ka_ref, va_ref, qi_ref, kiw_ref, c128_ref, s128_ref, c64_ref, s64_ref,
                   gq_ref, gk_ref, qa_o, ka_o, va_o, qi_o, klo_o, khi_o, wi_o, *, wi_scale):
    c128 = c128_ref[...]
    s128 = s128_ref[...]

    def norm_rope(x, g):
        ms = jnp.mean(x * x, axis=-1, keepdims=True)
        y = x * lax.rsqrt(ms + NORM_EPS) * g
        return y * c128 + pltpu.roll(y, 64, 1) * s128

    for h in range(A_HEADS):
        hs = slice(h * LANES, (h + 1) * LANES)
        qa_o[:, hs] = norm_rope(qa_ref[:, hs], gq_ref[...]).astype(BF16)
    for h in range(A_KV_HEADS):
        hs = slice(h * LANES, (h + 1) * LANES)
        ka_o[:, hs] = norm_rope(ka_ref[:, hs], gk_ref[...]).astype(BF16)
    va_o[...] = va_ref[...].astype(BF16)

    c64 = c64_ref[...]
    s64 = s64_ref[...]
    lane = lax.broadcasted_iota(jnp.int32, c64.shape, 1)
    first_half = (lane % 64) < 32
    for p in range(IDX_HEADS * IDX_DIM // LANES):
        ps = slice(p * LANES, (p + 1) * LANES)
        qi_o[:, ps] = _rope64_pairs(qi_ref[:, ps], c64, s64, first_half).astype(BF16)
    kiw = kiw_ref[...]
    ki = jnp.where(lane < 64, _rope64_pairs(kiw, c64, s64, first_half), 0.0)
    klo_o[...] = ki.astype(BF16)
    khi_o[...] = pltpu.roll(ki, 64, 1).astype(BF16)
    wi_o[...] = kiw * wi_scale


def _prep_a(proj, c128, s128, c64, s64, gq, gk, tp):
    m = proj.shape[0]
    row = lambda w, cb: pl.BlockSpec((tp, w), lambda i: (i, cb))
    vec = pl.BlockSpec((1, LANES), lambda i: (0, 0))
    out = lambda w: pl.BlockSpec((tp, w), lambda i: (i, 0))
    wi_scale = (IDX_HEADS ** -0.5) * (IDX_DIM ** -0.5)
    return pl.pallas_call(
        functools.partial(_prep_a_kernel, wi_scale=wi_scale),
        name="prep_a",
        grid=(m // tp,),
        in_specs=[
            row(2048, COL_QA // 2048), row(512, COL_KA // 512), row(512, COL_VA // 512),
            row(1024, COL_QI // 1024), row(LANES, COL_KIW // LANES),
            out(LANES), out(LANES), out(LANES), out(LANES), vec, vec,
        ],
        out_specs=[out(2048), out(512), out(512), out(1024), out(LANES), out(LANES), out(LANES)],
        out_shape=[
            jax.ShapeDtypeStruct((m, 2048), BF16), jax.ShapeDtypeStruct((m, 512), BF16),
            jax.ShapeDtypeStruct((m, 512), BF16), jax.ShapeDtypeStruct((m, 1024), BF16),
            jax.ShapeDtypeStruct((m, LANES), BF16), jax.ShapeDtypeStruct((m, LANES), BF16),
            jax.ShapeDtypeStruct((m, LANES), F32),
        ],
        compiler_params=_cparams(("parallel",)),
    )(proj, proj, proj, proj, proj, c128, s128, c64, s64, gq, gk)


def _mla_q_kernel(cq_ref, g_ref, w_ref, c64_ref, s64_ref, gqn_ref, gqp_ref, q_o, *, q_scale):
    qb = _dot(_rms_to_bf16(cq_ref[...], g_ref[...]), w_ref[...])
    c64 = c64_ref[...]
    s64 = s64_ref[...]
    lane = lax.broadcasted_iota(jnp.int32, c64.shape, 1)
    first_half = (lane % 64) < 32
    lo = lane < 64
    nope_w = B_HEADS * B_NOPE
    for pair in range(B_HEADS // 2):
        qp2 = qb[:, nope_w + pair * LANES:nope_w + (pair + 1) * LANES]
        qp2_sq = qp2 * qp2
        rot2 = _rope64_pairs(qp2 * gqp_ref[...], c64, s64, first_half)
        for h in (2 * pair, 2 * pair + 1):
            mine = lo if h % 2 == 0 else jnp.logical_not(lo)
            shift_lane = lane == (64 if h % 2 == 0 else 0)
            qn = qb[:, h * LANES:(h + 1) * LANES]
            ss = jnp.sum(qn * qn + jnp.where(mine, qp2_sq, 0.0), axis=-1, keepdims=True)
            r = lax.rsqrt(ss * (1.0 / B_QK) + NORM_EPS) * q_scale
            q_o[:, 2 * h * LANES:(2 * h + 1) * LANES] = (qn * r * gqn_ref[...]).astype(BF16)
            qpe = jnp.where(mine, rot2 * r, jnp.where(shift_lane, 1.0, 0.0))
            q_o[:, (2 * h + 1) * LANES:(2 * h + 2) * LANES] = qpe.astype(BF16)


def _mla_kv_kernel(ckv_ref, g_ref, w_ref, kr_ref, c64_ref, s64_ref, gkn_ref, gkp_ref, shift_ref, k_o, v_o):
    kv = _dot(_rms_to_bf16(ckv_ref[...], g_ref[...]), w_ref[...])
    c64 = c64_ref[...]
    s64 = s64_ref[...]
    lane = lax.broadcasted_iota(jnp.int32, c64.shape, 1)
    first_half = (lane % 64) < 32
    lo = lane < 64
    kr = jnp.where(lo, kr_ref[...], 0.0)
    kr_ss = jnp.sum(kr * kr, axis=-1, keepdims=True)
    kpe_lo = jnp.where(lo, _rope64_pairs(kr * gkp_ref[...], c64, s64, first_half), 0.0)
    kpe_hi = pltpu.roll(kpe_lo, 64, 1)
    nope_w = B_HEADS * B_NOPE
    for h in range(B_HEADS):
        hs = slice(h * LANES, (h + 1) * LANES)
        shift_lane = lane == (64 if h % 2 == 0 else 0)
        kn = kv[:, hs]
        rk = lax.rsqrt((jnp.sum(kn * kn, axis=-1, keepdims=True) + kr_ss) * (1.0 / B_QK) + NORM_EPS)
        k_o[:, 2 * h * LANES:(2 * h + 1) * LANES] = (kn * rk * gkn_ref[...]).astype(BF16)
        kpe = kpe_lo if h % 2 == 0 else kpe_hi
        k_o[:, (2 * h + 1) * LANES:(2 * h + 2) * LANES] = jnp.where(
            shift_lane, -shift_ref[...], kpe * rk).astype(BF16)
        v_o[:, hs] = kv[:, nope_w + h * LANES:nope_w + (h + 1) * LANES].astype(BF16)


def _mla_q(proj, g_cq, w_uq, c64, s64, gqn, gqp, tm):
    m = proj.shape[0]
    vec = lambda w: pl.BlockSpec((1, w), lambda i: (0, 0))
    rows = lambda w: pl.BlockSpec((tm, w), lambda i: (i, 0))
    return pl.pallas_call(
        functools.partial(_mla_q_kernel, q_scale=B_QK ** -0.5 * LOG2E),
        name="mla_q",
        grid=(m // tm,),
        in_specs=[
            pl.BlockSpec((tm, B_Q_LORA), lambda i: (i, COL_CQ // B_Q_LORA)),
            vec(B_Q_LORA),
            pl.BlockSpec(w_uq.shape, lambda i: (0, 0)),
            rows(LANES), rows(LANES), vec(LANES), vec(LANES),
        ],
        out_specs=rows(B_HEADS * 256),
        out_shape=jax.ShapeDtypeStruct((m, B_HEADS * 256), BF16),
        compiler_params=_cparams(("parallel",)),
    )(proj, g_cq, w_uq, c64, s64, gqn, gqp)


def _mla_kv(proj, g_ckv, w_ukv, c64, s64, gkn, gkp, shift, tm):
    m = proj.shape[0]
    vec = lambda w: pl.BlockSpec((1, w), lambda i: (0, 0))
    rows = lambda w: pl.BlockSpec((tm, w), lambda i: (i, 0))
    return pl.pallas_call(
        _mla_kv_kernel,
        name="mla_kv",
        grid=(m // tm,),
        in_specs=[
            pl.BlockSpec((tm, B_KV_LORA), lambda i: (i, COL_CKV // B_KV_LORA)),
            vec(B_KV_LORA),
            pl.BlockSpec(w_ukv.shape, lambda i: (0, 0)),
            pl.BlockSpec((tm, LANES), lambda i: (i, COL_KR // LANES)),
            rows(LANES), rows(LANES), vec(LANES), vec(LANES), vec(LANES),
        ],
        out_specs=[rows(B_HEADS * 256), rows(B_HEADS * B_V)],
        out_shape=[
            jax.ShapeDtypeStruct((m, B_HEADS * 256), BF16),
            jax.ShapeDtypeStruct((m, B_HEADS * B_V), BF16),
        ],
        compiler_params=_cparams(("parallel",)),
    )(proj, g_ckv, w_ukv, proj, c64, s64, gkn, gkp, shift)


def _indexer_kernel(q_ref, klo_ref, khi_ref, w_ref, sc_ref, th_ref, *, tq, tk, seq, k_sel):
    qi = pl.program_id(1)
    n_vis = (qi * tq) // tk + 1
    n_all = seq // tk
    row = qi * tq + lax.broadcasted_iota(jnp.int32, (tq, 1), 0)
    wcols = [w_ref[0, :, IDX_DIM + h:IDX_DIM + h + 1] for h in range(IDX_HEADS)]

    def gen(c, carry):
        rmax, rmin = carry
        off = pl.multiple_of(c * tk, tk)
        kl = klo_ref[0, pl.ds(off, tk), :]
        kh = khi_ref[0, pl.ds(off, tk), :]
        acc = jnp.zeros((tq, tk), F32)
        for p in range(IDX_HEADS // 2):
            qp = q_ref[0, :, p * LANES:(p + 1) * LANES]
            acc = acc + wcols[2 * p] * jnp.maximum(_dot_nt(qp, kl), 0.0)
            acc = acc + wcols[2 * p + 1] * jnp.maximum(_dot_nt(qp, kh), 0.0)
        col = off + lax.broadcasted_iota(jnp.int32, (tq, tk), 1)
        vis = col <= row
        sc_ref[0, :, pl.ds(off, tk)] = jnp.where(vis, acc, NEG)
        rmax = jnp.maximum(rmax, jnp.max(jnp.where(vis, acc, NEG), axis=-1, keepdims=True))
        rmin = jnp.minimum(rmin, jnp.min(jnp.where(vis, acc, POS_BIG), axis=-1, keepdims=True))
        return rmax, rmin

    rmax, rmin = lax.fori_loop(0, n_vis, gen,
                               (jnp.full((tq, 1), NEG, F32), jnp.full((tq, 1), POS_BIG, F32)))

    def fill(c, carry):
        sc_ref[0, :, pl.ds(pl.multiple_of(c * tk, tk), tk)] = jnp.full((tq, tk), NEG, F32)
        return carry

    lax.fori_loop(n_vis, n_all, fill, 0)

    cw = min(2 * tk, seq)
    n_cnt = (n_vis * tk + cw - 1) // cw

    def count_ge(mid, strict=False, open_groups=None):
        zeros = jnp.zeros((CNT_ROWS, LANES), F32)
        accs = []
        for r in range(tq // CNT_ROWS):
            rs = slice(r * CNT_ROWS, (r + 1) * CNT_ROWS)
            midb = mid[rs]

            def body(c, acc, rs=rs, midb=midb):
                off = pl.multiple_of(c * cw, cw)
                for j in range(cw // LANES):
                    blk = sc_ref[0, rs, pl.ds(off + j * LANES, LANES)]
                    acc = acc + jnp.where(blk > midb if strict else blk >= midb, 1.0, 0.0)
                return acc

            if open_groups is None:
                accs.append(lax.fori_loop(0, n_cnt, body, zeros))
            else:
                accs.append(lax.cond(open_groups[r] > 0.0,
                                     functools.partial(lax.fori_loop, 0, n_cnt, body, zeros),
                                     lambda: zeros))
        return jnp.concatenate(
            [jnp.broadcast_to(jnp.sum(a, axis=-1, keepdims=True), (CNT_ROWS, LANES)) for a in accs], axis=0)

    rowb = qi * tq + lax.broadcasted_iota(jnp.int32, (tq, LANES), 0)
    active0 = (rowb + 1) > k_sel
    lo0 = jnp.where(active0, jnp.broadcast_to(rmin, (tq, LANES)), TH_ALL)
    hi0 = jnp.where(active0, jnp.broadcast_to(rmax, (tq, LANES)), TH_ALL)

    def n_open(lo, hi):
        still = jnp.where(lo < hi, 1.0, 0.0)
        return tuple(jnp.sum(still[r * CNT_ROWS:(r + 1) * CNT_ROWS]) for r in range(tq // CNT_ROWS))

    def cond(st):
        it, nopen = st[0], st[1]
        return jnp.logical_and(functools.reduce(jnp.add, nopen) > 0.0, it < MAX_SEARCH_STEPS)

    def body(st):
        it, was_open, lo, hi, cnt_lo = st
        mid = lo + (hi - lo) * 0.5
        cnt = count_ge(mid, open_groups=was_open)
        nopen = n_open(lo, hi)
        ge = cnt >= k_sel
        lo_n = jnp.where(ge, mid, lo)
        hi_n = jnp.where(ge, hi, mid)
        hi_n = jnp.where(cnt == k_sel, lo_n, hi_n)
        hi_n = jnp.where(mid >= hi, lo_n, hi_n)
        hi_n = jnp.where(mid <= lo, lo_n, hi_n)
        return it + 1, nopen, lo_n, hi_n, jnp.where(ge, cnt, cnt_lo)

    st = lax.while_loop(cond, body, (jnp.int32(0), n_open(lo0, hi0), lo0, hi0,
                                     (rowb + 1).astype(F32)))
    th = st[2]
    th_ref[0] = th

    tied = jnp.logical_and(active0, st[4] > k_sel)

    @pl.when(jnp.sum(jnp.where(tied, 1.0, 0.0)) > 0.0)
    def _():
        keep_ties = k_sel - count_ge(th, strict=True)
        reps = tk // LANES
        th_w = jnp.concatenate([th] * reps, axis=1)
        keep_w = jnp.concatenate([keep_ties] * reps, axis=1)
        earlier = (lax.broadcasted_iota(jnp.int32, (tk, tk), 0)
                   < lax.broadcasted_iota(jnp.int32, (tk, tk), 1))
        tri = jnp.where(earlier, 1.0, 0.0).astype(BF16)

        def demote(c, seen):
            off = pl.multiple_of(c * tk, tk)
            blk = sc_ref[0, :, pl.ds(off, tk)]
            eq = blk == th_w
            eqf = jnp.where(eq, 1.0, 0.0)
            rank = _dot(eqf.astype(BF16), tri) + jnp.concatenate([seen] * reps, axis=1)
            sc_ref[0, :, pl.ds(off, tk)] = jnp.where(jnp.logical_and(eq, rank >= keep_w), NEG, blk)
            return seen + jnp.sum(eqf, axis=-1, keepdims=True)

        lax.fori_loop(0, n_vis, demote, jnp.zeros((tq, LANES), F32))


def _indexer(qi_r, klo, khi, wi, tq, tk, k_sel):
    b, s, _ = qi_r.shape
    return pl.pallas_call(
        functools.partial(_indexer_kernel, tq=tq, tk=tk, seq=s, k_sel=float(k_sel)),
        name="indexer",
        grid=(b, s // tq),
        in_specs=[
            pl.BlockSpec((1, tq, IDX_HEADS * IDX_DIM), lambda bi, qi: (bi, qi, 0)),
            pl.BlockSpec((1, s, LANES), lambda bi, qi: (bi, 0, 0)),
            pl.BlockSpec((1, s, LANES), lambda bi, qi: (bi, 0, 0)),
            pl.BlockSpec((1, tq, LANES), lambda bi, qi: (bi, qi, 0)),
        ],
        out_specs=[
            pl.BlockSpec((1, tq, s), lambda bi, qi: (bi, qi, 0)),
            pl.BlockSpec((1, tq, LANES), lambda bi, qi: (bi, qi, 0)),
        ],
        out_shape=[jax.ShapeDtypeStruct((b, s, s), F32), jax.ShapeDtypeStruct((b, s, LANES), F32)],
        compiler_params=_cparams(("parallel", "parallel")),
    )(qi_r, klo, khi, wi)


def _causal_pairs(s, tq, tk):
    qs, ks = [], []
    for qi in range(s // tq):
        for ki in range((qi * tq + tq - 1) // tk + 1):
            qs.append(qi)
            ks.append(ki)
    return jnp.asarray(np.array(qs, np.int32)), jnp.asarray(np.array(ks, np.int32))


def _softmax_step(s, v, rows, m_ref, l_ref, acc_ref, fixed_shift):
    blocks = [s[:, j * LANES:(j + 1) * LANES] for j in range(s.shape[1] // LANES)]
    if fixed_shift:
        ps = [jnp.exp2(blk) for blk in blocks]
        l_ref[rows] = l_ref[rows] + functools.reduce(jnp.add, ps)
        p = jnp.concatenate([blk.astype(BF16) for blk in ps], axis=1)
        acc_ref[rows] = acc_ref[rows] + _dot(p, v)
        return
    mx = functools.reduce(jnp.maximum, blocks)
    m_prev = m_ref[rows]
    m_new = jnp.maximum(m_prev, jnp.max(mx, axis=-1, keepdims=True))
    alpha = jnp.exp2(m_prev - m_new)
    ps = [jnp.exp2(blk - m_new) for blk in blocks]
    l_ref[rows] = alpha * l_ref[rows] + jnp.sum(functools.reduce(jnp.add, ps), axis=-1, keepdims=True)
    p = jnp.concatenate([blk.astype(BF16) for blk in ps], axis=1)
    acc_ref[rows] = alpha * acc_ref[rows] + _dot(p, v)
    m_ref[rows] = m_new


def _softmax_finish(rows, l_ref, acc_ref, fixed_shift):
    l = l_ref[rows]
    if fixed_shift:
        l = jnp.sum(l, axis=-1, keepdims=True)
    return acc_ref[rows] / l


def _dsa_kernel(qt_ref, kt_ref, q_ref, k_ref, v_ref, sc_ref, th_ref, shift_ref, o_ref, m_ref, l_ref,
                acc_ref, bias_ref, qs_ref, *, tq, tk, fixed_shift):
    step = pl.program_id(1)
    qi = qt_ref[step]
    ki = kt_ref[step]

    @pl.when(ki == 0)
    def _():
        m_ref[...] = jnp.full(m_ref.shape, M_INIT, F32)
        l_ref[...] = jnp.zeros(l_ref.shape, F32)
        acc_ref[...] = jnp.zeros(acc_ref.shape, F32)
        for h in range(A_HEADS):
            qs_ref[h * tq:(h + 1) * tq, :] = q_ref[0, :, h * LANES:(h + 1) * LANES]

    bias_ref[...] = jnp.where(sc_ref[0] >= th_ref[0][:, :1], -shift_ref[:, :1], NEG)
    grp = A_HEADS // A_KV_HEADS
    for g in range(A_KV_HEADS):
        rows = slice(g * grp * tq, (g + 1) * grp * tq)
        k = k_ref[0, :, g * LANES:(g + 1) * LANES]
        v = v_ref[0, :, g * LANES:(g + 1) * LANES]
        s = _dot_nt(qs_ref[rows, :], k)
        s = jnp.concatenate([s[j * tq:(j + 1) * tq] + bias_ref[...] for j in range(grp)], axis=0)
        _softmax_step(s, v, rows, m_ref, l_ref, acc_ref, fixed_shift)

    @pl.when(ki == (qi * tq + tq - 1) // tk)
    def _():
        for h in range(A_HEADS):
            out = _softmax_finish(slice(h * tq, (h + 1) * tq), l_ref, acc_ref, fixed_shift)
            o_ref[0, :, h * LANES:(h + 1) * LANES] = out.astype(o_ref.dtype)


def _dsa_attn(q, k, v, scores, th, shift, tq, tk, fixed_shift):
    b, s, dq = q.shape
    qt, kt = _causal_pairs(s, tq, tk)
    grid_spec = pltpu.PrefetchScalarGridSpec(
        num_scalar_prefetch=2,
        grid=(b, qt.shape[0]),
        in_specs=[
            pl.BlockSpec((1, tq, dq), lambda bi, p, qt, kt: (bi, qt[p], 0)),
            pl.BlockSpec((1, tk, k.shape[2]), lambda bi, p, qt, kt: (bi, kt[p], 0)),
            pl.BlockSpec((1, tk, v.shape[2]), lambda bi, p, qt, kt: (bi, kt[p], 0)),
            pl.BlockSpec((1, tq, tk), lambda bi, p, qt, kt: (bi, qt[p], kt[p])),
            pl.BlockSpec((1, tq, LANES), lambda bi, p, qt, kt: (bi, qt[p], 0)),
            pl.BlockSpec((1, LANES), lambda bi, p, qt, kt: (0, 0)),
        ],
        out_specs=pl.BlockSpec((1, tq, dq), lambda bi, p, qt, kt: (bi, qt[p], 0)),
        scratch_shapes=[
            pltpu.VMEM((A_HEADS * tq, LANES), F32),
            pltpu.VMEM((A_HEADS * tq, LANES), F32),
            pltpu.VMEM((A_HEADS * tq, A_HEAD_DIM), F32),
            pltpu.VMEM((tq, tk), F32),
            pltpu.VMEM((A_HEADS * tq, A_HEAD_DIM), BF16),
        ],
    )
    return pl.pallas_call(
        functools.partial(_dsa_kernel, tq=tq, tk=tk, fixed_shift=fixed_shift),
        name="dsa_attn_fixed" if fixed_shift else "dsa_attn_online",
        grid_spec=grid_spec,
        out_shape=jax.ShapeDtypeStruct((b, s, dq), BF16),
        compiler_params=_cparams(("parallel", "arbitrary")),
    )(qt, kt, q, k, v, scores, th, shift)


def _mla_kernel(q_ref, k_ref, v_ref, o_ref, m_ref, l_ref, acc_ref, *, tq, hg, fixed_shift):
    qi = pl.program_id(2)
    m_ref[...] = jnp.full(m_ref.shape, M_INIT, F32)
    l_ref[...] = jnp.zeros(l_ref.shape, F32)
    acc_ref[...] = jnp.zeros(acc_ref.shape, F32)

    def tile(k0, masked):
        if masked:
            keep = (lax.broadcasted_iota(jnp.int32, (tq, tq), 1)
                    <= lax.broadcasted_iota(jnp.int32, (tq, tq), 0))
        for h in range(hg):
            q = q_ref[0, :, h * 256:(h + 1) * 256]
            k = k_ref[0, pl.ds(k0, tq), h * 256:(h + 1) * 256]
            v = v_ref[0, pl.ds(k0, tq), h * B_V:(h + 1) * B_V]
            s = _dot_nt(q, k)
            if masked:
                s = jnp.where(keep, s, NEG)
            _softmax_step(s, v, slice(h * tq, (h + 1) * tq), m_ref, l_ref, acc_ref, fixed_shift)

    def below_diagonal(kp, carry):
        tile(pl.multiple_of(2 * kp * tq, tq), False)
        tile(pl.multiple_of((2 * kp + 1) * tq, tq), False)
        return carry

    lax.fori_loop(0, qi // 2, below_diagonal, 0)

    @pl.when(qi % 2 == 1)
    def _():
        tile(pl.multiple_of((qi - 1) * tq, tq), False)

    tile(pl.multiple_of(qi * tq, tq), True)
    for h in range(hg):
        out = _softmax_finish(slice(h * tq, (h + 1) * tq), l_ref, acc_ref, fixed_shift)
        o_ref[0, :, h * B_V:(h + 1) * B_V] = out.astype(o_ref.dtype)


def _mla_attn(q, k, v, tq, hg, fixed_shift):
    b, s, _ = q.shape
    return pl.pallas_call(
        functools.partial(_mla_kernel, tq=tq, hg=hg, fixed_shift=fixed_shift),
        name="mla_attn_fixed" if fixed_shift else "mla_attn_online",
        grid=(b, B_HEADS // hg, s // tq),
        in_specs=[
            pl.BlockSpec((1, tq, hg * 256), lambda bi, g, qi: (bi, qi, g)),
            pl.BlockSpec((1, s, hg * 256), lambda bi, g, qi: (bi, 0, g)),
            pl.BlockSpec((1, s, hg * B_V), lambda bi, g, qi: (bi, 0, g)),
        ],
        out_specs=pl.BlockSpec((1, tq, hg * B_V), lambda bi, g, qi: (bi, qi, g)),
        out_shape=jax.ShapeDtypeStruct((b, s, B_HEADS * B_V), BF16),
        scratch_shapes=[
            pltpu.VMEM((hg * tq, LANES), F32),
            pltpu.VMEM((hg * tq, LANES), F32),
            pltpu.VMEM((hg * tq, B_V), F32),
        ],
        compiler_params=_cparams(("parallel", "parallel", "arbitrary")),
    )(q, k, v)


def _regroup_w_in(w):
    sizes = (A_HEADS * A_HEAD_DIM, A_KV_HEADS * A_HEAD_DIM, A_KV_HEADS * A_HEAD_DIM,
             IDX_HEADS * IDX_DIM, IDX_DIM, IDX_HEADS, B_Q_LORA, B_KV_LORA, B_ROPE)
    offs = np.cumsum((0,) + sizes)
    w = w.astype(BF16)
    qa, ka, va, qi, ki, wi, cq, ckv, kr = (w[:, offs[j]:offs[j + 1]] for j in range(len(sizes)))
    gates = w[:, offs[-1]:]
    d = w.shape[0]
    kiw = jnp.concatenate([ki, wi, jnp.zeros((d, LANES - IDX_DIM - IDX_HEADS), w.dtype)], axis=1)
    krp = jnp.concatenate([kr, jnp.zeros((d, LANES - B_ROPE), w.dtype)], axis=1)
    return jnp.concatenate([qa, gates, qi, ka, va, cq, ckv, kiw, krp], axis=1).astype(BF16)


def _softmax_shift(d, g_q, g_k):
    bound = (d ** 0.5) * LOG2E * 1.02 * jnp.max(jnp.abs(g_q)) * jnp.max(jnp.abs(g_k))
    c = jnp.ceil(bound.astype(F32))
    ok = c <= MAX_FIXED_SHIFT
    return jnp.full((1, LANES), jnp.where(ok, c, 0.0), F32), ok


def _rope_tables(pos, d):
    half = d // 2
    freqs = jnp.power(jnp.float32(ROPE_THETA), -jnp.arange(half, dtype=F32) * (2.0 / d))
    ang = pos.astype(F32)[..., None] * freqs
    cos, sin = jnp.cos(ang), jnp.sin(ang)
    reps = LANES // d
    c = jnp.tile(jnp.concatenate([cos, cos], axis=-1), (1, 1, reps))
    s = jnp.tile(jnp.concatenate([-sin, sin], axis=-1), (1, 1, reps))
    return c.reshape(-1, LANES), s.reshape(-1, LANES)


def _layer(x, p, positions, g_mix_norm, w_in, g_qa, g_ka, g_cq, w_uq, g_ckv, w_ukv, g_qb, g_kb,
           w_out_a, w_out_b, w_o, g_ffn_norm, w_ffn_gate, w_ffn_up, w_ffn_down, g_ple_norm,
           w_ple_gate, w_ple_proj):
    b, s, d = x.shape
    m = b * s
    k_sel = min(TOPK_MAX, s // 4)
    xf = x.reshape(m, d)
    row = lambda g: g.reshape(1, -1).astype(F32)

    c128, s128 = _rope_tables(positions, A_HEAD_DIM)
    c64, s64 = _rope_tables(positions, IDX_DIM)

    proj = _norm_matmul("in_proj", xf, 0, d, row(g_mix_norm), _regroup_w_in(w_in), F32, tm=1024, tn=1536)

    qa_r, ka_r, va_b, qi_r, klo, khi, wi = _prep_a(
        proj, c128, s128, c64, s64, row(g_qa) * (A_HEAD_DIM ** -0.5 * LOG2E), row(g_ka), tp=512)
    r3 = lambda a: a.reshape(b, s, a.shape[-1])
    scores, th = _indexer(r3(qi_r), r3(klo), r3(khi), r3(wi), tq=512, tk=512, k_sel=k_sel)
    shift_a, fixed_a = _softmax_shift(A_HEAD_DIM, g_qa, g_ka)
    y_a = lax.cond(
        fixed_a,
        functools.partial(_dsa_attn, tq=512, tk=512, fixed_shift=True),
        functools.partial(_dsa_attn, tq=512, tk=512, fixed_shift=False),
        r3(qa_r), r3(ka_r), r3(va_b), scores, th, shift_a)

    w_uq_r = w_uq.reshape(B_Q_LORA, B_HEADS, B_QK)
    w_uq_r = jnp.concatenate([w_uq_r[:, :, :B_NOPE].reshape(B_Q_LORA, -1),
                              w_uq_r[:, :, B_NOPE:].reshape(B_Q_LORA, -1)], axis=1).astype(BF16)
    w_ukv_r = w_ukv.reshape(B_KV_LORA, B_HEADS, 2, B_NOPE)
    w_ukv_r = jnp.concatenate([w_ukv_r[:, :, 0, :].reshape(B_KV_LORA, -1),
                               w_ukv_r[:, :, 1, :].reshape(B_KV_LORA, -1)], axis=1).astype(BF16)
    pe2 = lambda g: jnp.tile(g[B_NOPE:], 2).reshape(1, LANES).astype(F32)
    shift_b, fixed_b = _softmax_shift(B_QK, g_qb, g_kb)
    q_m = _mla_q(proj, row(g_cq), w_uq_r, c64, s64, row(g_qb[:B_NOPE]), pe2(g_qb), tm=512)
    k_m, v_m = _mla_kv(proj, row(g_ckv), w_ukv_r, c64, s64, row(g_kb[:B_NOPE]), pe2(g_kb), shift_b, tm=512)
    y_b = lax.cond(
        fixed_b,
        functools.partial(_mla_attn, tq=512, hg=2, fixed_shift=True),
        functools.partial(_mla_attn, tq=512, hg=2, fixed_shift=False),
        r3(q_m), r3(k_m), r3(v_m))

    mixed = _merge(y_a.reshape(m, -1), y_b.reshape(m, -1), w_out_a.astype(BF16), w_out_b.astype(BF16),
                   proj, tm=1024, tn=512)
    x1 = _matmul_res("w_o", mixed, w_o.astype(BF16), xf, tm=1024, tn=1024)

    act = _ffn_up(x1, row(g_ffn_norm), w_ffn_gate.astype(BF16), w_ffn_up.astype(BF16), tm=1024, tn=512)
    x2 = _matmul_res("ffn_down", act, w_ffn_down.astype(BF16), x1, tm=512, tn=1024)

    out = _ple(x2, row(g_ple_norm), w_ple_gate.astype(BF16), p.reshape(m, -1),
               w_ple_proj.astype(BF16), tm=1024, tn=512)
    return out.reshape(b, s, d)


def kernel(x, p, positions, g_mix_norm, w_in, g_qa, g_ka, g_cq, w_uq, g_ckv, w_ukv, g_qb, g_kb,
           w_out_a, w_out_b, w_o, g_ffn_norm, w_ffn_gate, w_ffn_up, w_ffn_down, g_ple_norm,
           w_ple_gate, w_ple_proj):
    for i in range(p.shape[0]):
        x = _layer(x, p[i], positions, g_mix_norm[i], w_in[i], g_qa[i], g_ka[i], g_cq[i], w_uq[i],
                   g_ckv[i], w_ukv[i], g_qb[i], g_kb[i], w_out_a[i], w_out_b[i], w_o[i],
                   g_ffn_norm[i], w_ffn_gate[i], w_ffn_up[i], w_ffn_down[i], g_ple_norm[i],
                   w_ple_gate[i], w_ple_proj[i])
    return x
```

```python
import functools

import numpy as np
import jax
import jax.numpy as jnp
from jax import lax
from jax.experimental import pallas as pl
from jax.experimental.pallas import tpu as pltpu

F32 = jnp.float32
BF16 = jnp.bfloat16

ROPE_THETA = 10000.0
NORM_EPS = 1e-6
NEG = -1e30
M_INIT = -1e29
POS_BIG = 3e38
TH_ALL = 0.5 * NEG
LOG2E = 1.4426950408889634
MAX_SEARCH_STEPS = 512
UNKNOWN_COUNT = 1e9
CNT_ROWS = 128
MAX_FIXED_SHIFT = 60.0
A_HEADS = 16
A_KV_HEADS = 4
A_HEAD_DIM = 128
IDX_HEADS = 16
IDX_DIM = 64
TOPK_MAX = 256
B_HEADS = 16
B_Q_LORA = 512
B_KV_LORA = 256
B_NOPE = 128
B_ROPE = 64
B_V = 128
B_QK = B_NOPE + B_ROPE
LANES = 128
VMEM_LIMIT = 56 * 1024 * 1024

COL_QA = 0
COL_GATE_A = 2048
COL_GATE_B = 4096
COL_QI = 6144
COL_KA = 7168
COL_VA = 7680
COL_CQ = 8192
COL_CKV = 8704
COL_KIW = 8960
COL_KR = 9088
N_PROJ = 9216


def _dot(a, b):
    return jnp.dot(a, b, preferred_element_type=F32)


def _dot_nt(a, b):
    return lax.dot_general(a, b, (((1,), (1,)), ((), ())), preferred_element_type=F32)


def _cparams(sem):
    return pltpu.CompilerParams(dimension_semantics=sem, vmem_limit_bytes=VMEM_LIMIT)


def _rms_to_bf16(x, g):
    ms = jnp.mean(x * x, axis=-1, keepdims=True)
    return (x * lax.rsqrt(ms + NORM_EPS) * g).astype(BF16)


def _norm_matmul_kernel(x_ref, g_ref, w_ref, o_ref, hn_ref):
    @pl.when(pl.program_id(1) == 0)
    def _():
        hn_ref[...] = _rms_to_bf16(x_ref[...], g_ref[...])

    o_ref[...] = _dot(hn_ref[...], w_ref[...]).astype(o_ref.dtype)


def _norm_matmul(name, x, x_col_block, k, g, w, out_dtype, tm, tn):
    m = x.shape[0]
    n = w.shape[1]
    return pl.pallas_call(
        _norm_matmul_kernel,
        name=name,
        grid=(m // tm, n // tn),
        in_specs=[
            pl.BlockSpec((tm, k), lambda i, j: (i, x_col_block)),
            pl.BlockSpec((1, k), lambda i, j: (0, 0)),
            pl.BlockSpec((k, tn), lambda i, j: (0, j)),
        ],
        out_specs=pl.BlockSpec((tm, tn), lambda i, j: (i, j)),
        out_shape=jax.ShapeDtypeStruct((m, n), out_dtype),
        scratch_shapes=[pltpu.VMEM((tm, k), BF16)],
        compiler_params=_cparams(("parallel", "arbitrary")),
    )(x, g, w)


def _ffn_up_kernel(x_ref, g_ref, wg_ref, wu_ref, o_ref, hn_ref):
    @pl.when(pl.program_id(1) == 0)
    def _():
        hn_ref[...] = _rms_to_bf16(x_ref[...], g_ref[...])

    h = hn_ref[...]
    a = _dot(h, wg_ref[...])
    b = _dot(h, wu_ref[...])
    o_ref[...] = (a * jax.nn.sigmoid(a) * b).astype(o_ref.dtype)


def _ffn_up(x, g, wg, wu, tm, tn):
    m, k = x.shape
    n = wg.shape[1]
    return pl.pallas_call(
        _ffn_up_kernel,
        name="ffn_up",
        grid=(m // tm, n // tn),
        in_specs=[
            pl.BlockSpec((tm, k), lambda i, j: (i, 0)),
            pl.BlockSpec((1, k), lambda i, j: (0, 0)),
            pl.BlockSpec((k, tn), lambda i, j: (0, j)),
            pl.BlockSpec((k, tn), lambda i, j: (0, j)),
        ],
        out_specs=pl.BlockSpec((tm, tn), lambda i, j: (i, j)),
        out_shape=jax.ShapeDtypeStruct((m, n), BF16),
        scratch_shapes=[pltpu.VMEM((tm, k), BF16)],
        compiler_params=_cparams(("parallel", "arbitrary")),
    )(x, g, wg, wu)


def _matmul_res_kernel(a_ref, w_ref, r_ref, o_ref):
    o_ref[...] = r_ref[...] + _dot(a_ref[...], w_ref[...])


def _matmul_res(name, a, w, res, tm, tn):
    m, k = a.shape
    n = w.shape[1]
    return pl.pallas_call(
        _matmul_res_kernel,
        name=name,
        grid=(n // tn, m // tm),
        in_specs=[
            pl.BlockSpec((tm, k), lambda j, i: (i, 0)),
            pl.BlockSpec((k, tn), lambda j, i: (0, j)),
            pl.BlockSpec((tm, tn), lambda j, i: (i, j)),
        ],
        out_specs=pl.BlockSpec((tm, tn), lambda j, i: (i, j)),
        out_shape=jax.ShapeDtypeStruct((m, n), F32),
        compiler_params=_cparams(("parallel", "parallel")),
    )(a, w, res)


def _merge_kernel(ya_ref, yb_ref, wa_ref, wb_ref, ga_ref, gb_ref, o_ref):
    a = _dot(ya_ref[...], wa_ref[...])
    b = _dot(yb_ref[...], wb_ref[...])
    o_ref[...] = (jax.nn.sigmoid(ga_ref[...]) * a + jax.nn.sigmoid(gb_ref[...]) * b).astype(o_ref.dtype)


def _merge(ya, yb, wa, wb, proj, tm, tn):
    m, k = ya.shape
    n = wa.shape[1]
    ga0 = COL_GATE_A // tn
    gb0 = COL_GATE_B // tn
    return pl.pallas_call(
        _merge_kernel,
        name="merge",
        grid=(m // tm, n // tn),
        in_specs=[
            pl.BlockSpec((tm, k), lambda i, j: (i, 0)),
            pl.BlockSpec((tm, k), lambda i, j: (i, 0)),
            pl.BlockSpec((k, tn), lambda i, j: (0, j)),
            pl.BlockSpec((k, tn), lambda i, j: (0, j)),
            pl.BlockSpec((tm, tn), lambda i, j: (i, ga0 + j)),
            pl.BlockSpec((tm, tn), lambda i, j: (i, gb0 + j)),
        ],
        out_specs=pl.BlockSpec((tm, tn), lambda i, j: (i, j)),
        out_shape=jax.ShapeDtypeStruct((m, n), BF16),
        compiler_params=_cparams(("parallel", "parallel")),
    )(ya, yb, wa, wb, proj, proj)


def _ple_kernel(x_ref, g_ref, wg_ref, p_ref, wp_ref, xr_ref, o_ref, hn_ref):
    @pl.when(pl.program_id(1) == 0)
    def _():
        hn_ref[...] = _rms_to_bf16(x_ref[...], g_ref[...])

    gate = jax.nn.sigmoid(_dot(hn_ref[...], wg_ref[...]))
    emb = _dot(p_ref[...].astype(BF16), wp_ref[...])
    o_ref[...] = xr_ref[...] + gate * emb


def _ple(x, g, wg, p, wp, tm, tn):
    m, k = x.shape
    n = wg.shape[1]
    kp = p.shape[1]
    return pl.pallas_call(
        _ple_kernel,
        name="ple",
        grid=(m // tm, n // tn),
        in_specs=[
            pl.BlockSpec((tm, k), lambda i, j: (i, 0)),
            pl.BlockSpec((1, k), lambda i, j: (0, 0)),
            pl.BlockSpec((k, tn), lambda i, j: (0, j)),
            pl.BlockSpec((tm, kp), lambda i, j: (i, 0)),
            pl.BlockSpec((kp, tn), lambda i, j: (0, j)),
            pl.BlockSpec((tm, tn), lambda i, j: (i, j)),
        ],
        out_specs=pl.BlockSpec((tm, tn), lambda i, j: (i, j)),
        out_shape=jax.ShapeDtypeStruct((m, n), F32),
        scratch_shapes=[pltpu.VMEM((tm, k), BF16)],
        compiler_params=_cparams(("parallel", "arbitrary")),
    )(x, g, wg, p, wp, x)


def _rope64_pairs(x, c64, s64, first_half):
    partner = jnp.where(first_half, pltpu.roll(x, 96, 1), pltpu.roll(x, 32, 1))
    return x * c64 + partner * s64


def _prep_a_kernel(qa_ref, ka_ref, va_ref, qi_ref, kiw_ref, c128_ref, s128_ref, c64_ref, s64_ref,
                   gq_ref, gk_ref, qa_o, ka_o, va_o, qi_o, klo_o, khi_o, wi_o, *, wi_scale):
    c128 = c128_ref[...]
    s128 = s128_ref[...]

    def norm_rope(x, g):
        ms = jnp.mean(x * x, axis=-1, keepdims=True)
        y = x * lax.rsqrt(ms + NORM_EPS) * g
        return y * c128 + pltpu.roll(y, 64, 1) * s128

    for h in range(A_HEADS):
        hs = slice(h * LANES, (h + 1) * LANES)
        qa_o[:, hs] = norm_rope(qa_ref[:, hs], gq_ref[...]).astype(BF16)
    for h in range(A_KV_HEADS):
        hs = slice(h * LANES, (h + 1) * LANES)
        ka_o[:, hs] = norm_rope(ka_ref[:, hs], gk_ref[...]).astype(BF16)
    va_o[...] = va_ref[...].astype(BF16)

    c64 = c64_ref[...]
    s64 = s64_ref[...]
    lane = lax.broadcasted_iota(jnp.int32, c64.shape, 1)
    first_half = (lane % 64) < 32
    for p in range(IDX_HEADS * IDX_DIM // LANES):
        ps = slice(p * LANES, (p + 1) * LANES)
        qi_o[:, ps] = _rope64_pairs(qi_ref[:, ps], c64, s64, first_half).astype(BF16)
    kiw = kiw_ref[...]
    ki = jnp.where(lane < 64, _rope64_pairs(kiw, c64, s64, first_half), 0.0)
    klo_o[...] = ki.astype(BF16)
    khi_o[...] = pltpu.roll(ki, 64, 1).astype(BF16)
    wi_o[...] = kiw * wi_scale


def _prep_a(proj, c128, s128, c64, s64, gq, gk, tp):
    m = proj.shape[0]
    row = lambda w, cb: pl.BlockSpec((tp, w), lambda i: (i, cb))
    vec = pl.BlockSpec((1, LANES), lambda i: (0, 0))
    out = lambda w: pl.BlockSpec((tp, w), lambda i: (i, 0))
    wi_scale = (IDX_HEADS ** -0.5) * (IDX_DIM ** -0.5)
    return pl.pallas_call(
        functools.partial(_prep_a_kernel, wi_scale=wi_scale),
        name="prep_a",
        grid=(m // tp,),
        in_specs=[
            row(2048, COL_QA // 2048), row(512, COL_KA // 512), row(512, COL_VA // 512),
            row(1024, COL_QI // 1024), row(LANES, COL_KIW // LANES),
            out(LANES), out(LANES), out(LANES), out(LANES), vec, vec,
        ],
        out_specs=[out(2048), out(512), out(512), out(1024), out(LANES), out(LANES), out(LANES)],
        out_shape=[
            jax.ShapeDtypeStruct((m, 2048), BF16), jax.ShapeDtypeStruct((m, 512), BF16),
            jax.ShapeDtypeStruct((m, 512), BF16), jax.ShapeDtypeStruct((m, 1024), BF16),
            jax.ShapeDtypeStruct((m, LANES), BF16), jax.ShapeDtypeStruct((m, LANES), BF16),
            jax.ShapeDtypeStruct((m, LANES), F32),
        ],
        compiler_params=_cparams(("parallel",)),
    )(proj, proj, proj, proj, proj, c128, s128, c64, s64, gq, gk)


def _mla_q_kernel(cq_ref, g_ref, w_ref, c64_ref, s64_ref, gqn_ref, gqp_ref, q_o, *, q_scale):
    qb = _dot(_rms_to_bf16(cq_ref[...], g_ref[...]), w_ref[...])
    c64 = c64_ref[...]
    s64 = s64_ref[...]
    lane = lax.broadcasted_iota(jnp.int32, c64.shape, 1)
    first_half = (lane % 64) < 32
    lo = lane < 64
    nope_w = B_HEADS * B_NOPE
    for pair in range(B_HEADS // 2):
        qp2 = qb[:, nope_w + pair * LANES:nope_w + (pair + 1) * LANES]
        qp2_sq = qp2 * qp2
        rot2 = _rope64_pairs(qp2 * gqp_ref[...], c64, s64, first_half)
        for h in (2 * pair, 2 * pair + 1):
            mine = lo if h % 2 == 0 else jnp.logical_not(lo)
            shift_lane = lane == (64 if h % 2 == 0 else 0)
            qn = qb[:, h * LANES:(h + 1) * LANES]
            ss = jnp.sum(qn * qn + jnp.where(mine, qp2_sq, 0.0), axis=-1, keepdims=True)
            r = lax.rsqrt(ss * (1.0 / B_QK) + NORM_EPS) * q_scale
            q_o[:, 2 * h * LANES:(2 * h + 1) * LANES] = (qn * r * gqn_ref[...]).astype(BF16)
            qpe = jnp.where(mine, rot2 * r, jnp.where(shift_lane, 1.0, 0.0))
            q_o[:, (2 * h + 1) * LANES:(2 * h + 2) * LANES] = qpe.astype(BF16)


def _mla_kv_kernel(ckv_ref, g_ref, w_ref, kr_ref, c64_ref, s64_ref, gkn_ref, gkp_ref, shift_ref, k_o, v_o):
    kv = _dot(_rms_to_bf16(ckv_ref[...], g_ref[...]), w_ref[...])
    c64 = c64_ref[...]
    s64 = s64_ref[...]
    lane = lax.broadcasted_iota(jnp.int32, c64.shape, 1)
    first_half = (lane % 64) < 32
    lo = lane < 64
    kr = jnp.where(lo, kr_ref[...], 0.0)
    kr_ss = jnp.sum(kr * kr, axis=-1, keepdims=True)
    kpe_lo = jnp.where(lo, _rope64_pairs(kr * gkp_ref[...], c64, s64, first_half), 0.0)
    kpe_hi = pltpu.roll(kpe_lo, 64, 1)
    nope_w = B_HEADS * B_NOPE
    for h in range(B_HEADS):
        hs = slice(h * LANES, (h + 1) * LANES)
        shift_lane = lane == (64 if h % 2 == 0 else 0)
        kn = kv[:, hs]
        rk = lax.rsqrt((jnp.sum(kn * kn, axis=-1, keepdims=True) + kr_ss) * (1.0 / B_QK) + NORM_EPS)
        k_o[:, 2 * h * LANES:(2 * h + 1) * LANES] = (kn * rk * gkn_ref[...]).astype(BF16)
        kpe = kpe_lo if h % 2 == 0 else kpe_hi
        k_o[:, (2 * h + 1) * LANES:(2 * h + 2) * LANES] = jnp.where(
            shift_lane, -shift_ref[...], kpe * rk).astype(BF16)
        v_o[:, hs] = kv[:, nope_w + h * LANES:nope_w + (h + 1) * LANES].astype(BF16)


def _mla_q(proj, g_cq, w_uq, c64, s64, gqn, gqp, tm):
    m = proj.shape[0]
    vec = lambda w: pl.BlockSpec((1, w), lambda i: (0, 0))
    rows = lambda w: pl.BlockSpec((tm, w), lambda i: (i, 0))
    return pl.pallas_call(
        functools.partial(_mla_q_kernel, q_scale=B_QK ** -0.5 * LOG2E),
        name="mla_q",
        grid=(m // tm,),
        in_specs=[
            pl.BlockSpec((tm, B_Q_LORA), lambda i: (i, COL_CQ // B_Q_LORA)),
            vec(B_Q_LORA),
            pl.BlockSpec(w_uq.shape, lambda i: (0, 0)),
            rows(LANES), rows(LANES), vec(LANES), vec(LANES),
        ],
        out_specs=rows(B_HEADS * 256),
        out_shape=jax.ShapeDtypeStruct((m, B_HEADS * 256), BF16),
        compiler_params=_cparams(("parallel",)),
    )(proj, g_cq, w_uq, c64, s64, gqn, gqp)


def _mla_kv(proj, g_ckv, w_ukv, c64, s64, gkn, gkp, shift, tm):
    m = proj.shape[0]
    vec = lambda w: pl.BlockSpec((1, w), lambda i: (0, 0))
    rows = lambda w: pl.BlockSpec((tm, w), lambda i: (i, 0))
    return pl.pallas_call(
        _mla_kv_kernel,
        name="mla_kv",
        grid=(m // tm,),
        in_specs=[
            pl.BlockSpec((tm, B_KV_LORA), lambda i: (i, COL_CKV // B_KV_LORA)),
            vec(B_KV_LORA),
            pl.BlockSpec(w_ukv.shape, lambda i: (0, 0)),
            pl.BlockSpec((tm, LANES), lambda i: (i, COL_KR // LANES)),
            rows(LANES), rows(LANES), vec(LANES), vec(LANES), vec(LANES),
        ],
        out_specs=[rows(B_HEADS * 256), rows(B_HEADS * B_V)],
        out_shape=[
            jax.ShapeDtypeStruct((m, B_HEADS * 256), BF16),
            jax.ShapeDtypeStruct((m, B_HEADS * B_V), BF16),
        ],
        compiler_params=_cparams(("parallel",)),
    )(proj, g_ckv, w_ukv, proj, c64, s64, gkn, gkp, shift)


def _indexer_kernel(q_ref, klo_ref, khi_ref, w_ref, sc_ref, th_ref, *, tq, tk, seq, k_sel):
    qi = pl.program_id(1)
    n_vis = (qi * tq) // tk + 1
    n_all = seq // tk
    row = qi * tq + lax.broadcasted_iota(jnp.int32, (tq, 1), 0)
    wcols = [w_ref[0, :, IDX_DIM + h:IDX_DIM + h + 1] for h in range(IDX_HEADS)]

    def gen(c, carry):
        top1, top2, rmin = carry
        off = pl.multiple_of(c * tk, tk)
        kl = klo_ref[0, pl.ds(off, tk), :]
        kh = khi_ref[0, pl.ds(off, tk), :]
        acc = jnp.zeros((tq, tk), F32)
        for p in range(IDX_HEADS // 2):
            qp = q_ref[0, :, p * LANES:(p + 1) * LANES]
            acc = acc + wcols[2 * p] * jnp.maximum(_dot_nt(qp, kl), 0.0)
            acc = acc + wcols[2 * p + 1] * jnp.maximum(_dot_nt(qp, kh), 0.0)
        col = off + lax.broadcasted_iota(jnp.int32, (tq, tk), 1)
        vis = col <= row
        masked = jnp.where(vis, acc, NEG)
        sc_ref[0, :, pl.ds(off, tk)] = masked
        for j in range(tk // LANES):
            x = masked[:, j * LANES:(j + 1) * LANES]
            top2 = jnp.maximum(top2, jnp.minimum(top1, x))
            top1 = jnp.maximum(top1, x)
        rmin = jnp.minimum(rmin, jnp.min(jnp.where(vis, acc, POS_BIG), axis=-1, keepdims=True))
        return top1, top2, rmin

    neg_tile = jnp.full((tq, LANES), NEG, F32)
    top1, top2, rmin = lax.fori_loop(0, n_vis, gen, (neg_tile, neg_tile, jnp.full((tq, 1), POS_BIG, F32)))
    rmax = jnp.max(top1, axis=-1, keepdims=True)
    low2 = jnp.min(top2, axis=-1, keepdims=True)

    def fill(c, carry):
        sc_ref[0, :, pl.ds(pl.multiple_of(c * tk, tk), tk)] = jnp.full((tq, tk), NEG, F32)
        return carry

    lax.fori_loop(n_vis, n_all, fill, 0)

    cw = min(2 * tk, seq)
    n_cnt = (n_vis * tk + cw - 1) // cw

    def count_ge(mid, strict=False, open_groups=None):
        zeros = jnp.zeros((CNT_ROWS, LANES), F32)
        accs = []
        for r in range(tq // CNT_ROWS):
            rs = slice(r * CNT_ROWS, (r + 1) * CNT_ROWS)
            midb = mid[rs]

            def body(c, acc, rs=rs, midb=midb):
                off = pl.multiple_of(c * cw, cw)
                for j in range(cw // LANES):
                    blk = sc_ref[0, rs, pl.ds(off + j * LANES, LANES)]
                    acc = acc + jnp.where(blk > midb if strict else blk >= midb, 1.0, 0.0)
                return acc

            if open_groups is None:
                accs.append(lax.fori_loop(0, n_cnt, body, zeros))
            else:
                accs.append(lax.cond(open_groups[r] > 0.0,
                                     functools.partial(lax.fori_loop, 0, n_cnt, body, zeros),
                                     lambda: zeros))
        return jnp.concatenate(
            [jnp.broadcast_to(jnp.sum(a, axis=-1, keepdims=True), (CNT_ROWS, LANES)) for a in accs], axis=0)

    rowb = qi * tq + lax.broadcasted_iota(jnp.int32, (tq, LANES), 0)
    active0 = (rowb + 1) > k_sel
    have_low2 = jnp.broadcast_to(low2, (tq, LANES)) > TH_ALL
    lo0 = jnp.where(have_low2, jnp.broadcast_to(low2, (tq, LANES)), jnp.broadcast_to(rmin, (tq, LANES)))
    lo0 = jnp.where(active0, lo0, TH_ALL)
    hi0 = jnp.where(active0, jnp.broadcast_to(rmax, (tq, LANES)), TH_ALL)

    def n_open(lo, hi):
        still = jnp.where(lo < hi, 1.0, 0.0)
        return tuple(jnp.sum(still[r * CNT_ROWS:(r + 1) * CNT_ROWS]) for r in range(tq // CNT_ROWS))

    def cond(st):
        it, nopen = st[0], st[1]
        return jnp.logical_and(functools.reduce(jnp.add, nopen) > 0.0, it < MAX_SEARCH_STEPS)

    def body(st):
        it, was_open, lo, hi, cnt_lo = st
        mid = lo + (hi - lo) * 0.5
        cnt = count_ge(mid, open_groups=was_open)
        nopen = n_open(lo, hi)
        ge = cnt >= k_sel
        lo_n = jnp.where(ge, mid, lo)
        hi_n = jnp.where(ge, hi, mid)
        hi_n = jnp.where(cnt == k_sel, lo_n, hi_n)
        hi_n = jnp.where(mid >= hi, lo_n, hi_n)
        hi_n = jnp.where(mid <= lo, lo_n, hi_n)
        return it + 1, nopen, lo_n, hi_n, jnp.where(ge, cnt, cnt_lo)

    cnt_lo0 = jnp.where(have_low2, UNKNOWN_COUNT, (rowb + 1).astype(F32))
    st = lax.while_loop(cond, body, (jnp.int32(0), n_open(lo0, hi0), lo0, hi0, cnt_lo0))
    th = st[2]
    th_ref[0] = th

    tied = jnp.logical_and(active0, st[4] > k_sel)

    @pl.when(jnp.sum(jnp.where(tied, 1.0, 0.0)) > 0.0)
    def _():
        keep_ties = k_sel - count_ge(th, strict=True)
        reps = tk // LANES
        th_w = jnp.concatenate([th] * reps, axis=1)
        keep_w = jnp.concatenate([keep_ties] * reps, axis=1)
        earlier = (lax.broadcasted_iota(jnp.int32, (tk, tk), 0)
                   < lax.broadcasted_iota(jnp.int32, (tk, tk), 1))
        tri = jnp.where(earlier, 1.0, 0.0).astype(BF16)

        def demote(c, seen):
            off = pl.multiple_of(c * tk, tk)
            blk = sc_ref[0, :, pl.ds(off, tk)]
            eq = blk == th_w
            eqf = jnp.where(eq, 1.0, 0.0)
            rank = _dot(eqf.astype(BF16), tri) + jnp.concatenate([seen] * reps, axis=1)
            sc_ref[0, :, pl.ds(off, tk)] = jnp.where(jnp.logical_and(eq, rank >= keep_w), NEG, blk)
            return seen + jnp.sum(eqf, axis=-1, keepdims=True)

        lax.fori_loop(0, n_vis, demote, jnp.zeros((tq, LANES), F32))


def _indexer(qi_r, klo, khi, wi, tq, tk, k_sel):
    b, s, _ = qi_r.shape
    assert k_sel <= 2 * LANES
    return pl.pallas_call(
        functools.partial(_indexer_kernel, tq=tq, tk=tk, seq=s, k_sel=float(k_sel)),
        name="indexer",
        grid=(b, s // tq),
        in_specs=[
            pl.BlockSpec((1, tq, IDX_HEADS * IDX_DIM), lambda bi, qi: (bi, qi, 0)),
            pl.BlockSpec((1, s, LANES), lambda bi, qi: (bi, 0, 0)),
            pl.BlockSpec((1, s, LANES), lambda bi, qi: (bi, 0, 0)),
            pl.BlockSpec((1, tq, LANES), lambda bi, qi: (bi, qi, 0)),
        ],
        out_specs=[
            pl.BlockSpec((1, tq, s), lambda bi, qi: (bi, qi, 0)),
            pl.BlockSpec((1, tq, LANES), lambda bi, qi: (bi, qi, 0)),
        ],
        out_shape=[jax.ShapeDtypeStruct((b, s, s), F32), jax.ShapeDtypeStruct((b, s, LANES), F32)],
        compiler_params=_cparams(("parallel", "parallel")),
    )(qi_r, klo, khi, wi)


def _causal_pairs(s, tq, tk):
    qs, ks = [], []
    for qi in range(s // tq):
        for ki in range((qi * tq + tq - 1) // tk + 1):
            qs.append(qi)
            ks.append(ki)
    return jnp.asarray(np.array(qs, np.int32)), jnp.asarray(np.array(ks, np.int32))


def _softmax_step(s, v, rows, m_ref, l_ref, acc_ref, fixed_shift):
    blocks = [s[:, j * LANES:(j + 1) * LANES] for j in range(s.shape[1] // LANES)]
    if fixed_shift:
        ps = [jnp.exp2(blk) for blk in blocks]
        l_ref[rows] = l_ref[rows] + functools.reduce(jnp.add, ps)
        p = jnp.concatenate([blk.astype(BF16) for blk in ps], axis=1)
        acc_ref[rows] = acc_ref[rows] + _dot(p, v)
        return
    mx = functools.reduce(jnp.maximum, blocks)
    m_prev = m_ref[rows]
    m_new = jnp.maximum(m_prev, jnp.max(mx, axis=-1, keepdims=True))
    alpha = jnp.exp2(m_prev - m_new)
    ps = [jnp.exp2(blk - m_new) for blk in blocks]
    l_ref[rows] = alpha * l_ref[rows] + jnp.sum(functools.reduce(jnp.add, ps), axis=-1, keepdims=True)
    p = jnp.concatenate([blk.astype(BF16) for blk in ps], axis=1)
    acc_ref[rows] = alpha * acc_ref[rows] + _dot(p, v)
    m_ref[rows] = m_new


def _softmax_finish(rows, l_ref, acc_ref, fixed_shift):
    l = l_ref[rows]
    if fixed_shift:
        l = jnp.sum(l, axis=-1, keepdims=True)
    return acc_ref[rows] / l


def _dsa_kernel(qt_ref, kt_ref, q_ref, k_ref, v_ref, sc_ref, th_ref, shift_ref, o_ref, m_ref, l_ref,
                acc_ref, bias_ref, qs_ref, *, tq, tk, fixed_shift):
    step = pl.program_id(1)
    qi = qt_ref[step]
    ki = kt_ref[step]

    @pl.when(ki == 0)
    def _():
        m_ref[...] = jnp.full(m_ref.shape, M_INIT, F32)
        l_ref[...] = jnp.zeros(l_ref.shape, F32)
        acc_ref[...] = jnp.zeros(acc_ref.shape, F32)
        for h in range(A_HEADS):
            qs_ref[h * tq:(h + 1) * tq, :] = q_ref[0, :, h * LANES:(h + 1) * LANES]

    bias_ref[...] = jnp.where(sc_ref[0] >= th_ref[0][:, :1], -shift_ref[:, :1], NEG)
    grp = A_HEADS // A_KV_HEADS
    for g in range(A_KV_HEADS):
        rows = slice(g * grp * tq, (g + 1) * grp * tq)
        k = k_ref[0, :, g * LANES:(g + 1) * LANES]
        v = v_ref[0, :, g * LANES:(g + 1) * LANES]
        s = _dot_nt(qs_ref[rows, :], k)
        s = jnp.concatenate([s[j * tq:(j + 1) * tq] + bias_ref[...] for j in range(grp)], axis=0)
        _softmax_step(s, v, rows, m_ref, l_ref, acc_ref, fixed_shift)

    @pl.when(ki == (qi * tq + tq - 1) // tk)
    def _():
        for h in range(A_HEADS):
            out = _softmax_finish(slice(h * tq, (h + 1) * tq), l_ref, acc_ref, fixed_shift)
            o_ref[0, :, h * LANES:(h + 1) * LANES] = out.astype(o_ref.dtype)


def _dsa_attn(q, k, v, scores, th, shift, tq, tk, fixed_shift):
    b, s, dq = q.shape
    qt, kt = _causal_pairs(s, tq, tk)
    grid_spec = pltpu.PrefetchScalarGridSpec(
        num_scalar_prefetch=2,
        grid=(b, qt.shape[0]),
        in_specs=[
            pl.BlockSpec((1, tq, dq), lambda bi, p, qt, kt: (bi, qt[p], 0)),
            pl.BlockSpec((1, tk, k.shape[2]), lambda bi, p, qt, kt: (bi, kt[p], 0)),
            pl.BlockSpec((1, tk, v.shape[2]), lambda bi, p, qt, kt: (bi, kt[p], 0)),
            pl.BlockSpec((1, tq, tk), lambda bi, p, qt, kt: (bi, qt[p], kt[p])),
            pl.BlockSpec((1, tq, LANES), lambda bi, p, qt, kt: (bi, qt[p], 0)),
            pl.BlockSpec((1, LANES), lambda bi, p, qt, kt: (0, 0)),
        ],
        out_specs=pl.BlockSpec((1, tq, dq), lambda bi, p, qt, kt: (bi, qt[p], 0)),
        scratch_shapes=[
            pltpu.VMEM((A_HEADS * tq, LANES), F32),
            pltpu.VMEM((A_HEADS * tq, LANES), F32),
            pltpu.VMEM((A_HEADS * tq, A_HEAD_DIM), F32),
            pltpu.VMEM((tq, tk), F32),
            pltpu.VMEM((A_HEADS * tq, A_HEAD_DIM), BF16),
        ],
    )
    return pl.pallas_call(
        functools.partial(_dsa_kernel, tq=tq, tk=tk, fixed_shift=fixed_shift),
        name="dsa_attn_fixed" if fixed_shift else "dsa_attn_online",
        grid_spec=grid_spec,
        out_shape=jax.ShapeDtypeStruct((b, s, dq), BF16),
        compiler_params=_cparams(("parallel", "arbitrary")),
    )(qt, kt, q, k, v, scores, th, shift)


def _mla_kernel(qt_ref, kt_ref, q_ref, k_ref, v_ref, o_ref, m_ref, l_ref, acc_ref, *, tq, tk, hg,
                fixed_shift):
    step = pl.program_id(2)
    qi = qt_ref[step]
    ki = kt_ref[step]

    @pl.when(ki == 0)
    def _():
        m_ref[...] = jnp.full(m_ref.shape, M_INIT, F32)
        l_ref[...] = jnp.zeros(l_ref.shape, F32)
        acc_ref[...] = jnp.zeros(acc_ref.shape, F32)

    def heads(masked):
        if masked:
            keep = (lax.broadcasted_iota(jnp.int32, (tq, tk), 1)
                    <= lax.broadcasted_iota(jnp.int32, (tq, tk), 0))
        for h in range(hg):
            q = q_ref[0, :, h * 256:(h + 1) * 256]
            k = k_ref[0, :, h * 256:(h + 1) * 256]
            v = v_ref[0, :, h * B_V:(h + 1) * B_V]
            s = _dot_nt(q, k)
            if masked:
                s = jnp.where(keep, s, NEG)
            _softmax_step(s, v, slice(h * tq, (h + 1) * tq), m_ref, l_ref, acc_ref, fixed_shift)

    @pl.when(ki < qi)
    def _():
        heads(False)

    @pl.when(ki == qi)
    def _():
        heads(True)
        for h in range(hg):
            out = _softmax_finish(slice(h * tq, (h + 1) * tq), l_ref, acc_ref, fixed_shift)
            o_ref[0, :, h * B_V:(h + 1) * B_V] = out.astype(o_ref.dtype)


def _mla_attn(q, k, v, tq, tk, hg, fixed_shift):
    b, s, _ = q.shape
    assert tq == tk
    qt, kt = _causal_pairs(s, tq, tk)
    grid_spec = pltpu.PrefetchScalarGridSpec(
        num_scalar_prefetch=2,
        grid=(b, B_HEADS // hg, qt.shape[0]),
        in_specs=[
            pl.BlockSpec((1, tq, hg * 256), lambda bi, g, p, qt, kt: (bi, qt[p], g)),
            pl.BlockSpec((1, tk, hg * 256), lambda bi, g, p, qt, kt: (bi, kt[p], g)),
            pl.BlockSpec((1, tk, hg * B_V), lambda bi, g, p, qt, kt: (bi, kt[p], g)),
        ],
        out_specs=pl.BlockSpec((1, tq, hg * B_V), lambda bi, g, p, qt, kt: (bi, qt[p], g)),
        scratch_shapes=[
            pltpu.VMEM((hg * tq, LANES), F32),
            pltpu.VMEM((hg * tq, LANES), F32),
            pltpu.VMEM((hg * tq, B_V), F32),
        ],
    )
    return pl.pallas_call(
        functools.partial(_mla_kernel, tq=tq, tk=tk, hg=hg, fixed_shift=fixed_shift),
        name="mla_attn_fixed" if fixed_shift else "mla_attn_online",
        grid_spec=grid_spec,
        out_shape=jax.ShapeDtypeStruct((b, s, B_HEADS * B_V), BF16),
        compiler_params=_cparams(("parallel", "parallel", "arbitrary")),
    )(qt, kt, q, k, v)


def _regroup_w_in(w):
    sizes = (A_HEADS * A_HEAD_DIM, A_KV_HEADS * A_HEAD_DIM, A_KV_HEADS * A_HEAD_DIM,
             IDX_HEADS * IDX_DIM, IDX_DIM, IDX_HEADS, B_Q_LORA, B_KV_LORA, B_ROPE)
    offs = np.cumsum((0,) + sizes)
    w = w.astype(BF16)
    qa, ka, va, qi, ki, wi, cq, ckv, kr = (w[:, offs[j]:offs[j + 1]] for j in range(len(sizes)))
    gates = w[:, offs[-1]:]
    d = w.shape[0]
    kiw = jnp.concatenate([ki, wi, jnp.zeros((d, LANES - IDX_DIM - IDX_HEADS), w.dtype)], axis=1)
    krp = jnp.concatenate([kr, jnp.zeros((d, LANES - B_ROPE), w.dtype)], axis=1)
    return jnp.concatenate([qa, gates, qi, ka, va, cq, ckv, kiw, krp], axis=1).astype(BF16)


def _softmax_shift(d, g_q, g_k):
    bound = (d ** 0.5) * LOG2E * 1.02 * jnp.max(jnp.abs(g_q)) * jnp.max(jnp.abs(g_k))
    c = jnp.ceil(bound.astype(F32))
    ok = c <= MAX_FIXED_SHIFT
    return jnp.full((1, LANES), jnp.where(ok, c, 0.0), F32), ok


def _rope_tables(pos, d):
    half = d // 2
    freqs = jnp.power(jnp.float32(ROPE_THETA), -jnp.arange(half, dtype=F32) * (2.0 / d))
    ang = pos.astype(F32)[..., None] * freqs
    cos, sin = jnp.cos(ang), jnp.sin(ang)
    reps = LANES // d
    c = jnp.tile(jnp.concatenate([cos, cos], axis=-1), (1, 1, reps))
    s = jnp.tile(jnp.concatenate([-sin, sin], axis=-1), (1, 1, reps))
    return c.reshape(-1, LANES), s.reshape(-1, LANES)


def _layer(x, p, positions, g_mix_norm, w_in, g_qa, g_ka, g_cq, w_uq, g_ckv, w_ukv, g_qb, g_kb,
           w_out_a, w_out_b, w_o, g_ffn_norm, w_ffn_gate, w_ffn_up, w_ffn_down, g_ple_norm,
           w_ple_gate, w_ple_proj):
    b, s, d = x.shape
    m = b * s
    k_sel = min(TOPK_MAX, s // 4)
    xf = x.reshape(m, d)
    row = lambda g: g.reshape(1, -1).astype(F32)

    c128, s128 = _rope_tables(positions, A_HEAD_DIM)
    c64, s64 = _rope_tables(positions, IDX_DIM)

    proj = _norm_matmul("in_proj", xf, 0, d, row(g_mix_norm), _regroup_w_in(w_in), F32, tm=1024, tn=1536)

    qa_r, ka_r, va_b, qi_r, klo, khi, wi = _prep_a(
        proj, c128, s128, c64, s64, row(g_qa) * (A_HEAD_DIM ** -0.5 * LOG2E), row(g_ka), tp=512)
    r3 = lambda a: a.reshape(b, s, a.shape[-1])
    scores, th = _indexer(r3(qi_r), r3(klo), r3(khi), r3(wi), tq=512, tk=512, k_sel=k_sel)
    shift_a, fixed_a = _softmax_shift(A_HEAD_DIM, g_qa, g_ka)
    y_a = lax.cond(
        fixed_a,
        functools.partial(_dsa_attn, tq=512, tk=512, fixed_shift=True),
        functools.partial(_dsa_attn, tq=512, tk=512, fixed_shift=False),
        r3(qa_r), r3(ka_r), r3(va_b), scores, th, shift_a)

    w_uq_r = w_uq.reshape(B_Q_LORA, B_HEADS, B_QK)
    w_uq_r = jnp.concatenate([w_uq_r[:, :, :B_NOPE].reshape(B_Q_LORA, -1),
                              w_uq_r[:, :, B_NOPE:].reshape(B_Q_LORA, -1)], axis=1).astype(BF16)
    w_ukv_r = w_ukv.reshape(B_KV_LORA, B_HEADS, 2, B_NOPE)
    w_ukv_r = jnp.concatenate([w_ukv_r[:, :, 0, :].reshape(B_KV_LORA, -1),
                               w_ukv_r[:, :, 1, :].reshape(B_KV_LORA, -1)], axis=1).astype(BF16)
    pe2 = lambda g: jnp.tile(g[B_NOPE:], 2).reshape(1, LANES).astype(F32)
    shift_b, fixed_b = _softmax_shift(B_QK, g_qb, g_kb)
    q_m = _mla_q(proj, row(g_cq), w_uq_r, c64, s64, row(g_qb[:B_NOPE]), pe2(g_qb), tm=512)
    k_m, v_m = _mla_kv(proj, row(g_ckv), w_ukv_r, c64, s64, row(g_kb[:B_NOPE]), pe2(g_kb), shift_b, tm=512)
    y_b = lax.cond(
        fixed_b,
        functools.partial(_mla_attn, tq=512, tk=512, hg=8, fixed_shift=True),
        functools.partial(_mla_attn, tq=512, tk=512, hg=8, fixed_shift=False),
        r3(q_m), r3(k_m), r3(v_m))

    mixed = _merge(y_a.reshape(m, -1), y_b.reshape(m, -1), w_out_a.astype(BF16), w_out_b.astype(BF16),
                   proj, tm=1024, tn=512)
    x1 = _matmul_res("w_o", mixed, w_o.astype(BF16), xf, tm=1024, tn=1024)

    act = _ffn_up(x1, row(g_ffn_norm), w_ffn_gate.astype(BF16), w_ffn_up.astype(BF16), tm=1024, tn=512)
    x2 = _matmul_res("ffn_down", act, w_ffn_down.astype(BF16), x1, tm=512, tn=1024)

    out = _ple(x2, row(g_ple_norm), w_ple_gate.astype(BF16), p.reshape(m, -1),
               w_ple_proj.astype(BF16), tm=1024, tn=512)
    return out.reshape(b, s, d)


def kernel(x, p, positions, g_mix_norm, w_in, g_qa, g_ka, g_cq, w_uq, g_ckv, w_ukv, g_qb, g_kb,
           w_out_a, w_out_b, w_o, g_ffn_norm, w_ffn_gate, w_ffn_up, w_ffn_down, g_ple_norm,
           w_ple_gate, w_ple_proj):
    for i in range(p.shape[0]):
        x = _layer(x, p[i], positions, g_mix_norm[i], w_in[i], g_qa[i], g_ka[i], g_cq[i], w_uq[i],
                   g_ckv[i], w_ukv[i], g_qb[i], g_kb[i], w_out_a[i], w_out_b[i], w_o[i],
                   g_ffn_norm[i], w_ffn_gate[i], w_ffn_up[i], w_ffn_down[i], g_ple_norm[i],
                   w_ple_gate[i], w_ple_proj[i])
    return x
```

```python
import functools

import numpy as np
import jax
import jax.numpy as jnp
from jax import lax
from jax.experimental import pallas as pl
from jax.experimental.pallas import tpu as pltpu

F32 = jnp.float32
BF16 = jnp.bfloat16

ROPE_THETA = 10000.0
NORM_EPS = 1e-6
NEG = -1e30
M_INIT = -1e29
POS_BIG = 3e38
TH_ALL = 0.5 * NEG
LOG2E = 1.4426950408889634
MAX_SEARCH_STEPS = 512
CNT_ROWS = 128
MAX_FIXED_SHIFT = 60.0
A_HEADS = 16
A_KV_HEADS = 4
A_HEAD_DIM = 128
IDX_HEADS = 16
IDX_DIM = 64
TOPK_MAX = 256
B_HEADS = 16
B_Q_LORA = 512
B_KV_LORA = 256
B_NOPE = 128
B_ROPE = 64
B_V = 128
B_QK = B_NOPE + B_ROPE
LANES = 128
VMEM_LIMIT = 56 * 1024 * 1024

COL_QA = 0
COL_GATE_A = 2048
COL_GATE_B = 4096
COL_QI = 6144
COL_KA = 7168
COL_VA = 7680
COL_CQ = 8192
COL_CKV = 8704
COL_KIW = 8960
COL_KR = 9088
N_PROJ = 9216


def _dot(a, b):
    return jnp.dot(a, b, preferred_element_type=F32)


def _dot_nt(a, b):
    return lax.dot_general(a, b, (((1,), (1,)), ((), ())), preferred_element_type=F32)


def _cparams(sem):
    return pltpu.CompilerParams(dimension_semantics=sem, vmem_limit_bytes=VMEM_LIMIT)


def _rms_to_bf16(x, g):
    ms = jnp.mean(x * x, axis=-1, keepdims=True)
    return (x * lax.rsqrt(ms + NORM_EPS) * g).astype(BF16)


def _norm_matmul_kernel(x_ref, g_ref, w_ref, o_ref, hn_ref):
    @pl.when(pl.program_id(1) == 0)
    def _():
        hn_ref[...] = _rms_to_bf16(x_ref[...], g_ref[...])

    o_ref[...] = _dot(hn_ref[...], w_ref[...]).astype(o_ref.dtype)


def _norm_matmul(name, x, x_col_block, k, g, w, out_dtype, tm, tn):
    m = x.shape[0]
    n = w.shape[1]
    return pl.pallas_call(
        _norm_matmul_kernel,
        name=name,
        grid=(m // tm, n // tn),
        in_specs=[
            pl.BlockSpec((tm, k), lambda i, j: (i, x_col_block)),
            pl.BlockSpec((1, k), lambda i, j: (0, 0)),
            pl.BlockSpec((k, tn), lambda i, j: (0, j)),
        ],
        out_specs=pl.BlockSpec((tm, tn), lambda i, j: (i, j)),
        out_shape=jax.ShapeDtypeStruct((m, n), out_dtype),
        scratch_shapes=[pltpu.VMEM((tm, k), BF16)],
        compiler_params=_cparams(("parallel", "arbitrary")),
    )(x, g, w)


def _ffn_up_kernel(x_ref, g_ref, wg_ref, wu_ref, o_ref, hn_ref):
    @pl.when(pl.program_id(1) == 0)
    def _():
        hn_ref[...] = _rms_to_bf16(x_ref[...], g_ref[...])

    h = hn_ref[...]
    a = _dot(h, wg_ref[...])
    b = _dot(h, wu_ref[...])
    o_ref[...] = (a * jax.nn.sigmoid(a) * b).astype(o_ref.dtype)


def _ffn_up(x, g, wg, wu, tm, tn):
    m, k = x.shape
    n = wg.shape[1]
    return pl.pallas_call(
        _ffn_up_kernel,
        name="ffn_up",
        grid=(m // tm, n // tn),
        in_specs=[
            pl.BlockSpec((tm, k), lambda i, j: (i, 0)),
            pl.BlockSpec((1, k), lambda i, j: (0, 0)),
            pl.BlockSpec((k, tn), lambda i, j: (0, j)),
            pl.BlockSpec((k, tn), lambda i, j: (0, j)),
        ],
        out_specs=pl.BlockSpec((tm, tn), lambda i, j: (i, j)),
        out_shape=jax.ShapeDtypeStruct((m, n), BF16),
        scratch_shapes=[pltpu.VMEM((tm, k), BF16)],
        compiler_params=_cparams(("parallel", "arbitrary")),
    )(x, g, wg, wu)


def _matmul_res_kernel(a_ref, w_ref, r_ref, o_ref):
    o_ref[...] = r_ref[...] + _dot(a_ref[...], w_ref[...])


def _matmul_res(name, a, w, res, tm, tn):
    m, k = a.shape
    n = w.shape[1]
    return pl.pallas_call(
        _matmul_res_kernel,
        name=name,
        grid=(n // tn, m // tm),
        in_specs=[
            pl.BlockSpec((tm, k), lambda j, i: (i, 0)),
            pl.BlockSpec((k, tn), lambda j, i: (0, j)),
            pl.BlockSpec((tm, tn), lambda j, i: (i, j)),
        ],
        out_specs=pl.BlockSpec((tm, tn), lambda j, i: (i, j)),
        out_shape=jax.ShapeDtypeStruct((m, n), F32),
        compiler_params=_cparams(("parallel", "parallel")),
    )(a, w, res)


def _merge_kernel(ya_ref, yb_ref, wa_ref, wb_ref, ga_ref, gb_ref, o_ref):
    a = _dot(ya_ref[...], wa_ref[...])
    b = _dot(yb_ref[...], wb_ref[...])
    o_ref[...] = (jax.nn.sigmoid(ga_ref[...]) * a + jax.nn.sigmoid(gb_ref[...]) * b).astype(o_ref.dtype)


def _merge(ya, yb, wa, wb, proj, tm, tn):
    m, k = ya.shape
    n = wa.shape[1]
    ga0 = COL_GATE_A // tn
    gb0 = COL_GATE_B // tn
    return pl.pallas_call(
        _merge_kernel,
        name="merge",
        grid=(m // tm, n // tn),
        in_specs=[
            pl.BlockSpec((tm, k), lambda i, j: (i, 0)),
            pl.BlockSpec((tm, k), lambda i, j: (i, 0)),
            pl.BlockSpec((k, tn), lambda i, j: (0, j)),
            pl.BlockSpec((k, tn), lambda i, j: (0, j)),
            pl.BlockSpec((tm, tn), lambda i, j: (i, ga0 + j)),
            pl.BlockSpec((tm, tn), lambda i, j: (i, gb0 + j)),
        ],
        out_specs=pl.BlockSpec((tm, tn), lambda i, j: (i, j)),
        out_shape=jax.ShapeDtypeStruct((m, n), BF16),
        compiler_params=_cparams(("parallel", "parallel")),
    )(ya, yb, wa, wb, proj, proj)


def _ple_kernel(x_ref, g_ref, wg_ref, p_ref, wp_ref, o_ref):
    x = x_ref[...]
    gate = jax.nn.sigmoid(_dot(_rms_to_bf16(x, g_ref[...]), wg_ref[...]))
    emb = _dot(p_ref[...].astype(BF16), wp_ref[...])
    o_ref[...] = x + gate * emb


def _ple(x, g, wg, p, wp, tm):
    m, k = x.shape
    n = wg.shape[1]
    kp = p.shape[1]
    assert n == k
    return pl.pallas_call(
        _ple_kernel,
        name="ple",
        grid=(m // tm,),
        in_specs=[
            pl.BlockSpec((tm, k), lambda i: (i, 0)),
            pl.BlockSpec((1, k), lambda i: (0, 0)),
            pl.BlockSpec((k, n), lambda i: (0, 0)),
            pl.BlockSpec((tm, kp), lambda i: (i, 0)),
            pl.BlockSpec((kp, n), lambda i: (0, 0)),
        ],
        out_specs=pl.BlockSpec((tm, n), lambda i: (i, 0)),
        out_shape=jax.ShapeDtypeStruct((m, n), F32),
        compiler_params=_cparams(("parallel",)),
    )(x, g, wg, p, wp)


def _rope64_pairs(x, c64, s64, first_half):
    partner = jnp.where(first_half, pltpu.roll(x, 96, 1), pltpu.roll(x, 32, 1))
    return x * c64 + partner * s64


def _prep_a_kernel(qa_ref, ka_ref, va_ref, qi_ref, kiw_ref, c128_ref, s128_ref, c64_ref, s64_ref,
                   gq_ref, gk_ref, qa_o, ka_o, va_o, qi_o, klo_o, khi_o, wi_o, *, wi_scale):
    c128 = c128_ref[...]
    s128 = s128_ref[...]

    def norm_rope(x, g):
        ms = jnp.mean(x * x, axis=-1, keepdims=True)
        y = x * lax.rsqrt(ms + NORM_EPS) * g
        return y * c128 + pltpu.roll(y, 64, 1) * s128

    for h in range(A_HEADS):
        hs = slice(h * LANES, (h + 1) * LANES)
        qa_o[:, hs] = norm_rope(qa_ref[:, hs], gq_ref[...]).astype(BF16)
    for h in range(A_KV_HEADS):
        hs = slice(h * LANES, (h + 1) * LANES)
        ka_o[:, hs] = norm_rope(ka_ref[:, hs], gk_ref[...]).astype(BF16)
    va_o[...] = va_ref[...].astype(BF16)

    c64 = c64_ref[...]
    s64 = s64_ref[...]
    lane = lax.broadcasted_iota(jnp.int32, c64.shape, 1)
    first_half = (lane % 64) < 32
    for p in range(IDX_HEADS * IDX_DIM // LANES):
        ps = slice(p * LANES, (p + 1) * LANES)
        qi_o[:, ps] = _rope64_pairs(qi_ref[:, ps], c64, s64, first_half).astype(BF16)
    kiw = kiw_ref[...]
    ki = jnp.where(lane < 64, _rope64_pairs(kiw, c64, s64, first_half), 0.0)
    klo_o[...] = ki.astype(BF16)
    khi_o[...] = pltpu.roll(ki, 64, 1).astype(BF16)
    wi_o[...] = kiw * wi_scale


def _prep_a(proj, c128, s128, c64, s64, gq, gk, tp):
    m = proj.shape[0]
    row = lambda w, cb: pl.BlockSpec((tp, w), lambda i: (i, cb))
    vec = pl.BlockSpec((1, LANES), lambda i: (0, 0))
    out = lambda w: pl.BlockSpec((tp, w), lambda i: (i, 0))
    wi_scale = (IDX_HEADS ** -0.5) * (IDX_DIM ** -0.5)
    return pl.pallas_call(
        functools.partial(_prep_a_kernel, wi_scale=wi_scale),
        name="prep_a",
        grid=(m // tp,),
        in_specs=[
            row(2048, COL_QA // 2048), row(512, COL_KA // 512), row(512, COL_VA // 512),
            row(1024, COL_QI // 1024), row(LANES, COL_KIW // LANES),
            out(LANES), out(LANES), out(LANES), out(LANES), vec, vec,
        ],
        out_specs=[out(2048), out(512), out(512), out(1024), out(LANES), out(LANES), out(LANES)],
        out_shape=[
            jax.ShapeDtypeStruct((m, 2048), BF16), jax.ShapeDtypeStruct((m, 512), BF16),
            jax.ShapeDtypeStruct((m, 512), BF16), jax.ShapeDtypeStruct((m, 1024), BF16),
            jax.ShapeDtypeStruct((m, LANES), BF16), jax.ShapeDtypeStruct((m, LANES), BF16),
            jax.ShapeDtypeStruct((m, LANES), F32),
        ],
        compiler_params=_cparams(("parallel",)),
    )(proj, proj, proj, proj, proj, c128, s128, c64, s64, gq, gk)


def _mla_q_kernel(cq_ref, g_ref, w_ref, c64_ref, s64_ref, gqn_ref, gqp_ref, q_o, *, q_scale):
    qb = _dot(_rms_to_bf16(cq_ref[...], g_ref[...]), w_ref[...])
    c64 = c64_ref[...]
    s64 = s64_ref[...]
    lane = lax.broadcasted_iota(jnp.int32, c64.shape, 1)
    first_half = (lane % 64) < 32
    lo = lane < 64
    nope_w = B_HEADS * B_NOPE
    for pair in range(B_HEADS // 2):
        qp2 = qb[:, nope_w + pair * LANES:nope_w + (pair + 1) * LANES]
        qp2_sq = qp2 * qp2
        rot2 = _rope64_pairs(qp2 * gqp_ref[...], c64, s64, first_half)
        for h in (2 * pair, 2 * pair + 1):
            mine = lo if h % 2 == 0 else jnp.logical_not(lo)
            shift_lane = lane == (64 if h % 2 == 0 else 0)
            qn = qb[:, h * LANES:(h + 1) * LANES]
            ss = jnp.sum(qn * qn + jnp.where(mine, qp2_sq, 0.0), axis=-1, keepdims=True)
            r = lax.rsqrt(ss * (1.0 / B_QK) + NORM_EPS) * q_scale
            q_o[:, 2 * h * LANES:(2 * h + 1) * LANES] = (qn * r * gqn_ref[...]).astype(BF16)
            qpe = jnp.where(mine, rot2 * r, jnp.where(shift_lane, 1.0, 0.0))
            q_o[:, (2 * h + 1) * LANES:(2 * h + 2) * LANES] = qpe.astype(BF16)


def _mla_kv_kernel(ckv_ref, g_ref, w_ref, kr_ref, c64_ref, s64_ref, gkn_ref, gkp_ref, shift_ref, k_o, v_o):
    kv = _dot(_rms_to_bf16(ckv_ref[...], g_ref[...]), w_ref[...])
    c64 = c64_ref[...]
    s64 = s64_ref[...]
    lane = lax.broadcasted_iota(jnp.int32, c64.shape, 1)
    first_half = (lane % 64) < 32
    lo = lane < 64
    kr = jnp.where(lo, kr_ref[...], 0.0)
    kr_ss = jnp.sum(kr * kr, axis=-1, keepdims=True)
    kpe_lo = jnp.where(lo, _rope64_pairs(kr * gkp_ref[...], c64, s64, first_half), 0.0)
    kpe_hi = pltpu.roll(kpe_lo, 64, 1)
    nope_w = B_HEADS * B_NOPE
    for h in range(B_HEADS):
        hs = slice(h * LANES, (h + 1) * LANES)
        shift_lane = lane == (64 if h % 2 == 0 else 0)
        kn = kv[:, hs]
        rk = lax.rsqrt((jnp.sum(kn * kn, axis=-1, keepdims=True) + kr_ss) * (1.0 / B_QK) + NORM_EPS)
        k_o[:, 2 * h * LANES:(2 * h + 1) * LANES] = (kn * rk * gkn_ref[...]).astype(BF16)
        kpe = kpe_lo if h % 2 == 0 else kpe_hi
        k_o[:, (2 * h + 1) * LANES:(2 * h + 2) * LANES] = jnp.where(
            shift_lane, -shift_ref[...], kpe * rk).astype(BF16)
        v_o[:, hs] = kv[:, nope_w + h * LANES:nope_w + (h + 1) * LANES].astype(BF16)


def _mla_q(proj, g_cq, w_uq, c64, s64, gqn, gqp, tm):
    m = proj.shape[0]
    vec = lambda w: pl.BlockSpec((1, w), lambda i: (0, 0))
    rows = lambda w: pl.BlockSpec((tm, w), lambda i: (i, 0))
    return pl.pallas_call(
        functools.partial(_mla_q_kernel, q_scale=B_QK ** -0.5 * LOG2E),
        name="mla_q",
        grid=(m // tm,),
        in_specs=[
            pl.BlockSpec((tm, B_Q_LORA), lambda i: (i, COL_CQ // B_Q_LORA)),
            vec(B_Q_LORA),
            pl.BlockSpec(w_uq.shape, lambda i: (0, 0)),
            rows(LANES), rows(LANES), vec(LANES), vec(LANES),
        ],
        out_specs=rows(B_HEADS * 256),
        out_shape=jax.ShapeDtypeStruct((m, B_HEADS * 256), BF16),
        compiler_params=_cparams(("parallel",)),
    )(proj, g_cq, w_uq, c64, s64, gqn, gqp)


def _mla_kv(proj, g_ckv, w_ukv, c64, s64, gkn, gkp, shift, tm):
    m = proj.shape[0]
    vec = lambda w: pl.BlockSpec((1, w), lambda i: (0, 0))
    rows = lambda w: pl.BlockSpec((tm, w), lambda i: (i, 0))
    return pl.pallas_call(
        _mla_kv_kernel,
        name="mla_kv",
        grid=(m // tm,),
        in_specs=[
            pl.BlockSpec((tm, B_KV_LORA), lambda i: (i, COL_CKV // B_KV_LORA)),
            vec(B_KV_LORA),
            pl.BlockSpec(w_ukv.shape, lambda i: (0, 0)),
            pl.BlockSpec((tm, LANES), lambda i: (i, COL_KR // LANES)),
            rows(LANES), rows(LANES), vec(LANES), vec(LANES), vec(LANES),
        ],
        out_specs=[rows(B_HEADS * 256), rows(B_HEADS * B_V)],
        out_shape=[
            jax.ShapeDtypeStruct((m, B_HEADS * 256), BF16),
            jax.ShapeDtypeStruct((m, B_HEADS * B_V), BF16),
        ],
        compiler_params=_cparams(("parallel",)),
    )(proj, g_ckv, w_ukv, proj, c64, s64, gkn, gkp, shift)


def _indexer_kernel(q_ref, klo_ref, khi_ref, w_ref, sc_ref, th_ref, *, tq, tk, seq, k_sel):
    qi = pl.program_id(1)
    n_vis = (qi * tq) // tk + 1
    n_all = seq // tk
    row = qi * tq + lax.broadcasted_iota(jnp.int32, (tq, 1), 0)
    wcols = [w_ref[0, :, IDX_DIM + h:IDX_DIM + h + 1] for h in range(IDX_HEADS)]

    def gen(c, carry):
        rmax, rmin = carry
        off = pl.multiple_of(c * tk, tk)
        kl = klo_ref[0, pl.ds(off, tk), :]
        kh = khi_ref[0, pl.ds(off, tk), :]
        acc = jnp.zeros((tq, tk), F32)
        for p in range(IDX_HEADS // 2):
            qp = q_ref[0, :, p * LANES:(p + 1) * LANES]
            acc = acc + wcols[2 * p] * jnp.maximum(_dot_nt(qp, kl), 0.0)
            acc = acc + wcols[2 * p + 1] * jnp.maximum(_dot_nt(qp, kh), 0.0)
        col = off + lax.broadcasted_iota(jnp.int32, (tq, tk), 1)
        vis = col <= row
        sc_ref[0, :, pl.ds(off, tk)] = jnp.where(vis, acc, NEG)
        rmax = jnp.maximum(rmax, jnp.max(jnp.where(vis, acc, NEG), axis=-1, keepdims=True))
        rmin = jnp.minimum(rmin, jnp.min(jnp.where(vis, acc, POS_BIG), axis=-1, keepdims=True))
        return rmax, rmin

    rmax, rmin = lax.fori_loop(0, n_vis, gen,
                               (jnp.full((tq, 1), NEG, F32), jnp.full((tq, 1), POS_BIG, F32)))

    def fill(c, carry):
        sc_ref[0, :, pl.ds(pl.multiple_of(c * tk, tk), tk)] = jnp.full((tq, tk), NEG, F32)
        return carry

    lax.fori_loop(n_vis, n_all, fill, 0)

    cw = min(2 * tk, seq)
    n_cnt = (n_vis * tk + cw - 1) // cw

    def count_ge(mid, strict=False, open_groups=None):
        zeros = jnp.zeros((CNT_ROWS, LANES), F32)
        accs = []
        for r in range(tq // CNT_ROWS):
            rs = slice(r * CNT_ROWS, (r + 1) * CNT_ROWS)
            midb = mid[rs]

            def body(c, acc, rs=rs, midb=midb):
                off = pl.multiple_of(c * cw, cw)
                for j in range(cw // LANES):
                    blk = sc_ref[0, rs, pl.ds(off + j * LANES, LANES)]
                    acc = acc + jnp.where(blk > midb if strict else blk >= midb, 1.0, 0.0)
                return acc

            if open_groups is None:
                accs.append(lax.fori_loop(0, n_cnt, body, zeros))
            else:
                accs.append(lax.cond(open_groups[r] > 0.0,
                                     functools.partial(lax.fori_loop, 0, n_cnt, body, zeros),
                                     lambda: zeros))
        return jnp.concatenate(
            [jnp.broadcast_to(jnp.sum(a, axis=-1, keepdims=True), (CNT_ROWS, LANES)) for a in accs], axis=0)

    rowb = qi * tq + lax.broadcasted_iota(jnp.int32, (tq, LANES), 0)
    active0 = (rowb + 1) > k_sel
    lo0 = jnp.where(active0, jnp.broadcast_to(rmin, (tq, LANES)), TH_ALL)
    hi0 = jnp.where(active0, jnp.broadcast_to(rmax, (tq, LANES)), TH_ALL)

    def n_open(lo, hi):
        still = jnp.where(lo < hi, 1.0, 0.0)
        return tuple(jnp.sum(still[r * CNT_ROWS:(r + 1) * CNT_ROWS]) for r in range(tq // CNT_ROWS))

    def cond(st):
        it, nopen = st[0], st[1]
        return jnp.logical_and(functools.reduce(jnp.add, nopen) > 0.0, it < MAX_SEARCH_STEPS)

    def body(st):
        it, was_open, lo, hi, cnt_lo = st
        mid = lo + (hi - lo) * 0.5
        cnt = count_ge(mid, open_groups=was_open)
        nopen = n_open(lo, hi)
        ge = cnt >= k_sel
        lo_n = jnp.where(ge, mid, lo)
        hi_n = jnp.where(ge, hi, mid)
        hi_n = jnp.where(cnt == k_sel, lo_n, hi_n)
        hi_n = jnp.where(mid >= hi, lo_n, hi_n)
        hi_n = jnp.where(mid <= lo, lo_n, hi_n)
        return it + 1, nopen, lo_n, hi_n, jnp.where(ge, cnt, cnt_lo)

    st = lax.while_loop(cond, body, (jnp.int32(0), n_open(lo0, hi0), lo0, hi0,
                                     (rowb + 1).astype(F32)))
    th = st[2]
    th_ref[0] = th

    tied = jnp.logical_and(active0, st[4] > k_sel)

    @pl.when(jnp.sum(jnp.where(tied, 1.0, 0.0)) > 0.0)
    def _():
        keep_ties = k_sel - count_ge(th, strict=True)
        reps = tk // LANES
        th_w = jnp.concatenate([th] * reps, axis=1)
        keep_w = jnp.concatenate([keep_ties] * reps, axis=1)
        earlier = (lax.broadcasted_iota(jnp.int32, (tk, tk), 0)
                   < lax.broadcasted_iota(jnp.int32, (tk, tk), 1))
        tri = jnp.where(earlier, 1.0, 0.0).astype(BF16)

        def demote(c, seen):
            off = pl.multiple_of(c * tk, tk)
            blk = sc_ref[0, :, pl.ds(off, tk)]
            eq = blk == th_w
            eqf = jnp.where(eq, 1.0, 0.0)
            rank = _dot(eqf.astype(BF16), tri) + jnp.concatenate([seen] * reps, axis=1)
            sc_ref[0, :, pl.ds(off, tk)] = jnp.where(jnp.logical_and(eq, rank >= keep_w), NEG, blk)
            return seen + jnp.sum(eqf, axis=-1, keepdims=True)

        lax.fori_loop(0, n_vis, demote, jnp.zeros((tq, LANES), F32))


def _indexer(qi_r, klo, khi, wi, tq, tk, k_sel):
    b, s, _ = qi_r.shape
    return pl.pallas_call(
        functools.partial(_indexer_kernel, tq=tq, tk=tk, seq=s, k_sel=float(k_sel)),
        name="indexer",
        grid=(b, s // tq),
        in_specs=[
            pl.BlockSpec((1, tq, IDX_HEADS * IDX_DIM), lambda bi, qi: (bi, qi, 0)),
            pl.BlockSpec((1, s, LANES), lambda bi, qi: (bi, 0, 0)),
            pl.BlockSpec((1, s, LANES), lambda bi, qi: (bi, 0, 0)),
            pl.BlockSpec((1, tq, LANES), lambda bi, qi: (bi, qi, 0)),
        ],
        out_specs=[
            pl.BlockSpec((1, tq, s), lambda bi, qi: (bi, qi, 0)),
            pl.BlockSpec((1, tq, LANES), lambda bi, qi: (bi, qi, 0)),
        ],
        out_shape=[jax.ShapeDtypeStruct((b, s, s), F32), jax.ShapeDtypeStruct((b, s, LANES), F32)],
        compiler_params=_cparams(("parallel", "parallel")),
    )(qi_r, klo, khi, wi)


def _causal_pairs(s, tq, tk):
    qs, ks = [], []
    for qi in range(s // tq):
        for ki in range((qi * tq + tq - 1) // tk + 1):
            qs.append(qi)
            ks.append(ki)
    return jnp.asarray(np.array(qs, np.int32)), jnp.asarray(np.array(ks, np.int32))


def _softmax_step(s, v, rows, m_ref, l_ref, acc_ref, fixed_shift):
    blocks = [s[:, j * LANES:(j + 1) * LANES] for j in range(s.shape[1] // LANES)]
    if fixed_shift:
        ps = [jnp.exp2(blk) for blk in blocks]
        l_ref[rows] = l_ref[rows] + functools.reduce(jnp.add, ps)
        p = jnp.concatenate([blk.astype(BF16) for blk in ps], axis=1)
        acc_ref[rows] = acc_ref[rows] + _dot(p, v)
        return
    mx = functools.reduce(jnp.maximum, blocks)
    m_prev = m_ref[rows]
    m_new = jnp.maximum(m_prev, jnp.max(mx, axis=-1, keepdims=True))
    alpha = jnp.exp2(m_prev - m_new)
    ps = [jnp.exp2(blk - m_new) for blk in blocks]
    l_ref[rows] = alpha * l_ref[rows] + jnp.sum(functools.reduce(jnp.add, ps), axis=-1, keepdims=True)
    p = jnp.concatenate([blk.astype(BF16) for blk in ps], axis=1)
    acc_ref[rows] = alpha * acc_ref[rows] + _dot(p, v)
    m_ref[rows] = m_new


def _softmax_finish(rows, l_ref, acc_ref, fixed_shift):
    l = l_ref[rows]
    if fixed_shift:
        l = jnp.sum(l, axis=-1, keepdims=True)
    return acc_ref[rows] / l


def _dsa_kernel(qt_ref, kt_ref, q_ref, k_ref, v_ref, sc_ref, th_ref, shift_ref, o_ref, m_ref, l_ref,
                acc_ref, bias_ref, qs_ref, *, tq, tk, fixed_shift):
    step = pl.program_id(1)
    qi = qt_ref[step]
    ki = kt_ref[step]

    @pl.when(ki == 0)
    def _():
        m_ref[...] = jnp.full(m_ref.shape, M_INIT, F32)
        l_ref[...] = jnp.zeros(l_ref.shape, F32)
        acc_ref[...] = jnp.zeros(acc_ref.shape, F32)
        for h in range(A_HEADS):
            qs_ref[h * tq:(h + 1) * tq, :] = q_ref[0, :, h * LANES:(h + 1) * LANES]

    bias_ref[...] = jnp.where(sc_ref[0] >= th_ref[0][:, :1], -shift_ref[:, :1], NEG)
    grp = A_HEADS // A_KV_HEADS
    for g in range(A_KV_HEADS):
        rows = slice(g * grp * tq, (g + 1) * grp * tq)
        k = k_ref[0, :, g * LANES:(g + 1) * LANES]
        v = v_ref[0, :, g * LANES:(g + 1) * LANES]
        s = _dot_nt(qs_ref[rows, :], k)
        s = jnp.concatenate([s[j * tq:(j + 1) * tq] + bias_ref[...] for j in range(grp)], axis=0)
        _softmax_step(s, v, rows, m_ref, l_ref, acc_ref, fixed_shift)

    @pl.when(ki == (qi * tq + tq - 1) // tk)
    def _():
        for h in range(A_HEADS):
            out = _softmax_finish(slice(h * tq, (h + 1) * tq), l_ref, acc_ref, fixed_shift)
            o_ref[0, :, h * LANES:(h + 1) * LANES] = out.astype(o_ref.dtype)


def _dsa_attn(q, k, v, scores, th, shift, tq, tk, fixed_shift):
    b, s, dq = q.shape
    qt, kt = _causal_pairs(s, tq, tk)
    grid_spec = pltpu.PrefetchScalarGridSpec(
        num_scalar_prefetch=2,
        grid=(b, qt.shape[0]),
        in_specs=[
            pl.BlockSpec((1, tq, dq), lambda bi, p, qt, kt: (bi, qt[p], 0)),
            pl.BlockSpec((1, tk, k.shape[2]), lambda bi, p, qt, kt: (bi, kt[p], 0)),
            pl.BlockSpec((1, tk, v.shape[2]), lambda bi, p, qt, kt: (bi, kt[p], 0)),
            pl.BlockSpec((1, tq, tk), lambda bi, p, qt, kt: (bi, qt[p], kt[p])),
            pl.BlockSpec((1, tq, LANES), lambda bi, p, qt, kt: (bi, qt[p], 0)),
            pl.BlockSpec((1, LANES), lambda bi, p, qt, kt: (0, 0)),
        ],
        out_specs=pl.BlockSpec((1, tq, dq), lambda bi, p, qt, kt: (bi, qt[p], 0)),
        scratch_shapes=[
            pltpu.VMEM((A_HEADS * tq, LANES), F32),
            pltpu.VMEM((A_HEADS * tq, LANES), F32),
            pltpu.VMEM((A_HEADS * tq, A_HEAD_DIM), F32),
            pltpu.VMEM((tq, tk), F32),
            pltpu.VMEM((A_HEADS * tq, A_HEAD_DIM), BF16),
        ],
    )
    return pl.pallas_call(
        functools.partial(_dsa_kernel, tq=tq, tk=tk, fixed_shift=fixed_shift),
        name="dsa_attn_fixed" if fixed_shift else "dsa_attn_online",
        grid_spec=grid_spec,
        out_shape=jax.ShapeDtypeStruct((b, s, dq), BF16),
        compiler_params=_cparams(("parallel", "arbitrary")),
    )(qt, kt, q, k, v, scores, th, shift)


def _mla_kernel(qt_ref, kt_ref, q_ref, k_ref, v_ref, o_ref, m_ref, l_ref, acc_ref, *, tq, tk, hg,
                fixed_shift):
    step = pl.program_id(2)
    qi = qt_ref[step]
    ki = kt_ref[step]

    @pl.when(ki == 0)
    def _():
        m_ref[...] = jnp.full(m_ref.shape, M_INIT, F32)
        l_ref[...] = jnp.zeros(l_ref.shape, F32)
        acc_ref[...] = jnp.zeros(acc_ref.shape, F32)

    def heads(masked):
        if masked:
            keep = (lax.broadcasted_iota(jnp.int32, (tq, tk), 1)
                    <= lax.broadcasted_iota(jnp.int32, (tq, tk), 0))
        for h in range(hg):
            q = q_ref[0, :, h * 256:(h + 1) * 256]
            k = k_ref[0, :, h * 256:(h + 1) * 256]
            v = v_ref[0, :, h * B_V:(h + 1) * B_V]
            s = _dot_nt(q, k)
            if masked:
                s = jnp.where(keep, s, NEG)
            _softmax_step(s, v, slice(h * tq, (h + 1) * tq), m_ref, l_ref, acc_ref, fixed_shift)

    @pl.when(ki < qi)
    def _():
        heads(False)

    @pl.when(ki == qi)
    def _():
        heads(True)
        for h in range(hg):
            out = _softmax_finish(slice(h * tq, (h + 1) * tq), l_ref, acc_ref, fixed_shift)
            o_ref[0, :, h * B_V:(h + 1) * B_V] = out.astype(o_ref.dtype)


def _mla_attn(q, k, v, tq, tk, hg, fixed_shift):
    b, s, _ = q.shape
    assert tq == tk
    qt, kt = _causal_pairs(s, tq, tk)
    grid_spec = pltpu.PrefetchScalarGridSpec(
        num_scalar_prefetch=2,
        grid=(b, B_HEADS // hg, qt.shape[0]),
        in_specs=[
            pl.BlockSpec((1, tq, hg * 256), lambda bi, g, p, qt, kt: (bi, qt[p], g)),
            pl.BlockSpec((1, tk, hg * 256), lambda bi, g, p, qt, kt: (bi, kt[p], g)),
            pl.BlockSpec((1, tk, hg * B_V), lambda bi, g, p, qt, kt: (bi, kt[p], g)),
        ],
        out_specs=pl.BlockSpec((1, tq, hg * B_V), lambda bi, g, p, qt, kt: (bi, qt[p], g)),
        scratch_shapes=[
            pltpu.VMEM((hg * tq, LANES), F32),
            pltpu.VMEM((hg * tq, LANES), F32),
            pltpu.VMEM((hg * tq, B_V), F32),
        ],
    )
    return pl.pallas_call(
        functools.partial(_mla_kernel, tq=tq, tk=tk, hg=hg, fixed_shift=fixed_shift),
        name="mla_attn_fixed" if fixed_shift else "mla_attn_online",
        grid_spec=grid_spec,
        out_shape=jax.ShapeDtypeStruct((b, s, B_HEADS * B_V), BF16),
        compiler_params=_cparams(("parallel", "parallel", "arbitrary")),
    )(qt, kt, q, k, v)


def _regroup_w_in(w):
    sizes = (A_HEADS * A_HEAD_DIM, A_KV_HEADS * A_HEAD_DIM, A_KV_HEADS * A_HEAD_DIM,
             IDX_HEADS * IDX_DIM, IDX_DIM, IDX_HEADS, B_Q_LORA, B_KV_LORA, B_ROPE)
    offs = np.cumsum((0,) + sizes)
    w = w.astype(BF16)
    qa, ka, va, qi, ki, wi, cq, ckv, kr = (w[:, offs[j]:offs[j + 1]] for j in range(len(sizes)))
    gates = w[:, offs[-1]:]
    d = w.shape[0]
    kiw = jnp.concatenate([ki, wi, jnp.zeros((d, LANES - IDX_DIM - IDX_HEADS), w.dtype)], axis=1)
    krp = jnp.concatenate([kr, jnp.zeros((d, LANES - B_ROPE), w.dtype)], axis=1)
    return jnp.concatenate([qa, gates, qi, ka, va, cq, ckv, kiw, krp], axis=1).astype(BF16)


def _softmax_shift(d, g_q, g_k):
    bound = (d ** 0.5) * LOG2E * 1.02 * jnp.max(jnp.abs(g_q)) * jnp.max(jnp.abs(g_k))
    c = jnp.ceil(bound.astype(F32))
    ok = c <= MAX_FIXED_SHIFT
    return jnp.full((1, LANES), jnp.where(ok, c, 0.0), F32), ok


def _rope_tables(pos, d):
    half = d // 2
    freqs = jnp.power(jnp.float32(ROPE_THETA), -jnp.arange(half, dtype=F32) * (2.0 / d))
    ang = pos.astype(F32)[..., None] * freqs
    cos, sin = jnp.cos(ang), jnp.sin(ang)
    reps = LANES // d
    c = jnp.tile(jnp.concatenate([cos, cos], axis=-1), (1, 1, reps))
    s = jnp.tile(jnp.concatenate([-sin, sin], axis=-1), (1, 1, reps))
    return c.reshape(-1, LANES), s.reshape(-1, LANES)


def _layer(x, p, positions, g_mix_norm, w_in, g_qa, g_ka, g_cq, w_uq, g_ckv, w_ukv, g_qb, g_kb,
           w_out_a, w_out_b, w_o, g_ffn_norm, w_ffn_gate, w_ffn_up, w_ffn_down, g_ple_norm,
           w_ple_gate, w_ple_proj):
    b, s, d = x.shape
    m = b * s
    k_sel = min(TOPK_MAX, s // 4)
    xf = x.reshape(m, d)
    row = lambda g: g.reshape(1, -1).astype(F32)

    c128, s128 = _rope_tables(positions, A_HEAD_DIM)
    c64, s64 = _rope_tables(positions, IDX_DIM)

    proj = _norm_matmul("in_proj", xf, 0, d, row(g_mix_norm), _regroup_w_in(w_in), F32, tm=1024, tn=1536)

    qa_r, ka_r, va_b, qi_r, klo, khi, wi = _prep_a(
        proj, c128, s128, c64, s64, row(g_qa) * (A_HEAD_DIM ** -0.5 * LOG2E), row(g_ka), tp=512)
    r3 = lambda a: a.reshape(b, s, a.shape[-1])
    scores, th = _indexer(r3(qi_r), r3(klo), r3(khi), r3(wi), tq=512, tk=512, k_sel=k_sel)
    shift_a, fixed_a = _softmax_shift(A_HEAD_DIM, g_qa, g_ka)
    y_a = lax.cond(
        fixed_a,
        functools.partial(_dsa_attn, tq=512, tk=512, fixed_shift=True),
        functools.partial(_dsa_attn, tq=512, tk=512, fixed_shift=False),
        r3(qa_r), r3(ka_r), r3(va_b), scores, th, shift_a)

    w_uq_r = w_uq.reshape(B_Q_LORA, B_HEADS, B_QK)
    w_uq_r = jnp.concatenate([w_uq_r[:, :, :B_NOPE].reshape(B_Q_LORA, -1),
                              w_uq_r[:, :, B_NOPE:].reshape(B_Q_LORA, -1)], axis=1).astype(BF16)
    w_ukv_r = w_ukv.reshape(B_KV_LORA, B_HEADS, 2, B_NOPE)
    w_ukv_r = jnp.concatenate([w_ukv_r[:, :, 0, :].reshape(B_KV_LORA, -1),
                               w_ukv_r[:, :, 1, :].reshape(B_KV_LORA, -1)], axis=1).astype(BF16)
    pe2 = lambda g: jnp.tile(g[B_NOPE:], 2).reshape(1, LANES).astype(F32)
    shift_b, fixed_b = _softmax_shift(B_QK, g_qb, g_kb)
    q_m = _mla_q(proj, row(g_cq), w_uq_r, c64, s64, row(g_qb[:B_NOPE]), pe2(g_qb), tm=512)
    k_m, v_m = _mla_kv(proj, row(g_ckv), w_ukv_r, c64, s64, row(g_kb[:B_NOPE]), pe2(g_kb), shift_b, tm=512)
    y_b = lax.cond(
        fixed_b,
        functools.partial(_mla_attn, tq=512, tk=512, hg=8, fixed_shift=True),
        functools.partial(_mla_attn, tq=512, tk=512, hg=8, fixed_shift=False),
        r3(q_m), r3(k_m), r3(v_m))

    mixed = _merge(y_a.reshape(m, -1), y_b.reshape(m, -1), w_out_a.astype(BF16), w_out_b.astype(BF16),
                   proj, tm=1024, tn=512)
    x1 = _matmul_res("w_o", mixed, w_o.astype(BF16), xf, tm=1024, tn=1024)

    act = _ffn_up(x1, row(g_ffn_norm), w_ffn_gate.astype(BF16), w_ffn_up.astype(BF16), tm=1024, tn=512)
    x2 = _matmul_res("ffn_down", act, w_ffn_down.astype(BF16), x1, tm=512, tn=1024)

    out = _ple(x2, row(g_ple_norm), w_ple_gate.astype(BF16), p.reshape(m, -1),
               w_ple_proj.astype(BF16), tm=512)
    return out.reshape(b, s, d)


def kernel(x, p, positions, g_mix_norm, w_in, g_qa, g_ka, g_cq, w_uq, g_ckv, w_ukv, g_qb, g_kb,
           w_out_a, w_out_b, w_o, g_ffn_norm, w_ffn_gate, w_ffn_up, w_ffn_down, g_ple_norm,
           w_ple_gate, w_ple_proj):
    for i in range(p.shape[0]):
        x = _layer(x, p[i], positions, g_mix_norm[i], w_in[i], g_qa[i], g_ka[i], g_cq[i], w_uq[i],
                   g_ckv[i], w_ukv[i], g_qb[i], g_kb[i], w_out_a[i], w_out_b[i], w_o[i],
                   g_ffn_norm[i], w_ffn_gate[i], w_ffn_up[i], w_ffn_down[i], g_ple_norm[i],
                   w_ple_gate[i], w_ple_proj[i])
    return x
```

```python
import functools

import numpy as np
import jax
import jax.numpy as jnp
from jax import lax
from jax.experimental import pallas as pl
from jax.experimental.pallas import tpu as pltpu

F32 = jnp.float32
BF16 = jnp.bfloat16

ROPE_THETA = 10000.0
NORM_EPS = 1e-6
NEG = -1e30
M_INIT = -1e29
POS_BIG = 3e38
TH_ALL = 0.5 * NEG
LOG2E = 1.4426950408889634
MAX_SEARCH_STEPS = 512
CNT_ROWS = 128
MAX_FIXED_SHIFT = 60.0
A_HEADS = 16
A_KV_HEADS = 4
A_HEAD_DIM = 128
IDX_HEADS = 16
IDX_DIM = 64
TOPK_MAX = 256
B_HEADS = 16
B_Q_LORA = 512
B_KV_LORA = 256
B_NOPE = 128
B_ROPE = 64
B_V = 128
B_QK = B_NOPE + B_ROPE
LANES = 128
VMEM_LIMIT = 56 * 1024 * 1024

COL_QA = 0
COL_GATE_A = 2048
COL_GATE_B = 4096
COL_QI = 6144
COL_KA = 7168
COL_VA = 7680
COL_CQ = 8192
COL_CKV = 8704
COL_KIW = 8960
COL_KR = 9088
N_PROJ = 9216


def _dot(a, b):
    return jnp.dot(a, b, preferred_element_type=F32)


def _dot_nt(a, b):
    return lax.dot_general(a, b, (((1,), (1,)), ((), ())), preferred_element_type=F32)


def _cparams(sem):
    return pltpu.CompilerParams(dimension_semantics=sem, vmem_limit_bytes=VMEM_LIMIT)


def _rms_to_bf16(x, g):
    ms = jnp.mean(x * x, axis=-1, keepdims=True)
    return (x * lax.rsqrt(ms + NORM_EPS) * g).astype(BF16)


def _norm_matmul_kernel(x_ref, g_ref, w_ref, o_ref, hn_ref):
    @pl.when(pl.program_id(1) == 0)
    def _():
        hn_ref[...] = _rms_to_bf16(x_ref[...], g_ref[...])

    o_ref[...] = _dot(hn_ref[...], w_ref[...]).astype(o_ref.dtype)


def _norm_matmul(name, x, x_col_block, k, g, w, out_dtype, tm, tn):
    m = x.shape[0]
    n = w.shape[1]
    return pl.pallas_call(
        _norm_matmul_kernel,
        name=name,
        grid=(m // tm, n // tn),
        in_specs=[
            pl.BlockSpec((tm, k), lambda i, j: (i, x_col_block)),
            pl.BlockSpec((1, k), lambda i, j: (0, 0)),
            pl.BlockSpec((k, tn), lambda i, j: (0, j)),
        ],
        out_specs=pl.BlockSpec((tm, tn), lambda i, j: (i, j)),
        out_shape=jax.ShapeDtypeStruct((m, n), out_dtype),
        scratch_shapes=[pltpu.VMEM((tm, k), BF16)],
        compiler_params=_cparams(("parallel", "arbitrary")),
    )(x, g, w)


def _ffn_up_kernel(x_ref, g_ref, wg_ref, wu_ref, o_ref, hn_ref):
    @pl.when(pl.program_id(1) == 0)
    def _():
        hn_ref[...] = _rms_to_bf16(x_ref[...], g_ref[...])

    h = hn_ref[...]
    a = _dot(h, wg_ref[...])
    b = _dot(h, wu_ref[...])
    o_ref[...] = (a * jax.nn.sigmoid(a) * b).astype(o_ref.dtype)


def _ffn_up(x, g, wg, wu, tm, tn):
    m, k = x.shape
    n = wg.shape[1]
    return pl.pallas_call(
        _ffn_up_kernel,
        name="ffn_up",
        grid=(m // tm, n // tn),
        in_specs=[
            pl.BlockSpec((tm, k), lambda i, j: (i, 0)),
            pl.BlockSpec((1, k), lambda i, j: (0, 0)),
            pl.BlockSpec((k, tn), lambda i, j: (0, j)),
            pl.BlockSpec((k, tn), lambda i, j: (0, j)),
        ],
        out_specs=pl.BlockSpec((tm, tn), lambda i, j: (i, j)),
        out_shape=jax.ShapeDtypeStruct((m, n), BF16),
        scratch_shapes=[pltpu.VMEM((tm, k), BF16)],
        compiler_params=_cparams(("parallel", "arbitrary")),
    )(x, g, wg, wu)


def _matmul_res_kernel(a_ref, w_ref, r_ref, o_ref):
    o_ref[...] = r_ref[...] + _dot(a_ref[...], w_ref[...])


def _matmul_res(name, a, w, res, tm, tn):
    m, k = a.shape
    n = w.shape[1]
    return pl.pallas_call(
        _matmul_res_kernel,
        name=name,
        grid=(n // tn, m // tm),
        in_specs=[
            pl.BlockSpec((tm, k), lambda j, i: (i, 0)),
            pl.BlockSpec((k, tn), lambda j, i: (0, j)),
            pl.BlockSpec((tm, tn), lambda j, i: (i, j)),
        ],
        out_specs=pl.BlockSpec((tm, tn), lambda j, i: (i, j)),
        out_shape=jax.ShapeDtypeStruct((m, n), F32),
        compiler_params=_cparams(("parallel", "parallel")),
    )(a, w, res)


def _merge_kernel(ya_ref, yb_ref, wa_ref, wb_ref, ga_ref, gb_ref, o_ref):
    a = _dot(ya_ref[...], wa_ref[...])
    b = _dot(yb_ref[...], wb_ref[...])
    o_ref[...] = (jax.nn.sigmoid(ga_ref[...]) * a + jax.nn.sigmoid(gb_ref[...]) * b).astype(o_ref.dtype)


def _merge(ya, yb, wa, wb, proj, tm, tn):
    m, k = ya.shape
    n = wa.shape[1]
    ga0 = COL_GATE_A // tn
    gb0 = COL_GATE_B // tn
    return pl.pallas_call(
        _merge_kernel,
        name="merge",
        grid=(m // tm, n // tn),
        in_specs=[
            pl.BlockSpec((tm, k), lambda i, j: (i, 0)),
            pl.BlockSpec((tm, k), lambda i, j: (i, 0)),
            pl.BlockSpec((k, tn), lambda i, j: (0, j)),
            pl.BlockSpec((k, tn), lambda i, j: (0, j)),
            pl.BlockSpec((tm, tn), lambda i, j: (i, ga0 + j)),
            pl.BlockSpec((tm, tn), lambda i, j: (i, gb0 + j)),
        ],
        out_specs=pl.BlockSpec((tm, tn), lambda i, j: (i, j)),
        out_shape=jax.ShapeDtypeStruct((m, n), BF16),
        compiler_params=_cparams(("parallel", "parallel")),
    )(ya, yb, wa, wb, proj, proj)


def _ple_kernel(x_ref, g_ref, wg_ref, p_ref, wp_ref, o_ref):
    x = x_ref[...]
    gate = jax.nn.sigmoid(_dot(_rms_to_bf16(x, g_ref[...]), wg_ref[...]))
    emb = _dot(p_ref[...].astype(BF16), wp_ref[...])
    o_ref[...] = x + gate * emb


def _ple(x, g, wg, p, wp, tm):
    m, k = x.shape
    n = wg.shape[1]
    kp = p.shape[1]
    assert n == k
    return pl.pallas_call(
        _ple_kernel,
        name="ple",
        grid=(m // tm,),
        in_specs=[
            pl.BlockSpec((tm, k), lambda i: (i, 0)),
            pl.BlockSpec((1, k), lambda i: (0, 0)),
            pl.BlockSpec((k, n), lambda i: (0, 0)),
            pl.BlockSpec((tm, kp), lambda i: (i, 0)),
            pl.BlockSpec((kp, n), lambda i: (0, 0)),
        ],
        out_specs=pl.BlockSpec((tm, n), lambda i: (i, 0)),
        out_shape=jax.ShapeDtypeStruct((m, n), F32),
        compiler_params=_cparams(("parallel",)),
    )(x, g, wg, p, wp)


def _rope64_pairs(x, c64, s64, first_half):
    partner = jnp.where(first_half, pltpu.roll(x, 96, 1), pltpu.roll(x, 32, 1))
    return x * c64 + partner * s64


def _prep_a_kernel(qa_ref, ka_ref, va_ref, qi_ref, kiw_ref, c128_ref, s128_ref, c64_ref, s64_ref,
                   gq_ref, gk_ref, qa_o, ka_o, va_o, qi_o, klo_o, khi_o, wi_o, *, wi_scale):
    c128 = c128_ref[...]
    s128 = s128_ref[...]

    def norm_rope(x, g):
        ms = jnp.mean(x * x, axis=-1, keepdims=True)
        y = x * lax.rsqrt(ms + NORM_EPS) * g
        return y * c128 + pltpu.roll(y, 64, 1) * s128

    for h in range(A_HEADS):
        hs = slice(h * LANES, (h + 1) * LANES)
        qa_o[:, hs] = norm_rope(qa_ref[:, hs], gq_ref[...]).astype(BF16)
    for h in range(A_KV_HEADS):
        hs = slice(h * LANES, (h + 1) * LANES)
        ka_o[:, hs] = norm_rope(ka_ref[:, hs], gk_ref[...]).astype(BF16)
    va_o[...] = va_ref[...].astype(BF16)

    c64 = c64_ref[...]
    s64 = s64_ref[...]
    lane = lax.broadcasted_iota(jnp.int32, c64.shape, 1)
    first_half = (lane % 64) < 32
    for p in range(IDX_HEADS * IDX_DIM // LANES):
        ps = slice(p * LANES, (p + 1) * LANES)
        qi_o[:, ps] = _rope64_pairs(qi_ref[:, ps], c64, s64, first_half).astype(BF16)
    kiw = kiw_ref[...]
    ki = jnp.where(lane < 64, _rope64_pairs(kiw, c64, s64, first_half), 0.0)
    klo_o[...] = ki.astype(BF16)
    khi_o[...] = pltpu.roll(ki, 64, 1).astype(BF16)
    wi_o[...] = kiw * wi_scale


def _prep_a(proj, c128, s128, c64, s64, gq, gk, tp):
    m = proj.shape[0]
    row = lambda w, cb: pl.BlockSpec((tp, w), lambda i: (i, cb))
    vec = pl.BlockSpec((1, LANES), lambda i: (0, 0))
    out = lambda w: pl.BlockSpec((tp, w), lambda i: (i, 0))
    wi_scale = (IDX_HEADS ** -0.5) * (IDX_DIM ** -0.5)
    return pl.pallas_call(
        functools.partial(_prep_a_kernel, wi_scale=wi_scale),
        name="prep_a",
        grid=(m // tp,),
        in_specs=[
            row(2048, COL_QA // 2048), row(512, COL_KA // 512), row(512, COL_VA // 512),
            row(1024, COL_QI // 1024), row(LANES, COL_KIW // LANES),
            out(LANES), out(LANES), out(LANES), out(LANES), vec, vec,
        ],
        out_specs=[out(2048), out(512), out(512), out(1024), out(LANES), out(LANES), out(LANES)],
        out_shape=[
            jax.ShapeDtypeStruct((m, 2048), BF16), jax.ShapeDtypeStruct((m, 512), BF16),
            jax.ShapeDtypeStruct((m, 512), BF16), jax.ShapeDtypeStruct((m, 1024), BF16),
            jax.ShapeDtypeStruct((m, LANES), BF16), jax.ShapeDtypeStruct((m, LANES), BF16),
            jax.ShapeDtypeStruct((m, LANES), F32),
        ],
        compiler_params=_cparams(("parallel",)),
    )(proj, proj, proj, proj, proj, c128, s128, c64, s64, gq, gk)


def _mla_q_kernel(cq_ref, g_ref, w_ref, c64_ref, s64_ref, gqn_ref, gqp_ref, q_o, *, q_scale):
    qb = _dot(_rms_to_bf16(cq_ref[...], g_ref[...]), w_ref[...])
    c64 = c64_ref[...]
    s64 = s64_ref[...]
    lane = lax.broadcasted_iota(jnp.int32, c64.shape, 1)
    first_half = (lane % 64) < 32
    lo = lane < 64
    nope_w = B_HEADS * B_NOPE
    for pair in range(B_HEADS // 2):
        qp2 = qb[:, nope_w + pair * LANES:nope_w + (pair + 1) * LANES]
        qp2_sq = qp2 * qp2
        rot2 = _rope64_pairs(qp2 * gqp_ref[...], c64, s64, first_half)
        for h in (2 * pair, 2 * pair + 1):
            mine = lo if h % 2 == 0 else jnp.logical_not(lo)
            shift_lane = lane == (64 if h % 2 == 0 else 0)
            qn = qb[:, h * LANES:(h + 1) * LANES]
            ss = jnp.sum(qn * qn + jnp.where(mine, qp2_sq, 0.0), axis=-1, keepdims=True)
            r = lax.rsqrt(ss * (1.0 / B_QK) + NORM_EPS) * q_scale
            q_o[:, 2 * h * LANES:(2 * h + 1) * LANES] = (qn * r * gqn_ref[...]).astype(BF16)
            qpe = jnp.where(mine, rot2 * r, jnp.where(shift_lane, 1.0, 0.0))
            q_o[:, (2 * h + 1) * LANES:(2 * h + 2) * LANES] = qpe.astype(BF16)


def _mla_kv_kernel(ckv_ref, g_ref, w_ref, kr_ref, c64_ref, s64_ref, gkn_ref, gkp_ref, shift_ref, k_o, v_o):
    kv = _dot(_rms_to_bf16(ckv_ref[...], g_ref[...]), w_ref[...])
    c64 = c64_ref[...]
    s64 = s64_ref[...]
    lane = lax.broadcasted_iota(jnp.int32, c64.shape, 1)
    first_half = (lane % 64) < 32
    lo = lane < 64
    kr = jnp.where(lo, kr_ref[...], 0.0)
    kr_ss = jnp.sum(kr * kr, axis=-1, keepdims=True)
    kpe_lo = jnp.where(lo, _rope64_pairs(kr * gkp_ref[...], c64, s64, first_half), 0.0)
    kpe_hi = pltpu.roll(kpe_lo, 64, 1)
    nope_w = B_HEADS * B_NOPE
    for h in range(B_HEADS):
        hs = slice(h * LANES, (h + 1) * LANES)
        shift_lane = lane == (64 if h % 2 == 0 else 0)
        kn = kv[:, hs]
        rk = lax.rsqrt((jnp.sum(kn * kn, axis=-1, keepdims=True) + kr_ss) * (1.0 / B_QK) + NORM_EPS)
        k_o[:, 2 * h * LANES:(2 * h + 1) * LANES] = (kn * rk * gkn_ref[...]).astype(BF16)
        kpe = kpe_lo if h % 2 == 0 else kpe_hi
        k_o[:, (2 * h + 1) * LANES:(2 * h + 2) * LANES] = jnp.where(
            shift_lane, -shift_ref[...], kpe * rk).astype(BF16)
        v_o[:, hs] = kv[:, nope_w + h * LANES:nope_w + (h + 1) * LANES].astype(BF16)


def _mla_q(proj, g_cq, w_uq, c64, s64, gqn, gqp, tm):
    m = proj.shape[0]
    vec = lambda w: pl.BlockSpec((1, w), lambda i: (0, 0))
    rows = lambda w: pl.BlockSpec((tm, w), lambda i: (i, 0))
    return pl.pallas_call(
        functools.partial(_mla_q_kernel, q_scale=B_QK ** -0.5 * LOG2E),
        name="mla_q",
        grid=(m // tm,),
        in_specs=[
            pl.BlockSpec((tm, B_Q_LORA), lambda i: (i, COL_CQ // B_Q_LORA)),
            vec(B_Q_LORA),
            pl.BlockSpec(w_uq.shape, lambda i: (0, 0)),
            rows(LANES), rows(LANES), vec(LANES), vec(LANES),
        ],
        out_specs=rows(B_HEADS * 256),
        out_shape=jax.ShapeDtypeStruct((m, B_HEADS * 256), BF16),
        compiler_params=_cparams(("parallel",)),
    )(proj, g_cq, w_uq, c64, s64, gqn, gqp)


def _mla_kv(proj, g_ckv, w_ukv, c64, s64, gkn, gkp, shift, tm):
    m = proj.shape[0]
    vec = lambda w: pl.BlockSpec((1, w), lambda i: (0, 0))
    rows = lambda w: pl.BlockSpec((tm, w), lambda i: (i, 0))
    return pl.pallas_call(
        _mla_kv_kernel,
        name="mla_kv",
        grid=(m // tm,),
        in_specs=[
            pl.BlockSpec((tm, B_KV_LORA), lambda i: (i, COL_CKV // B_KV_LORA)),
            vec(B_KV_LORA),
            pl.BlockSpec(w_ukv.shape, lambda i: (0, 0)),
            pl.BlockSpec((tm, LANES), lambda i: (i, COL_KR // LANES)),
            rows(LANES), rows(LANES), vec(LANES), vec(LANES), vec(LANES),
        ],
        out_specs=[rows(B_HEADS * 256), rows(B_HEADS * B_V)],
        out_shape=[
            jax.ShapeDtypeStruct((m, B_HEADS * 256), BF16),
            jax.ShapeDtypeStruct((m, B_HEADS * B_V), BF16),
        ],
        compiler_params=_cparams(("parallel",)),
    )(proj, g_ckv, w_ukv, proj, c64, s64, gkn, gkp, shift)


def _indexer_kernel(q_ref, klo_ref, khi_ref, w_ref, sc_ref, th_ref, *, tq, tk, seq, k_sel):
    qi = pl.program_id(1)
    n_vis = (qi * tq) // tk + 1
    n_all = seq // tk
    row = qi * tq + lax.broadcasted_iota(jnp.int32, (tq, 1), 0)
    wcols = [w_ref[0, :, IDX_DIM + h:IDX_DIM + h + 1] for h in range(IDX_HEADS)]

    def gen(c, carry):
        rmax, rmin = carry
        off = pl.multiple_of(c * tk, tk)
        kl = klo_ref[0, pl.ds(off, tk), :]
        kh = khi_ref[0, pl.ds(off, tk), :]
        acc = jnp.zeros((tq, tk), F32)
        for p in range(IDX_HEADS // 2):
            qp = q_ref[0, :, p * LANES:(p + 1) * LANES]
            acc = acc + wcols[2 * p] * jnp.maximum(_dot_nt(qp, kl), 0.0)
            acc = acc + wcols[2 * p + 1] * jnp.maximum(_dot_nt(qp, kh), 0.0)
        col = off + lax.broadcasted_iota(jnp.int32, (tq, tk), 1)
        vis = col <= row
        sc_ref[0, :, pl.ds(off, tk)] = jnp.where(vis, acc, NEG)
        rmax = jnp.maximum(rmax, jnp.max(jnp.where(vis, acc, NEG), axis=-1, keepdims=True))
        rmin = jnp.minimum(rmin, jnp.min(jnp.where(vis, acc, POS_BIG), axis=-1, keepdims=True))
        return rmax, rmin

    rmax, rmin = lax.fori_loop(0, n_vis, gen,
                               (jnp.full((tq, 1), NEG, F32), jnp.full((tq, 1), POS_BIG, F32)))

    def fill(c, carry):
        sc_ref[0, :, pl.ds(pl.multiple_of(c * tk, tk), tk)] = jnp.full((tq, tk), NEG, F32)
        return carry

    lax.fori_loop(n_vis, n_all, fill, 0)

    cw = min(2 * tk, seq)
    n_cnt = (n_vis * tk + cw - 1) // cw

    def count_ge(mid, strict=False, open_groups=None):
        zeros = jnp.zeros((CNT_ROWS, LANES), F32)
        accs = []
        for r in range(tq // CNT_ROWS):
            rs = slice(r * CNT_ROWS, (r + 1) * CNT_ROWS)
            midb = mid[rs]

            def body(c, acc, rs=rs, midb=midb):
                off = pl.multiple_of(c * cw, cw)
                for j in range(cw // LANES):
                    blk = sc_ref[0, rs, pl.ds(off + j * LANES, LANES)]
                    acc = acc + jnp.where(blk > midb if strict else blk >= midb, 1.0, 0.0)
                return acc

            if open_groups is None:
                accs.append(lax.fori_loop(0, n_cnt, body, zeros))
            else:
                accs.append(lax.cond(open_groups[r] > 0.0,
                                     functools.partial(lax.fori_loop, 0, n_cnt, body, zeros),
                                     lambda: zeros))
        return jnp.concatenate(
            [jnp.broadcast_to(jnp.sum(a, axis=-1, keepdims=True), (CNT_ROWS, LANES)) for a in accs], axis=0)

    rowb = qi * tq + lax.broadcasted_iota(jnp.int32, (tq, LANES), 0)
    active0 = (rowb + 1) > k_sel
    lo0 = jnp.where(active0, jnp.broadcast_to(rmin, (tq, LANES)), TH_ALL)
    hi0 = jnp.where(active0, jnp.broadcast_to(rmax, (tq, LANES)), TH_ALL)

    def n_open(lo, hi):
        still = jnp.where(lo < hi, 1.0, 0.0)
        return tuple(jnp.sum(still[r * CNT_ROWS:(r + 1) * CNT_ROWS]) for r in range(tq // CNT_ROWS))

    def cond(st):
        it, nopen = st[0], st[1]
        return jnp.logical_and(functools.reduce(jnp.add, nopen) > 0.0, it < MAX_SEARCH_STEPS)

    def body(st):
        it, was_open, lo, hi, cnt_lo = st
        mid = lo + (hi - lo) * 0.5
        cnt = count_ge(mid, open_groups=was_open)
        nopen = n_open(lo, hi)
        ge = cnt >= k_sel
        lo_n = jnp.where(ge, mid, lo)
        hi_n = jnp.where(ge, hi, mid)
        hi_n = jnp.where(cnt == k_sel, lo_n, hi_n)
        hi_n = jnp.where(mid >= hi, lo_n, hi_n)
        hi_n = jnp.where(mid <= lo, lo_n, hi_n)
        return it + 1, nopen, lo_n, hi_n, jnp.where(ge, cnt, cnt_lo)

    st = lax.while_loop(cond, body, (jnp.int32(0), n_open(lo0, hi0), lo0, hi0,
                                     (rowb + 1).astype(F32)))
    th = st[2]
    th_ref[0] = th

    tied = jnp.logical_and(active0, st[4] > k_sel)

    @pl.when(jnp.sum(jnp.where(tied, 1.0, 0.0)) > 0.0)
    def _():
        keep_ties = k_sel - count_ge(th, strict=True)
        reps = tk // LANES
        th_w = jnp.concatenate([th] * reps, axis=1)
        keep_w = jnp.concatenate([keep_ties] * reps, axis=1)
        earlier = (lax.broadcasted_iota(jnp.int32, (tk, tk), 0)
                   < lax.broadcasted_iota(jnp.int32, (tk, tk), 1))
        tri = jnp.where(earlier, 1.0, 0.0).astype(BF16)

        def demote(c, seen):
            off = pl.multiple_of(c * tk, tk)
            blk = sc_ref[0, :, pl.ds(off, tk)]
            eq = blk == th_w
            eqf = jnp.where(eq, 1.0, 0.0)
            rank = _dot(eqf.astype(BF16), tri) + jnp.concatenate([seen] * reps, axis=1)
            sc_ref[0, :, pl.ds(off, tk)] = jnp.where(jnp.logical_and(eq, rank >= keep_w), NEG, blk)
            return seen + jnp.sum(eqf, axis=-1, keepdims=True)

        lax.fori_loop(0, n_vis, demote, jnp.zeros((tq, LANES), F32))


def _indexer(qi_r, klo, khi, wi, tq, tk, k_sel):
    b, s, _ = qi_r.shape
    return pl.pallas_call(
        functools.partial(_indexer_kernel, tq=tq, tk=tk, seq=s, k_sel=float(k_sel)),
        name="indexer",
        grid=(b, s // tq),
        in_specs=[
            pl.BlockSpec((1, tq, IDX_HEADS * IDX_DIM), lambda bi, qi: (bi, qi, 0)),
            pl.BlockSpec((1, s, LANES), lambda bi, qi: (bi, 0, 0)),
            pl.BlockSpec((1, s, LANES), lambda bi, qi: (bi, 0, 0)),
            pl.BlockSpec((1, tq, LANES), lambda bi, qi: (bi, qi, 0)),
        ],
        out_specs=[
            pl.BlockSpec((1, tq, s), lambda bi, qi: (bi, qi, 0)),
            pl.BlockSpec((1, tq, LANES), lambda bi, qi: (bi, qi, 0)),
        ],
        out_shape=[jax.ShapeDtypeStruct((b, s, s), F32), jax.ShapeDtypeStruct((b, s, LANES), F32)],
        compiler_params=_cparams(("parallel", "parallel")),
    )(qi_r, klo, khi, wi)


def _causal_pairs(s, tq, tk):
    qs, ks = [], []
    for qi in range(s // tq):
        for ki in range((qi * tq + tq - 1) // tk + 1):
            qs.append(qi)
            ks.append(ki)
    return jnp.asarray(np.array(qs, np.int32)), jnp.asarray(np.array(ks, np.int32))


def _softmax_step(s, v, rows, m_ref, l_ref, acc_ref, fixed_shift):
    blocks = [s[:, j * LANES:(j + 1) * LANES] for j in range(s.shape[1] // LANES)]
    if fixed_shift:
        ps = [jnp.exp2(blk) for blk in blocks]
        l_ref[rows] = l_ref[rows] + functools.reduce(jnp.add, ps)
        p = jnp.concatenate([blk.astype(BF16) for blk in ps], axis=1)
        acc_ref[rows] = acc_ref[rows] + _dot(p, v)
        return
    mx = functools.reduce(jnp.maximum, blocks)
    m_prev = m_ref[rows]
    m_new = jnp.maximum(m_prev, jnp.max(mx, axis=-1, keepdims=True))
    alpha = jnp.exp2(m_prev - m_new)
    ps = [jnp.exp2(blk - m_new) for blk in blocks]
    l_ref[rows] = alpha * l_ref[rows] + jnp.sum(functools.reduce(jnp.add, ps), axis=-1, keepdims=True)
    p = jnp.concatenate([blk.astype(BF16) for blk in ps], axis=1)
    acc_ref[rows] = alpha * acc_ref[rows] + _dot(p, v)
    m_ref[rows] = m_new


def _softmax_finish(rows, l_ref, acc_ref, fixed_shift):
    l = l_ref[rows]
    if fixed_shift:
        l = jnp.sum(l, axis=-1, keepdims=True)
    return acc_ref[rows] / l


def _dsa_kernel(qt_ref, kt_ref, q_ref, k_ref, v_ref, sc_ref, th_ref, shift_ref, o_ref, m_ref, l_ref,
                acc_ref, bias_ref, qs_ref, *, tq, tk, fixed_shift):
    step = pl.program_id(1)
    qi = qt_ref[step]
    ki = kt_ref[step]

    @pl.when(ki == 0)
    def _():
        m_ref[...] = jnp.full(m_ref.shape, M_INIT, F32)
        l_ref[...] = jnp.zeros(l_ref.shape, F32)
        acc_ref[...] = jnp.zeros(acc_ref.shape, F32)
        for h in range(A_HEADS):
            qs_ref[h * tq:(h + 1) * tq, :] = q_ref[0, :, h * LANES:(h + 1) * LANES]

    bias_ref[...] = jnp.where(sc_ref[0] >= th_ref[0][:, :1], -shift_ref[:, :1], NEG)
    grp = A_HEADS // A_KV_HEADS
    for g in range(A_KV_HEADS):
        rows = slice(g * grp * tq, (g + 1) * grp * tq)
        k = k_ref[0, :, g * LANES:(g + 1) * LANES]
        v = v_ref[0, :, g * LANES:(g + 1) * LANES]
        s = _dot_nt(qs_ref[rows, :], k)
        s = jnp.concatenate([s[j * tq:(j + 1) * tq] + bias_ref[...] for j in range(grp)], axis=0)
        _softmax_step(s, v, rows, m_ref, l_ref, acc_ref, fixed_shift)

    @pl.when(ki == (qi * tq + tq - 1) // tk)
    def _():
        for h in range(A_HEADS):
            out = _softmax_finish(slice(h * tq, (h + 1) * tq), l_ref, acc_ref, fixed_shift)
            o_ref[0, :, h * LANES:(h + 1) * LANES] = out.astype(o_ref.dtype)


def _dsa_attn(q, k, v, scores, th, shift, tq, tk, fixed_shift):
    b, s, dq = q.shape
    qt, kt = _causal_pairs(s, tq, tk)
    grid_spec = pltpu.PrefetchScalarGridSpec(
        num_scalar_prefetch=2,
        grid=(b, qt.shape[0]),
        in_specs=[
            pl.BlockSpec((1, tq, dq), lambda bi, p, qt, kt: (bi, qt[p], 0)),
            pl.BlockSpec((1, tk, k.shape[2]), lambda bi, p, qt, kt: (bi, kt[p], 0)),
            pl.BlockSpec((1, tk, v.shape[2]), lambda bi, p, qt, kt: (bi, kt[p], 0)),
            pl.BlockSpec((1, tq, tk), lambda bi, p, qt, kt: (bi, qt[p], kt[p])),
            pl.BlockSpec((1, tq, LANES), lambda bi, p, qt, kt: (bi, qt[p], 0)),
            pl.BlockSpec((1, LANES), lambda bi, p, qt, kt: (0, 0)),
        ],
        out_specs=pl.BlockSpec((1, tq, dq), lambda bi, p, qt, kt: (bi, qt[p], 0)),
        scratch_shapes=[
            pltpu.VMEM((A_HEADS * tq, LANES), F32),
            pltpu.VMEM((A_HEADS * tq, LANES), F32),
            pltpu.VMEM((A_HEADS * tq, A_HEAD_DIM), F32),
            pltpu.VMEM((tq, tk), F32),
            pltpu.VMEM((A_HEADS * tq, A_HEAD_DIM), BF16),
        ],
    )
    return pl.pallas_call(
        functools.partial(_dsa_kernel, tq=tq, tk=tk, fixed_shift=fixed_shift),
        name="dsa_attn_fixed" if fixed_shift else "dsa_attn_online",
        grid_spec=grid_spec,
        out_shape=jax.ShapeDtypeStruct((b, s, dq), BF16),
        compiler_params=_cparams(("parallel", "arbitrary")),
    )(qt, kt, q, k, v, scores, th, shift)


def _mla_kernel(qt_ref, kt_ref, q_ref, k_ref, v_ref, o_ref, m_ref, l_ref, acc_ref, *, tq, tk, hg,
                fixed_shift):
    step = pl.program_id(2)
    qi = qt_ref[step]
    ki = kt_ref[step]

    @pl.when(ki == 0)
    def _():
        m_ref[...] = jnp.full(m_ref.shape, M_INIT, F32)
        l_ref[...] = jnp.zeros(l_ref.shape, F32)
        acc_ref[...] = jnp.zeros(acc_ref.shape, F32)

    def heads(masked):
        if masked:
            keep = (lax.broadcasted_iota(jnp.int32, (tq, tk), 1)
                    <= lax.broadcasted_iota(jnp.int32, (tq, tk), 0))
        for h in range(hg):
            q = q_ref[0, :, h * 256:(h + 1) * 256]
            k = k_ref[0, :, h * 256:(h + 1) * 256]
            v = v_ref[0, :, h * B_V:(h + 1) * B_V]
            s = _dot_nt(q, k)
            if masked:
                s = jnp.where(keep, s, NEG)
            _softmax_step(s, v, slice(h * tq, (h + 1) * tq), m_ref, l_ref, acc_ref, fixed_shift)

    @pl.when(ki < qi)
    def _():
        heads(False)

    @pl.when(ki == qi)
    def _():
        heads(True)
        for h in range(hg):
            out = _softmax_finish(slice(h * tq, (h + 1) * tq), l_ref, acc_ref, fixed_shift)
            o_ref[0, :, h * B_V:(h + 1) * B_V] = out.astype(o_ref.dtype)


def _mla_attn(q, k, v, tq, tk, hg, fixed_shift):
    b, s, _ = q.shape
    assert tq == tk
    qt, kt = _causal_pairs(s, tq, tk)
    grid_spec = pltpu.PrefetchScalarGridSpec(
        num_scalar_prefetch=2,
        grid=(b, B_HEADS // hg, qt.shape[0]),
        in_specs=[
            pl.BlockSpec((1, tq, hg * 256), lambda bi, g, p, qt, kt: (bi, qt[p], g)),
            pl.BlockSpec((1, tk, hg * 256), lambda bi, g, p, qt, kt: (bi, kt[p], g)),
            pl.BlockSpec((1, tk, hg * B_V), lambda bi, g, p, qt, kt: (bi, kt[p], g)),
        ],
        out_specs=pl.BlockSpec((1, tq, hg * B_V), lambda bi, g, p, qt, kt: (bi, qt[p], g)),
        scratch_shapes=[
            pltpu.VMEM((hg * tq, LANES), F32),
            pltpu.VMEM((hg * tq, LANES), F32),
            pltpu.VMEM((hg * tq, B_V), F32),
        ],
    )
    return pl.pallas_call(
        functools.partial(_mla_kernel, tq=tq, tk=tk, hg=hg, fixed_shift=fixed_shift),
        name="mla_attn_fixed" if fixed_shift else "mla_attn_online",
        grid_spec=grid_spec,
        out_shape=jax.ShapeDtypeStruct((b, s, B_HEADS * B_V), BF16),
        compiler_params=_cparams(("parallel", "parallel", "arbitrary")),
    )(qt, kt, q, k, v)


def _regroup_w_in(w):
    sizes = (A_HEADS * A_HEAD_DIM, A_KV_HEADS * A_HEAD_DIM, A_KV_HEADS * A_HEAD_DIM,
             IDX_HEADS * IDX_DIM, IDX_DIM, IDX_HEADS, B_Q_LORA, B_KV_LORA, B_ROPE)
    offs = np.cumsum((0,) + sizes)
    w = w.astype(BF16)
    qa, ka, va, qi, ki, wi, cq, ckv, kr = (w[:, offs[j]:offs[j + 1]] for j in range(len(sizes)))
    gates = w[:, offs[-1]:]
    d = w.shape[0]
    kiw = jnp.concatenate([ki, wi, jnp.zeros((d, LANES - IDX_DIM - IDX_HEADS), w.dtype)], axis=1)
    krp = jnp.concatenate([kr, jnp.zeros((d, LANES - B_ROPE), w.dtype)], axis=1)
    return jnp.concatenate([qa, gates, qi, ka, va, cq, ckv, kiw, krp], axis=1).astype(BF16)


def _softmax_shift(d, g_q, g_k):
    bound = (d ** 0.5) * LOG2E * 1.02 * jnp.max(jnp.abs(g_q)) * jnp.max(jnp.abs(g_k))
    c = jnp.ceil(bound.astype(F32))
    ok = c <= MAX_FIXED_SHIFT
    return jnp.full((1, LANES), jnp.where(ok, c, 0.0), F32), ok


def _rope_tables(pos, d):
    half = d // 2
    freqs = jnp.power(jnp.float32(ROPE_THETA), -jnp.arange(half, dtype=F32) * (2.0 / d))
    ang = pos.astype(F32)[..., None] * freqs
    cos, sin = jnp.cos(ang), jnp.sin(ang)
    reps = LANES // d
    c = jnp.tile(jnp.concatenate([cos, cos], axis=-1), (1, 1, reps))
    s = jnp.tile(jnp.concatenate([-sin, sin], axis=-1), (1, 1, reps))
    return c.reshape(-1, LANES), s.reshape(-1, LANES)


def _layer(x, p, positions, g_mix_norm, w_in, g_qa, g_ka, g_cq, w_uq, g_ckv, w_ukv, g_qb, g_kb,
           w_out_a, w_out_b, w_o, g_ffn_norm, w_ffn_gate, w_ffn_up, w_ffn_down, g_ple_norm,
           w_ple_gate, w_ple_proj):
    b, s, d = x.shape
    m = b * s
    k_sel = min(TOPK_MAX, s // 4)
    xf = x.reshape(m, d)
    row = lambda g: g.reshape(1, -1).astype(F32)

    c128, s128 = _rope_tables(positions, A_HEAD_DIM)
    c64, s64 = _rope_tables(positions, IDX_DIM)

    proj = _norm_matmul("in_proj", xf, 0, d, row(g_mix_norm), _regroup_w_in(w_in), F32, tm=1024, tn=1536)

    qa_r, ka_r, va_b, qi_r, klo, khi, wi = _prep_a(
        proj, c128, s128, c64, s64, row(g_qa) * (A_HEAD_DIM ** -0.5 * LOG2E), row(g_ka), tp=512)
    r3 = lambda a: a.reshape(b, s, a.shape[-1])
    scores, th = _indexer(r3(qi_r), r3(klo), r3(khi), r3(wi), tq=512, tk=512, k_sel=k_sel)
    shift_a, fixed_a = _softmax_shift(A_HEAD_DIM, g_qa, g_ka)
    y_a = lax.cond(
        fixed_a,
        functools.partial(_dsa_attn, tq=512, tk=512, fixed_shift=True),
        functools.partial(_dsa_attn, tq=512, tk=512, fixed_shift=False),
        r3(qa_r), r3(ka_r), r3(va_b), scores, th, shift_a)

    w_uq_r = w_uq.reshape(B_Q_LORA, B_HEADS, B_QK)
    w_uq_r = jnp.concatenate([w_uq_r[:, :, :B_NOPE].reshape(B_Q_LORA, -1),
                              w_uq_r[:, :, B_NOPE:].reshape(B_Q_LORA, -1)], axis=1).astype(BF16)
    w_ukv_r = w_ukv.reshape(B_KV_LORA, B_HEADS, 2, B_NOPE)
    w_ukv_r = jnp.concatenate([w_ukv_r[:, :, 0, :].reshape(B_KV_LORA, -1),
                               w_ukv_r[:, :, 1, :].reshape(B_KV_LORA, -1)], axis=1).astype(BF16)
    pe2 = lambda g: jnp.tile(g[B_NOPE:], 2).reshape(1, LANES).astype(F32)
    shift_b, fixed_b = _softmax_shift(B_QK, g_qb, g_kb)
    q_m = _mla_q(proj, row(g_cq), w_uq_r, c64, s64, row(g_qb[:B_NOPE]), pe2(g_qb), tm=512)
    k_m, v_m = _mla_kv(proj, row(g_ckv), w_ukv_r, c64, s64, row(g_kb[:B_NOPE]), pe2(g_kb), shift_b, tm=512)
    y_b = lax.cond(
        fixed_b,
        functools.partial(_mla_attn, tq=512, tk=512, hg=16, fixed_shift=True),
        functools.partial(_mla_attn, tq=512, tk=512, hg=16, fixed_shift=False),
        r3(q_m), r3(k_m), r3(v_m))

    mixed = _merge(y_a.reshape(m, -1), y_b.reshape(m, -1), w_out_a.astype(BF16), w_out_b.astype(BF16),
                   proj, tm=1024, tn=512)
    x1 = _matmul_res("w_o", mixed, w_o.astype(BF16), xf, tm=1024, tn=1024)

    act = _ffn_up(x1, row(g_ffn_norm), w_ffn_gate.astype(BF16), w_ffn_up.astype(BF16), tm=1024, tn=512)
    x2 = _matmul_res("ffn_down", act, w_ffn_down.astype(BF16), x1, tm=512, tn=1024)

    out = _ple(x2, row(g_ple_norm), w_ple_gate.astype(BF16), p.reshape(m, -1),
               w_ple_proj.astype(BF16), tm=512)
    return out.reshape(b, s, d)


def kernel(x, p, positions, g_mix_norm, w_in, g_qa, g_ka, g_cq, w_uq, g_ckv, w_ukv, g_qb, g_kb,
           w_out_a, w_out_b, w_o, g_ffn_norm, w_ffn_gate, w_ffn_up, w_ffn_down, g_ple_norm,
           w_ple_gate, w_ple_proj):
    for i in range(p.shape[0]):
        x = _layer(x, p[i], positions, g_mix_norm[i], w_in[i], g_qa[i], g_ka[i], g_cq[i], w_uq[i],
                   g_ckv[i], w_ukv[i], g_qb[i], g_kb[i], w_out_a[i], w_out_b[i], w_o[i],
                   g_ffn_norm[i], w_ffn_gate[i], w_ffn_up[i], w_ffn_down[i], g_ple_norm[i],
                   w_ple_gate[i], w_ple_proj[i])
    return x
```

```python
import functools

import numpy as np
import jax
import jax.numpy as jnp
from jax import lax
from jax.experimental import pallas as pl
from jax.experimental.pallas import tpu as pltpu

F32 = jnp.float32
BF16 = jnp.bfloat16

ROPE_THETA = 10000.0
NORM_EPS = 1e-6
NEG = -1e30
M_INIT = -1e29
POS_BIG = 3e38
TH_ALL = 0.5 * NEG
LOG2E = 1.4426950408889634
MAX_SEARCH_STEPS = 512
CNT_ROWS = 128
MAX_FIXED_SHIFT = 60.0
A_HEADS = 16
A_KV_HEADS = 4
A_HEAD_DIM = 128
IDX_HEADS = 16
IDX_DIM = 64
TOPK_MAX = 256
B_HEADS = 16
B_Q_LORA = 512
B_KV_LORA = 256
B_NOPE = 128
B_ROPE = 64
B_V = 128
B_QK = B_NOPE + B_ROPE
LANES = 128
VMEM_LIMIT = 56 * 1024 * 1024

COL_QA = 0
COL_GATE_A = 2048
COL_GATE_B = 4096
COL_QI = 6144
COL_KA = 7168
COL_VA = 7680
COL_CQ = 8192
COL_CKV = 8704
COL_KIW = 8960
COL_KR = 9088
N_PROJ = 9216


def _dot(a, b):
    return jnp.dot(a, b, preferred_element_type=F32)


def _dot_nt(a, b):
    return lax.dot_general(a, b, (((1,), (1,)), ((), ())), preferred_element_type=F32)


def _cparams(sem):
    return pltpu.CompilerParams(dimension_semantics=sem, vmem_limit_bytes=VMEM_LIMIT)


def _rms_to_bf16(x, g):
    ms = jnp.mean(x * x, axis=-1, keepdims=True)
    return (x * lax.rsqrt(ms + NORM_EPS) * g).astype(BF16)


def _norm_matmul_kernel(x_ref, g_ref, w_ref, o_ref, hn_ref):
    @pl.when(pl.program_id(1) == 0)
    def _():
        hn_ref[...] = _rms_to_bf16(x_ref[...], g_ref[...])

    o_ref[...] = _dot(hn_ref[...], w_ref[...]).astype(o_ref.dtype)


def _norm_matmul(name, x, x_col_block, k, g, w, out_dtype, tm, tn):
    m = x.shape[0]
    n = w.shape[1]
    return pl.pallas_call(
        _norm_matmul_kernel,
        name=name,
        grid=(m // tm, n // tn),
        in_specs=[
            pl.BlockSpec((tm, k), lambda i, j: (i, x_col_block)),
            pl.BlockSpec((1, k), lambda i, j: (0, 0)),
            pl.BlockSpec((k, tn), lambda i, j: (0, j)),
        ],
        out_specs=pl.BlockSpec((tm, tn), lambda i, j: (i, j)),
        out_shape=jax.ShapeDtypeStruct((m, n), out_dtype),
        scratch_shapes=[pltpu.VMEM((tm, k), BF16)],
        compiler_params=_cparams(("parallel", "arbitrary")),
    )(x, g, w)


def _ffn_up_kernel(x_ref, g_ref, wg_ref, wu_ref, o_ref, hn_ref):
    @pl.when(pl.program_id(1) == 0)
    def _():
        hn_ref[...] = _rms_to_bf16(x_ref[...], g_ref[...])

    h = hn_ref[...]
    a = _dot(h, wg_ref[...])
    b = _dot(h, wu_ref[...])
    o_ref[...] = (a * jax.nn.sigmoid(a) * b).astype(o_ref.dtype)


def _ffn_up(x, g, wg, wu, tm, tn):
    m, k = x.shape
    n = wg.shape[1]
    return pl.pallas_call(
        _ffn_up_kernel,
        name="ffn_up",
        grid=(m // tm, n // tn),
        in_specs=[
            pl.BlockSpec((tm, k), lambda i, j: (i, 0)),
            pl.BlockSpec((1, k), lambda i, j: (0, 0)),
            pl.BlockSpec((k, tn), lambda i, j: (0, j)),
            pl.BlockSpec((k, tn), lambda i, j: (0, j)),
        ],
        out_specs=pl.BlockSpec((tm, tn), lambda i, j: (i, j)),
        out_shape=jax.ShapeDtypeStruct((m, n), BF16),
        scratch_shapes=[pltpu.VMEM((tm, k), BF16)],
        compiler_params=_cparams(("parallel", "arbitrary")),
    )(x, g, wg, wu)


def _matmul_res_kernel(a_ref, w_ref, r_ref, o_ref):
    o_ref[...] = r_ref[...] + _dot(a_ref[...], w_ref[...])


def _matmul_res(name, a, w, res, tm, tn):
    m, k = a.shape
    n = w.shape[1]
    return pl.pallas_call(
        _matmul_res_kernel,
        name=name,
        grid=(n // tn, m // tm),
        in_specs=[
            pl.BlockSpec((tm, k), lambda j, i: (i, 0)),
            pl.BlockSpec((k, tn), lambda j, i: (0, j)),
            pl.BlockSpec((tm, tn), lambda j, i: (i, j)),
        ],
        out_specs=pl.BlockSpec((tm, tn), lambda j, i: (i, j)),
        out_shape=jax.ShapeDtypeStruct((m, n), F32),
        compiler_params=_cparams(("parallel", "parallel")),
    )(a, w, res)


def _merge_kernel(ya_ref, yb_ref, wa_ref, wb_ref, ga_ref, gb_ref, o_ref):
    a = _dot(ya_ref[...], wa_ref[...])
    b = _dot(yb_ref[...], wb_ref[...])
    o_ref[...] = (jax.nn.sigmoid(ga_ref[...]) * a + jax.nn.sigmoid(gb_ref[...]) * b).astype(o_ref.dtype)


def _merge(ya, yb, wa, wb, proj, tm, tn):
    m, k = ya.shape
    n = wa.shape[1]
    ga0 = COL_GATE_A // tn
    gb0 = COL_GATE_B // tn
    return pl.pallas_call(
        _merge_kernel,
        name="merge",
        grid=(m // tm, n // tn),
        in_specs=[
            pl.BlockSpec((tm, k), lambda i, j: (i, 0)),
            pl.BlockSpec((tm, k), lambda i, j: (i, 0)),
            pl.BlockSpec((k, tn), lambda i, j: (0, j)),
            pl.BlockSpec((k, tn), lambda i, j: (0, j)),
            pl.BlockSpec((tm, tn), lambda i, j: (i, ga0 + j)),
            pl.BlockSpec((tm, tn), lambda i, j: (i, gb0 + j)),
        ],
        out_specs=pl.BlockSpec((tm, tn), lambda i, j: (i, j)),
        out_shape=jax.ShapeDtypeStruct((m, n), BF16),
        compiler_params=_cparams(("parallel", "parallel")),
    )(ya, yb, wa, wb, proj, proj)


def _ple_kernel(x_ref, g_ref, wg_ref, p_ref, wp_ref, o_ref):
    x = x_ref[...]
    gate = jax.nn.sigmoid(_dot(_rms_to_bf16(x, g_ref[...]), wg_ref[...]))
    emb = _dot(p_ref[...].astype(BF16), wp_ref[...])
    o_ref[...] = x + gate * emb


def _ple(x, g, wg, p, wp, tm):
    m, k = x.shape
    n = wg.shape[1]
    kp = p.shape[1]
    assert n == k
    return pl.pallas_call(
        _ple_kernel,
        name="ple",
        grid=(m // tm,),
        in_specs=[
            pl.BlockSpec((tm, k), lambda i: (i, 0)),
            pl.BlockSpec((1, k), lambda i: (0, 0)),
            pl.BlockSpec((k, n), lambda i: (0, 0)),
            pl.BlockSpec((tm, kp), lambda i: (i, 0)),
            pl.BlockSpec((kp, n), lambda i: (0, 0)),
        ],
        out_specs=pl.BlockSpec((tm, n), lambda i: (i, 0)),
        out_shape=jax.ShapeDtypeStruct((m, n), F32),
        compiler_params=_cparams(("parallel",)),
    )(x, g, wg, p, wp)


def _rope64_pairs(x, c64, s64, first_half):
    partner = jnp.where(first_half, pltpu.roll(x, 96, 1), pltpu.roll(x, 32, 1))
    return x * c64 + partner * s64


def _prep_a_kernel(qa_ref, ka_ref, va_ref, qi_ref, kiw_ref, c128_ref, s128_ref, c64_ref, s64_ref,
                   gq_ref, gk_ref, qa_o, ka_o, va_o, qi_o, klo_o, khi_o, wi_o, *, wi_scale):
    c128 = c128_ref[...]
    s128 = s128_ref[...]

    def norm_rope(x, g):
        ms = jnp.mean(x * x, axis=-1, keepdims=True)
        y = x * lax.rsqrt(ms + NORM_EPS) * g
        return y * c128 + pltpu.roll(y, 64, 1) * s128

    for h in range(A_HEADS):
        hs = slice(h * LANES, (h + 1) * LANES)
        qa_o[:, hs] = norm_rope(qa_ref[:, hs], gq_ref[...]).astype(BF16)
    for h in range(A_KV_HEADS):
        hs = slice(h * LANES, (h + 1) * LANES)
        ka_o[:, hs] = norm_rope(ka_ref[:, hs], gk_ref[...]).astype(BF16)
    va_o[...] = va_ref[...].astype(BF16)

    c64 = c64_ref[...]
    s64 = s64_ref[...]
    lane = lax.broadcasted_iota(jnp.int32, c64.shape, 1)
    first_half = (lane % 64) < 32
    for p in range(IDX_HEADS * IDX_DIM // LANES):
        ps = slice(p * LANES, (p + 1) * LANES)
        qi_o[:, ps] = _rope64_pairs(qi_ref[:, ps], c64, s64, first_half).astype(BF16)
    kiw = kiw_ref[...]
    ki = jnp.where(lane < 64, _rope64_pairs(kiw, c64, s64, first_half), 0.0)
    klo_o[...] = ki.astype(BF16)
    khi_o[...] = pltpu.roll(ki, 64, 1).astype(BF16)
    wi_o[...] = kiw * wi_scale


def _prep_a(proj, c128, s128, c64, s64, gq, gk, tp):
    m = proj.shape[0]
    row = lambda w, cb: pl.BlockSpec((tp, w), lambda i: (i, cb))
    vec = pl.BlockSpec((1, LANES), lambda i: (0, 0))
    out = lambda w: pl.BlockSpec((tp, w), lambda i: (i, 0))
    wi_scale = (IDX_HEADS ** -0.5) * (IDX_DIM ** -0.5)
    return pl.pallas_call(
        functools.partial(_prep_a_kernel, wi_scale=wi_scale),
        name="prep_a",
        grid=(m // tp,),
        in_specs=[
            row(2048, COL_QA // 2048), row(512, COL_KA // 512), row(512, COL_VA // 512),
            row(1024, COL_QI // 1024), row(LANES, COL_KIW // LANES),
            out(LANES), out(LANES), out(LANES), out(LANES), vec, vec,
        ],
        out_specs=[out(2048), out(512), out(512), out(1024), out(LANES), out(LANES), out(LANES)],
        out_shape=[
            jax.ShapeDtypeStruct((m, 2048), BF16), jax.ShapeDtypeStruct((m, 512), BF16),
            jax.ShapeDtypeStruct((m, 512), BF16), jax.ShapeDtypeStruct((m, 1024), BF16),
            jax.ShapeDtypeStruct((m, LANES), BF16), jax.ShapeDtypeStruct((m, LANES), BF16),
            jax.ShapeDtypeStruct((m, LANES), F32),
        ],
        compiler_params=_cparams(("parallel",)),
    )(proj, proj, proj, proj, proj, c128, s128, c64, s64, gq, gk)


def _mla_q_kernel(cq_ref, g_ref, w_ref, c64_ref, s64_ref, gqn_ref, gqp_ref, q_o, *, q_scale):
    qb = _dot(_rms_to_bf16(cq_ref[...], g_ref[...]), w_ref[...])
    c64 = c64_ref[...]
    s64 = s64_ref[...]
    lane = lax.broadcasted_iota(jnp.int32, c64.shape, 1)
    first_half = (lane % 64) < 32
    lo = lane < 64
    nope_w = B_HEADS * B_NOPE
    for pair in range(B_HEADS // 2):
        qp2 = qb[:, nope_w + pair * LANES:nope_w + (pair + 1) * LANES]
        qp2_sq = qp2 * qp2
        rot2 = _rope64_pairs(qp2 * gqp_ref[...], c64, s64, first_half)
        for h in (2 * pair, 2 * pair + 1):
            mine = lo if h % 2 == 0 else jnp.logical_not(lo)
            shift_lane = lane == (64 if h % 2 == 0 else 0)
            qn = qb[:, h * LANES:(h + 1) * LANES]
            ss = jnp.sum(qn * qn + jnp.where(mine, qp2_sq, 0.0), axis=-1, keepdims=True)
            r = lax.rsqrt(ss * (1.0 / B_QK) + NORM_EPS) * q_scale
            q_o[:, 2 * h * LANES:(2 * h + 1) * LANES] = (qn * r * gqn_ref[...]).astype(BF16)
            qpe = jnp.where(mine, rot2 * r, jnp.where(shift_lane, 1.0, 0.0))
            q_o[:, (2 * h + 1) * LANES:(2 * h + 2) * LANES] = qpe.astype(BF16)


def _mla_kv_kernel(ckv_ref, g_ref, w_ref, kr_ref, c64_ref, s64_ref, gkn_ref, gkp_ref, shift_ref, k_o, v_o):
    kv = _dot(_rms_to_bf16(ckv_ref[...], g_ref[...]), w_ref[...])
    c64 = c64_ref[...]
    s64 = s64_ref[...]
    lane = lax.broadcasted_iota(jnp.int32, c64.shape, 1)
    first_half = (lane % 64) < 32
    lo = lane < 64
    kr = jnp.where(lo, kr_ref[...], 0.0)
    kr_ss = jnp.sum(kr * kr, axis=-1, keepdims=True)
    kpe_lo = jnp.where(lo, _rope64_pairs(kr * gkp_ref[...], c64, s64, first_half), 0.0)
    kpe_hi = pltpu.roll(kpe_lo, 64, 1)
    nope_w = B_HEADS * B_NOPE
    for h in range(B_HEADS):
        hs = slice(h * LANES, (h + 1) * LANES)
        shift_lane = lane == (64 if h % 2 == 0 else 0)
        kn = kv[:, hs]
        rk = lax.rsqrt((jnp.sum(kn * kn, axis=-1, keepdims=True) + kr_ss) * (1.0 / B_QK) + NORM_EPS)
        k_o[:, 2 * h * LANES:(2 * h + 1) * LANES] = (kn * rk * gkn_ref[...]).astype(BF16)
        kpe = kpe_lo if h % 2 == 0 else kpe_hi
        k_o[:, (2 * h + 1) * LANES:(2 * h + 2) * LANES] = jnp.where(
            shift_lane, -shift_ref[...], kpe * rk).astype(BF16)
        v_o[:, hs] = kv[:, nope_w + h * LANES:nope_w + (h + 1) * LANES].astype(BF16)


def _mla_q(proj, g_cq, w_uq, c64, s64, gqn, gqp, tm):
    m = proj.shape[0]
    vec = lambda w: pl.BlockSpec((1, w), lambda i: (0, 0))
    rows = lambda w: pl.BlockSpec((tm, w), lambda i: (i, 0))
    return pl.pallas_call(
        functools.partial(_mla_q_kernel, q_scale=B_QK ** -0.5 * LOG2E),
        name="mla_q",
        grid=(m // tm,),
        in_specs=[
            pl.BlockSpec((tm, B_Q_LORA), lambda i: (i, COL_CQ // B_Q_LORA)),
            vec(B_Q_LORA),
            pl.BlockSpec(w_uq.shape, lambda i: (0, 0)),
            rows(LANES), rows(LANES), vec(LANES), vec(LANES),
        ],
        out_specs=rows(B_HEADS * 256),
        out_shape=jax.ShapeDtypeStruct((m, B_HEADS * 256), BF16),
        compiler_params=_cparams(("parallel",)),
    )(proj, g_cq, w_uq, c64, s64, gqn, gqp)


def _mla_kv(proj, g_ckv, w_ukv, c64, s64, gkn, gkp, shift, tm):
    m = proj.shape[0]
    vec = lambda w: pl.BlockSpec((1, w), lambda i: (0, 0))
    rows = lambda w: pl.BlockSpec((tm, w), lambda i: (i, 0))
    return pl.pallas_call(
        _mla_kv_kernel,
        name="mla_kv",
        grid=(m // tm,),
        in_specs=[
            pl.BlockSpec((tm, B_KV_LORA), lambda i: (i, COL_CKV // B_KV_LORA)),
            vec(B_KV_LORA),
            pl.BlockSpec(w_ukv.shape, lambda i: (0, 0)),
            pl.BlockSpec((tm, LANES), lambda i: (i, COL_KR // LANES)),
            rows(LANES), rows(LANES), vec(LANES), vec(LANES), vec(LANES),
        ],
        out_specs=[rows(B_HEADS * 256), rows(B_HEADS * B_V)],
        out_shape=[
            jax.ShapeDtypeStruct((m, B_HEADS * 256), BF16),
            jax.ShapeDtypeStruct((m, B_HEADS * B_V), BF16),
        ],
        compiler_params=_cparams(("parallel",)),
    )(proj, g_ckv, w_ukv, proj, c64, s64, gkn, gkp, shift)


def _indexer_kernel(q_ref, klo_ref, khi_ref, w_ref, sc_ref, th_ref, *, tq, tk, seq, k_sel):
    qi = pl.program_id(1)
    n_vis = (qi * tq) // tk + 1
    n_all = seq // tk
    row = qi * tq + lax.broadcasted_iota(jnp.int32, (tq, 1), 0)
    wcols = [w_ref[0, :, IDX_DIM + h:IDX_DIM + h + 1] for h in range(IDX_HEADS)]

    def gen(c, carry):
        rmax, rmin = carry
        off = pl.multiple_of(c * tk, tk)
        kl = klo_ref[0, pl.ds(off, tk), :]
        kh = khi_ref[0, pl.ds(off, tk), :]
        acc = jnp.zeros((tq, tk), F32)
        for p in range(IDX_HEADS // 2):
            qp = q_ref[0, :, p * LANES:(p + 1) * LANES]
            acc = acc + wcols[2 * p] * jnp.maximum(_dot_nt(qp, kl), 0.0)
            acc = acc + wcols[2 * p + 1] * jnp.maximum(_dot_nt(qp, kh), 0.0)
        col = off + lax.broadcasted_iota(jnp.int32, (tq, tk), 1)
        vis = col <= row
        sc_ref[0, :, pl.ds(off, tk)] = jnp.where(vis, acc, NEG)
        rmax = jnp.maximum(rmax, jnp.max(jnp.where(vis, acc, NEG), axis=-1, keepdims=True))
        rmin = jnp.minimum(rmin, jnp.min(jnp.where(vis, acc, POS_BIG), axis=-1, keepdims=True))
        return rmax, rmin

    rmax, rmin = lax.fori_loop(0, n_vis, gen,
                               (jnp.full((tq, 1), NEG, F32), jnp.full((tq, 1), POS_BIG, F32)))

    def fill(c, carry):
        sc_ref[0, :, pl.ds(pl.multiple_of(c * tk, tk), tk)] = jnp.full((tq, tk), NEG, F32)
        return carry

    lax.fori_loop(n_vis, n_all, fill, 0)

    cw = min(2 * tk, seq)
    n_cnt = (n_vis * tk + cw - 1) // cw

    def count_ge(mid, strict=False, open_groups=None):
        zeros = jnp.zeros((CNT_ROWS, LANES), F32)
        accs = []
        for r in range(tq // CNT_ROWS):
            rs = slice(r * CNT_ROWS, (r + 1) * CNT_ROWS)
            midb = mid[rs]

            def body(c, acc, rs=rs, midb=midb):
                off = pl.multiple_of(c * cw, cw)
                for j in range(cw // LANES):
                    blk = sc_ref[0, rs, pl.ds(off + j * LANES, LANES)]
                    acc = acc + jnp.where(blk > midb if strict else blk >= midb, 1.0, 0.0)
                return acc

            if open_groups is None:
                accs.append(lax.fori_loop(0, n_cnt, body, zeros))
            else:
                accs.append(lax.cond(open_groups[r] > 0.0,
                                     functools.partial(lax.fori_loop, 0, n_cnt, body, zeros),
                                     lambda: zeros))
        return jnp.concatenate(
            [jnp.broadcast_to(jnp.sum(a, axis=-1, keepdims=True), (CNT_ROWS, LANES)) for a in accs], axis=0)

    rowb = qi * tq + lax.broadcasted_iota(jnp.int32, (tq, LANES), 0)
    active0 = (rowb + 1) > k_sel
    lo0 = jnp.where(active0, jnp.broadcast_to(rmin, (tq, LANES)), TH_ALL)
    hi0 = jnp.where(active0, jnp.broadcast_to(rmax, (tq, LANES)), TH_ALL)

    def n_open(lo, hi):
        still = jnp.where(lo < hi, 1.0, 0.0)
        return tuple(jnp.sum(still[r * CNT_ROWS:(r + 1) * CNT_ROWS]) for r in range(tq // CNT_ROWS))

    def cond(st):
        it, nopen = st[0], st[1]
        return jnp.logical_and(functools.reduce(jnp.add, nopen) > 0.0, it < MAX_SEARCH_STEPS)

    def body(st):
        it, was_open, lo, hi, cnt_lo = st
        mid = lo + (hi - lo) * 0.5
        cnt = count_ge(mid, open_groups=was_open)
        nopen = n_open(lo, hi)
        ge = cnt >= k_sel
        lo_n = jnp.where(ge, mid, lo)
        hi_n = jnp.where(ge, hi, mid)
        hi_n = jnp.where(cnt == k_sel, lo_n, hi_n)
        hi_n = jnp.where(mid >= hi, lo_n, hi_n)
        hi_n = jnp.where(mid <= lo, lo_n, hi_n)
        return it + 1, nopen, lo_n, hi_n, jnp.where(ge, cnt, cnt_lo)

    st = lax.while_loop(cond, body, (jnp.int32(0), n_open(lo0, hi0), lo0, hi0,
                                     (rowb + 1).astype(F32)))
    th = st[2]
    th_ref[0] = th

    tied = jnp.logical_and(active0, st[4] > k_sel)

    @pl.when(jnp.sum(jnp.where(tied, 1.0, 0.0)) > 0.0)
    def _():
        keep_ties = k_sel - count_ge(th, strict=True)
        reps = tk // LANES
        th_w = jnp.concatenate([th] * reps, axis=1)
        keep_w = jnp.concatenate([keep_ties] * reps, axis=1)
        earlier = (lax.broadcasted_iota(jnp.int32, (tk, tk), 0)
                   < lax.broadcasted_iota(jnp.int32, (tk, tk), 1))
        tri = jnp.where(earlier, 1.0, 0.0).astype(BF16)

        def demote(c, seen):
            off = pl.multiple_of(c * tk, tk)
            blk = sc_ref[0, :, pl.ds(off, tk)]
            eq = blk == th_w
            eqf = jnp.where(eq, 1.0, 0.0)
            rank = _dot(eqf.astype(BF16), tri) + jnp.concatenate([seen] * reps, axis=1)
            sc_ref[0, :, pl.ds(off, tk)] = jnp.where(jnp.logical_and(eq, rank >= keep_w), NEG, blk)
            return seen + jnp.sum(eqf, axis=-1, keepdims=True)

        lax.fori_loop(0, n_vis, demote, jnp.zeros((tq, LANES), F32))


def _indexer(qi_r, klo, khi, wi, tq, tk, k_sel):
    b, s, _ = qi_r.shape
    return pl.pallas_call(
        functools.partial(_indexer_kernel, tq=tq, tk=tk, seq=s, k_sel=float(k_sel)),
        name="indexer",
        grid=(b, s // tq),
        in_specs=[
            pl.BlockSpec((1, tq, IDX_HEADS * IDX_DIM), lambda bi, qi: (bi, qi, 0)),
            pl.BlockSpec((1, s, LANES), lambda bi, qi: (bi, 0, 0)),
            pl.BlockSpec((1, s, LANES), lambda bi, qi: (bi, 0, 0)),
            pl.BlockSpec((1, tq, LANES), lambda bi, qi: (bi, qi, 0)),
        ],
        out_specs=[
            pl.BlockSpec((1, tq, s), lambda bi, qi: (bi, qi, 0)),
            pl.BlockSpec((1, tq, LANES), lambda bi, qi: (bi, qi, 0)),
        ],
        out_shape=[jax.ShapeDtypeStruct((b, s, s), F32), jax.ShapeDtypeStruct((b, s, LANES), F32)],
        compiler_params=_cparams(("parallel", "parallel")),
    )(qi_r, klo, khi, wi)


def _causal_pairs(s, tq, tk):
    qs, ks = [], []
    for qi in range(s // tq):
        for ki in range((qi * tq + tq - 1) // tk + 1):
            qs.append(qi)
            ks.append(ki)
    return jnp.asarray(np.array(qs, np.int32)), jnp.asarray(np.array(ks, np.int32))


def _softmax_step(s, v, rows, m_ref, l_ref, acc_ref, fixed_shift, sums_in_matmul=False):
    blocks = [s[:, j * LANES:(j + 1) * LANES] for j in range(s.shape[1] // LANES)]
    dv = v.shape[1]
    if fixed_shift and sums_in_matmul:
        p = jnp.concatenate([jnp.exp2(blk).astype(BF16) for blk in blocks], axis=1)
        acc_ref[rows] = acc_ref[rows] + _dot(p, jnp.concatenate([v, jnp.ones_like(v)], axis=1))
        return
    if fixed_shift:
        ps = [jnp.exp2(blk) for blk in blocks]
        l_ref[rows] = l_ref[rows] + functools.reduce(jnp.add, ps)
        p = jnp.concatenate([blk.astype(BF16) for blk in ps], axis=1)
        acc_ref[rows, :dv] = acc_ref[rows, :dv] + _dot(p, v)
        return
    mx = functools.reduce(jnp.maximum, blocks)
    m_prev = m_ref[rows]
    m_new = jnp.maximum(m_prev, jnp.max(mx, axis=-1, keepdims=True))
    alpha = jnp.exp2(m_prev - m_new)
    ps = [jnp.exp2(blk - m_new) for blk in blocks]
    l_ref[rows] = alpha * l_ref[rows] + jnp.sum(functools.reduce(jnp.add, ps), axis=-1, keepdims=True)
    p = jnp.concatenate([blk.astype(BF16) for blk in ps], axis=1)
    acc_ref[rows, :dv] = alpha * acc_ref[rows, :dv] + _dot(p, v)
    m_ref[rows] = m_new


def _softmax_finish(rows, l_ref, acc_ref, dv, fixed_shift, sums_in_matmul=False):
    if fixed_shift and sums_in_matmul:
        acc = acc_ref[rows]
        return acc[:, :dv] / acc[:, dv:]
    l = l_ref[rows]
    if fixed_shift:
        l = jnp.sum(l, axis=-1, keepdims=True)
    return acc_ref[rows, :dv] / l


def _dsa_kernel(qt_ref, kt_ref, q_ref, k_ref, v_ref, sc_ref, th_ref, shift_ref, o_ref, m_ref, l_ref,
                acc_ref, bias_ref, qs_ref, *, tq, tk, fixed_shift):
    step = pl.program_id(1)
    qi = qt_ref[step]
    ki = kt_ref[step]

    @pl.when(ki == 0)
    def _():
        m_ref[...] = jnp.full(m_ref.shape, M_INIT, F32)
        l_ref[...] = jnp.zeros(l_ref.shape, F32)
        acc_ref[...] = jnp.zeros(acc_ref.shape, F32)
        for h in range(A_HEADS):
            qs_ref[h * tq:(h + 1) * tq, :] = q_ref[0, :, h * LANES:(h + 1) * LANES]

    bias_ref[...] = jnp.where(sc_ref[0] >= th_ref[0][:, :1], -shift_ref[:, :1], NEG)
    grp = A_HEADS // A_KV_HEADS
    for g in range(A_KV_HEADS):
        rows = slice(g * grp * tq, (g + 1) * grp * tq)
        k = k_ref[0, :, g * LANES:(g + 1) * LANES]
        v = v_ref[0, :, g * LANES:(g + 1) * LANES]
        s = _dot_nt(qs_ref[rows, :], k)
        s = jnp.concatenate([s[j * tq:(j + 1) * tq] + bias_ref[...] for j in range(grp)], axis=0)
        _softmax_step(s, v, rows, m_ref, l_ref, acc_ref, fixed_shift, sums_in_matmul=True)

    @pl.when(ki == (qi * tq + tq - 1) // tk)
    def _():
        for h in range(A_HEADS):
            out = _softmax_finish(slice(h * tq, (h + 1) * tq), l_ref, acc_ref, A_HEAD_DIM, fixed_shift,
                                  sums_in_matmul=True)
            o_ref[0, :, h * LANES:(h + 1) * LANES] = out.astype(o_ref.dtype)


def _dsa_attn(q, k, v, scores, th, shift, tq, tk, fixed_shift):
    b, s, dq = q.shape
    qt, kt = _causal_pairs(s, tq, tk)
    grid_spec = pltpu.PrefetchScalarGridSpec(
        num_scalar_prefetch=2,
        grid=(b, qt.shape[0]),
        in_specs=[
            pl.BlockSpec((1, tq, dq), lambda bi, p, qt, kt: (bi, qt[p], 0)),
            pl.BlockSpec((1, tk, k.shape[2]), lambda bi, p, qt, kt: (bi, kt[p], 0)),
            pl.BlockSpec((1, tk, v.shape[2]), lambda bi, p, qt, kt: (bi, kt[p], 0)),
            pl.BlockSpec((1, tq, tk), lambda bi, p, qt, kt: (bi, qt[p], kt[p])),
            pl.BlockSpec((1, tq, LANES), lambda bi, p, qt, kt: (bi, qt[p], 0)),
            pl.BlockSpec((1, LANES), lambda bi, p, qt, kt: (0, 0)),
        ],
        out_specs=pl.BlockSpec((1, tq, dq), lambda bi, p, qt, kt: (bi, qt[p], 0)),
        scratch_shapes=[
            pltpu.VMEM((A_HEADS * tq, LANES), F32),
            pltpu.VMEM((A_HEADS * tq, LANES), F32),
            pltpu.VMEM((A_HEADS * tq, 2 * A_HEAD_DIM), F32),
            pltpu.VMEM((tq, tk), F32),
            pltpu.VMEM((A_HEADS * tq, A_HEAD_DIM), BF16),
        ],
    )
    return pl.pallas_call(
        functools.partial(_dsa_kernel, tq=tq, tk=tk, fixed_shift=fixed_shift),
        name="dsa_attn_fixed" if fixed_shift else "dsa_attn_online",
        grid_spec=grid_spec,
        out_shape=jax.ShapeDtypeStruct((b, s, dq), BF16),
        compiler_params=_cparams(("parallel", "arbitrary")),
    )(qt, kt, q, k, v, scores, th, shift)


def _mla_kernel(qt_ref, kt_ref, q_ref, k_ref, v_ref, o_ref, m_ref, l_ref, acc_ref, *, tq, tk, hg,
                fixed_shift):
    step = pl.program_id(2)
    qi = qt_ref[step]
    ki = kt_ref[step]

    @pl.when(ki == 0)
    def _():
        m_ref[...] = jnp.full(m_ref.shape, M_INIT, F32)
        l_ref[...] = jnp.zeros(l_ref.shape, F32)
        acc_ref[...] = jnp.zeros(acc_ref.shape, F32)

    def heads(masked):
        if masked:
            keep = (lax.broadcasted_iota(jnp.int32, (tq, tk), 1)
                    <= lax.broadcasted_iota(jnp.int32, (tq, tk), 0))
        for h in range(hg):
            q = q_ref[0, :, h * 256:(h + 1) * 256]
            k = k_ref[0, :, h * 256:(h + 1) * 256]
            v = v_ref[0, :, h * B_V:(h + 1) * B_V]
            s = _dot_nt(q, k)
            if masked:
                s = jnp.where(keep, s, NEG)
            _softmax_step(s, v, slice(h * tq, (h + 1) * tq), m_ref, l_ref, acc_ref, fixed_shift)

    @pl.when(ki < qi)
    def _():
        heads(False)

    @pl.when(ki == qi)
    def _():
        heads(True)
        for h in range(hg):
            out = _softmax_finish(slice(h * tq, (h + 1) * tq), l_ref, acc_ref, B_V, fixed_shift)
            o_ref[0, :, h * B_V:(h + 1) * B_V] = out.astype(o_ref.dtype)


def _mla_attn(q, k, v, tq, tk, hg, fixed_shift):
    b, s, _ = q.shape
    assert tq == tk
    qt, kt = _causal_pairs(s, tq, tk)
    grid_spec = pltpu.PrefetchScalarGridSpec(
        num_scalar_prefetch=2,
        grid=(b, B_HEADS // hg, qt.shape[0]),
        in_specs=[
            pl.BlockSpec((1, tq, hg * 256), lambda bi, g, p, qt, kt: (bi, qt[p], g)),
            pl.BlockSpec((1, tk, hg * 256), lambda bi, g, p, qt, kt: (bi, kt[p], g)),
            pl.BlockSpec((1, tk, hg * B_V), lambda bi, g, p, qt, kt: (bi, kt[p], g)),
        ],
        out_specs=pl.BlockSpec((1, tq, hg * B_V), lambda bi, g, p, qt, kt: (bi, qt[p], g)),
        scratch_shapes=[
            pltpu.VMEM((hg * tq, LANES), F32),
            pltpu.VMEM((hg * tq, LANES), F32),
            pltpu.VMEM((hg * tq, B_V), F32),
        ],
    )
    return pl.pallas_call(
        functools.partial(_mla_kernel, tq=tq, tk=tk, hg=hg, fixed_shift=fixed_shift),
        name="mla_attn_fixed" if fixed_shift else "mla_attn_online",
        grid_spec=grid_spec,
        out_shape=jax.ShapeDtypeStruct((b, s, B_HEADS * B_V), BF16),
        compiler_params=_cparams(("parallel", "parallel", "arbitrary")),
    )(qt, kt, q, k, v)


def _regroup_w_in(w):
    sizes = (A_HEADS * A_HEAD_DIM, A_KV_HEADS * A_HEAD_DIM, A_KV_HEADS * A_HEAD_DIM,
             IDX_HEADS * IDX_DIM, IDX_DIM, IDX_HEADS, B_Q_LORA, B_KV_LORA, B_ROPE)
    offs = np.cumsum((0,) + sizes)
    w = w.astype(BF16)
    qa, ka, va, qi, ki, wi, cq, ckv, kr = (w[:, offs[j]:offs[j + 1]] for j in range(len(sizes)))
    gates = w[:, offs[-1]:]
    d = w.shape[0]
    kiw = jnp.concatenate([ki, wi, jnp.zeros((d, LANES - IDX_DIM - IDX_HEADS), w.dtype)], axis=1)
    krp = jnp.concatenate([kr, jnp.zeros((d, LANES - B_ROPE), w.dtype)], axis=1)
    return jnp.concatenate([qa, gates, qi, ka, va, cq, ckv, kiw, krp], axis=1).astype(BF16)


def _softmax_shift(d, g_q, g_k):
    bound = (d ** 0.5) * LOG2E * 1.02 * jnp.max(jnp.abs(g_q)) * jnp.max(jnp.abs(g_k))
    c = jnp.ceil(bound.astype(F32))
    ok = c <= MAX_FIXED_SHIFT
    return jnp.full((1, LANES), jnp.where(ok, c, 0.0), F32), ok


def _rope_tables(pos, d):
    half = d // 2
    freqs = jnp.power(jnp.float32(ROPE_THETA), -jnp.arange(half, dtype=F32) * (2.0 / d))
    ang = pos.astype(F32)[..., None] * freqs
    cos, sin = jnp.cos(ang), jnp.sin(ang)
    reps = LANES // d
    c = jnp.tile(jnp.concatenate([cos, cos], axis=-1), (1, 1, reps))
    s = jnp.tile(jnp.concatenate([-sin, sin], axis=-1), (1, 1, reps))
    return c.reshape(-1, LANES), s.reshape(-1, LANES)


def _layer(x, p, positions, g_mix_norm, w_in, g_qa, g_ka, g_cq, w_uq, g_ckv, w_ukv, g_qb, g_kb,
           w_out_a, w_out_b, w_o, g_ffn_norm, w_ffn_gate, w_ffn_up, w_ffn_down, g_ple_norm,
           w_ple_gate, w_ple_proj):
    b, s, d = x.shape
    m = b * s
    k_sel = min(TOPK_MAX, s // 4)
    xf = x.reshape(m, d)
    row = lambda g: g.reshape(1, -1).astype(F32)

    c128, s128 = _rope_tables(positions, A_HEAD_DIM)
    c64, s64 = _rope_tables(positions, IDX_DIM)

    proj = _norm_matmul("in_proj", xf, 0, d, row(g_mix_norm), _regroup_w_in(w_in), F32, tm=1024, tn=1536)

    qa_r, ka_r, va_b, qi_r, klo, khi, wi = _prep_a(
        proj, c128, s128, c64, s64, row(g_qa) * (A_HEAD_DIM ** -0.5 * LOG2E), row(g_ka), tp=512)
    r3 = lambda a: a.reshape(b, s, a.shape[-1])
    scores, th = _indexer(r3(qi_r), r3(klo), r3(khi), r3(wi), tq=512, tk=512, k_sel=k_sel)
    shift_a, fixed_a = _softmax_shift(A_HEAD_DIM, g_qa, g_ka)
    y_a = lax.cond(
        fixed_a,
        functools.partial(_dsa_attn, tq=512, tk=512, fixed_shift=True),
        functools.partial(_dsa_attn, tq=512, tk=512, fixed_shift=False),
        r3(qa_r), r3(ka_r), r3(va_b), scores, th, shift_a)

    w_uq_r = w_uq.reshape(B_Q_LORA, B_HEADS, B_QK)
    w_uq_r = jnp.concatenate([w_uq_r[:, :, :B_NOPE].reshape(B_Q_LORA, -1),
                              w_uq_r[:, :, B_NOPE:].reshape(B_Q_LORA, -1)], axis=1).astype(BF16)
    w_ukv_r = w_ukv.reshape(B_KV_LORA, B_HEADS, 2, B_NOPE)
    w_ukv_r = jnp.concatenate([w_ukv_r[:, :, 0, :].reshape(B_KV_LORA, -1),
                               w_ukv_r[:, :, 1, :].reshape(B_KV_LORA, -1)], axis=1).astype(BF16)
    pe2 = lambda g: jnp.tile(g[B_NOPE:], 2).reshape(1, LANES).astype(F32)
    shift_b, fixed_b = _softmax_shift(B_QK, g_qb, g_kb)
    q_m = _mla_q(proj, row(g_cq), w_uq_r, c64, s64, row(g_qb[:B_NOPE]), pe2(g_qb), tm=512)
    k_m, v_m = _mla_kv(proj, row(g_ckv), w_ukv_r, c64, s64, row(g_kb[:B_NOPE]), pe2(g_kb), shift_b, tm=512)
    y_b = lax.cond(
        fixed_b,
        functools.partial(_mla_attn, tq=512, tk=512, hg=16, fixed_shift=True),
        functools.partial(_mla_attn, tq=512, tk=512, hg=16, fixed_shift=False),
        r3(q_m), r3(k_m), r3(v_m))

    mixed = _merge(y_a.reshape(m, -1), y_b.reshape(m, -1), w_out_a.astype(BF16), w_out_b.astype(BF16),
                   proj, tm=1024, tn=512)
    x1 = _matmul_res("w_o", mixed, w_o.astype(BF16), xf, tm=1024, tn=1024)

    act = _ffn_up(x1, row(g_ffn_norm), w_ffn_gate.astype(BF16), w_ffn_up.astype(BF16), tm=1024, tn=512)
    x2 = _matmul_res("ffn_down", act, w_ffn_down.astype(BF16), x1, tm=512, tn=1024)

    out = _ple(x2, row(g_ple_norm), w_ple_gate.astype(BF16), p.reshape(m, -1),
               w_ple_proj.astype(BF16), tm=512)
    return out.reshape(b, s, d)


def kernel(x, p, positions, g_mix_norm, w_in, g_qa, g_ka, g_cq, w_uq, g_ckv, w_ukv, g_qb, g_kb,
           w_out_a, w_out_b, w_o, g_ffn_norm, w_ffn_gate, w_ffn_up, w_ffn_down, g_ple_norm,
           w_ple_gate, w_ple_proj):
    for i in range(p.shape[0]):
        x = _layer(x, p[i], positions, g_mix_norm[i], w_in[i], g_qa[i], g_ka[i], g_cq[i], w_uq[i],
                   g_ckv[i], w_ukv[i], g_qb[i], g_kb[i], w_out_a[i], w_out_b[i], w_o[i],
                   g_ffn_norm[i], w_ffn_gate[i], w_ffn_up[i], w_ffn_down[i], g_ple_norm[i],
                   w_ple_gate[i], w_ple_proj[i])
    return x
```

```python
import functools

import numpy as np
import jax
import jax.numpy as jnp
from jax import lax
from jax.experimental import pallas as pl
from jax.experimental.pallas import tpu as pltpu

F32 = jnp.float32
BF16 = jnp.bfloat16

ROPE_THETA = 10000.0
NORM_EPS = 1e-6
NEG = -1e30
M_INIT = -1e29
POS_BIG = 3e38
TH_ALL = 0.5 * NEG
LOG2E = 1.4426950408889634
MAX_SEARCH_STEPS = 512
CNT_ROWS = 128
MAX_FIXED_SHIFT = 60.0
A_HEADS = 16
A_KV_HEADS = 4
A_HEAD_DIM = 128
IDX_HEADS = 16
IDX_DIM = 64
TOPK_MAX = 256
B_HEADS = 16
B_Q_LORA = 512
B_KV_LORA = 256
B_NOPE = 128
B_ROPE = 64
B_V = 128
B_QK = B_NOPE + B_ROPE
LANES = 128
VMEM_LIMIT = 56 * 1024 * 1024

TILE_IN_PROJ = (1024, 1536)
TILE_PREP_ROWS = 512
TILE_ATTN = 512
TILE_MLA_UP_ROWS = 512
TILE_MERGE = (1024, 512)
TILE_W_O = (1024, 1024)
TILE_FFN_UP = (1024, 512)
TILE_FFN_DOWN = (512, 1024)
TILE_PLE_ROWS = 512

COL_QA = 0
COL_GATE_A = 2048
COL_GATE_B = 4096
COL_QI = 6144
COL_KA = 7168
COL_VA = 7680
COL_CQ = 8192
COL_CKV = 8704
COL_KIW = 8960
COL_KR = 9088
N_PROJ = 9216


def _dot(a, b):
    return jnp.dot(a, b, preferred_element_type=F32)


def _dot_nt(a, b):
    return lax.dot_general(a, b, (((1,), (1,)), ((), ())), preferred_element_type=F32)


def _cparams(sem):
    return pltpu.CompilerParams(dimension_semantics=sem, vmem_limit_bytes=VMEM_LIMIT)


def _rms_to_bf16(x, g):
    ms = jnp.mean(x * x, axis=-1, keepdims=True)
    return (x * lax.rsqrt(ms + NORM_EPS) * g).astype(BF16)


def _norm_matmul_kernel(x_ref, g_ref, w_ref, o_ref, hn_ref):
    @pl.when(pl.program_id(1) == 0)
    def _():
        hn_ref[...] = _rms_to_bf16(x_ref[...], g_ref[...])

    o_ref[...] = _dot(hn_ref[...], w_ref[...]).astype(o_ref.dtype)


def _norm_matmul(name, x, x_col_block, k, g, w, out_dtype, tm, tn):
    m = x.shape[0]
    n = w.shape[1]
    return pl.pallas_call(
        _norm_matmul_kernel,
        name=name,
        grid=(m // tm, n // tn),
        in_specs=[
            pl.BlockSpec((tm, k), lambda i, j: (i, x_col_block)),
            pl.BlockSpec((1, k), lambda i, j: (0, 0)),
            pl.BlockSpec((k, tn), lambda i, j: (0, j)),
        ],
        out_specs=pl.BlockSpec((tm, tn), lambda i, j: (i, j)),
        out_shape=jax.ShapeDtypeStruct((m, n), out_dtype),
        scratch_shapes=[pltpu.VMEM((tm, k), BF16)],
        compiler_params=_cparams(("parallel", "arbitrary")),
    )(x, g, w)


def _ffn_up_kernel(x_ref, g_ref, wg_ref, wu_ref, o_ref, hn_ref):
    @pl.when(pl.program_id(1) == 0)
    def _():
        hn_ref[...] = _rms_to_bf16(x_ref[...], g_ref[...])

    h = hn_ref[...]
    a = _dot(h, wg_ref[...])
    b = _dot(h, wu_ref[...])
    o_ref[...] = (a * jax.nn.sigmoid(a) * b).astype(o_ref.dtype)


def _ffn_up(x, g, wg, wu, tm, tn):
    m, k = x.shape
    n = wg.shape[1]
    return pl.pallas_call(
        _ffn_up_kernel,
        name="ffn_up",
        grid=(m // tm, n // tn),
        in_specs=[
            pl.BlockSpec((tm, k), lambda i, j: (i, 0)),
            pl.BlockSpec((1, k), lambda i, j: (0, 0)),
            pl.BlockSpec((k, tn), lambda i, j: (0, j)),
            pl.BlockSpec((k, tn), lambda i, j: (0, j)),
        ],
        out_specs=pl.BlockSpec((tm, tn), lambda i, j: (i, j)),
        out_shape=jax.ShapeDtypeStruct((m, n), BF16),
        scratch_shapes=[pltpu.VMEM((tm, k), BF16)],
        compiler_params=_cparams(("parallel", "arbitrary")),
    )(x, g, wg, wu)


def _matmul_res_kernel(a_ref, w_ref, r_ref, o_ref):
    o_ref[...] = r_ref[...] + _dot(a_ref[...], w_ref[...])


def _matmul_res(name, a, w, res, tm, tn):
    m, k = a.shape
    n = w.shape[1]
    return pl.pallas_call(
        _matmul_res_kernel,
        name=name,
        grid=(n // tn, m // tm),
        in_specs=[
            pl.BlockSpec((tm, k), lambda j, i: (i, 0)),
            pl.BlockSpec((k, tn), lambda j, i: (0, j)),
            pl.BlockSpec((tm, tn), lambda j, i: (i, j)),
        ],
        out_specs=pl.BlockSpec((tm, tn), lambda j, i: (i, j)),
        out_shape=jax.ShapeDtypeStruct((m, n), F32),
        compiler_params=_cparams(("parallel", "parallel")),
    )(a, w, res)


def _merge_kernel(ya_ref, yb_ref, wa_ref, wb_ref, ga_ref, gb_ref, o_ref):
    a = _dot(ya_ref[...], wa_ref[...])
    b = _dot(yb_ref[...], wb_ref[...])
    o_ref[...] = (jax.nn.sigmoid(ga_ref[...]) * a + jax.nn.sigmoid(gb_ref[...]) * b).astype(o_ref.dtype)


def _merge(ya, yb, wa, wb, proj, tm, tn):
    m, k = ya.shape
    n = wa.shape[1]
    ga0 = COL_GATE_A // tn
    gb0 = COL_GATE_B // tn
    return pl.pallas_call(
        _merge_kernel,
        name="merge",
        grid=(m // tm, n // tn),
        in_specs=[
            pl.BlockSpec((tm, k), lambda i, j: (i, 0)),
            pl.BlockSpec((tm, k), lambda i, j: (i, 0)),
            pl.BlockSpec((k, tn), lambda i, j: (0, j)),
            pl.BlockSpec((k, tn), lambda i, j: (0, j)),
            pl.BlockSpec((tm, tn), lambda i, j: (i, ga0 + j)),
            pl.BlockSpec((tm, tn), lambda i, j: (i, gb0 + j)),
        ],
        out_specs=pl.BlockSpec((tm, tn), lambda i, j: (i, j)),
        out_shape=jax.ShapeDtypeStruct((m, n), BF16),
        compiler_params=_cparams(("parallel", "parallel")),
    )(ya, yb, wa, wb, proj, proj)


def _ple_kernel(x_ref, g_ref, wg_ref, p_ref, wp_ref, o_ref):
    x = x_ref[...]
    gate = jax.nn.sigmoid(_dot(_rms_to_bf16(x, g_ref[...]), wg_ref[...]))
    emb = _dot(p_ref[...].astype(BF16), wp_ref[...])
    o_ref[...] = x + gate * emb


def _ple(x, g, wg, p, wp, tm):
    m, k = x.shape
    n = wg.shape[1]
    kp = p.shape[1]
    assert n == k
    return pl.pallas_call(
        _ple_kernel,
        name="ple",
        grid=(m // tm,),
        in_specs=[
            pl.BlockSpec((tm, k), lambda i: (i, 0)),
            pl.BlockSpec((1, k), lambda i: (0, 0)),
            pl.BlockSpec((k, n), lambda i: (0, 0)),
            pl.BlockSpec((tm, kp), lambda i: (i, 0)),
            pl.BlockSpec((kp, n), lambda i: (0, 0)),
        ],
        out_specs=pl.BlockSpec((tm, n), lambda i: (i, 0)),
        out_shape=jax.ShapeDtypeStruct((m, n), F32),
        compiler_params=_cparams(("parallel",)),
    )(x, g, wg, p, wp)


def _rope64_pairs(x, c64, s64, first_half):
    partner = jnp.where(first_half, pltpu.roll(x, 96, 1), pltpu.roll(x, 32, 1))
    return x * c64 + partner * s64


def _prep_a_kernel(qa_ref, ka_ref, va_ref, qi_ref, kiw_ref, c128_ref, s128_ref, c64_ref, s64_ref,
                   gq_ref, gk_ref, qa_o, ka_o, va_o, qi_o, klo_o, khi_o, wi_o, *, wi_scale):
    c128 = c128_ref[...]
    s128 = s128_ref[...]

    def norm_rope(x, g):
        ms = jnp.mean(x * x, axis=-1, keepdims=True)
        y = x * lax.rsqrt(ms + NORM_EPS) * g
        return y * c128 + pltpu.roll(y, 64, 1) * s128

    for h in range(A_HEADS):
        hs = slice(h * LANES, (h + 1) * LANES)
        qa_o[:, hs] = norm_rope(qa_ref[:, hs], gq_ref[...]).astype(BF16)
    for h in range(A_KV_HEADS):
        hs = slice(h * LANES, (h + 1) * LANES)
        ka_o[:, hs] = norm_rope(ka_ref[:, hs], gk_ref[...]).astype(BF16)
    va_o[...] = va_ref[...].astype(BF16)

    c64 = c64_ref[...]
    s64 = s64_ref[...]
    lane = lax.broadcasted_iota(jnp.int32, c64.shape, 1)
    first_half = (lane % 64) < 32
    for p in range(IDX_HEADS * IDX_DIM // LANES):
        ps = slice(p * LANES, (p + 1) * LANES)
        qi_o[:, ps] = _rope64_pairs(qi_ref[:, ps], c64, s64, first_half).astype(BF16)
    kiw = kiw_ref[...]
    ki = jnp.where(lane < 64, _rope64_pairs(kiw, c64, s64, first_half), 0.0)
    klo_o[...] = ki.astype(BF16)
    khi_o[...] = pltpu.roll(ki, 64, 1).astype(BF16)
    wi_o[...] = kiw * wi_scale


def _prep_a(proj, c128, s128, c64, s64, gq, gk, tp):
    m = proj.shape[0]
    row = lambda w, cb: pl.BlockSpec((tp, w), lambda i: (i, cb))
    vec = pl.BlockSpec((1, LANES), lambda i: (0, 0))
    out = lambda w: pl.BlockSpec((tp, w), lambda i: (i, 0))
    wi_scale = (IDX_HEADS ** -0.5) * (IDX_DIM ** -0.5)
    return pl.pallas_call(
        functools.partial(_prep_a_kernel, wi_scale=wi_scale),
        name="prep_a",
        grid=(m // tp,),
        in_specs=[
            row(2048, COL_QA // 2048), row(512, COL_KA // 512), row(512, COL_VA // 512),
            row(1024, COL_QI // 1024), row(LANES, COL_KIW // LANES),
            out(LANES), out(LANES), out(LANES), out(LANES), vec, vec,
        ],
        out_specs=[out(2048), out(512), out(512), out(1024), out(LANES), out(LANES), out(LANES)],
        out_shape=[
            jax.ShapeDtypeStruct((m, 2048), BF16), jax.ShapeDtypeStruct((m, 512), BF16),
            jax.ShapeDtypeStruct((m, 512), BF16), jax.ShapeDtypeStruct((m, 1024), BF16),
            jax.ShapeDtypeStruct((m, LANES), BF16), jax.ShapeDtypeStruct((m, LANES), BF16),
            jax.ShapeDtypeStruct((m, LANES), F32),
        ],
        compiler_params=_cparams(("parallel",)),
    )(proj, proj, proj, proj, proj, c128, s128, c64, s64, gq, gk)


def _mla_q_kernel(cq_ref, g_ref, w_ref, c64_ref, s64_ref, gqn_ref, gqp_ref, q_o, *, q_scale):
    qb = _dot(_rms_to_bf16(cq_ref[...], g_ref[...]), w_ref[...])
    c64 = c64_ref[...]
    s64 = s64_ref[...]
    lane = lax.broadcasted_iota(jnp.int32, c64.shape, 1)
    first_half = (lane % 64) < 32
    lo = lane < 64
    nope_w = B_HEADS * B_NOPE
    for pair in range(B_HEADS // 2):
        qp2 = qb[:, nope_w + pair * LANES:nope_w + (pair + 1) * LANES]
        qp2_sq = qp2 * qp2
        rot2 = _rope64_pairs(qp2 * gqp_ref[...], c64, s64, first_half)
        for h in (2 * pair, 2 * pair + 1):
            mine = lo if h % 2 == 0 else jnp.logical_not(lo)
            shift_lane = lane == (64 if h % 2 == 0 else 0)
            qn = qb[:, h * LANES:(h + 1) * LANES]
            ss = jnp.sum(qn * qn + jnp.where(mine, qp2_sq, 0.0), axis=-1, keepdims=True)
            r = lax.rsqrt(ss * (1.0 / B_QK) + NORM_EPS) * q_scale
            q_o[:, 2 * h * LANES:(2 * h + 1) * LANES] = (qn * r * gqn_ref[...]).astype(BF16)
            qpe = jnp.where(mine, rot2 * r, jnp.where(shift_lane, 1.0, 0.0))
            q_o[:, (2 * h + 1) * LANES:(2 * h + 2) * LANES] = qpe.astype(BF16)


def _mla_kv_kernel(ckv_ref, g_ref, w_ref, kr_ref, c64_ref, s64_ref, gkn_ref, gkp_ref, shift_ref, k_o, v_o):
    kv = _dot(_rms_to_bf16(ckv_ref[...], g_ref[...]), w_ref[...])
    c64 = c64_ref[...]
    s64 = s64_ref[...]
    lane = lax.broadcasted_iota(jnp.int32, c64.shape, 1)
    first_half = (lane % 64) < 32
    lo = lane < 64
    kr = jnp.where(lo, kr_ref[...], 0.0)
    kr_ss = jnp.sum(kr * kr, axis=-1, keepdims=True)
    kpe_lo = jnp.where(lo, _rope64_pairs(kr * gkp_ref[...], c64, s64, first_half), 0.0)
    kpe_hi = pltpu.roll(kpe_lo, 64, 1)
    nope_w = B_HEADS * B_NOPE
    for h in range(B_HEADS):
        hs = slice(h * LANES, (h + 1) * LANES)
        shift_lane = lane == (64 if h % 2 == 0 else 0)
        kn = kv[:, hs]
        rk = lax.rsqrt((jnp.sum(kn * kn, axis=-1, keepdims=True) + kr_ss) * (1.0 / B_QK) + NORM_EPS)
        k_o[:, 2 * h * LANES:(2 * h + 1) * LANES] = (kn * rk * gkn_ref[...]).astype(BF16)
        kpe = kpe_lo if h % 2 == 0 else kpe_hi
        k_o[:, (2 * h + 1) * LANES:(2 * h + 2) * LANES] = jnp.where(
            shift_lane, -shift_ref[...], kpe * rk).astype(BF16)
        v_o[:, hs] = kv[:, nope_w + h * LANES:nope_w + (h + 1) * LANES].astype(BF16)


def _mla_q(proj, g_cq, w_uq, c64, s64, gqn, gqp, tm):
    m = proj.shape[0]
    vec = lambda w: pl.BlockSpec((1, w), lambda i: (0, 0))
    rows = lambda w: pl.BlockSpec((tm, w), lambda i: (i, 0))
    return pl.pallas_call(
        functools.partial(_mla_q_kernel, q_scale=B_QK ** -0.5 * LOG2E),
        name="mla_q",
        grid=(m // tm,),
        in_specs=[
            pl.BlockSpec((tm, B_Q_LORA), lambda i: (i, COL_CQ // B_Q_LORA)),
            vec(B_Q_LORA),
            pl.BlockSpec(w_uq.shape, lambda i: (0, 0)),
            rows(LANES), rows(LANES), vec(LANES), vec(LANES),
        ],
        out_specs=rows(B_HEADS * 256),
        out_shape=jax.ShapeDtypeStruct((m, B_HEADS * 256), BF16),
        compiler_params=_cparams(("parallel",)),
    )(proj, g_cq, w_uq, c64, s64, gqn, gqp)


def _mla_kv(proj, g_ckv, w_ukv, c64, s64, gkn, gkp, shift, tm):
    m = proj.shape[0]
    vec = lambda w: pl.BlockSpec((1, w), lambda i: (0, 0))
    rows = lambda w: pl.BlockSpec((tm, w), lambda i: (i, 0))
    return pl.pallas_call(
        _mla_kv_kernel,
        name="mla_kv",
        grid=(m // tm,),
        in_specs=[
            pl.BlockSpec((tm, B_KV_LORA), lambda i: (i, COL_CKV // B_KV_LORA)),
            vec(B_KV_LORA),
            pl.BlockSpec(w_ukv.shape, lambda i: (0, 0)),
            pl.BlockSpec((tm, LANES), lambda i: (i, COL_KR // LANES)),
            rows(LANES), rows(LANES), vec(LANES), vec(LANES), vec(LANES),
        ],
        out_specs=[rows(B_HEADS * 256), rows(B_HEADS * B_V)],
        out_shape=[
            jax.ShapeDtypeStruct((m, B_HEADS * 256), BF16),
            jax.ShapeDtypeStruct((m, B_HEADS * B_V), BF16),
        ],
        compiler_params=_cparams(("parallel",)),
    )(proj, g_ckv, w_ukv, proj, c64, s64, gkn, gkp, shift)


def _indexer_kernel(q_ref, klo_ref, khi_ref, w_ref, sc_ref, th_ref, *, tq, tk, seq, k_sel):
    qi = pl.program_id(1)
    n_vis = (qi * tq) // tk + 1
    n_all = seq // tk
    row = qi * tq + lax.broadcasted_iota(jnp.int32, (tq, 1), 0)
    wcols = [w_ref[0, :, IDX_DIM + h:IDX_DIM + h + 1] for h in range(IDX_HEADS)]

    def gen(c, carry):
        rmax, rmin = carry
        off = pl.multiple_of(c * tk, tk)
        kl = klo_ref[0, pl.ds(off, tk), :]
        kh = khi_ref[0, pl.ds(off, tk), :]
        acc = jnp.zeros((tq, tk), F32)
        for p in range(IDX_HEADS // 2):
            qp = q_ref[0, :, p * LANES:(p + 1) * LANES]
            acc = acc + wcols[2 * p] * jnp.maximum(_dot_nt(qp, kl), 0.0)
            acc = acc + wcols[2 * p + 1] * jnp.maximum(_dot_nt(qp, kh), 0.0)
        col = off + lax.broadcasted_iota(jnp.int32, (tq, tk), 1)
        vis = col <= row
        sc_ref[0, :, pl.ds(off, tk)] = jnp.where(vis, acc, NEG)
        rmax = jnp.maximum(rmax, jnp.max(jnp.where(vis, acc, NEG), axis=-1, keepdims=True))
        rmin = jnp.minimum(rmin, jnp.min(jnp.where(vis, acc, POS_BIG), axis=-1, keepdims=True))
        return rmax, rmin

    rmax, rmin = lax.fori_loop(0, n_vis, gen,
                               (jnp.full((tq, 1), NEG, F32), jnp.full((tq, 1), POS_BIG, F32)))

    def fill(c, carry):
        sc_ref[0, :, pl.ds(pl.multiple_of(c * tk, tk), tk)] = jnp.full((tq, tk), NEG, F32)
        return carry

    lax.fori_loop(n_vis, n_all, fill, 0)

    cw = min(2 * tk, seq)
    n_cnt = (n_vis * tk + cw - 1) // cw

    def count_ge(mid, strict=False, open_groups=None):
        zeros = jnp.zeros((CNT_ROWS, LANES), F32)
        accs = []
        for r in range(tq // CNT_ROWS):
            rs = slice(r * CNT_ROWS, (r + 1) * CNT_ROWS)
            midb = mid[rs]

            def body(c, acc, rs=rs, midb=midb):
                off = pl.multiple_of(c * cw, cw)
                for j in range(cw // LANES):
                    blk = sc_ref[0, rs, pl.ds(off + j * LANES, LANES)]
                    acc = acc + jnp.where(blk > midb if strict else blk >= midb, 1.0, 0.0)
                return acc

            if open_groups is None:
                accs.append(lax.fori_loop(0, n_cnt, body, zeros))
            else:
                accs.append(lax.cond(open_groups[r] > 0.0,
                                     functools.partial(lax.fori_loop, 0, n_cnt, body, zeros),
                                     lambda: zeros))
        return jnp.concatenate(
            [jnp.broadcast_to(jnp.sum(a, axis=-1, keepdims=True), (CNT_ROWS, LANES)) for a in accs], axis=0)

    rowb = qi * tq + lax.broadcasted_iota(jnp.int32, (tq, LANES), 0)
    active0 = (rowb + 1) > k_sel
    lo0 = jnp.where(active0, jnp.broadcast_to(rmin, (tq, LANES)), TH_ALL)
    hi0 = jnp.where(active0, jnp.broadcast_to(rmax, (tq, LANES)), TH_ALL)

    def n_open(lo, hi):
        still = jnp.where(lo < hi, 1.0, 0.0)
        return tuple(jnp.sum(still[r * CNT_ROWS:(r + 1) * CNT_ROWS]) for r in range(tq // CNT_ROWS))

    def cond(st):
        it, nopen = st[0], st[1]
        return jnp.logical_and(functools.reduce(jnp.add, nopen) > 0.0, it < MAX_SEARCH_STEPS)

    def body(st):
        it, was_open, lo, hi, cnt_lo = st
        mid = lo + (hi - lo) * 0.5
        cnt = count_ge(mid, open_groups=was_open)
        nopen = n_open(lo, hi)
        ge = cnt >= k_sel
        lo_n = jnp.where(ge, mid, lo)
        hi_n = jnp.where(ge, hi, mid)
        hi_n = jnp.where(cnt == k_sel, lo_n, hi_n)
        hi_n = jnp.where(mid >= hi, lo_n, hi_n)
        hi_n = jnp.where(mid <= lo, lo_n, hi_n)
        return it + 1, nopen, lo_n, hi_n, jnp.where(ge, cnt, cnt_lo)

    st = lax.while_loop(cond, body, (jnp.int32(0), n_open(lo0, hi0), lo0, hi0,
                                     (rowb + 1).astype(F32)))
    th = st[2]
    th_ref[0] = th

    tied = jnp.logical_and(active0, st[4] > k_sel)

    @pl.when(jnp.sum(jnp.where(tied, 1.0, 0.0)) > 0.0)
    def _():
        keep_ties = k_sel - count_ge(th, strict=True)
        reps = tk // LANES
        th_w = jnp.concatenate([th] * reps, axis=1)
        keep_w = jnp.concatenate([keep_ties] * reps, axis=1)
        earlier = (lax.broadcasted_iota(jnp.int32, (tk, tk), 0)
                   < lax.broadcasted_iota(jnp.int32, (tk, tk), 1))
        tri = jnp.where(earlier, 1.0, 0.0).astype(BF16)

        def demote(c, seen):
            off = pl.multiple_of(c * tk, tk)
            blk = sc_ref[0, :, pl.ds(off, tk)]
            eq = blk == th_w
            eqf = jnp.where(eq, 1.0, 0.0)
            rank = _dot(eqf.astype(BF16), tri) + jnp.concatenate([seen] * reps, axis=1)
            sc_ref[0, :, pl.ds(off, tk)] = jnp.where(jnp.logical_and(eq, rank >= keep_w), NEG, blk)
            return seen + jnp.sum(eqf, axis=-1, keepdims=True)

        lax.fori_loop(0, n_vis, demote, jnp.zeros((tq, LANES), F32))


def _indexer(qi_r, klo, khi, wi, tq, tk, k_sel):
    b, s, _ = qi_r.shape
    return pl.pallas_call(
        functools.partial(_indexer_kernel, tq=tq, tk=tk, seq=s, k_sel=float(k_sel)),
        name="indexer",
        grid=(b, s // tq),
        in_specs=[
            pl.BlockSpec((1, tq, IDX_HEADS * IDX_DIM), lambda bi, qi: (bi, qi, 0)),
            pl.BlockSpec((1, s, LANES), lambda bi, qi: (bi, 0, 0)),
            pl.BlockSpec((1, s, LANES), lambda bi, qi: (bi, 0, 0)),
            pl.BlockSpec((1, tq, LANES), lambda bi, qi: (bi, qi, 0)),
        ],
        out_specs=[
            pl.BlockSpec((1, tq, s), lambda bi, qi: (bi, qi, 0)),
            pl.BlockSpec((1, tq, LANES), lambda bi, qi: (bi, qi, 0)),
        ],
        out_shape=[jax.ShapeDtypeStruct((b, s, s), F32), jax.ShapeDtypeStruct((b, s, LANES), F32)],
        compiler_params=_cparams(("parallel", "parallel")),
    )(qi_r, klo, khi, wi)


def _causal_pairs(s, tq, tk):
    qs, ks = [], []
    for qi in range(s // tq):
        for ki in range((qi * tq + tq - 1) // tk + 1):
            qs.append(qi)
            ks.append(ki)
    return jnp.asarray(np.array(qs, np.int32)), jnp.asarray(np.array(ks, np.int32))


def _softmax_step(s, v, rows, m_ref, l_ref, acc_ref, fixed_shift, sums_in_matmul=False):
    blocks = [s[:, j * LANES:(j + 1) * LANES] for j in range(s.shape[1] // LANES)]
    dv = v.shape[1]
    if fixed_shift and sums_in_matmul:
        p = jnp.concatenate([jnp.exp2(blk).astype(BF16) for blk in blocks], axis=1)
        acc_ref[rows] = acc_ref[rows] + _dot(p, jnp.concatenate([v, jnp.ones_like(v)], axis=1))
        return
    if fixed_shift:
        ps = [jnp.exp2(blk) for blk in blocks]
        l_ref[rows] = l_ref[rows] + functools.reduce(jnp.add, ps)
        p = jnp.concatenate([blk.astype(BF16) for blk in ps], axis=1)
        acc_ref[rows, :dv] = acc_ref[rows, :dv] + _dot(p, v)
        return
    mx = functools.reduce(jnp.maximum, blocks)
    m_prev = m_ref[rows]
    m_new = jnp.maximum(m_prev, jnp.max(mx, axis=-1, keepdims=True))
    alpha = jnp.exp2(m_prev - m_new)
    ps = [jnp.exp2(blk - m_new) for blk in blocks]
    l_ref[rows] = alpha * l_ref[rows] + jnp.sum(functools.reduce(jnp.add, ps), axis=-1, keepdims=True)
    p = jnp.concatenate([blk.astype(BF16) for blk in ps], axis=1)
    acc_ref[rows, :dv] = alpha * acc_ref[rows, :dv] + _dot(p, v)
    m_ref[rows] = m_new


def _softmax_finish(rows, l_ref, acc_ref, dv, fixed_shift, sums_in_matmul=False):
    if fixed_shift and sums_in_matmul:
        acc = acc_ref[rows]
        return acc[:, :dv] / acc[:, dv:]
    l = l_ref[rows]
    if fixed_shift:
        l = jnp.sum(l, axis=-1, keepdims=True)
    return acc_ref[rows, :dv] / l


def _dsa_kernel(qt_ref, kt_ref, q_ref, k_ref, v_ref, sc_ref, th_ref, shift_ref, o_ref, m_ref, l_ref,
                acc_ref, bias_ref, qs_ref, *, tq, tk, fixed_shift):
    step = pl.program_id(1)
    qi = qt_ref[step]
    ki = kt_ref[step]

    @pl.when(ki == 0)
    def _():
        m_ref[...] = jnp.full(m_ref.shape, M_INIT, F32)
        l_ref[...] = jnp.zeros(l_ref.shape, F32)
        acc_ref[...] = jnp.zeros(acc_ref.shape, F32)
        for h in range(A_HEADS):
            qs_ref[h * tq:(h + 1) * tq, :] = q_ref[0, :, h * LANES:(h + 1) * LANES]

    bias_ref[...] = jnp.where(sc_ref[0] >= th_ref[0][:, :1], -shift_ref[:, :1], NEG)
    grp = A_HEADS // A_KV_HEADS
    for g in range(A_KV_HEADS):
        rows = slice(g * grp * tq, (g + 1) * grp * tq)
        k = k_ref[0, :, g * LANES:(g + 1) * LANES]
        v = v_ref[0, :, g * LANES:(g + 1) * LANES]
        s = _dot_nt(qs_ref[rows, :], k)
        s = jnp.concatenate([s[j * tq:(j + 1) * tq] + bias_ref[...] for j in range(grp)], axis=0)
        _softmax_step(s, v, rows, m_ref, l_ref, acc_ref, fixed_shift, sums_in_matmul=True)

    @pl.when(ki == (qi * tq + tq - 1) // tk)
    def _():
        for h in range(A_HEADS):
            out = _softmax_finish(slice(h * tq, (h + 1) * tq), l_ref, acc_ref, A_HEAD_DIM, fixed_shift,
                                  sums_in_matmul=True)
            o_ref[0, :, h * LANES:(h + 1) * LANES] = out.astype(o_ref.dtype)


def _dsa_attn(q, k, v, scores, th, shift, tq, tk, fixed_shift):
    b, s, dq = q.shape
    qt, kt = _causal_pairs(s, tq, tk)
    grid_spec = pltpu.PrefetchScalarGridSpec(
        num_scalar_prefetch=2,
        grid=(b, qt.shape[0]),
        in_specs=[
            pl.BlockSpec((1, tq, dq), lambda bi, p, qt, kt: (bi, qt[p], 0)),
            pl.BlockSpec((1, tk, k.shape[2]), lambda bi, p, qt, kt: (bi, kt[p], 0)),
            pl.BlockSpec((1, tk, v.shape[2]), lambda bi, p, qt, kt: (bi, kt[p], 0)),
            pl.BlockSpec((1, tq, tk), lambda bi, p, qt, kt: (bi, qt[p], kt[p])),
            pl.BlockSpec((1, tq, LANES), lambda bi, p, qt, kt: (bi, qt[p], 0)),
            pl.BlockSpec((1, LANES), lambda bi, p, qt, kt: (0, 0)),
        ],
        out_specs=pl.BlockSpec((1, tq, dq), lambda bi, p, qt, kt: (bi, qt[p], 0)),
        scratch_shapes=[
            pltpu.VMEM((A_HEADS * tq, LANES), F32),
            pltpu.VMEM((A_HEADS * tq, LANES), F32),
            pltpu.VMEM((A_HEADS * tq, 2 * A_HEAD_DIM), F32),
            pltpu.VMEM((tq, tk), F32),
            pltpu.VMEM((A_HEADS * tq, A_HEAD_DIM), BF16),
        ],
    )
    return pl.pallas_call(
        functools.partial(_dsa_kernel, tq=tq, tk=tk, fixed_shift=fixed_shift),
        name="dsa_attn_fixed" if fixed_shift else "dsa_attn_online",
        grid_spec=grid_spec,
        out_shape=jax.ShapeDtypeStruct((b, s, dq), BF16),
        compiler_params=_cparams(("parallel", "arbitrary")),
    )(qt, kt, q, k, v, scores, th, shift)


def _mla_kernel(qt_ref, kt_ref, q_ref, k_ref, v_ref, o_ref, m_ref, l_ref, acc_ref, *, tq, tk, hg,
                fixed_shift):
    step = pl.program_id(2)
    qi = qt_ref[step]
    ki = kt_ref[step]

    @pl.when(ki == 0)
    def _():
        m_ref[...] = jnp.full(m_ref.shape, M_INIT, F32)
        l_ref[...] = jnp.zeros(l_ref.shape, F32)
        acc_ref[...] = jnp.zeros(acc_ref.shape, F32)

    def heads(masked):
        if masked:
            keep = (lax.broadcasted_iota(jnp.int32, (tq, tk), 1)
                    <= lax.broadcasted_iota(jnp.int32, (tq, tk), 0))
        for h in range(hg):
            q = q_ref[0, :, h * 256:(h + 1) * 256]
            k = k_ref[0, :, h * 256:(h + 1) * 256]
            v = v_ref[0, :, h * B_V:(h + 1) * B_V]
            s = _dot_nt(q, k)
            if masked:
                s = jnp.where(keep, s, NEG)
            _softmax_step(s, v, slice(h * tq, (h + 1) * tq), m_ref, l_ref, acc_ref, fixed_shift)

    @pl.when(ki < qi)
    def _():
        heads(False)

    @pl.when(ki == qi)
    def _():
        heads(True)
        for h in range(hg):
            out = _softmax_finish(slice(h * tq, (h + 1) * tq), l_ref, acc_ref, B_V, fixed_shift)
            o_ref[0, :, h * B_V:(h + 1) * B_V] = out.astype(o_ref.dtype)


def _mla_attn(q, k, v, tq, tk, hg, fixed_shift):
    b, s, _ = q.shape
    assert tq == tk
    qt, kt = _causal_pairs(s, tq, tk)
    grid_spec = pltpu.PrefetchScalarGridSpec(
        num_scalar_prefetch=2,
        grid=(b, B_HEADS // hg, qt.shape[0]),
        in_specs=[
            pl.BlockSpec((1, tq, hg * 256), lambda bi, g, p, qt, kt: (bi, qt[p], g)),
            pl.BlockSpec((1, tk, hg * 256), lambda bi, g, p, qt, kt: (bi, kt[p], g)),
            pl.BlockSpec((1, tk, hg * B_V), lambda bi, g, p, qt, kt: (bi, kt[p], g)),
        ],
        out_specs=pl.BlockSpec((1, tq, hg * B_V), lambda bi, g, p, qt, kt: (bi, qt[p], g)),
        scratch_shapes=[
            pltpu.VMEM((hg * tq, LANES), F32),
            pltpu.VMEM((hg * tq, LANES), F32),
            pltpu.VMEM((hg * tq, B_V), F32),
        ],
    )
    return pl.pallas_call(
        functools.partial(_mla_kernel, tq=tq, tk=tk, hg=hg, fixed_shift=fixed_shift),
        name="mla_attn_fixed" if fixed_shift else "mla_attn_online",
        grid_spec=grid_spec,
        out_shape=jax.ShapeDtypeStruct((b, s, B_HEADS * B_V), BF16),
        compiler_params=_cparams(("parallel", "parallel", "arbitrary")),
    )(qt, kt, q, k, v)


def _regroup_w_in(w):
    sizes = (A_HEADS * A_HEAD_DIM, A_KV_HEADS * A_HEAD_DIM, A_KV_HEADS * A_HEAD_DIM,
             IDX_HEADS * IDX_DIM, IDX_DIM, IDX_HEADS, B_Q_LORA, B_KV_LORA, B_ROPE)
    offs = np.cumsum((0,) + sizes)
    w = w.astype(BF16)
    qa, ka, va, qi, ki, wi, cq, ckv, kr = (w[:, offs[j]:offs[j + 1]] for j in range(len(sizes)))
    gates = w[:, offs[-1]:]
    d = w.shape[0]
    kiw = jnp.concatenate([ki, wi, jnp.zeros((d, LANES - IDX_DIM - IDX_HEADS), w.dtype)], axis=1)
    krp = jnp.concatenate([kr, jnp.zeros((d, LANES - B_ROPE), w.dtype)], axis=1)
    return jnp.concatenate([qa, gates, qi, ka, va, cq, ckv, kiw, krp], axis=1).astype(BF16)


def _softmax_shift(d, g_q, g_k):
    bound = (d ** 0.5) * LOG2E * 1.02 * jnp.max(jnp.abs(g_q)) * jnp.max(jnp.abs(g_k))
    c = jnp.ceil(bound.astype(F32))
    ok = c <= MAX_FIXED_SHIFT
    return jnp.full((1, LANES), jnp.where(ok, c, 0.0), F32), ok


def _rope_tables(pos, d):
    half = d // 2
    freqs = jnp.power(jnp.float32(ROPE_THETA), -jnp.arange(half, dtype=F32) * (2.0 / d))
    ang = pos.astype(F32)[..., None] * freqs
    cos, sin = jnp.cos(ang), jnp.sin(ang)
    reps = LANES // d
    c = jnp.tile(jnp.concatenate([cos, cos], axis=-1), (1, 1, reps))
    s = jnp.tile(jnp.concatenate([-sin, sin], axis=-1), (1, 1, reps))
    return c.reshape(-1, LANES), s.reshape(-1, LANES)


def _layer(x, p, positions, g_mix_norm, w_in, g_qa, g_ka, g_cq, w_uq, g_ckv, w_ukv, g_qb, g_kb,
           w_out_a, w_out_b, w_o, g_ffn_norm, w_ffn_gate, w_ffn_up, w_ffn_down, g_ple_norm,
           w_ple_gate, w_ple_proj):
    b, s, d = x.shape
    m = b * s
    k_sel = min(TOPK_MAX, s // 4)
    xf = x.reshape(m, d)
    row = lambda g: g.reshape(1, -1).astype(F32)

    c128, s128 = _rope_tables(positions, A_HEAD_DIM)
    c64, s64 = _rope_tables(positions, IDX_DIM)

    proj = _norm_matmul("in_proj", xf, 0, d, row(g_mix_norm), _regroup_w_in(w_in), F32,
                        tm=TILE_IN_PROJ[0], tn=TILE_IN_PROJ[1])

    qa_r, ka_r, va_b, qi_r, klo, khi, wi = _prep_a(
        proj, c128, s128, c64, s64, row(g_qa) * (A_HEAD_DIM ** -0.5 * LOG2E), row(g_ka), tp=TILE_PREP_ROWS)
    r3 = lambda a: a.reshape(b, s, a.shape[-1])
    scores, th = _indexer(r3(qi_r), r3(klo), r3(khi), r3(wi), tq=TILE_ATTN, tk=TILE_ATTN, k_sel=k_sel)
    shift_a, fixed_a = _softmax_shift(A_HEAD_DIM, g_qa, g_ka)
    y_a = lax.cond(
        fixed_a,
        functools.partial(_dsa_attn, tq=TILE_ATTN, tk=TILE_ATTN, fixed_shift=True),
        functools.partial(_dsa_attn, tq=TILE_ATTN, tk=TILE_ATTN, fixed_shift=False),
        r3(qa_r), r3(ka_r), r3(va_b), scores, th, shift_a)

    w_uq_r = w_uq.reshape(B_Q_LORA, B_HEADS, B_QK)
    w_uq_r = jnp.concatenate([w_uq_r[:, :, :B_NOPE].reshape(B_Q_LORA, -1),
                              w_uq_r[:, :, B_NOPE:].reshape(B_Q_LORA, -1)], axis=1).astype(BF16)
    w_ukv_r = w_ukv.reshape(B_KV_LORA, B_HEADS, 2, B_NOPE)
    w_ukv_r = jnp.concatenate([w_ukv_r[:, :, 0, :].reshape(B_KV_LORA, -1),
                               w_ukv_r[:, :, 1, :].reshape(B_KV_LORA, -1)], axis=1).astype(BF16)
    pe2 = lambda g: jnp.tile(g[B_NOPE:], 2).reshape(1, LANES).astype(F32)
    shift_b, fixed_b = _softmax_shift(B_QK, g_qb, g_kb)
    q_m = _mla_q(proj, row(g_cq), w_uq_r, c64, s64, row(g_qb[:B_NOPE]), pe2(g_qb), tm=TILE_MLA_UP_ROWS)
    k_m, v_m = _mla_kv(proj, row(g_ckv), w_ukv_r, c64, s64, row(g_kb[:B_NOPE]), pe2(g_kb), shift_b,
                       tm=TILE_MLA_UP_ROWS)
    y_b = lax.cond(
        fixed_b,
        functools.partial(_mla_attn, tq=TILE_ATTN, tk=TILE_ATTN, hg=B_HEADS, fixed_shift=True),
        functools.partial(_mla_attn, tq=TILE_ATTN, tk=TILE_ATTN, hg=B_HEADS, fixed_shift=False),
        r3(q_m), r3(k_m), r3(v_m))

    mixed = _merge(y_a.reshape(m, -1), y_b.reshape(m, -1), w_out_a.astype(BF16), w_out_b.astype(BF16),
                   proj, tm=TILE_MERGE[0], tn=TILE_MERGE[1])
    x1 = _matmul_res("w_o", mixed, w_o.astype(BF16), xf, tm=TILE_W_O[0], tn=TILE_W_O[1])

    act = _ffn_up(x1, row(g_ffn_norm), w_ffn_gate.astype(BF16), w_ffn_up.astype(BF16),
                  tm=TILE_FFN_UP[0], tn=TILE_FFN_UP[1])
    x2 = _matmul_res("ffn_down", act, w_ffn_down.astype(BF16), x1,
                     tm=TILE_FFN_DOWN[0], tn=TILE_FFN_DOWN[1])

    out = _ple(x2, row(g_ple_norm), w_ple_gate.astype(BF16), p.reshape(m, -1),
               w_ple_proj.astype(BF16), tm=TILE_PLE_ROWS)
    return out.reshape(b, s, d)


def kernel(x, p, positions, g_mix_norm, w_in, g_qa, g_ka, g_cq, w_uq, g_ckv, w_ukv, g_qb, g_kb,
           w_out_a, w_out_b, w_o, g_ffn_norm, w_ffn_gate, w_ffn_up, w_ffn_down, g_ple_norm,
           w_ple_gate, w_ple_proj):
    for i in range(p.shape[0]):
        x = _layer(x, p[i], positions, g_mix_norm[i], w_in[i], g_qa[i], g_ka[i], g_cq[i], w_uq[i],
                   g_ckv[i], w_ukv[i], g_qb[i], g_kb[i], w_out_a[i], w_out_b[i], w_o[i],
                   g_ffn_norm[i], w_ffn_gate[i], w_ffn_up[i], w_ffn_down[i], g_ple_norm[i],
                   w_ple_gate[i], w_ple_proj[i])
    return x
```

```python
import functools

import numpy as np
import jax
import jax.numpy as jnp
from jax import lax
from jax.experimental import pallas as pl
from jax.experimental.pallas import tpu as pltpu

F32 = jnp.float32
BF16 = jnp.bfloat16

ROPE_THETA = 10000.0
NORM_EPS = 1e-6
NEG = -1e30
M_INIT = -1e29
POS_BIG = 3e38
TH_ALL = 0.5 * NEG
LOG2E = 1.4426950408889634
MAX_SEARCH_STEPS = 512
CNT_ROWS = 128
MAX_FIXED_SHIFT = 60.0
A_HEADS = 16
A_KV_HEADS = 4
A_HEAD_DIM = 128
IDX_HEADS = 16
IDX_DIM = 64
TOPK_MAX = 256
B_HEADS = 16
B_Q_LORA = 512
B_KV_LORA = 256
B_NOPE = 128
B_ROPE = 64
B_V = 128
B_QK = B_NOPE + B_ROPE
LANES = 128
VMEM_LIMIT = 56 * 1024 * 1024

TILE_IN_PROJ = (1024, 1536)
TILE_PREP_ROWS = 512
TILE_ATTN = 512
TILE_MLA_UP_ROWS = 512
TILE_MERGE = (1024, 512)
TILE_W_O = (1024, 1024)
TILE_FFN_UP = (1024, 512)
TILE_FFN_DOWN = (512, 1024)
TILE_PLE_ROWS = 512

COL_QA = 0
COL_GATE_A = 2048
COL_GATE_B = 4096
COL_QI = 6144
COL_KA = 7168
COL_VA = 7680
COL_CQ = 8192
COL_CKV = 8704
COL_KIW = 8960
COL_KR = 9088
N_PROJ = 9216


def _dot(a, b):
    return jnp.dot(a, b, preferred_element_type=F32)


def _dot_nt(a, b):
    return lax.dot_general(a, b, (((1,), (1,)), ((), ())), preferred_element_type=F32)


def _cparams(sem):
    return pltpu.CompilerParams(dimension_semantics=sem, vmem_limit_bytes=VMEM_LIMIT)


def _rms_to_bf16(x, g):
    ms = jnp.mean(x * x, axis=-1, keepdims=True)
    return (x * lax.rsqrt(ms + NORM_EPS) * g).astype(BF16)


def _norm_matmul_kernel(x_ref, g_ref, w_ref, o_ref, hn_ref):
    @pl.when(pl.program_id(1) == 0)
    def _():
        hn_ref[...] = _rms_to_bf16(x_ref[...], g_ref[...])

    o_ref[...] = _dot(hn_ref[...], w_ref[...]).astype(o_ref.dtype)


def _norm_matmul(name, x, x_col_block, k, g, w, out_dtype, tm, tn):
    m = x.shape[0]
    n = w.shape[1]
    return pl.pallas_call(
        _norm_matmul_kernel,
        name=name,
        grid=(m // tm, n // tn),
        in_specs=[
            pl.BlockSpec((tm, k), lambda i, j: (i, x_col_block)),
            pl.BlockSpec((1, k), lambda i, j: (0, 0)),
            pl.BlockSpec((k, tn), lambda i, j: (0, j)),
        ],
        out_specs=pl.BlockSpec((tm, tn), lambda i, j: (i, j)),
        out_shape=jax.ShapeDtypeStruct((m, n), out_dtype),
        scratch_shapes=[pltpu.VMEM((tm, k), BF16)],
        compiler_params=_cparams(("parallel", "arbitrary")),
    )(x, g, w)


def _ffn_up_kernel(x_ref, g_ref, wg_ref, wu_ref, o_ref, hn_ref):
    @pl.when(pl.program_id(1) == 0)
    def _():
        hn_ref[...] = _rms_to_bf16(x_ref[...], g_ref[...])

    h = hn_ref[...]
    a = _dot(h, wg_ref[...])
    b = _dot(h, wu_ref[...])
    o_ref[...] = (a * jax.nn.sigmoid(a) * b).astype(o_ref.dtype)


def _ffn_up(x, g, wg, wu, tm, tn):
    m, k = x.shape
    n = wg.shape[1]
    return pl.pallas_call(
        _ffn_up_kernel,
        name="ffn_up",
        grid=(m // tm, n // tn),
        in_specs=[
            pl.BlockSpec((tm, k), lambda i, j: (i, 0)),
            pl.BlockSpec((1, k), lambda i, j: (0, 0)),
            pl.BlockSpec((k, tn), lambda i, j: (0, j)),
            pl.BlockSpec((k, tn), lambda i, j: (0, j)),
        ],
        out_specs=pl.BlockSpec((tm, tn), lambda i, j: (i, j)),
        out_shape=jax.ShapeDtypeStruct((m, n), BF16),
        scratch_shapes=[pltpu.VMEM((tm, k), BF16)],
        compiler_params=_cparams(("parallel", "arbitrary")),
    )(x, g, wg, wu)


def _matmul_res_kernel(a_ref, w_ref, r_ref, o_ref):
    o_ref[...] = r_ref[...] + _dot(a_ref[...], w_ref[...])


def _matmul_res(name, a, w, res, tm, tn):
    m, k = a.shape
    n = w.shape[1]
    return pl.pallas_call(
        _matmul_res_kernel,
        name=name,
        grid=(n // tn, m // tm),
        in_specs=[
            pl.BlockSpec((tm, k), lambda j, i: (i, 0)),
            pl.BlockSpec((k, tn), lambda j, i: (0, j)),
            pl.BlockSpec((tm, tn), lambda j, i: (i, j)),
        ],
        out_specs=pl.BlockSpec((tm, tn), lambda j, i: (i, j)),
        out_shape=jax.ShapeDtypeStruct((m, n), F32),
        compiler_params=_cparams(("parallel", "parallel")),
    )(a, w, res)


def _merge_kernel(ya_ref, yb_ref, wa_ref, wb_ref, ga_ref, gb_ref, o_ref):
    a = _dot(ya_ref[...], wa_ref[...])
    b = _dot(yb_ref[...], wb_ref[...])
    o_ref[...] = (jax.nn.sigmoid(ga_ref[...]) * a + jax.nn.sigmoid(gb_ref[...]) * b).astype(o_ref.dtype)


def _merge(ya, yb, wa, wb, proj, tm, tn):
    m, k = ya.shape
    n = wa.shape[1]
    ga0 = COL_GATE_A // tn
    gb0 = COL_GATE_B // tn
    return pl.pallas_call(
        _merge_kernel,
        name="merge",
        grid=(m // tm, n // tn),
        in_specs=[
            pl.BlockSpec((tm, k), lambda i, j: (i, 0)),
            pl.BlockSpec((tm, k), lambda i, j: (i, 0)),
            pl.BlockSpec((k, tn), lambda i, j: (0, j)),
            pl.BlockSpec((k, tn), lambda i, j: (0, j)),
            pl.BlockSpec((tm, tn), lambda i, j: (i, ga0 + j)),
            pl.BlockSpec((tm, tn), lambda i, j: (i, gb0 + j)),
        ],
        out_specs=pl.BlockSpec((tm, tn), lambda i, j: (i, j)),
        out_shape=jax.ShapeDtypeStruct((m, n), BF16),
        compiler_params=_cparams(("parallel", "parallel")),
    )(ya, yb, wa, wb, proj, proj)


def _ple_kernel(x_ref, g_ref, wg_ref, p_ref, wp_ref, o_ref):
    x = x_ref[...]
    gate = jax.nn.sigmoid(_dot(_rms_to_bf16(x, g_ref[...]), wg_ref[...]))
    emb = _dot(p_ref[...].astype(BF16), wp_ref[...])
    o_ref[...] = x + gate * emb


def _ple(x, g, wg, p, wp, tm):
    m, k = x.shape
    n = wg.shape[1]
    kp = p.shape[1]
    assert n == k
    return pl.pallas_call(
        _ple_kernel,
        name="ple",
        grid=(m // tm,),
        in_specs=[
            pl.BlockSpec((tm, k), lambda i: (i, 0)),
            pl.BlockSpec((1, k), lambda i: (0, 0)),
            pl.BlockSpec((k, n), lambda i: (0, 0)),
            pl.BlockSpec((tm, kp), lambda i: (i, 0)),
            pl.BlockSpec((kp, n), lambda i: (0, 0)),
        ],
        out_specs=pl.BlockSpec((tm, n), lambda i: (i, 0)),
        out_shape=jax.ShapeDtypeStruct((m, n), F32),
        compiler_params=_cparams(("parallel",)),
    )(x, g, wg, p, wp)


def _rope64_pairs(x, c64, s64, first_half):
    partner = jnp.where(first_half, pltpu.roll(x, 96, 1), pltpu.roll(x, 32, 1))
    return x * c64 + partner * s64


def _prep_a_kernel(qa_ref, ka_ref, va_ref, qi_ref, kiw_ref, c128_ref, s128_ref, c64_ref, s64_ref,
                   gq_ref, gk_ref, qa_o, ka_o, va_o, qi_o, klo_o, khi_o, wi_o, *, wi_scale):
    c128 = c128_ref[...]
    s128 = s128_ref[...]

    def norm_rope(x, g):
        ms = jnp.mean(x * x, axis=-1, keepdims=True)
        y = x * lax.rsqrt(ms + NORM_EPS) * g
        return y * c128 + pltpu.roll(y, 64, 1) * s128

    for h in range(A_HEADS):
        hs = slice(h * LANES, (h + 1) * LANES)
        qa_o[:, hs] = norm_rope(qa_ref[:, hs], gq_ref[...]).astype(BF16)
    for h in range(A_KV_HEADS):
        hs = slice(h * LANES, (h + 1) * LANES)
        ka_o[:, hs] = norm_rope(ka_ref[:, hs], gk_ref[...]).astype(BF16)
    va_o[...] = va_ref[...].astype(BF16)

    c64 = c64_ref[...]
    s64 = s64_ref[...]
    lane = lax.broadcasted_iota(jnp.int32, c64.shape, 1)
    first_half = (lane % 64) < 32
    for p in range(IDX_HEADS * IDX_DIM // LANES):
        ps = slice(p * LANES, (p + 1) * LANES)
        qi_o[:, ps] = _rope64_pairs(qi_ref[:, ps], c64, s64, first_half).astype(BF16)
    kiw = kiw_ref[...]
    ki = jnp.where(lane < 64, _rope64_pairs(kiw, c64, s64, first_half), 0.0)
    klo_o[...] = ki.astype(BF16)
    khi_o[...] = pltpu.roll(ki, 64, 1).astype(BF16)
    wi_o[...] = kiw * wi_scale


def _prep_a(proj, c128, s128, c64, s64, gq, gk, tp):
    m = proj.shape[0]
    row = lambda w, cb: pl.BlockSpec((tp, w), lambda i: (i, cb))
    vec = pl.BlockSpec((1, LANES), lambda i: (0, 0))
    out = lambda w: pl.BlockSpec((tp, w), lambda i: (i, 0))
    wi_scale = (IDX_HEADS ** -0.5) * (IDX_DIM ** -0.5)
    return pl.pallas_call(
        functools.partial(_prep_a_kernel, wi_scale=wi_scale),
        name="prep_a",
        grid=(m // tp,),
        in_specs=[
            row(2048, COL_QA // 2048), row(512, COL_KA // 512), row(512, COL_VA // 512),
            row(1024, COL_QI // 1024), row(LANES, COL_KIW // LANES),
            out(LANES), out(LANES), out(LANES), out(LANES), vec, vec,
        ],
        out_specs=[out(2048), out(512), out(512), out(1024), out(LANES), out(LANES), out(LANES)],
        out_shape=[
            jax.ShapeDtypeStruct((m, 2048), BF16), jax.ShapeDtypeStruct((m, 512), BF16),
            jax.ShapeDtypeStruct((m, 512), BF16), jax.ShapeDtypeStruct((m, 1024), BF16),
            jax.ShapeDtypeStruct((m, LANES), BF16), jax.ShapeDtypeStruct((m, LANES), BF16),
            jax.ShapeDtypeStruct((m, LANES), F32),
        ],
        compiler_params=_cparams(("parallel",)),
    )(proj, proj, proj, proj, proj, c128, s128, c64, s64, gq, gk)


def _mla_q_kernel(cq_ref, g_ref, w_ref, c64_ref, s64_ref, gqn_ref, gqp_ref, q_o, *, q_scale):
    qb = _dot(_rms_to_bf16(cq_ref[...], g_ref[...]), w_ref[...])
    c64 = c64_ref[...]
    s64 = s64_ref[...]
    lane = lax.broadcasted_iota(jnp.int32, c64.shape, 1)
    first_half = (lane % 64) < 32
    lo = lane < 64
    nope_w = B_HEADS * B_NOPE
    for pair in range(B_HEADS // 2):
        qp2 = qb[:, nope_w + pair * LANES:nope_w + (pair + 1) * LANES]
        qp2_sq = qp2 * qp2
        rot2 = _rope64_pairs(qp2 * gqp_ref[...], c64, s64, first_half)
        for h in (2 * pair, 2 * pair + 1):
            mine = lo if h % 2 == 0 else jnp.logical_not(lo)
            shift_lane = lane == (64 if h % 2 == 0 else 0)
            qn = qb[:, h * LANES:(h + 1) * LANES]
            ss = jnp.sum(qn * qn + jnp.where(mine, qp2_sq, 0.0), axis=-1, keepdims=True)
            r = lax.rsqrt(ss * (1.0 / B_QK) + NORM_EPS) * q_scale
            q_o[:, 2 * h * LANES:(2 * h + 1) * LANES] = (qn * r * gqn_ref[...]).astype(BF16)
            qpe = jnp.where(mine, rot2 * r, jnp.where(shift_lane, 1.0, 0.0))
            q_o[:, (2 * h + 1) * LANES:(2 * h + 2) * LANES] = qpe.astype(BF16)


def _mla_kv_kernel(ckv_ref, g_ref, w_ref, kr_ref, c64_ref, s64_ref, gkn_ref, gkp_ref, shift_ref, k_o, v_o):
    kv = _dot(_rms_to_bf16(ckv_ref[...], g_ref[...]), w_ref[...])
    c64 = c64_ref[...]
    s64 = s64_ref[...]
    lane = lax.broadcasted_iota(jnp.int32, c64.shape, 1)
    first_half = (lane % 64) < 32
    lo = lane < 64
    kr = jnp.where(lo, kr_ref[...], 0.0)
    kr_ss = jnp.sum(kr * kr, axis=-1, keepdims=True)
    kpe_lo = jnp.where(lo, _rope64_pairs(kr * gkp_ref[...], c64, s64, first_half), 0.0)
    kpe_hi = pltpu.roll(kpe_lo, 64, 1)
    nope_w = B_HEADS * B_NOPE
    for h in range(B_HEADS):
        hs = slice(h * LANES, (h + 1) * LANES)
        shift_lane = lane == (64 if h % 2 == 0 else 0)
        kn = kv[:, hs]
        rk = lax.rsqrt((jnp.sum(kn * kn, axis=-1, keepdims=True) + kr_ss) * (1.0 / B_QK) + NORM_EPS)
        k_o[:, 2 * h * LANES:(2 * h + 1) * LANES] = (kn * rk * gkn_ref[...]).astype(BF16)
        kpe = kpe_lo if h % 2 == 0 else kpe_hi
        k_o[:, (2 * h + 1) * LANES:(2 * h + 2) * LANES] = jnp.where(
            shift_lane, -shift_ref[...], kpe * rk).astype(BF16)
        v_o[:, hs] = kv[:, nope_w + h * LANES:nope_w + (h + 1) * LANES].astype(BF16)


def _mla_q(proj, g_cq, w_uq, c64, s64, gqn, gqp, tm):
    m = proj.shape[0]
    vec = lambda w: pl.BlockSpec((1, w), lambda i: (0, 0))
    rows = lambda w: pl.BlockSpec((tm, w), lambda i: (i, 0))
    return pl.pallas_call(
        functools.partial(_mla_q_kernel, q_scale=B_QK ** -0.5 * LOG2E),
        name="mla_q",
        grid=(m // tm,),
        in_specs=[
            pl.BlockSpec((tm, B_Q_LORA), lambda i: (i, COL_CQ // B_Q_LORA)),
            vec(B_Q_LORA),
            pl.BlockSpec(w_uq.shape, lambda i: (0, 0)),
            rows(LANES), rows(LANES), vec(LANES), vec(LANES),
        ],
        out_specs=rows(B_HEADS * 256),
        out_shape=jax.ShapeDtypeStruct((m, B_HEADS * 256), BF16),
        compiler_params=_cparams(("parallel",)),
    )(proj, g_cq, w_uq, c64, s64, gqn, gqp)


def _mla_kv(proj, g_ckv, w_ukv, c64, s64, gkn, gkp, shift, tm):
    m = proj.shape[0]
    vec = lambda w: pl.BlockSpec((1, w), lambda i: (0, 0))
    rows = lambda w: pl.BlockSpec((tm, w), lambda i: (i, 0))
    return pl.pallas_call(
        _mla_kv_kernel,
        name="mla_kv",
        grid=(m // tm,),
        in_specs=[
            pl.BlockSpec((tm, B_KV_LORA), lambda i: (i, COL_CKV // B_KV_LORA)),
            vec(B_KV_LORA),
            pl.BlockSpec(w_ukv.shape, lambda i: (0, 0)),
            pl.BlockSpec((tm, LANES), lambda i: (i, COL_KR // LANES)),
            rows(LANES), rows(LANES), vec(LANES), vec(LANES), vec(LANES),
        ],
        out_specs=[rows(B_HEADS * 256), rows(B_HEADS * B_V)],
        out_shape=[
            jax.ShapeDtypeStruct((m, B_HEADS * 256), BF16),
            jax.ShapeDtypeStruct((m, B_HEADS * B_V), BF16),
        ],
        compiler_params=_cparams(("parallel",)),
    )(proj, g_ckv, w_ukv, proj, c64, s64, gkn, gkp, shift)


def _indexer_kernel(q_ref, klo_ref, khi_ref, w_ref, sc_ref, th_ref, *, tq, tk, seq, k_sel):
    qi = pl.program_id(1)
    n_vis = (qi * tq) // tk + 1
    n_all = seq // tk
    row = qi * tq + lax.broadcasted_iota(jnp.int32, (tq, 1), 0)
    wcols = [w_ref[0, :, IDX_DIM + h:IDX_DIM + h + 1] for h in range(IDX_HEADS)]

    def gen(c, carry):
        rmax, rmin = carry
        off = pl.multiple_of(c * tk, tk)
        kl = klo_ref[0, pl.ds(off, tk), :]
        kh = khi_ref[0, pl.ds(off, tk), :]
        acc = jnp.zeros((tq, tk), F32)
        for p in range(IDX_HEADS // 2):
            qp = q_ref[0, :, p * LANES:(p + 1) * LANES]
            acc = acc + wcols[2 * p] * jnp.maximum(_dot_nt(qp, kl), 0.0)
            acc = acc + wcols[2 * p + 1] * jnp.maximum(_dot_nt(qp, kh), 0.0)
        col = off + lax.broadcasted_iota(jnp.int32, (tq, tk), 1)
        vis = col <= row
        sc_ref[0, :, pl.ds(off, tk)] = jnp.where(vis, acc, NEG)
        rmax = jnp.maximum(rmax, jnp.max(jnp.where(vis, acc, NEG), axis=-1, keepdims=True))
        rmin = jnp.minimum(rmin, jnp.min(jnp.where(vis, acc, POS_BIG), axis=-1, keepdims=True))
        return rmax, rmin

    rmax, rmin = lax.fori_loop(0, n_vis, gen,
                               (jnp.full((tq, 1), NEG, F32), jnp.full((tq, 1), POS_BIG, F32)))

    def fill(c, carry):
        sc_ref[0, :, pl.ds(pl.multiple_of(c * tk, tk), tk)] = jnp.full((tq, tk), NEG, F32)
        return carry

    lax.fori_loop(n_vis, n_all, fill, 0)

    cw = min(2 * tk, seq)
    n_cnt = (n_vis * tk + cw - 1) // cw

    def count_ge(mid, strict=False, open_groups=None):
        zeros = jnp.zeros((CNT_ROWS, LANES), F32)
        accs = []
        for r in range(tq // CNT_ROWS):
            rs = slice(r * CNT_ROWS, (r + 1) * CNT_ROWS)
            midb = mid[rs]

            def body(c, acc, rs=rs, midb=midb):
                slab = sc_ref[0, rs, pl.ds(pl.multiple_of(c * cw, cw), cw)]
                for j in range(cw // LANES):
                    blk = slab[:, j * LANES:(j + 1) * LANES]
                    acc = acc + jnp.where(blk > midb if strict else blk >= midb, 1.0, 0.0)
                return acc

            if open_groups is None:
                accs.append(lax.fori_loop(0, n_cnt, body, zeros))
            else:
                accs.append(lax.cond(open_groups[r] > 0.0,
                                     functools.partial(lax.fori_loop, 0, n_cnt, body, zeros),
                                     lambda: zeros))
        return jnp.concatenate(
            [jnp.broadcast_to(jnp.sum(a, axis=-1, keepdims=True), (CNT_ROWS, LANES)) for a in accs], axis=0)

    rowb = qi * tq + lax.broadcasted_iota(jnp.int32, (tq, LANES), 0)
    active0 = (rowb + 1) > k_sel
    lo0 = jnp.where(active0, jnp.broadcast_to(rmin, (tq, LANES)), TH_ALL)
    hi0 = jnp.where(active0, jnp.broadcast_to(rmax, (tq, LANES)), TH_ALL)

    def n_open(lo, hi):
        still = jnp.where(lo < hi, 1.0, 0.0)
        return tuple(jnp.sum(still[r * CNT_ROWS:(r + 1) * CNT_ROWS]) for r in range(tq // CNT_ROWS))

    def cond(st):
        it, nopen = st[0], st[1]
        return jnp.logical_and(functools.reduce(jnp.add, nopen) > 0.0, it < MAX_SEARCH_STEPS)

    def body(st):
        it, was_open, lo, hi, cnt_lo = st
        mid = lo + (hi - lo) * 0.5
        cnt = count_ge(mid, open_groups=was_open)
        nopen = n_open(lo, hi)
        ge = cnt >= k_sel
        lo_n = jnp.where(ge, mid, lo)
        hi_n = jnp.where(ge, hi, mid)
        hi_n = jnp.where(cnt == k_sel, lo_n, hi_n)
        hi_n = jnp.where(mid >= hi, lo_n, hi_n)
        hi_n = jnp.where(mid <= lo, lo_n, hi_n)
        return it + 1, nopen, lo_n, hi_n, jnp.where(ge, cnt, cnt_lo)

    st = lax.while_loop(cond, body, (jnp.int32(0), n_open(lo0, hi0), lo0, hi0,
                                     (rowb + 1).astype(F32)))
    th = st[2]
    th_ref[0] = th

    tied = jnp.logical_and(active0, st[4] > k_sel)

    @pl.when(jnp.sum(jnp.where(tied, 1.0, 0.0)) > 0.0)
    def _():
        keep_ties = k_sel - count_ge(th, strict=True)
        reps = tk // LANES
        th_w = jnp.concatenate([th] * reps, axis=1)
        keep_w = jnp.concatenate([keep_ties] * reps, axis=1)
        earlier = (lax.broadcasted_iota(jnp.int32, (tk, tk), 0)
                   < lax.broadcasted_iota(jnp.int32, (tk, tk), 1))
        tri = jnp.where(earlier, 1.0, 0.0).astype(BF16)

        def demote(c, seen):
            off = pl.multiple_of(c * tk, tk)
            blk = sc_ref[0, :, pl.ds(off, tk)]
            eq = blk == th_w
            eqf = jnp.where(eq, 1.0, 0.0)
            rank = _dot(eqf.astype(BF16), tri) + jnp.concatenate([seen] * reps, axis=1)
            sc_ref[0, :, pl.ds(off, tk)] = jnp.where(jnp.logical_and(eq, rank >= keep_w), NEG, blk)
            return seen + jnp.sum(eqf, axis=-1, keepdims=True)

        lax.fori_loop(0, n_vis, demote, jnp.zeros((tq, LANES), F32))


def _indexer(qi_r, klo, khi, wi, tq, tk, k_sel):
    b, s, _ = qi_r.shape
    return pl.pallas_call(
        functools.partial(_indexer_kernel, tq=tq, tk=tk, seq=s, k_sel=float(k_sel)),
        name="indexer",
        grid=(b, s // tq),
        in_specs=[
            pl.BlockSpec((1, tq, IDX_HEADS * IDX_DIM), lambda bi, qi: (bi, qi, 0)),
            pl.BlockSpec((1, s, LANES), lambda bi, qi: (bi, 0, 0)),
            pl.BlockSpec((1, s, LANES), lambda bi, qi: (bi, 0, 0)),
            pl.BlockSpec((1, tq, LANES), lambda bi, qi: (bi, qi, 0)),
        ],
        out_specs=[
            pl.BlockSpec((1, tq, s), lambda bi, qi: (bi, qi, 0)),
            pl.BlockSpec((1, tq, LANES), lambda bi, qi: (bi, qi, 0)),
        ],
        out_shape=[jax.ShapeDtypeStruct((b, s, s), F32), jax.ShapeDtypeStruct((b, s, LANES), F32)],
        compiler_params=_cparams(("parallel", "parallel")),
    )(qi_r, klo, khi, wi)


def _causal_pairs(s, tq, tk):
    qs, ks = [], []
    for qi in range(s // tq):
        for ki in range((qi * tq + tq - 1) // tk + 1):
            qs.append(qi)
            ks.append(ki)
    return jnp.asarray(np.array(qs, np.int32)), jnp.asarray(np.array(ks, np.int32))


def _softmax_step(s, v, rows, m_ref, l_ref, acc_ref, fixed_shift, sums_in_matmul=False):
    blocks = [s[:, j * LANES:(j + 1) * LANES] for j in range(s.shape[1] // LANES)]
    dv = v.shape[1]
    if fixed_shift and sums_in_matmul:
        p = jnp.concatenate([jnp.exp2(blk).astype(BF16) for blk in blocks], axis=1)
        acc_ref[rows] = acc_ref[rows] + _dot(p, jnp.concatenate([v, jnp.ones_like(v)], axis=1))
        return
    if fixed_shift:
        ps = [jnp.exp2(blk) for blk in blocks]
        l_ref[rows] = l_ref[rows] + functools.reduce(jnp.add, ps)
        p = jnp.concatenate([blk.astype(BF16) for blk in ps], axis=1)
        acc_ref[rows, :dv] = acc_ref[rows, :dv] + _dot(p, v)
        return
    mx = functools.reduce(jnp.maximum, blocks)
    m_prev = m_ref[rows]
    m_new = jnp.maximum(m_prev, jnp.max(mx, axis=-1, keepdims=True))
    alpha = jnp.exp2(m_prev - m_new)
    ps = [jnp.exp2(blk - m_new) for blk in blocks]
    l_ref[rows] = alpha * l_ref[rows] + jnp.sum(functools.reduce(jnp.add, ps), axis=-1, keepdims=True)
    p = jnp.concatenate([blk.astype(BF16) for blk in ps], axis=1)
    acc_ref[rows, :dv] = alpha * acc_ref[rows, :dv] + _dot(p, v)
    m_ref[rows] = m_new


def _softmax_finish(rows, l_ref, acc_ref, dv, fixed_shift, sums_in_matmul=False):
    if fixed_shift and sums_in_matmul:
        acc = acc_ref[rows]
        return acc[:, :dv] / acc[:, dv:]
    l = l_ref[rows]
    if fixed_shift:
        l = jnp.sum(l, axis=-1, keepdims=True)
    return acc_ref[rows, :dv] / l


def _dsa_kernel(qt_ref, kt_ref, q_ref, k_ref, v_ref, sc_ref, th_ref, shift_ref, o_ref, m_ref, l_ref,
                acc_ref, bias_ref, qs_ref, *, tq, tk, fixed_shift):
    step = pl.program_id(1)
    qi = qt_ref[step]
    ki = kt_ref[step]

    @pl.when(ki == 0)
    def _():
        m_ref[...] = jnp.full(m_ref.shape, M_INIT, F32)
        l_ref[...] = jnp.zeros(l_ref.shape, F32)
        acc_ref[...] = jnp.zeros(acc_ref.shape, F32)
        for h in range(A_HEADS):
            qs_ref[h * tq:(h + 1) * tq, :] = q_ref[0, :, h * LANES:(h + 1) * LANES]

    bias_ref[...] = jnp.where(sc_ref[0] >= th_ref[0][:, :1], -shift_ref[:, :1], NEG)
    grp = A_HEADS // A_KV_HEADS
    for g in range(A_KV_HEADS):
        rows = slice(g * grp * tq, (g + 1) * grp * tq)
        k = k_ref[0, :, g * LANES:(g + 1) * LANES]
        v = v_ref[0, :, g * LANES:(g + 1) * LANES]
        s = _dot_nt(qs_ref[rows, :], k)
        s = jnp.concatenate([s[j * tq:(j + 1) * tq] + bias_ref[...] for j in range(grp)], axis=0)
        _softmax_step(s, v, rows, m_ref, l_ref, acc_ref, fixed_shift, sums_in_matmul=True)

    @pl.when(ki == (qi * tq + tq - 1) // tk)
    def _():
        for h in range(A_HEADS):
            out = _softmax_finish(slice(h * tq, (h + 1) * tq), l_ref, acc_ref, A_HEAD_DIM, fixed_shift,
                                  sums_in_matmul=True)
            o_ref[0, :, h * LANES:(h + 1) * LANES] = out.astype(o_ref.dtype)


def _dsa_attn(q, k, v, scores, th, shift, tq, tk, fixed_shift):
    b, s, dq = q.shape
    qt, kt = _causal_pairs(s, tq, tk)
    grid_spec = pltpu.PrefetchScalarGridSpec(
        num_scalar_prefetch=2,
        grid=(b, qt.shape[0]),
        in_specs=[
            pl.BlockSpec((1, tq, dq), lambda bi, p, qt, kt: (bi, qt[p], 0)),
            pl.BlockSpec((1, tk, k.shape[2]), lambda bi, p, qt, kt: (bi, kt[p], 0)),
            pl.BlockSpec((1, tk, v.shape[2]), lambda bi, p, qt, kt: (bi, kt[p], 0)),
            pl.BlockSpec((1, tq, tk), lambda bi, p, qt, kt: (bi, qt[p], kt[p])),
            pl.BlockSpec((1, tq, LANES), lambda bi, p, qt, kt: (bi, qt[p], 0)),
            pl.BlockSpec((1, LANES), lambda bi, p, qt, kt: (0, 0)),
        ],
        out_specs=pl.BlockSpec((1, tq, dq), lambda bi, p, qt, kt: (bi, qt[p], 0)),
        scratch_shapes=[
            pltpu.VMEM((A_HEADS * tq, LANES), F32),
            pltpu.VMEM((A_HEADS * tq, LANES), F32),
            pltpu.VMEM((A_HEADS * tq, 2 * A_HEAD_DIM), F32),
            pltpu.VMEM((tq, tk), F32),
            pltpu.VMEM((A_HEADS * tq, A_HEAD_DIM), BF16),
        ],
    )
    return pl.pallas_call(
        functools.partial(_dsa_kernel, tq=tq, tk=tk, fixed_shift=fixed_shift),
        name="dsa_attn_fixed" if fixed_shift else "dsa_attn_online",
        grid_spec=grid_spec,
        out_shape=jax.ShapeDtypeStruct((b, s, dq), BF16),
        compiler_params=_cparams(("parallel", "arbitrary")),
    )(qt, kt, q, k, v, scores, th, shift)


def _mla_kernel(qt_ref, kt_ref, q_ref, k_ref, v_ref, o_ref, m_ref, l_ref, acc_ref, *, tq, tk, hg,
                fixed_shift):
    step = pl.program_id(2)
    qi = qt_ref[step]
    ki = kt_ref[step]

    @pl.when(ki == 0)
    def _():
        m_ref[...] = jnp.full(m_ref.shape, M_INIT, F32)
        l_ref[...] = jnp.zeros(l_ref.shape, F32)
        acc_ref[...] = jnp.zeros(acc_ref.shape, F32)

    def heads(masked):
        if masked:
            keep = (lax.broadcasted_iota(jnp.int32, (tq, tk), 1)
                    <= lax.broadcasted_iota(jnp.int32, (tq, tk), 0))
        for h in range(hg):
            q = q_ref[0, :, h * 256:(h + 1) * 256]
            k = k_ref[0, :, h * 256:(h + 1) * 256]
            v = v_ref[0, :, h * B_V:(h + 1) * B_V]
            s = _dot_nt(q, k)
            if masked:
                s = jnp.where(keep, s, NEG)
            _softmax_step(s, v, slice(h * tq, (h + 1) * tq), m_ref, l_ref, acc_ref, fixed_shift)

    @pl.when(ki < qi)
    def _():
        heads(False)

    @pl.when(ki == qi)
    def _():
        heads(True)
        for h in range(hg):
            out = _softmax_finish(slice(h * tq, (h + 1) * tq), l_ref, acc_ref, B_V, fixed_shift)
            o_ref[0, :, h * B_V:(h + 1) * B_V] = out.astype(o_ref.dtype)


def _mla_attn(q, k, v, tq, tk, hg, fixed_shift):
    b, s, _ = q.shape
    assert tq == tk
    qt, kt = _causal_pairs(s, tq, tk)
    grid_spec = pltpu.PrefetchScalarGridSpec(
        num_scalar_prefetch=2,
        grid=(b, B_HEADS // hg, qt.shape[0]),
        in_specs=[
            pl.BlockSpec((1, tq, hg * 256), lambda bi, g, p, qt, kt: (bi, qt[p], g)),
            pl.BlockSpec((1, tk, hg * 256), lambda bi, g, p, qt, kt: (bi, kt[p], g)),
            pl.BlockSpec((1, tk, hg * B_V), lambda bi, g, p, qt, kt: (bi, kt[p], g)),
        ],
        out_specs=pl.BlockSpec((1, tq, hg * B_V), lambda bi, g, p, qt, kt: (bi, qt[p], g)),
        scratch_shapes=[
            pltpu.VMEM((hg * tq, LANES), F32),
            pltpu.VMEM((hg * tq, LANES), F32),
            pltpu.VMEM((hg * tq, B_V), F32),
        ],
    )
    return pl.pallas_call(
        functools.partial(_mla_kernel, tq=tq, tk=tk, hg=hg, fixed_shift=fixed_shift),
        name="mla_attn_fixed" if fixed_shift else "mla_attn_online",
        grid_spec=grid_spec,
        out_shape=jax.ShapeDtypeStruct((b, s, B_HEADS * B_V), BF16),
        compiler_params=_cparams(("parallel", "parallel", "arbitrary")),
    )(qt, kt, q, k, v)


def _regroup_w_in(w):
    sizes = (A_HEADS * A_HEAD_DIM, A_KV_HEADS * A_HEAD_DIM, A_KV_HEADS * A_HEAD_DIM,
             IDX_HEADS * IDX_DIM, IDX_DIM, IDX_HEADS, B_Q_LORA, B_KV_LORA, B_ROPE)
    offs = np.cumsum((0,) + sizes)
    w = w.astype(BF16)
    qa, ka, va, qi, ki, wi, cq, ckv, kr = (w[:, offs[j]:offs[j + 1]] for j in range(len(sizes)))
    gates = w[:, offs[-1]:]
    d = w.shape[0]
    kiw = jnp.concatenate([ki, wi, jnp.zeros((d, LANES - IDX_DIM - IDX_HEADS), w.dtype)], axis=1)
    krp = jnp.concatenate([kr, jnp.zeros((d, LANES - B_ROPE), w.dtype)], axis=1)
    return jnp.concatenate([qa, gates, qi, ka, va, cq, ckv, kiw, krp], axis=1).astype(BF16)


def _softmax_shift(d, g_q, g_k):
    bound = (d ** 0.5) * LOG2E * 1.02 * jnp.max(jnp.abs(g_q)) * jnp.max(jnp.abs(g_k))
    c = jnp.ceil(bound.astype(F32))
    ok = c <= MAX_FIXED_SHIFT
    return jnp.full((1, LANES), jnp.where(ok, c, 0.0), F32), ok


def _rope_tables(pos, d):
    half = d // 2
    freqs = jnp.power(jnp.float32(ROPE_THETA), -jnp.arange(half, dtype=F32) * (2.0 / d))
    ang = pos.astype(F32)[..., None] * freqs
    cos, sin = jnp.cos(ang), jnp.sin(ang)
    reps = LANES // d
    c = jnp.tile(jnp.concatenate([cos, cos], axis=-1), (1, 1, reps))
    s = jnp.tile(jnp.concatenate([-sin, sin], axis=-1), (1, 1, reps))
    return c.reshape(-1, LANES), s.reshape(-1, LANES)


def _layer(x, p, positions, g_mix_norm, w_in, g_qa, g_ka, g_cq, w_uq, g_ckv, w_ukv, g_qb, g_kb,
           w_out_a, w_out_b, w_o, g_ffn_norm, w_ffn_gate, w_ffn_up, w_ffn_down, g_ple_norm,
           w_ple_gate, w_ple_proj):
    b, s, d = x.shape
    m = b * s
    k_sel = min(TOPK_MAX, s // 4)
    xf = x.reshape(m, d)
    row = lambda g: g.reshape(1, -1).astype(F32)

    c128, s128 = _rope_tables(positions, A_HEAD_DIM)
    c64, s64 = _rope_tables(positions, IDX_DIM)

    proj = _norm_matmul("in_proj", xf, 0, d, row(g_mix_norm), _regroup_w_in(w_in), F32,
                        tm=TILE_IN_PROJ[0], tn=TILE_IN_PROJ[1])

    qa_r, ka_r, va_b, qi_r, klo, khi, wi = _prep_a(
        proj, c128, s128, c64, s64, row(g_qa) * (A_HEAD_DIM ** -0.5 * LOG2E), row(g_ka), tp=TILE_PREP_ROWS)
    r3 = lambda a: a.reshape(b, s, a.shape[-1])
    scores, th = _indexer(r3(qi_r), r3(klo), r3(khi), r3(wi), tq=TILE_ATTN, tk=TILE_ATTN, k_sel=k_sel)
    shift_a, fixed_a = _softmax_shift(A_HEAD_DIM, g_qa, g_ka)
    y_a = lax.cond(
        fixed_a,
        functools.partial(_dsa_attn, tq=TILE_ATTN, tk=TILE_ATTN, fixed_shift=True),
        functools.partial(_dsa_attn, tq=TILE_ATTN, tk=TILE_ATTN, fixed_shift=False),
        r3(qa_r), r3(ka_r), r3(va_b), scores, th, shift_a)

    w_uq_r = w_uq.reshape(B_Q_LORA, B_HEADS, B_QK)
    w_uq_r = jnp.concatenate([w_uq_r[:, :, :B_NOPE].reshape(B_Q_LORA, -1),
                              w_uq_r[:, :, B_NOPE:].reshape(B_Q_LORA, -1)], axis=1).astype(BF16)
    w_ukv_r = w_ukv.reshape(B_KV_LORA, B_HEADS, 2, B_NOPE)
    w_ukv_r = jnp.concatenate([w_ukv_r[:, :, 0, :].reshape(B_KV_LORA, -1),
                               w_ukv_r[:, :, 1, :].reshape(B_KV_LORA, -1)], axis=1).astype(BF16)
    pe2 = lambda g: jnp.tile(g[B_NOPE:], 2).reshape(1, LANES).astype(F32)
    shift_b, fixed_b = _softmax_shift(B_QK, g_qb, g_kb)
    q_m = _mla_q(proj, row(g_cq), w_uq_r, c64, s64, row(g_qb[:B_NOPE]), pe2(g_qb), tm=TILE_MLA_UP_ROWS)
    k_m, v_m = _mla_kv(proj, row(g_ckv), w_ukv_r, c64, s64, row(g_kb[:B_NOPE]), pe2(g_kb), shift_b,
                       tm=TILE_MLA_UP_ROWS)
    y_b = lax.cond(
        fixed_b,
        functools.partial(_mla_attn, tq=TILE_ATTN, tk=TILE_ATTN, hg=B_HEADS, fixed_shift=True),
        functools.partial(_mla_attn, tq=TILE_ATTN, tk=TILE_ATTN, hg=B_HEADS, fixed_shift=False),
        r3(q_m), r3(k_m), r3(v_m))

    mixed = _merge(y_a.reshape(m, -1), y_b.reshape(m, -1), w_out_a.astype(BF16), w_out_b.astype(BF16),
                   proj, tm=TILE_MERGE[0], tn=TILE_MERGE[1])
    x1 = _matmul_res("w_o", mixed, w_o.astype(BF16), xf, tm=TILE_W_O[0], tn=TILE_W_O[1])

    act = _ffn_up(x1, row(g_ffn_norm), w_ffn_gate.astype(BF16), w_ffn_up.astype(BF16),
                  tm=TILE_FFN_UP[0], tn=TILE_FFN_UP[1])
    x2 = _matmul_res("ffn_down", act, w_ffn_down.astype(BF16), x1,
                     tm=TILE_FFN_DOWN[0], tn=TILE_FFN_DOWN[1])

    out = _ple(x2, row(g_ple_norm), w_ple_gate.astype(BF16), p.reshape(m, -1),
               w_ple_proj.astype(BF16), tm=TILE_PLE_ROWS)
    return out.reshape(b, s, d)


def kernel(x, p, positions, g_mix_norm, w_in, g_qa, g_ka, g_cq, w_uq, g_ckv, w_ukv, g_qb, g_kb,
           w_out_a, w_out_b, w_o, g_ffn_norm, w_ffn_gate, w_ffn_up, w_ffn_down, g_ple_norm,
           w_ple_gate, w_ple_proj):
    for i in range(p.shape[0]):
        x = _layer(x, p[i], positions, g_mix_norm[i], w_in[i], g_qa[i], g_ka[i], g_cq[i], w_uq[i],
                   g_ckv[i], w_ukv[i], g_qb[i], g_kb[i], w_out_a[i], w_out_b[i], w_o[i],
                   g_ffn_norm[i], w_ffn_gate[i], w_ffn_up[i], w_ffn_down[i], g_ple_norm[i],
                   w_ple_gate[i], w_ple_proj[i])
    return x
```

```python
import functools

import numpy as np
import jax
import jax.numpy as jnp
from jax import lax
from jax.experimental import pallas as pl
from jax.experimental.pallas import tpu as pltpu

F32 = jnp.float32
BF16 = jnp.bfloat16

ROPE_THETA = 10000.0
NORM_EPS = 1e-6
NEG = -1e30
M_INIT = -1e29
POS_BIG = 3e38
TH_ALL = 0.5 * NEG
LOG2E = 1.4426950408889634
MAX_SEARCH_STEPS = 512
CNT_ROWS = 128
MAX_FIXED_SHIFT = 60.0
A_HEADS = 16
A_KV_HEADS = 4
A_HEAD_DIM = 128
IDX_HEADS = 16
IDX_DIM = 64
TOPK_MAX = 256
B_HEADS = 16
B_Q_LORA = 512
B_KV_LORA = 256
B_NOPE = 128
B_ROPE = 64
B_V = 128
B_QK = B_NOPE + B_ROPE
LANES = 128
VMEM_LIMIT = 56 * 1024 * 1024

TILE_IN_PROJ = (1024, 1536)
TILE_PREP_ROWS = 512
TILE_ATTN = 512
TILE_MLA_UP_ROWS = 512
TILE_MERGE = (1024, 512)
TILE_W_O = (1024, 1024)
TILE_FFN_UP = (1024, 512)
TILE_FFN_DOWN = (512, 1024)
TILE_PLE_ROWS = 512

COL_QA = 0
COL_GATE_A = 2048
COL_GATE_B = 4096
COL_QI = 6144
COL_KA = 7168
COL_VA = 7680
COL_CQ = 8192
COL_CKV = 8704
COL_KIW = 8960
COL_KR = 9088
N_PROJ = 9216


def _dot(a, b):
    return jnp.dot(a, b, preferred_element_type=F32)


def _dot_nt(a, b):
    return lax.dot_general(a, b, (((1,), (1,)), ((), ())), preferred_element_type=F32)


def _cparams(sem):
    return pltpu.CompilerParams(dimension_semantics=sem, vmem_limit_bytes=VMEM_LIMIT)


def _rms_to_bf16(x, g):
    ms = jnp.mean(x * x, axis=-1, keepdims=True)
    return (x * lax.rsqrt(ms + NORM_EPS) * g).astype(BF16)


def _norm_matmul_kernel(x_ref, g_ref, w_ref, o_ref, hn_ref):
    @pl.when(pl.program_id(1) == 0)
    def _():
        hn_ref[...] = _rms_to_bf16(x_ref[...], g_ref[...])

    o_ref[...] = _dot(hn_ref[...], w_ref[...]).astype(o_ref.dtype)


def _norm_matmul(name, x, x_col_block, k, g, w, out_dtype, tm, tn):
    m = x.shape[0]
    n = w.shape[1]
    return pl.pallas_call(
        _norm_matmul_kernel,
        name=name,
        grid=(m // tm, n // tn),
        in_specs=[
            pl.BlockSpec((tm, k), lambda i, j: (i, x_col_block)),
            pl.BlockSpec((1, k), lambda i, j: (0, 0)),
            pl.BlockSpec((k, tn), lambda i, j: (0, j)),
        ],
        out_specs=pl.BlockSpec((tm, tn), lambda i, j: (i, j)),
        out_shape=jax.ShapeDtypeStruct((m, n), out_dtype),
        scratch_shapes=[pltpu.VMEM((tm, k), BF16)],
        compiler_params=_cparams(("parallel", "arbitrary")),
    )(x, g, w)


def _ffn_up_kernel(x_ref, g_ref, wg_ref, wu_ref, o_ref, hn_ref):
    @pl.when(pl.program_id(1) == 0)
    def _():
        hn_ref[...] = _rms_to_bf16(x_ref[...], g_ref[...])

    h = hn_ref[...]
    a = _dot(h, wg_ref[...])
    b = _dot(h, wu_ref[...])
    o_ref[...] = (a * jax.nn.sigmoid(a) * b).astype(o_ref.dtype)


def _ffn_up(x, g, wg, wu, tm, tn):
    m, k = x.shape
    n = wg.shape[1]
    return pl.pallas_call(
        _ffn_up_kernel,
        name="ffn_up",
        grid=(m // tm, n // tn),
        in_specs=[
            pl.BlockSpec((tm, k), lambda i, j: (i, 0)),
            pl.BlockSpec((1, k), lambda i, j: (0, 0)),
            pl.BlockSpec((k, tn), lambda i, j: (0, j)),
            pl.BlockSpec((k, tn), lambda i, j: (0, j)),
        ],
        out_specs=pl.BlockSpec((tm, tn), lambda i, j: (i, j)),
        out_shape=jax.ShapeDtypeStruct((m, n), BF16),
        scratch_shapes=[pltpu.VMEM((tm, k), BF16)],
        compiler_params=_cparams(("parallel", "arbitrary")),
    )(x, g, wg, wu)


def _matmul_res_kernel(a_ref, w_ref, r_ref, o_ref):
    o_ref[...] = r_ref[...] + _dot(a_ref[...], w_ref[...])


def _matmul_res(name, a, w, res, tm, tn):
    m, k = a.shape
    n = w.shape[1]
    return pl.pallas_call(
        _matmul_res_kernel,
        name=name,
        grid=(n // tn, m // tm),
        in_specs=[
            pl.BlockSpec((tm, k), lambda j, i: (i, 0)),
            pl.BlockSpec((k, tn), lambda j, i: (0, j)),
            pl.BlockSpec((tm, tn), lambda j, i: (i, j)),
        ],
        out_specs=pl.BlockSpec((tm, tn), lambda j, i: (i, j)),
        out_shape=jax.ShapeDtypeStruct((m, n), F32),
        compiler_params=_cparams(("parallel", "parallel")),
    )(a, w, res)


def _merge_kernel(ya_ref, yb_ref, wa_ref, wb_ref, ga_ref, gb_ref, o_ref):
    a = _dot(ya_ref[...], wa_ref[...])
    b = _dot(yb_ref[...], wb_ref[...])
    o_ref[...] = (jax.nn.sigmoid(ga_ref[...]) * a + jax.nn.sigmoid(gb_ref[...]) * b).astype(o_ref.dtype)


def _merge(ya, yb, wa, wb, proj, tm, tn):
    m, k = ya.shape
    n = wa.shape[1]
    ga0 = COL_GATE_A // tn
    gb0 = COL_GATE_B // tn
    return pl.pallas_call(
        _merge_kernel,
        name="merge",
        grid=(m // tm, n // tn),
        in_specs=[
            pl.BlockSpec((tm, k), lambda i, j: (i, 0)),
            pl.BlockSpec((tm, k), lambda i, j: (i, 0)),
            pl.BlockSpec((k, tn), lambda i, j: (0, j)),
            pl.BlockSpec((k, tn), lambda i, j: (0, j)),
            pl.BlockSpec((tm, tn), lambda i, j: (i, ga0 + j)),
            pl.BlockSpec((tm, tn), lambda i, j: (i, gb0 + j)),
        ],
        out_specs=pl.BlockSpec((tm, tn), lambda i, j: (i, j)),
        out_shape=jax.ShapeDtypeStruct((m, n), BF16),
        compiler_params=_cparams(("parallel", "parallel")),
    )(ya, yb, wa, wb, proj, proj)


def _ple_kernel(x_ref, g_ref, wg_ref, p_ref, wp_ref, o_ref):
    x = x_ref[...]
    gate = jax.nn.sigmoid(_dot(_rms_to_bf16(x, g_ref[...]), wg_ref[...]))
    emb = _dot(p_ref[...].astype(BF16), wp_ref[...])
    o_ref[...] = x + gate * emb


def _ple(x, g, wg, p, wp, tm):
    m, k = x.shape
    n = wg.shape[1]
    kp = p.shape[1]
    assert n == k
    return pl.pallas_call(
        _ple_kernel,
        name="ple",
        grid=(m // tm,),
        in_specs=[
            pl.BlockSpec((tm, k), lambda i: (i, 0)),
            pl.BlockSpec((1, k), lambda i: (0, 0)),
            pl.BlockSpec((k, n), lambda i: (0, 0)),
            pl.BlockSpec((tm, kp), lambda i: (i, 0)),
            pl.BlockSpec((kp, n), lambda i: (0, 0)),
        ],
        out_specs=pl.BlockSpec((tm, n), lambda i: (i, 0)),
        out_shape=jax.ShapeDtypeStruct((m, n), F32),
        compiler_params=_cparams(("parallel",)),
    )(x, g, wg, p, wp)


def _rope64_pairs(x, c64, s64, first_half):
    partner = jnp.where(first_half, pltpu.roll(x, 96, 1), pltpu.roll(x, 32, 1))
    return x * c64 + partner * s64


def _prep_a_kernel(qa_ref, ka_ref, va_ref, qi_ref, kiw_ref, c128_ref, s128_ref, c64_ref, s64_ref,
                   gq_ref, gk_ref, qa_o, ka_o, va_o, qi_o, klo_o, khi_o, wi_o, *, wi_scale):
    c128 = c128_ref[...]
    s128 = s128_ref[...]

    ones = jnp.ones((LANES, LANES), BF16)

    def norm_rope(x, g):
        sq = x * x
        hi = sq.astype(BF16)
        lo = (sq - hi.astype(F32)).astype(BF16)
        ms = (_dot(hi, ones) + _dot(lo, ones)) * (1.0 / LANES)
        y = x * lax.rsqrt(ms + NORM_EPS) * g
        return y * c128 + pltpu.roll(y, 64, 1) * s128

    for h in range(A_HEADS):
        hs = slice(h * LANES, (h + 1) * LANES)
        qa_o[:, hs] = norm_rope(qa_ref[:, hs], gq_ref[...]).astype(BF16)
    for h in range(A_KV_HEADS):
        hs = slice(h * LANES, (h + 1) * LANES)
        ka_o[:, hs] = norm_rope(ka_ref[:, hs], gk_ref[...]).astype(BF16)
    va_o[...] = va_ref[...].astype(BF16)

    c64 = c64_ref[...]
    s64 = s64_ref[...]
    lane = lax.broadcasted_iota(jnp.int32, c64.shape, 1)
    first_half = (lane % 64) < 32
    for p in range(IDX_HEADS * IDX_DIM // LANES):
        ps = slice(p * LANES, (p + 1) * LANES)
        qi_o[:, ps] = _rope64_pairs(qi_ref[:, ps], c64, s64, first_half).astype(BF16)
    kiw = kiw_ref[...]
    ki = jnp.where(lane < 64, _rope64_pairs(kiw, c64, s64, first_half), 0.0)
    klo_o[...] = ki.astype(BF16)
    khi_o[...] = pltpu.roll(ki, 64, 1).astype(BF16)
    wi_o[...] = kiw * wi_scale


def _prep_a(proj, c128, s128, c64, s64, gq, gk, tp):
    m = proj.shape[0]
    row = lambda w, cb: pl.BlockSpec((tp, w), lambda i: (i, cb))
    vec = pl.BlockSpec((1, LANES), lambda i: (0, 0))
    out = lambda w: pl.BlockSpec((tp, w), lambda i: (i, 0))
    wi_scale = (IDX_HEADS ** -0.5) * (IDX_DIM ** -0.5)
    return pl.pallas_call(
        functools.partial(_prep_a_kernel, wi_scale=wi_scale),
        name="prep_a",
        grid=(m // tp,),
        in_specs=[
            row(2048, COL_QA // 2048), row(512, COL_KA // 512), row(512, COL_VA // 512),
            row(1024, COL_QI // 1024), row(LANES, COL_KIW // LANES),
            out(LANES), out(LANES), out(LANES), out(LANES), vec, vec,
        ],
        out_specs=[out(2048), out(512), out(512), out(1024), out(LANES), out(LANES), out(LANES)],
        out_shape=[
            jax.ShapeDtypeStruct((m, 2048), BF16), jax.ShapeDtypeStruct((m, 512), BF16),
            jax.ShapeDtypeStruct((m, 512), BF16), jax.ShapeDtypeStruct((m, 1024), BF16),
            jax.ShapeDtypeStruct((m, LANES), BF16), jax.ShapeDtypeStruct((m, LANES), BF16),
            jax.ShapeDtypeStruct((m, LANES), F32),
        ],
        compiler_params=_cparams(("parallel",)),
    )(proj, proj, proj, proj, proj, c128, s128, c64, s64, gq, gk)


def _mla_q_kernel(cq_ref, g_ref, w_ref, c64_ref, s64_ref, gqn_ref, gqp_ref, q_o, *, q_scale):
    qb = _dot(_rms_to_bf16(cq_ref[...], g_ref[...]), w_ref[...])
    c64 = c64_ref[...]
    s64 = s64_ref[...]
    lane = lax.broadcasted_iota(jnp.int32, c64.shape, 1)
    first_half = (lane % 64) < 32
    lo = lane < 64
    nope_w = B_HEADS * B_NOPE
    for pair in range(B_HEADS // 2):
        qp2 = qb[:, nope_w + pair * LANES:nope_w + (pair + 1) * LANES]
        qp2_sq = qp2 * qp2
        rot2 = _rope64_pairs(qp2 * gqp_ref[...], c64, s64, first_half)
        for h in (2 * pair, 2 * pair + 1):
            mine = lo if h % 2 == 0 else jnp.logical_not(lo)
            shift_lane = lane == (64 if h % 2 == 0 else 0)
            qn = qb[:, h * LANES:(h + 1) * LANES]
            ss = jnp.sum(qn * qn + jnp.where(mine, qp2_sq, 0.0), axis=-1, keepdims=True)
            r = lax.rsqrt(ss * (1.0 / B_QK) + NORM_EPS) * q_scale
            q_o[:, 2 * h * LANES:(2 * h + 1) * LANES] = (qn * r * gqn_ref[...]).astype(BF16)
            qpe = jnp.where(mine, rot2 * r, jnp.where(shift_lane, 1.0, 0.0))
            q_o[:, (2 * h + 1) * LANES:(2 * h + 2) * LANES] = qpe.astype(BF16)


def _mla_kv_kernel(ckv_ref, g_ref, w_ref, kr_ref, c64_ref, s64_ref, gkn_ref, gkp_ref, shift_ref, k_o, v_o):
    kv = _dot(_rms_to_bf16(ckv_ref[...], g_ref[...]), w_ref[...])
    c64 = c64_ref[...]
    s64 = s64_ref[...]
    lane = lax.broadcasted_iota(jnp.int32, c64.shape, 1)
    first_half = (lane % 64) < 32
    lo = lane < 64
    kr = jnp.where(lo, kr_ref[...], 0.0)
    kr_ss = jnp.sum(kr * kr, axis=-1, keepdims=True)
    kpe_lo = jnp.where(lo, _rope64_pairs(kr * gkp_ref[...], c64, s64, first_half), 0.0)
    kpe_hi = pltpu.roll(kpe_lo, 64, 1)
    nope_w = B_HEADS * B_NOPE
    for h in range(B_HEADS):
        hs = slice(h * LANES, (h + 1) * LANES)
        shift_lane = lane == (64 if h % 2 == 0 else 0)
        kn = kv[:, hs]
        rk = lax.rsqrt((jnp.sum(kn * kn, axis=-1, keepdims=True) + kr_ss) * (1.0 / B_QK) + NORM_EPS)
        k_o[:, 2 * h * LANES:(2 * h + 1) * LANES] = (kn * rk * gkn_ref[...]).astype(BF16)
        kpe = kpe_lo if h % 2 == 0 else kpe_hi
        k_o[:, (2 * h + 1) * LANES:(2 * h + 2) * LANES] = jnp.where(
            shift_lane, -shift_ref[...], kpe * rk).astype(BF16)
        v_o[:, hs] = kv[:, nope_w + h * LANES:nope_w + (h + 1) * LANES].astype(BF16)


def _mla_q(proj, g_cq, w_uq, c64, s64, gqn, gqp, tm):
    m = proj.shape[0]
    vec = lambda w: pl.BlockSpec((1, w), lambda i: (0, 0))
    rows = lambda w: pl.BlockSpec((tm, w), lambda i: (i, 0))
    return pl.pallas_call(
        functools.partial(_mla_q_kernel, q_scale=B_QK ** -0.5 * LOG2E),
        name="mla_q",
        grid=(m // tm,),
        in_specs=[
            pl.BlockSpec((tm, B_Q_LORA), lambda i: (i, COL_CQ // B_Q_LORA)),
            vec(B_Q_LORA),
            pl.BlockSpec(w_uq.shape, lambda i: (0, 0)),
            rows(LANES), rows(LANES), vec(LANES), vec(LANES),
        ],
        out_specs=rows(B_HEADS * 256),
        out_shape=jax.ShapeDtypeStruct((m, B_HEADS * 256), BF16),
        compiler_params=_cparams(("parallel",)),
    )(proj, g_cq, w_uq, c64, s64, gqn, gqp)


def _mla_kv(proj, g_ckv, w_ukv, c64, s64, gkn, gkp, shift, tm):
    m = proj.shape[0]
    vec = lambda w: pl.BlockSpec((1, w), lambda i: (0, 0))
    rows = lambda w: pl.BlockSpec((tm, w), lambda i: (i, 0))
    return pl.pallas_call(
        _mla_kv_kernel,
        name="mla_kv",
        grid=(m // tm,),
        in_specs=[
            pl.BlockSpec((tm, B_KV_LORA), lambda i: (i, COL_CKV // B_KV_LORA)),
            vec(B_KV_LORA),
            pl.BlockSpec(w_ukv.shape, lambda i: (0, 0)),
            pl.BlockSpec((tm, LANES), lambda i: (i, COL_KR // LANES)),
            rows(LANES), rows(LANES), vec(LANES), vec(LANES), vec(LANES),
        ],
        out_specs=[rows(B_HEADS * 256), rows(B_HEADS * B_V)],
        out_shape=[
            jax.ShapeDtypeStruct((m, B_HEADS * 256), BF16),
            jax.ShapeDtypeStruct((m, B_HEADS * B_V), BF16),
        ],
        compiler_params=_cparams(("parallel",)),
    )(proj, g_ckv, w_ukv, proj, c64, s64, gkn, gkp, shift)


def _indexer_kernel(q_ref, klo_ref, khi_ref, w_ref, sc_ref, th_ref, *, tq, tk, seq, k_sel):
    qi = pl.program_id(1)
    n_vis = (qi * tq) // tk + 1
    n_all = seq // tk
    row = qi * tq + lax.broadcasted_iota(jnp.int32, (tq, 1), 0)
    wcols = [w_ref[0, :, IDX_DIM + h:IDX_DIM + h + 1] for h in range(IDX_HEADS)]

    def gen(c, carry):
        rmax, rmin = carry
        off = pl.multiple_of(c * tk, tk)
        kl = klo_ref[0, pl.ds(off, tk), :]
        kh = khi_ref[0, pl.ds(off, tk), :]
        acc = jnp.zeros((tq, tk), F32)
        for p in range(IDX_HEADS // 2):
            qp = q_ref[0, :, p * LANES:(p + 1) * LANES]
            acc = acc + wcols[2 * p] * jnp.maximum(_dot_nt(qp, kl), 0.0)
            acc = acc + wcols[2 * p + 1] * jnp.maximum(_dot_nt(qp, kh), 0.0)
        col = off + lax.broadcasted_iota(jnp.int32, (tq, tk), 1)
        vis = col <= row
        sc_ref[0, :, pl.ds(off, tk)] = jnp.where(vis, acc, NEG)
        rmax = jnp.maximum(rmax, jnp.max(jnp.where(vis, acc, NEG), axis=-1, keepdims=True))
        rmin = jnp.minimum(rmin, jnp.min(jnp.where(vis, acc, POS_BIG), axis=-1, keepdims=True))
        return rmax, rmin

    rmax, rmin = lax.fori_loop(0, n_vis, gen,
                               (jnp.full((tq, 1), NEG, F32), jnp.full((tq, 1), POS_BIG, F32)))

    def fill(c, carry):
        sc_ref[0, :, pl.ds(pl.multiple_of(c * tk, tk), tk)] = jnp.full((tq, tk), NEG, F32)
        return carry

    lax.fori_loop(n_vis, n_all, fill, 0)

    cw = min(2 * tk, seq)
    n_cnt = (n_vis * tk + cw - 1) // cw

    def count_ge(mid, strict=False, open_groups=None):
        zeros = jnp.zeros((CNT_ROWS, LANES), F32)
        accs = []
        for r in range(tq // CNT_ROWS):
            rs = slice(r * CNT_ROWS, (r + 1) * CNT_ROWS)
            midb = mid[rs]

            def body(c, acc, rs=rs, midb=midb):
                slab = sc_ref[0, rs, pl.ds(pl.multiple_of(c * cw, cw), cw)]
                for j in range(cw // LANES):
                    blk = slab[:, j * LANES:(j + 1) * LANES]
                    acc = acc + jnp.where(blk > midb if strict else blk >= midb, 1.0, 0.0)
                return acc

            if open_groups is None:
                accs.append(lax.fori_loop(0, n_cnt, body, zeros))
            else:
                accs.append(lax.cond(open_groups[r] > 0.0,
                                     functools.partial(lax.fori_loop, 0, n_cnt, body, zeros),
                                     lambda: zeros))
        return jnp.concatenate(
            [jnp.broadcast_to(jnp.sum(a, axis=-1, keepdims=True), (CNT_ROWS, LANES)) for a in accs], axis=0)

    rowb = qi * tq + lax.broadcasted_iota(jnp.int32, (tq, LANES), 0)
    active0 = (rowb + 1) > k_sel
    lo0 = jnp.where(active0, jnp.broadcast_to(rmin, (tq, LANES)), TH_ALL)
    hi0 = jnp.where(active0, jnp.broadcast_to(rmax, (tq, LANES)), TH_ALL)

    def n_open(lo, hi):
        still = jnp.where(lo < hi, 1.0, 0.0)
        return tuple(jnp.sum(still[r * CNT_ROWS:(r + 1) * CNT_ROWS]) for r in range(tq // CNT_ROWS))

    def cond(st):
        it, nopen = st[0], st[1]
        return jnp.logical_and(functools.reduce(jnp.add, nopen) > 0.0, it < MAX_SEARCH_STEPS)

    def body(st):
        it, was_open, lo, hi, cnt_lo = st
        mid = lo + (hi - lo) * 0.5
        cnt = count_ge(mid, open_groups=was_open)
        nopen = n_open(lo, hi)
        ge = cnt >= k_sel
        lo_n = jnp.where(ge, mid, lo)
        hi_n = jnp.where(ge, hi, mid)
        hi_n = jnp.where(cnt == k_sel, lo_n, hi_n)
        hi_n = jnp.where(mid >= hi, lo_n, hi_n)
        hi_n = jnp.where(mid <= lo, lo_n, hi_n)
        return it + 1, nopen, lo_n, hi_n, jnp.where(ge, cnt, cnt_lo)

    st = lax.while_loop(cond, body, (jnp.int32(0), n_open(lo0, hi0), lo0, hi0,
                                     (rowb + 1).astype(F32)))
    th = st[2]
    th_ref[0] = th

    tied = jnp.logical_and(active0, st[4] > k_sel)

    @pl.when(jnp.sum(jnp.where(tied, 1.0, 0.0)) > 0.0)
    def _():
        keep_ties = k_sel - count_ge(th, strict=True)
        reps = tk // LANES
        th_w = jnp.concatenate([th] * reps, axis=1)
        keep_w = jnp.concatenate([keep_ties] * reps, axis=1)
        earlier = (lax.broadcasted_iota(jnp.int32, (tk, tk), 0)
                   < lax.broadcasted_iota(jnp.int32, (tk, tk), 1))
        tri = jnp.where(earlier, 1.0, 0.0).astype(BF16)

        def demote(c, seen):
            off = pl.multiple_of(c * tk, tk)
            blk = sc_ref[0, :, pl.ds(off, tk)]
            eq = blk == th_w
            eqf = jnp.where(eq, 1.0, 0.0)
            rank = _dot(eqf.astype(BF16), tri) + jnp.concatenate([seen] * reps, axis=1)
            sc_ref[0, :, pl.ds(off, tk)] = jnp.where(jnp.logical_and(eq, rank >= keep_w), NEG, blk)
            return seen + jnp.sum(eqf, axis=-1, keepdims=True)

        lax.fori_loop(0, n_vis, demote, jnp.zeros((tq, LANES), F32))


def _indexer(qi_r, klo, khi, wi, tq, tk, k_sel):
    b, s, _ = qi_r.shape
    return pl.pallas_call(
        functools.partial(_indexer_kernel, tq=tq, tk=tk, seq=s, k_sel=float(k_sel)),
        name="indexer",
        grid=(b, s // tq),
        in_specs=[
            pl.BlockSpec((1, tq, IDX_HEADS * IDX_DIM), lambda bi, qi: (bi, qi, 0)),
            pl.BlockSpec((1, s, LANES), lambda bi, qi: (bi, 0, 0)),
            pl.BlockSpec((1, s, LANES), lambda bi, qi: (bi, 0, 0)),
            pl.BlockSpec((1, tq, LANES), lambda bi, qi: (bi, qi, 0)),
        ],
        out_specs=[
            pl.BlockSpec((1, tq, s), lambda bi, qi: (bi, qi, 0)),
            pl.BlockSpec((1, tq, LANES), lambda bi, qi: (bi, qi, 0)),
        ],
        out_shape=[jax.ShapeDtypeStruct((b, s, s), F32), jax.ShapeDtypeStruct((b, s, LANES), F32)],
        compiler_params=_cparams(("parallel", "parallel")),
    )(qi_r, klo, khi, wi)


def _causal_pairs(s, tq, tk):
    qs, ks = [], []
    for qi in range(s // tq):
        for ki in range((qi * tq + tq - 1) // tk + 1):
            qs.append(qi)
            ks.append(ki)
    return jnp.asarray(np.array(qs, np.int32)), jnp.asarray(np.array(ks, np.int32))


def _softmax_step(s, v, rows, m_ref, l_ref, acc_ref, fixed_shift, sums_in_matmul=False):
    blocks = [s[:, j * LANES:(j + 1) * LANES] for j in range(s.shape[1] // LANES)]
    dv = v.shape[1]
    if fixed_shift and sums_in_matmul:
        p = jnp.concatenate([jnp.exp2(blk).astype(BF16) for blk in blocks], axis=1)
        acc_ref[rows] = acc_ref[rows] + _dot(p, jnp.concatenate([v, jnp.ones_like(v)], axis=1))
        return
    if fixed_shift:
        ps = [jnp.exp2(blk) for blk in blocks]
        l_ref[rows] = l_ref[rows] + functools.reduce(jnp.add, ps)
        p = jnp.concatenate([blk.astype(BF16) for blk in ps], axis=1)
        acc_ref[rows, :dv] = acc_ref[rows, :dv] + _dot(p, v)
        return
    mx = functools.reduce(jnp.maximum, blocks)
    m_prev = m_ref[rows]
    m_new = jnp.maximum(m_prev, jnp.max(mx, axis=-1, keepdims=True))
    alpha = jnp.exp2(m_prev - m_new)
    ps = [jnp.exp2(blk - m_new) for blk in blocks]
    l_ref[rows] = alpha * l_ref[rows] + jnp.sum(functools.reduce(jnp.add, ps), axis=-1, keepdims=True)
    p = jnp.concatenate([blk.astype(BF16) for blk in ps], axis=1)
    acc_ref[rows, :dv] = alpha * acc_ref[rows, :dv] + _dot(p, v)
    m_ref[rows] = m_new


def _softmax_finish(rows, l_ref, acc_ref, dv, fixed_shift, sums_in_matmul=False):
    if fixed_shift and sums_in_matmul:
        acc = acc_ref[rows]
        return acc[:, :dv] / acc[:, dv:]
    l = l_ref[rows]
    if fixed_shift:
        l = jnp.sum(l, axis=-1, keepdims=True)
    return acc_ref[rows, :dv] / l


def _dsa_kernel(qt_ref, kt_ref, q_ref, k_ref, v_ref, sc_ref, th_ref, shift_ref, o_ref, m_ref, l_ref,
                acc_ref, bias_ref, qs_ref, *, tq, tk, fixed_shift):
    step = pl.program_id(1)
    qi = qt_ref[step]
    ki = kt_ref[step]

    @pl.when(ki == 0)
    def _():
        m_ref[...] = jnp.full(m_ref.shape, M_INIT, F32)
        l_ref[...] = jnp.zeros(l_ref.shape, F32)
        acc_ref[...] = jnp.zeros(acc_ref.shape, F32)
        for h in range(A_HEADS):
            qs_ref[h * tq:(h + 1) * tq, :] = q_ref[0, :, h * LANES:(h + 1) * LANES]

    bias_ref[...] = jnp.where(sc_ref[0] >= th_ref[0][:, :1], -shift_ref[:, :1], NEG)
    grp = A_HEADS // A_KV_HEADS
    for g in range(A_KV_HEADS):
        rows = slice(g * grp * tq, (g + 1) * grp * tq)
        k = k_ref[0, :, g * LANES:(g + 1) * LANES]
        v = v_ref[0, :, g * LANES:(g + 1) * LANES]
        s = _dot_nt(qs_ref[rows, :], k)
        s = jnp.concatenate([s[j * tq:(j + 1) * tq] + bias_ref[...] for j in range(grp)], axis=0)
        _softmax_step(s, v, rows, m_ref, l_ref, acc_ref, fixed_shift, sums_in_matmul=True)

    @pl.when(ki == (qi * tq + tq - 1) // tk)
    def _():
        for h in range(A_HEADS):
            out = _softmax_finish(slice(h * tq, (h + 1) * tq), l_ref, acc_ref, A_HEAD_DIM, fixed_shift,
                                  sums_in_matmul=True)
            o_ref[0, :, h * LANES:(h + 1) * LANES] = out.astype(o_ref.dtype)


def _dsa_attn(q, k, v, scores, th, shift, tq, tk, fixed_shift):
    b, s, dq = q.shape
    qt, kt = _causal_pairs(s, tq, tk)
    grid_spec = pltpu.PrefetchScalarGridSpec(
        num_scalar_prefetch=2,
        grid=(b, qt.shape[0]),
        in_specs=[
            pl.BlockSpec((1, tq, dq), lambda bi, p, qt, kt: (bi, qt[p], 0)),
            pl.BlockSpec((1, tk, k.shape[2]), lambda bi, p, qt, kt: (bi, kt[p], 0)),
            pl.BlockSpec((1, tk, v.shape[2]), lambda bi, p, qt, kt: (bi, kt[p], 0)),
            pl.BlockSpec((1, tq, tk), lambda bi, p, qt, kt: (bi, qt[p], kt[p])),
            pl.BlockSpec((1, tq, LANES), lambda bi, p, qt, kt: (bi, qt[p], 0)),
            pl.BlockSpec((1, LANES), lambda bi, p, qt, kt: (0, 0)),
        ],
        out_specs=pl.BlockSpec((1, tq, dq), lambda bi, p, qt, kt: (bi, qt[p], 0)),
        scratch_shapes=[
            pltpu.VMEM((A_HEADS * tq, LANES), F32),
            pltpu.VMEM((A_HEADS * tq, LANES), F32),
            pltpu.VMEM((A_HEADS * tq, 2 * A_HEAD_DIM), F32),
            pltpu.VMEM((tq, tk), F32),
            pltpu.VMEM((A_HEADS * tq, A_HEAD_DIM), BF16),
        ],
    )
    return pl.pallas_call(
        functools.partial(_dsa_kernel, tq=tq, tk=tk, fixed_shift=fixed_shift),
        name="dsa_attn_fixed" if fixed_shift else "dsa_attn_online",
        grid_spec=grid_spec,
        out_shape=jax.ShapeDtypeStruct((b, s, dq), BF16),
        compiler_params=_cparams(("parallel", "arbitrary")),
    )(qt, kt, q, k, v, scores, th, shift)


def _mla_kernel(qt_ref, kt_ref, q_ref, k_ref, v_ref, o_ref, m_ref, l_ref, acc_ref, *, tq, tk, hg,
                fixed_shift):
    step = pl.program_id(2)
    qi = qt_ref[step]
    ki = kt_ref[step]

    @pl.when(ki == 0)
    def _():
        m_ref[...] = jnp.full(m_ref.shape, M_INIT, F32)
        l_ref[...] = jnp.zeros(l_ref.shape, F32)
        acc_ref[...] = jnp.zeros(acc_ref.shape, F32)

    def heads(masked):
        if masked:
            keep = (lax.broadcasted_iota(jnp.int32, (tq, tk), 1)
                    <= lax.broadcasted_iota(jnp.int32, (tq, tk), 0))
        for h in range(hg):
            q = q_ref[0, :, h * 256:(h + 1) * 256]
            k = k_ref[0, :, h * 256:(h + 1) * 256]
            v = v_ref[0, :, h * B_V:(h + 1) * B_V]
            s = _dot_nt(q, k)
            if masked:
                s = jnp.where(keep, s, NEG)
            _softmax_step(s, v, slice(h * tq, (h + 1) * tq), m_ref, l_ref, acc_ref, fixed_shift)

    @pl.when(ki < qi)
    def _():
        heads(False)

    @pl.when(ki == qi)
    def _():
        heads(True)
        for h in range(hg):
            out = _softmax_finish(slice(h * tq, (h + 1) * tq), l_ref, acc_ref, B_V, fixed_shift)
            o_ref[0, :, h * B_V:(h + 1) * B_V] = out.astype(o_ref.dtype)


def _mla_attn(q, k, v, tq, tk, hg, fixed_shift):
    b, s, _ = q.shape
    assert tq == tk
    qt, kt = _causal_pairs(s, tq, tk)
    grid_spec = pltpu.PrefetchScalarGridSpec(
        num_scalar_prefetch=2,
        grid=(b, B_HEADS // hg, qt.shape[0]),
        in_specs=[
            pl.BlockSpec((1, tq, hg * 256), lambda bi, g, p, qt, kt: (bi, qt[p], g)),
            pl.BlockSpec((1, tk, hg * 256), lambda bi, g, p, qt, kt: (bi, kt[p], g)),
            pl.BlockSpec((1, tk, hg * B_V), lambda bi, g, p, qt, kt: (bi, kt[p], g)),
        ],
        out_specs=pl.BlockSpec((1, tq, hg * B_V), lambda bi, g, p, qt, kt: (bi, qt[p], g)),
        scratch_shapes=[
            pltpu.VMEM((hg * tq, LANES), F32),
            pltpu.VMEM((hg * tq, LANES), F32),
            pltpu.VMEM((hg * tq, B_V), F32),
        ],
    )
    return pl.pallas_call(
        functools.partial(_mla_kernel, tq=tq, tk=tk, hg=hg, fixed_shift=fixed_shift),
        name="mla_attn_fixed" if fixed_shift else "mla_attn_online",
        grid_spec=grid_spec,
        out_shape=jax.ShapeDtypeStruct((b, s, B_HEADS * B_V), BF16),
        compiler_params=_cparams(("parallel", "parallel", "arbitrary")),
    )(qt, kt, q, k, v)


def _regroup_w_in(w):
    sizes = (A_HEADS * A_HEAD_DIM, A_KV_HEADS * A_HEAD_DIM, A_KV_HEADS * A_HEAD_DIM,
             IDX_HEADS * IDX_DIM, IDX_DIM, IDX_HEADS, B_Q_LORA, B_KV_LORA, B_ROPE)
    offs = np.cumsum((0,) + sizes)
    w = w.astype(BF16)
    qa, ka, va, qi, ki, wi, cq, ckv, kr = (w[:, offs[j]:offs[j + 1]] for j in range(len(sizes)))
    gates = w[:, offs[-1]:]
    d = w.shape[0]
    kiw = jnp.concatenate([ki, wi, jnp.zeros((d, LANES - IDX_DIM - IDX_HEADS), w.dtype)], axis=1)
    krp = jnp.concatenate([kr, jnp.zeros((d, LANES - B_ROPE), w.dtype)], axis=1)
    return jnp.concatenate([qa, gates, qi, ka, va, cq, ckv, kiw, krp], axis=1).astype(BF16)


def _softmax_shift(d, g_q, g_k):
    bound = (d ** 0.5) * LOG2E * 1.02 * jnp.max(jnp.abs(g_q)) * jnp.max(jnp.abs(g_k))
    c = jnp.ceil(bound.astype(F32))
    ok = c <= MAX_FIXED_SHIFT
    return jnp.full((1, LANES), jnp.where(ok, c, 0.0), F32), ok


def _rope_tables(pos, d):
    half = d // 2
    freqs = jnp.power(jnp.float32(ROPE_THETA), -jnp.arange(half, dtype=F32) * (2.0 / d))
    ang = pos.astype(F32)[..., None] * freqs
    cos, sin = jnp.cos(ang), jnp.sin(ang)
    reps = LANES // d
    c = jnp.tile(jnp.concatenate([cos, cos], axis=-1), (1, 1, reps))
    s = jnp.tile(jnp.concatenate([-sin, sin], axis=-1), (1, 1, reps))
    return c.reshape(-1, LANES), s.reshape(-1, LANES)


def _layer(x, p, positions, g_mix_norm, w_in, g_qa, g_ka, g_cq, w_uq, g_ckv, w_ukv, g_qb, g_kb,
           w_out_a, w_out_b, w_o, g_ffn_norm, w_ffn_gate, w_ffn_up, w_ffn_down, g_ple_norm,
           w_ple_gate, w_ple_proj):
    b, s, d = x.shape
    m = b * s
    k_sel = min(TOPK_MAX, s // 4)
    xf = x.reshape(m, d)
    row = lambda g: g.reshape(1, -1).astype(F32)

    c128, s128 = _rope_tables(positions, A_HEAD_DIM)
    c64, s64 = _rope_tables(positions, IDX_DIM)

    proj = _norm_matmul("in_proj", xf, 0, d, row(g_mix_norm), _regroup_w_in(w_in), F32,
                        tm=TILE_IN_PROJ[0], tn=TILE_IN_PROJ[1])

    qa_r, ka_r, va_b, qi_r, klo, khi, wi = _prep_a(
        proj, c128, s128, c64, s64, row(g_qa) * (A_HEAD_DIM ** -0.5 * LOG2E), row(g_ka), tp=TILE_PREP_ROWS)
    r3 = lambda a: a.reshape(b, s, a.shape[-1])
    scores, th = _indexer(r3(qi_r), r3(klo), r3(khi), r3(wi), tq=TILE_ATTN, tk=TILE_ATTN, k_sel=k_sel)
    shift_a, fixed_a = _softmax_shift(A_HEAD_DIM, g_qa, g_ka)
    y_a = lax.cond(
        fixed_a,
        functools.partial(_dsa_attn, tq=TILE_ATTN, tk=TILE_ATTN, fixed_shift=True),
        functools.partial(_dsa_attn, tq=TILE_ATTN, tk=TILE_ATTN, fixed_shift=False),
        r3(qa_r), r3(ka_r), r3(va_b), scores, th, shift_a)

    w_uq_r = w_uq.reshape(B_Q_LORA, B_HEADS, B_QK)
    w_uq_r = jnp.concatenate([w_uq_r[:, :, :B_NOPE].reshape(B_Q_LORA, -1),
                              w_uq_r[:, :, B_NOPE:].reshape(B_Q_LORA, -1)], axis=1).astype(BF16)
    w_ukv_r = w_ukv.reshape(B_KV_LORA, B_HEADS, 2, B_NOPE)
    w_ukv_r = jnp.concatenate([w_ukv_r[:, :, 0, :].reshape(B_KV_LORA, -1),
                               w_ukv_r[:, :, 1, :].reshape(B_KV_LORA, -1)], axis=1).astype(BF16)
    pe2 = lambda g: jnp.tile(g[B_NOPE:], 2).reshape(1, LANES).astype(F32)
    shift_b, fixed_b = _softmax_shift(B_QK, g_qb, g_kb)
    q_m = _mla_q(proj, row(g_cq), w_uq_r, c64, s64, row(g_qb[:B_NOPE]), pe2(g_qb), tm=TILE_MLA_UP_ROWS)
    k_m, v_m = _mla_kv(proj, row(g_ckv), w_ukv_r, c64, s64, row(g_kb[:B_NOPE]), pe2(g_kb), shift_b,
                       tm=TILE_MLA_UP_ROWS)
    y_b = lax.cond(
        fixed_b,
        functools.partial(_mla_attn, tq=TILE_ATTN, tk=TILE_ATTN, hg=B_HEADS, fixed_shift=True),
        functools.partial(_mla_attn, tq=TILE_ATTN, tk=TILE_ATTN, hg=B_HEADS, fixed_shift=False),
        r3(q_m), r3(k_m), r3(v_m))

    mixed = _merge(y_a.reshape(m, -1), y_b.reshape(m, -1), w_out_a.astype(BF16), w_out_b.astype(BF16),
                   proj, tm=TILE_MERGE[0], tn=TILE_MERGE[1])
    x1 = _matmul_res("w_o", mixed, w_o.astype(BF16), xf, tm=TILE_W_O[0], tn=TILE_W_O[1])

    act = _ffn_up(x1, row(g_ffn_norm), w_ffn_gate.astype(BF16), w_ffn_up.astype(BF16),
                  tm=TILE_FFN_UP[0], tn=TILE_FFN_UP[1])
    x2 = _matmul_res("ffn_down", act, w_ffn_down.astype(BF16), x1,
                     tm=TILE_FFN_DOWN[0], tn=TILE_FFN_DOWN[1])

    out = _ple(x2, row(g_ple_norm), w_ple_gate.astype(BF16), p.reshape(m, -1),
               w_ple_proj.astype(BF16), tm=TILE_PLE_ROWS)
    return out.reshape(b, s, d)


def kernel(x, p, positions, g_mix_norm, w_in, g_qa, g_ka, g_cq, w_uq, g_ckv, w_ukv, g_qb, g_kb,
           w_out_a, w_out_b, w_o, g_ffn_norm, w_ffn_gate, w_ffn_up, w_ffn_down, g_ple_norm,
           w_ple_gate, w_ple_proj):
    for i in range(p.shape[0]):
        x = _layer(x, p[i], positions, g_mix_norm[i], w_in[i], g_qa[i], g_ka[i], g_cq[i], w_uq[i],
                   g_ckv[i], w_ukv[i], g_qb[i], g_kb[i], w_out_a[i], w_out_b[i], w_o[i],
                   g_ffn_norm[i], w_ffn_gate[i], w_ffn_up[i], w_ffn_down[i], g_ple_norm[i],
                   w_ple_gate[i], w_ple_proj[i])
    return x
```

```python
import functools

import numpy as np
import jax
import jax.numpy as jnp
from jax import lax
from jax.experimental import pallas as pl
from jax.experimental.pallas import tpu as pltpu

F32 = jnp.float32
BF16 = jnp.bfloat16

ROPE_THETA = 10000.0
NORM_EPS = 1e-6
NEG = -1e30
M_INIT = -1e29
POS_BIG = 3e38
TH_ALL = 0.5 * NEG
LOG2E = 1.4426950408889634
MAX_SEARCH_STEPS = 512
CNT_ROWS = 128
MAX_FIXED_SHIFT = 60.0
A_HEADS = 16
A_KV_HEADS = 4
A_HEAD_DIM = 128
IDX_HEADS = 16
IDX_DIM = 64
TOPK_MAX = 256
B_HEADS = 16
B_Q_LORA = 512
B_KV_LORA = 256
B_NOPE = 128
B_ROPE = 64
B_V = 128
B_QK = B_NOPE + B_ROPE
LANES = 128
VMEM_LIMIT = 56 * 1024 * 1024

TILE_IN_PROJ = (1024, 1536)
TILE_PREP_ROWS = 512
TILE_ATTN = 512
TILE_MLA_UP_ROWS = 512
TILE_MERGE = (1024, 512)
TILE_W_O = (1024, 1024)
TILE_FFN_UP = (1024, 512)
TILE_FFN_DOWN = (512, 1024)
TILE_PLE_ROWS = 512

COL_QA = 0
COL_GATE_A = 2048
COL_GATE_B = 4096
COL_QI = 6144
COL_KA = 7168
COL_VA = 7680
COL_CQ = 8192
COL_CKV = 8704
COL_KIW = 8960
COL_KR = 9088
N_PROJ = 9216


def _dot(a, b):
    return jnp.dot(a, b, preferred_element_type=F32)


def _dot_nt(a, b):
    return lax.dot_general(a, b, (((1,), (1,)), ((), ())), preferred_element_type=F32)


def _cparams(sem):
    return pltpu.CompilerParams(dimension_semantics=sem, vmem_limit_bytes=VMEM_LIMIT)


def _rms_to_bf16(x, g):
    ms = jnp.mean(x * x, axis=-1, keepdims=True)
    return (x * lax.rsqrt(ms + NORM_EPS) * g).astype(BF16)


def _norm_matmul_kernel(x_ref, g_ref, w_ref, o_ref, hn_ref):
    @pl.when(pl.program_id(1) == 0)
    def _():
        hn_ref[...] = _rms_to_bf16(x_ref[...], g_ref[...])

    o_ref[...] = _dot(hn_ref[...], w_ref[...]).astype(o_ref.dtype)


def _norm_matmul(name, x, x_col_block, k, g, w, out_dtype, tm, tn):
    m = x.shape[0]
    n = w.shape[1]
    return pl.pallas_call(
        _norm_matmul_kernel,
        name=name,
        grid=(m // tm, n // tn),
        in_specs=[
            pl.BlockSpec((tm, k), lambda i, j: (i, x_col_block)),
            pl.BlockSpec((1, k), lambda i, j: (0, 0)),
            pl.BlockSpec((k, tn), lambda i, j: (0, j)),
        ],
        out_specs=pl.BlockSpec((tm, tn), lambda i, j: (i, j)),
        out_shape=jax.ShapeDtypeStruct((m, n), out_dtype),
        scratch_shapes=[pltpu.VMEM((tm, k), BF16)],
        compiler_params=_cparams(("parallel", "arbitrary")),
    )(x, g, w)


def _ffn_up_kernel(x_ref, g_ref, wg_ref, wu_ref, o_ref, hn_ref):
    @pl.when(pl.program_id(1) == 0)
    def _():
        hn_ref[...] = _rms_to_bf16(x_ref[...], g_ref[...])

    h = hn_ref[...]
    a = _dot(h, wg_ref[...])
    b = _dot(h, wu_ref[...])
    o_ref[...] = (a * jax.nn.sigmoid(a) * b).astype(o_ref.dtype)


def _ffn_up(x, g, wg, wu, tm, tn):
    m, k = x.shape
    n = wg.shape[1]
    return pl.pallas_call(
        _ffn_up_kernel,
        name="ffn_up",
        grid=(m // tm, n // tn),
        in_specs=[
            pl.BlockSpec((tm, k), lambda i, j: (i, 0)),
            pl.BlockSpec((1, k), lambda i, j: (0, 0)),
            pl.BlockSpec((k, tn), lambda i, j: (0, j)),
            pl.BlockSpec((k, tn), lambda i, j: (0, j)),
        ],
        out_specs=pl.BlockSpec((tm, tn), lambda i, j: (i, j)),
        out_shape=jax.ShapeDtypeStruct((m, n), BF16),
        scratch_shapes=[pltpu.VMEM((tm, k), BF16)],
        compiler_params=_cparams(("parallel", "arbitrary")),
    )(x, g, wg, wu)


def _matmul_res_kernel(a_ref, w_ref, r_ref, o_ref):
    o_ref[...] = r_ref[...] + _dot(a_ref[...], w_ref[...])


def _matmul_res(name, a, w, res, tm, tn):
    m, k = a.shape
    n = w.shape[1]
    return pl.pallas_call(
        _matmul_res_kernel,
        name=name,
        grid=(n // tn, m // tm),
        in_specs=[
            pl.BlockSpec((tm, k), lambda j, i: (i, 0)),
            pl.BlockSpec((k, tn), lambda j, i: (0, j)),
            pl.BlockSpec((tm, tn), lambda j, i: (i, j)),
        ],
        out_specs=pl.BlockSpec((tm, tn), lambda j, i: (i, j)),
        out_shape=jax.ShapeDtypeStruct((m, n), F32),
        compiler_params=_cparams(("parallel", "parallel")),
    )(a, w, res)


def _merge_kernel(ya_ref, yb_ref, wa_ref, wb_ref, ga_ref, gb_ref, o_ref):
    a = _dot(ya_ref[...], wa_ref[...])
    b = _dot(yb_ref[...], wb_ref[...])
    o_ref[...] = (jax.nn.sigmoid(ga_ref[...]) * a + jax.nn.sigmoid(gb_ref[...]) * b).astype(o_ref.dtype)


def _merge(ya, yb, wa, wb, proj, tm, tn):
    m, k = ya.shape
    n = wa.shape[1]
    ga0 = COL_GATE_A // tn
    gb0 = COL_GATE_B // tn
    return pl.pallas_call(
        _merge_kernel,
        name="merge",
        grid=(m // tm, n // tn),
        in_specs=[
            pl.BlockSpec((tm, k), lambda i, j: (i, 0)),
            pl.BlockSpec((tm, k), lambda i, j: (i, 0)),
            pl.BlockSpec((k, tn), lambda i, j: (0, j)),
            pl.BlockSpec((k, tn), lambda i, j: (0, j)),
            pl.BlockSpec((tm, tn), lambda i, j: (i, ga0 + j)),
            pl.BlockSpec((tm, tn), lambda i, j: (i, gb0 + j)),
        ],
        out_specs=pl.BlockSpec((tm, tn), lambda i, j: (i, j)),
        out_shape=jax.ShapeDtypeStruct((m, n), BF16),
        compiler_params=_cparams(("parallel", "parallel")),
    )(ya, yb, wa, wb, proj, proj)


def _ple_kernel(x_ref, g_ref, wg_ref, p_ref, wp_ref, o_ref):
    x = x_ref[...]
    gate = jax.nn.sigmoid(_dot(_rms_to_bf16(x, g_ref[...]), wg_ref[...]))
    emb = _dot(p_ref[...].astype(BF16), wp_ref[...])
    o_ref[...] = x + gate * emb


def _ple(x, g, wg, p, wp, tm):
    m, k = x.shape
    n = wg.shape[1]
    kp = p.shape[1]
    assert n == k
    return pl.pallas_call(
        _ple_kernel,
        name="ple",
        grid=(m // tm,),
        in_specs=[
            pl.BlockSpec((tm, k), lambda i: (i, 0)),
            pl.BlockSpec((1, k), lambda i: (0, 0)),
            pl.BlockSpec((k, n), lambda i: (0, 0)),
            pl.BlockSpec((tm, kp), lambda i: (i, 0)),
            pl.BlockSpec((kp, n), lambda i: (0, 0)),
        ],
        out_specs=pl.BlockSpec((tm, n), lambda i: (i, 0)),
        out_shape=jax.ShapeDtypeStruct((m, n), F32),
        compiler_params=_cparams(("parallel",)),
    )(x, g, wg, p, wp)


def _rope64_pairs(x, c64, s64, first_half):
    partner = jnp.where(first_half, pltpu.roll(x, 96, 1), pltpu.roll(x, 32, 1))
    return x * c64 + partner * s64


def _prep_a_kernel(qa_ref, ka_ref, va_ref, qi_ref, kiw_ref, c128_ref, s128_ref, c64_ref, s64_ref,
                   gq_ref, gk_ref, qa_o, ka_o, va_o, qi_o, klo_o, khi_o, wi_o, *, wi_scale):
    c128 = c128_ref[...]
    s128 = s128_ref[...]

    def norm_rope(x, g):
        ms = jnp.mean(x * x, axis=-1, keepdims=True)
        y = x * lax.rsqrt(ms + NORM_EPS) * g
        return y * c128 + pltpu.roll(y, 64, 1) * s128

    for h in range(A_HEADS):
        hs = slice(h * LANES, (h + 1) * LANES)
        qa_o[:, hs] = norm_rope(qa_ref[:, hs], gq_ref[...]).astype(BF16)
    for h in range(A_KV_HEADS):
        hs = slice(h * LANES, (h + 1) * LANES)
        ka_o[:, hs] = norm_rope(ka_ref[:, hs], gk_ref[...]).astype(BF16)
    va_o[...] = va_ref[...].astype(BF16)

    c64 = c64_ref[...]
    s64 = s64_ref[...]
    lane = lax.broadcasted_iota(jnp.int32, c64.shape, 1)
    first_half = (lane % 64) < 32
    for p in range(IDX_HEADS * IDX_DIM // LANES):
        ps = slice(p * LANES, (p + 1) * LANES)
        qi_o[:, ps] = _rope64_pairs(qi_ref[:, ps], c64, s64, first_half).astype(BF16)
    kiw = kiw_ref[...]
    ki = jnp.where(lane < 64, _rope64_pairs(kiw, c64, s64, first_half), 0.0)
    klo_o[...] = ki.astype(BF16)
    khi_o[...] = pltpu.roll(ki, 64, 1).astype(BF16)
    wi_o[...] = kiw * wi_scale


def _prep_a(proj, c128, s128, c64, s64, gq, gk, tp):
    m = proj.shape[0]
    row = lambda w, cb: pl.BlockSpec((tp, w), lambda i: (i, cb))
    vec = pl.BlockSpec((1, LANES), lambda i: (0, 0))
    out = lambda w: pl.BlockSpec((tp, w), lambda i: (i, 0))
    wi_scale = (IDX_HEADS ** -0.5) * (IDX_DIM ** -0.5)
    return pl.pallas_call(
        functools.partial(_prep_a_kernel, wi_scale=wi_scale),
        name="prep_a",
        grid=(m // tp,),
        in_specs=[
            row(2048, COL_QA // 2048), row(512, COL_KA // 512), row(512, COL_VA // 512),
            row(1024, COL_QI // 1024), row(LANES, COL_KIW // LANES),
            out(LANES), out(LANES), out(LANES), out(LANES), vec, vec,
        ],
        out_specs=[out(2048), out(512), out(512), out(1024), out(LANES), out(LANES), out(LANES)],
        out_shape=[
            jax.ShapeDtypeStruct((m, 2048), BF16), jax.ShapeDtypeStruct((m, 512), BF16),
            jax.ShapeDtypeStruct((m, 512), BF16), jax.ShapeDtypeStruct((m, 1024), BF16),
            jax.ShapeDtypeStruct((m, LANES), BF16), jax.ShapeDtypeStruct((m, LANES), BF16),
            jax.ShapeDtypeStruct((m, LANES), F32),
        ],
        compiler_params=_cparams(("parallel",)),
    )(proj, proj, proj, proj, proj, c128, s128, c64, s64, gq, gk)


def _mla_q_kernel(cq_ref, g_ref, w_ref, c64_ref, s64_ref, gqn_ref, gqp_ref, q_o, *, q_scale):
    qb = _dot(_rms_to_bf16(cq_ref[...], g_ref[...]), w_ref[...])
    c64 = c64_ref[...]
    s64 = s64_ref[...]
    lane = lax.broadcasted_iota(jnp.int32, c64.shape, 1)
    first_half = (lane % 64) < 32
    lo = lane < 64
    nope_w = B_HEADS * B_NOPE
    for pair in range(B_HEADS // 2):
        qp2 = qb[:, nope_w + pair * LANES:nope_w + (pair + 1) * LANES]
        qp2_sq = qp2 * qp2
        rot2 = _rope64_pairs(qp2 * gqp_ref[...], c64, s64, first_half)
        for h in (2 * pair, 2 * pair + 1):
            mine = lo if h % 2 == 0 else jnp.logical_not(lo)
            shift_lane = lane == (64 if h % 2 == 0 else 0)
            qn = qb[:, h * LANES:(h + 1) * LANES]
            ss = jnp.sum(qn * qn + jnp.where(mine, qp2_sq, 0.0), axis=-1, keepdims=True)
            r = lax.rsqrt(ss * (1.0 / B_QK) + NORM_EPS) * q_scale
            q_o[:, 2 * h * LANES:(2 * h + 1) * LANES] = (qn * r * gqn_ref[...]).astype(BF16)
            qpe = jnp.where(mine, rot2 * r, jnp.where(shift_lane, 1.0, 0.0))
            q_o[:, (2 * h + 1) * LANES:(2 * h + 2) * LANES] = qpe.astype(BF16)


def _mla_kv_kernel(ckv_ref, g_ref, w_ref, kr_ref, c64_ref, s64_ref, gkn_ref, gkp_ref, shift_ref, k_o, v_o):
    kv = _dot(_rms_to_bf16(ckv_ref[...], g_ref[...]), w_ref[...])
    c64 = c64_ref[...]
    s64 = s64_ref[...]
    lane = lax.broadcasted_iota(jnp.int32, c64.shape, 1)
    first_half = (lane % 64) < 32
    lo = lane < 64
    kr = jnp.where(lo, kr_ref[...], 0.0)
    kr_ss = jnp.sum(kr * kr, axis=-1, keepdims=True)
    kpe_lo = jnp.where(lo, _rope64_pairs(kr * gkp_ref[...], c64, s64, first_half), 0.0)
    kpe_hi = pltpu.roll(kpe_lo, 64, 1)
    nope_w = B_HEADS * B_NOPE
    for h in range(B_HEADS):
        hs = slice(h * LANES, (h + 1) * LANES)
        shift_lane = lane == (64 if h % 2 == 0 else 0)
        kn = kv[:, hs]
        rk = lax.rsqrt((jnp.sum(kn * kn, axis=-1, keepdims=True) + kr_ss) * (1.0 / B_QK) + NORM_EPS)
        k_o[:, 2 * h * LANES:(2 * h + 1) * LANES] = (kn * rk * gkn_ref[...]).astype(BF16)
        kpe = kpe_lo if h % 2 == 0 else kpe_hi
        k_o[:, (2 * h + 1) * LANES:(2 * h + 2) * LANES] = jnp.where(
            shift_lane, -shift_ref[...], kpe * rk).astype(BF16)
        v_o[:, hs] = kv[:, nope_w + h * LANES:nope_w + (h + 1) * LANES].astype(BF16)


def _mla_q(proj, g_cq, w_uq, c64, s64, gqn, gqp, tm):
    m = proj.shape[0]
    vec = lambda w: pl.BlockSpec((1, w), lambda i: (0, 0))
    rows = lambda w: pl.BlockSpec((tm, w), lambda i: (i, 0))
    return pl.pallas_call(
        functools.partial(_mla_q_kernel, q_scale=B_QK ** -0.5 * LOG2E),
        name="mla_q",
        grid=(m // tm,),
        in_specs=[
            pl.BlockSpec((tm, B_Q_LORA), lambda i: (i, COL_CQ // B_Q_LORA)),
            vec(B_Q_LORA),
            pl.BlockSpec(w_uq.shape, lambda i: (0, 0)),
            rows(LANES), rows(LANES), vec(LANES), vec(LANES),
        ],
        out_specs=rows(B_HEADS * 256),
        out_shape=jax.ShapeDtypeStruct((m, B_HEADS * 256), BF16),
        compiler_params=_cparams(("parallel",)),
    )(proj, g_cq, w_uq, c64, s64, gqn, gqp)


def _mla_kv(proj, g_ckv, w_ukv, c64, s64, gkn, gkp, shift, tm):
    m = proj.shape[0]
    vec = lambda w: pl.BlockSpec((1, w), lambda i: (0, 0))
    rows = lambda w: pl.BlockSpec((tm, w), lambda i: (i, 0))
    return pl.pallas_call(
        _mla_kv_kernel,
        name="mla_kv",
        grid=(m // tm,),
        in_specs=[
            pl.BlockSpec((tm, B_KV_LORA), lambda i: (i, COL_CKV // B_KV_LORA)),
            vec(B_KV_LORA),
            pl.BlockSpec(w_ukv.shape, lambda i: (0, 0)),
            pl.BlockSpec((tm, LANES), lambda i: (i, COL_KR // LANES)),
            rows(LANES), rows(LANES), vec(LANES), vec(LANES), vec(LANES),
        ],
        out_specs=[rows(B_HEADS * 256), rows(B_HEADS * B_V)],
        out_shape=[
            jax.ShapeDtypeStruct((m, B_HEADS * 256), BF16),
            jax.ShapeDtypeStruct((m, B_HEADS * B_V), BF16),
        ],
        compiler_params=_cparams(("parallel",)),
    )(proj, g_ckv, w_ukv, proj, c64, s64, gkn, gkp, shift)


def _indexer_kernel(q_ref, klo_ref, khi_ref, w_ref, sc_ref, th_ref, *, tq, tk, seq, k_sel):
    qi = pl.program_id(1)
    n_vis = (qi * tq) // tk + 1
    n_all = seq // tk
    row = qi * tq + lax.broadcasted_iota(jnp.int32, (tq, 1), 0)
    wcols = [w_ref[0, :, IDX_DIM + h:IDX_DIM + h + 1] for h in range(IDX_HEADS)]

    def gen(c, carry):
        rmax, rmin = carry
        off = pl.multiple_of(c * tk, tk)
        kl = klo_ref[0, pl.ds(off, tk), :]
        kh = khi_ref[0, pl.ds(off, tk), :]
        acc = jnp.zeros((tq, tk), F32)
        for p in range(IDX_HEADS // 2):
            qp = q_ref[0, :, p * LANES:(p + 1) * LANES]
            acc = acc + wcols[2 * p] * jnp.maximum(_dot_nt(qp, kl), 0.0)
            acc = acc + wcols[2 * p + 1] * jnp.maximum(_dot_nt(qp, kh), 0.0)
        col = off + lax.broadcasted_iota(jnp.int32, (tq, tk), 1)
        vis = col <= row
        sc_ref[0, :, pl.ds(off, tk)] = jnp.where(vis, acc, NEG)
        rmax = jnp.maximum(rmax, jnp.max(jnp.where(vis, acc, NEG), axis=-1, keepdims=True))
        rmin = jnp.minimum(rmin, jnp.min(jnp.where(vis, acc, POS_BIG), axis=-1, keepdims=True))
        return rmax, rmin

    rmax, rmin = lax.fori_loop(0, n_vis, gen,
                               (jnp.full((tq, 1), NEG, F32), jnp.full((tq, 1), POS_BIG, F32)))

    def fill(c, carry):
        sc_ref[0, :, pl.ds(pl.multiple_of(c * tk, tk), tk)] = jnp.full((tq, tk), NEG, F32)
        return carry

    lax.fori_loop(n_vis, n_all, fill, 0)

    cw = min(2 * tk, seq)
    n_cnt = (n_vis * tk + cw - 1) // cw

    def count_ge(mid, strict=False, open_groups=None):
        zeros = jnp.zeros((CNT_ROWS, LANES), F32)
        accs = []
        for r in range(tq // CNT_ROWS):
            rs = slice(r * CNT_ROWS, (r + 1) * CNT_ROWS)
            midb = mid[rs]

            def body(c, acc, rs=rs, midb=midb):
                slab = sc_ref[0, rs, pl.ds(pl.multiple_of(c * cw, cw), cw)]
                for j in range(cw // LANES):
                    blk = slab[:, j * LANES:(j + 1) * LANES]
                    acc = acc + jnp.where(blk > midb if strict else blk >= midb, 1.0, 0.0)
                return acc

            if open_groups is None:
                accs.append(lax.fori_loop(0, n_cnt, body, zeros))
            else:
                accs.append(lax.cond(open_groups[r] > 0.0,
                                     functools.partial(lax.fori_loop, 0, n_cnt, body, zeros),
                                     lambda: zeros))
        return jnp.concatenate(
            [jnp.broadcast_to(jnp.sum(a, axis=-1, keepdims=True), (CNT_ROWS, LANES)) for a in accs], axis=0)

    rowb = qi * tq + lax.broadcasted_iota(jnp.int32, (tq, LANES), 0)
    active0 = (rowb + 1) > k_sel
    lo0 = jnp.where(active0, jnp.broadcast_to(rmin, (tq, LANES)), TH_ALL)
    hi0 = jnp.where(active0, jnp.broadcast_to(rmax, (tq, LANES)), TH_ALL)

    def n_open(lo, hi):
        still = jnp.where(lo < hi, 1.0, 0.0)
        return tuple(jnp.sum(still[r * CNT_ROWS:(r + 1) * CNT_ROWS]) for r in range(tq // CNT_ROWS))

    def cond(st):
        it, nopen = st[0], st[1]
        return jnp.logical_and(functools.reduce(jnp.add, nopen) > 0.0, it < MAX_SEARCH_STEPS)

    def body(st):
        it, was_open, lo, hi, cnt_lo = st
        mid = lo + (hi - lo) * 0.5
        cnt = count_ge(mid, open_groups=was_open)
        nopen = n_open(lo, hi)
        ge = cnt >= k_sel
        lo_n = jnp.where(ge, mid, lo)
        hi_n = jnp.where(ge, hi, mid)
        hi_n = jnp.where(cnt == k_sel, lo_n, hi_n)
        hi_n = jnp.where(mid >= hi, lo_n, hi_n)
        hi_n = jnp.where(mid <= lo, lo_n, hi_n)
        return it + 1, nopen, lo_n, hi_n, jnp.where(ge, cnt, cnt_lo)

    st = lax.while_loop(cond, body, (jnp.int32(0), n_open(lo0, hi0), lo0, hi0,
                                     (rowb + 1).astype(F32)))
    th = st[2]
    th_ref[0] = th

    tied = jnp.logical_and(active0, st[4] > k_sel)

    @pl.when(jnp.sum(jnp.where(tied, 1.0, 0.0)) > 0.0)
    def _():
        keep_ties = k_sel - count_ge(th, strict=True)
        reps = tk // LANES
        th_w = jnp.concatenate([th] * reps, axis=1)
        keep_w = jnp.concatenate([keep_ties] * reps, axis=1)
        earlier = (lax.broadcasted_iota(jnp.int32, (tk, tk), 0)
                   < lax.broadcasted_iota(jnp.int32, (tk, tk), 1))
        tri = jnp.where(earlier, 1.0, 0.0).astype(BF16)

        def demote(c, seen):
            off = pl.multiple_of(c * tk, tk)
            blk = sc_ref[0, :, pl.ds(off, tk)]
            eq = blk == th_w
            eqf = jnp.where(eq, 1.0, 0.0)
            rank = _dot(eqf.astype(BF16), tri) + jnp.concatenate([seen] * reps, axis=1)
            sc_ref[0, :, pl.ds(off, tk)] = jnp.where(jnp.logical_and(eq, rank >= keep_w), NEG, blk)
            return seen + jnp.sum(eqf, axis=-1, keepdims=True)

        lax.fori_loop(0, n_vis, demote, jnp.zeros((tq, LANES), F32))


def _indexer(qi_r, klo, khi, wi, tq, tk, k_sel):
    b, s, _ = qi_r.shape
    return pl.pallas_call(
        functools.partial(_indexer_kernel, tq=tq, tk=tk, seq=s, k_sel=float(k_sel)),
        name="indexer",
        grid=(b, s // tq),
        in_specs=[
            pl.BlockSpec((1, tq, IDX_HEADS * IDX_DIM), lambda bi, qi: (bi, qi, 0)),
            pl.BlockSpec((1, s, LANES), lambda bi, qi: (bi, 0, 0)),
            pl.BlockSpec((1, s, LANES), lambda bi, qi: (bi, 0, 0)),
            pl.BlockSpec((1, tq, LANES), lambda bi, qi: (bi, qi, 0)),
        ],
        out_specs=[
            pl.BlockSpec((1, tq, s), lambda bi, qi: (bi, qi, 0)),
            pl.BlockSpec((1, tq, LANES), lambda bi, qi: (bi, qi, 0)),
        ],
        out_shape=[jax.ShapeDtypeStruct((b, s, s), F32), jax.ShapeDtypeStruct((b, s, LANES), F32)],
        compiler_params=_cparams(("parallel", "parallel")),
    )(qi_r, klo, khi, wi)


def _causal_pairs(s, tq, tk):
    qs, ks = [], []
    for qi in range(s // tq):
        for ki in range((qi * tq + tq - 1) // tk + 1):
            qs.append(qi)
            ks.append(ki)
    return jnp.asarray(np.array(qs, np.int32)), jnp.asarray(np.array(ks, np.int32))


def _softmax_step(s, v, rows, m_ref, l_ref, acc_ref, fixed_shift, sums_in_matmul=False):
    blocks = [s[:, j * LANES:(j + 1) * LANES] for j in range(s.shape[1] // LANES)]
    dv = v.shape[1]
    if fixed_shift and sums_in_matmul:
        p = jnp.concatenate([jnp.exp2(blk).astype(BF16) for blk in blocks], axis=1)
        acc_ref[rows] = acc_ref[rows] + _dot(p, jnp.concatenate([v, jnp.ones_like(v)], axis=1))
        return
    if fixed_shift:
        ps = [jnp.exp2(blk) for blk in blocks]
        l_ref[rows] = l_ref[rows] + functools.reduce(jnp.add, ps)
        p = jnp.concatenate([blk.astype(BF16) for blk in ps], axis=1)
        acc_ref[rows, :dv] = acc_ref[rows, :dv] + _dot(p, v)
        return
    mx = functools.reduce(jnp.maximum, blocks)
    m_prev = m_ref[rows]
    m_new = jnp.maximum(m_prev, jnp.max(mx, axis=-1, keepdims=True))
    alpha = jnp.exp2(m_prev - m_new)
    ps = [jnp.exp2(blk - m_new) for blk in blocks]
    l_ref[rows] = alpha * l_ref[rows] + jnp.sum(functools.reduce(jnp.add, ps), axis=-1, keepdims=True)
    p = jnp.concatenate([blk.astype(BF16) for blk in ps], axis=1)
    acc_ref[rows, :dv] = alpha * acc_ref[rows, :dv] + _dot(p, v)
    m_ref[rows] = m_new


def _softmax_finish(rows, l_ref, acc_ref, dv, fixed_shift, sums_in_matmul=False):
    if fixed_shift and sums_in_matmul:
        acc = acc_ref[rows]
        return acc[:, :dv] / acc[:, dv:]
    l = l_ref[rows]
    if fixed_shift:
        l = jnp.sum(l, axis=-1, keepdims=True)
    return acc_ref[rows, :dv] / l


def _dsa_kernel(qt_ref, kt_ref, q_ref, k_ref, v_ref, sc_ref, th_ref, shift_ref, o_ref, m_ref, l_ref,
                acc_ref, bias_ref, qs_ref, *, tq, tk, fixed_shift):
    step = pl.program_id(1)
    qi = qt_ref[step]
    ki = kt_ref[step]

    @pl.when(ki == 0)
    def _():
        m_ref[...] = jnp.full(m_ref.shape, M_INIT, F32)
        l_ref[...] = jnp.zeros(l_ref.shape, F32)
        acc_ref[...] = jnp.zeros(acc_ref.shape, F32)
        for h in range(A_HEADS):
            qs_ref[h * tq:(h + 1) * tq, :] = q_ref[0, :, h * LANES:(h + 1) * LANES]

    bias_ref[...] = jnp.where(sc_ref[0] >= th_ref[0][:, :1], -shift_ref[:, :1], NEG)
    grp = A_HEADS // A_KV_HEADS
    for g in range(A_KV_HEADS):
        rows = slice(g * grp * tq, (g + 1) * grp * tq)
        k = k_ref[0, :, g * LANES:(g + 1) * LANES]
        v = v_ref[0, :, g * LANES:(g + 1) * LANES]
        s = _dot_nt(qs_ref[rows, :], k)
        s = jnp.concatenate([s[j * tq:(j + 1) * tq] + bias_ref[...] for j in range(grp)], axis=0)
        _softmax_step(s, v, rows, m_ref, l_ref, acc_ref, fixed_shift, sums_in_matmul=True)

    @pl.when(ki == (qi * tq + tq - 1) // tk)
    def _():
        for h in range(A_HEADS):
            out = _softmax_finish(slice(h * tq, (h + 1) * tq), l_ref, acc_ref, A_HEAD_DIM, fixed_shift,
                                  sums_in_matmul=True)
            o_ref[0, :, h * LANES:(h + 1) * LANES] = out.astype(o_ref.dtype)


def _dsa_attn(q, k, v, scores, th, shift, tq, tk, fixed_shift):
    b, s, dq = q.shape
    qt, kt = _causal_pairs(s, tq, tk)
    grid_spec = pltpu.PrefetchScalarGridSpec(
        num_scalar_prefetch=2,
        grid=(b, qt.shape[0]),
        in_specs=[
            pl.BlockSpec((1, tq, dq), lambda bi, p, qt, kt: (bi, qt[p], 0)),
            pl.BlockSpec((1, tk, k.shape[2]), lambda bi, p, qt, kt: (bi, kt[p], 0)),
            pl.BlockSpec((1, tk, v.shape[2]), lambda bi, p, qt, kt: (bi, kt[p], 0)),
            pl.BlockSpec((1, tq, tk), lambda bi, p, qt, kt: (bi, qt[p], kt[p])),
            pl.BlockSpec((1, tq, LANES), lambda bi, p, qt, kt: (bi, qt[p], 0)),
            pl.BlockSpec((1, LANES), lambda bi, p, qt, kt: (0, 0)),
        ],
        out_specs=pl.BlockSpec((1, tq, dq), lambda bi, p, qt, kt: (bi, qt[p], 0)),
        scratch_shapes=[
            pltpu.VMEM((A_HEADS * tq, LANES), F32),
            pltpu.VMEM((A_HEADS * tq, LANES), F32),
            pltpu.VMEM((A_HEADS * tq, 2 * A_HEAD_DIM), F32),
            pltpu.VMEM((tq, tk), F32),
            pltpu.VMEM((A_HEADS * tq, A_HEAD_DIM), BF16),
        ],
    )
    return pl.pallas_call(
        functools.partial(_dsa_kernel, tq=tq, tk=tk, fixed_shift=fixed_shift),
        name="dsa_attn_fixed" if fixed_shift else "dsa_attn_online",
        grid_spec=grid_spec,
        out_shape=jax.ShapeDtypeStruct((b, s, dq), BF16),
        compiler_params=_cparams(("parallel", "arbitrary")),
    )(qt, kt, q, k, v, scores, th, shift)


def _mla_kernel(qt_ref, kt_ref, q_ref, k_ref, v_ref, o_ref, m_ref, l_ref, acc_ref, *, tq, tk, hg,
                fixed_shift):
    step = pl.program_id(2)
    qi = qt_ref[step]
    ki = kt_ref[step]

    @pl.when(ki == 0)
    def _():
        m_ref[...] = jnp.full(m_ref.shape, M_INIT, F32)
        l_ref[...] = jnp.zeros(l_ref.shape, F32)
        acc_ref[...] = jnp.zeros(acc_ref.shape, F32)

    def heads(masked):
        if masked:
            keep = (lax.broadcasted_iota(jnp.int32, (tq, tk), 1)
                    <= lax.broadcasted_iota(jnp.int32, (tq, tk), 0))
        for h in range(hg):
            q = q_ref[0, :, h * 256:(h + 1) * 256]
            k = k_ref[0, :, h * 256:(h + 1) * 256]
            v = v_ref[0, :, h * B_V:(h + 1) * B_V]
            s = _dot_nt(q, k)
            if masked:
                s = jnp.where(keep, s, NEG)
            _softmax_step(s, v, slice(h * tq, (h + 1) * tq), m_ref, l_ref, acc_ref, fixed_shift)

    @pl.when(ki < qi)
    def _():
        heads(False)

    @pl.when(ki == qi)
    def _():
        heads(True)
        for h in range(hg):
            out = _softmax_finish(slice(h * tq, (h + 1) * tq), l_ref, acc_ref, B_V, fixed_shift)
            o_ref[0, :, h * B_V:(h + 1) * B_V] = out.astype(o_ref.dtype)


def _mla_attn(q, k, v, tq, tk, hg, fixed_shift):
    b, s, _ = q.shape
    assert tq == tk
    qt, kt = _causal_pairs(s, tq, tk)
    grid_spec = pltpu.PrefetchScalarGridSpec(
        num_scalar_prefetch=2,
        grid=(b, B_HEADS // hg, qt.shape[0]),
        in_specs=[
            pl.BlockSpec((1, tq, hg * 256), lambda bi, g, p, qt, kt: (bi, qt[p], g)),
            pl.BlockSpec((1, tk, hg * 256), lambda bi, g, p, qt, kt: (bi, kt[p], g)),
            pl.BlockSpec((1, tk, hg * B_V), lambda bi, g, p, qt, kt: (bi, kt[p], g)),
        ],
        out_specs=pl.BlockSpec((1, tq, hg * B_V), lambda bi, g, p, qt, kt: (bi, qt[p], g)),
        scratch_shapes=[
            pltpu.VMEM((hg * tq, LANES), F32),
            pltpu.VMEM((hg * tq, LANES), F32),
            pltpu.VMEM((hg * tq, B_V), F32),
        ],
    )
    return pl.pallas_call(
        functools.partial(_mla_kernel, tq=tq, tk=tk, hg=hg, fixed_shift=fixed_shift),
        name="mla_attn_fixed" if fixed_shift else "mla_attn_online",
        grid_spec=grid_spec,
        out_shape=jax.ShapeDtypeStruct((b, s, B_HEADS * B_V), BF16),
        compiler_params=_cparams(("parallel", "parallel", "arbitrary")),
    )(qt, kt, q, k, v)


def _regroup_w_in(w):
    sizes = (A_HEADS * A_HEAD_DIM, A_KV_HEADS * A_HEAD_DIM, A_KV_HEADS * A_HEAD_DIM,
             IDX_HEADS * IDX_DIM, IDX_DIM, IDX_HEADS, B_Q_LORA, B_KV_LORA, B_ROPE)
    offs = np.cumsum((0,) + sizes)
    w = w.astype(BF16)
    qa, ka, va, qi, ki, wi, cq, ckv, kr = (w[:, offs[j]:offs[j + 1]] for j in range(len(sizes)))
    gates = w[:, offs[-1]:]
    d = w.shape[0]
    kiw = jnp.concatenate([ki, wi, jnp.zeros((d, LANES - IDX_DIM - IDX_HEADS), w.dtype)], axis=1)
    krp = jnp.concatenate([kr, jnp.zeros((d, LANES - B_ROPE), w.dtype)], axis=1)
    return jnp.concatenate([qa, gates, qi, ka, va, cq, ckv, kiw, krp], axis=1).astype(BF16)


def _softmax_shift(d, g_q, g_k):
    bound = (d ** 0.5) * LOG2E * 1.02 * jnp.max(jnp.abs(g_q)) * jnp.max(jnp.abs(g_k))
    c = jnp.ceil(bound.astype(F32))
    ok = c <= MAX_FIXED_SHIFT
    return jnp.full((1, LANES), jnp.where(ok, c, 0.0), F32), ok


def _rope_tables(pos, d):
    half = d // 2
    freqs = jnp.power(jnp.float32(ROPE_THETA), -jnp.arange(half, dtype=F32) * (2.0 / d))
    ang = pos.astype(F32)[..., None] * freqs
    cos, sin = jnp.cos(ang), jnp.sin(ang)
    reps = LANES // d
    c = jnp.tile(jnp.concatenate([cos, cos], axis=-1), (1, 1, reps))
    s = jnp.tile(jnp.concatenate([-sin, sin], axis=-1), (1, 1, reps))
    return c.reshape(-1, LANES), s.reshape(-1, LANES)


def _layer(x, p, positions, g_mix_norm, w_in, g_qa, g_ka, g_cq, w_uq, g_ckv, w_ukv, g_qb, g_kb,
           w_out_a, w_out_b, w_o, g_ffn_norm, w_ffn_gate, w_ffn_up, w_ffn_down, g_ple_norm,
           w_ple_gate, w_ple_proj):
    b, s, d = x.shape
    m = b * s
    k_sel = min(TOPK_MAX, s // 4)
    xf = x.reshape(m, d)
    row = lambda g: g.reshape(1, -1).astype(F32)

    c128, s128 = _rope_tables(positions, A_HEAD_DIM)
    c64, s64 = _rope_tables(positions, IDX_DIM)

    proj = _norm_matmul("in_proj", xf, 0, d, row(g_mix_norm), _regroup_w_in(w_in), F32,
                        tm=TILE_IN_PROJ[0], tn=TILE_IN_PROJ[1])

    qa_r, ka_r, va_b, qi_r, klo, khi, wi = _prep_a(
        proj, c128, s128, c64, s64, row(g_qa) * (A_HEAD_DIM ** -0.5 * LOG2E), row(g_ka), tp=TILE_PREP_ROWS)
    r3 = lambda a: a.reshape(b, s, a.shape[-1])
    scores, th = _indexer(r3(qi_r), r3(klo), r3(khi), r3(wi), tq=TILE_ATTN, tk=TILE_ATTN, k_sel=k_sel)
    shift_a, fixed_a = _softmax_shift(A_HEAD_DIM, g_qa, g_ka)
    y_a = lax.cond(
        fixed_a,
        functools.partial(_dsa_attn, tq=TILE_ATTN, tk=TILE_ATTN, fixed_shift=True),
        functools.partial(_dsa_attn, tq=TILE_ATTN, tk=TILE_ATTN, fixed_shift=False),
        r3(qa_r), r3(ka_r), r3(va_b), scores, th, shift_a)

    w_uq_r = w_uq.reshape(B_Q_LORA, B_HEADS, B_QK)
    w_uq_r = jnp.concatenate([w_uq_r[:, :, :B_NOPE].reshape(B_Q_LORA, -1),
                              w_uq_r[:, :, B_NOPE:].reshape(B_Q_LORA, -1)], axis=1).astype(BF16)
    w_ukv_r = w_ukv.reshape(B_KV_LORA, B_HEADS, 2, B_NOPE)
    w_ukv_r = jnp.concatenate([w_ukv_r[:, :, 0, :].reshape(B_KV_LORA, -1),
                               w_ukv_r[:, :, 1, :].reshape(B_KV_LORA, -1)], axis=1).astype(BF16)
    pe2 = lambda g: jnp.tile(g[B_NOPE:], 2).reshape(1, LANES).astype(F32)
    shift_b, fixed_b = _softmax_shift(B_QK, g_qb, g_kb)
    q_m = _mla_q(proj, row(g_cq), w_uq_r, c64, s64, row(g_qb[:B_NOPE]), pe2(g_qb), tm=TILE_MLA_UP_ROWS)
    k_m, v_m = _mla_kv(proj, row(g_ckv), w_ukv_r, c64, s64, row(g_kb[:B_NOPE]), pe2(g_kb), shift_b,
                       tm=TILE_MLA_UP_ROWS)
    y_b = lax.cond(
        fixed_b,
        functools.partial(_mla_attn, tq=TILE_ATTN, tk=TILE_ATTN, hg=B_HEADS, fixed_shift=True),
        functools.partial(_mla_attn, tq=TILE_ATTN, tk=TILE_ATTN, hg=B_HEADS, fixed_shift=False),
        r3(q_m), r3(k_m), r3(v_m))

    mixed = _merge(y_a.reshape(m, -1), y_b.reshape(m, -1), w_out_a.astype(BF16), w_out_b.astype(BF16),
                   proj, tm=TILE_MERGE[0], tn=TILE_MERGE[1])
    x1 = _matmul_res("w_o", mixed, w_o.astype(BF16), xf, tm=TILE_W_O[0], tn=TILE_W_O[1])

    act = _ffn_up(x1, row(g_ffn_norm), w_ffn_gate.astype(BF16), w_ffn_up.astype(BF16),
                  tm=TILE_FFN_UP[0], tn=TILE_FFN_UP[1])
    x2 = _matmul_res("ffn_down", act, w_ffn_down.astype(BF16), x1,
                     tm=TILE_FFN_DOWN[0], tn=TILE_FFN_DOWN[1])

    out = _ple(x2, row(g_ple_norm), w_ple_gate.astype(BF16), p.reshape(m, -1),
               w_ple_proj.astype(BF16), tm=TILE_PLE_ROWS)
    return out.reshape(b, s, d)


def kernel(x, p, positions, g_mix_norm, w_in, g_qa, g_ka, g_cq, w_uq, g_ckv, w_ukv, g_qb, g_kb,
           w_out_a, w_out_b, w_o, g_ffn_norm, w_ffn_gate, w_ffn_up, w_ffn_down, g_ple_norm,
           w_ple_gate, w_ple_proj):
    for i in range(p.shape[0]):
        x = _layer(x, p[i], positions, g_mix_norm[i], w_in[i], g_qa[i], g_ka[i], g_cq[i], w_uq[i],
                   g_ckv[i], w_ukv[i], g_qb[i], g_kb[i], w_out_a[i], w_out_b[i], w_o[i],
                   g_ffn_norm[i], w_ffn_gate[i], w_ffn_up[i], w_ffn_down[i], g_ple_norm[i],
                   w_ple_gate[i], w_ple_proj[i])
    return x
```

```python
import functools

import numpy as np
import jax
import jax.numpy as jnp
from jax import lax
from jax.experimental import pallas as pl
from jax.experimental.pallas import tpu as pltpu

F32 = jnp.float32
BF16 = jnp.bfloat16

ROPE_THETA = 10000.0
NORM_EPS = 1e-6
NEG = -1e30
M_INIT = -1e29
POS_BIG = 3e38
TH_ALL = 0.5 * NEG
LOG2E = 1.4426950408889634
MAX_SEARCH_STEPS = 512
CNT_ROWS = 128
MAX_FIXED_SHIFT = 60.0
A_HEADS = 16
A_KV_HEADS = 4
A_HEAD_DIM = 128
IDX_HEADS = 16
IDX_DIM = 64
TOPK_MAX = 256
B_HEADS = 16
B_Q_LORA = 512
B_KV_LORA = 256
B_NOPE = 128
B_ROPE = 64
B_V = 128
B_QK = B_NOPE + B_ROPE
LANES = 128
VMEM_LIMIT = 56 * 1024 * 1024

TILE_IN_PROJ = (1024, 1536)
TILE_PREP_ROWS = 512
TILE_ATTN = 512
TILE_MLA_UP_ROWS = 512
TILE_MERGE = (1024, 512)
TILE_W_O_ROWS = 512
TILE_FFN_UP = (2048, 512)
TILE_FFN_DOWN = (512, 1024)
TILE_PLE_ROWS = 512

COL_QA = 0
COL_GATE_A = 2048
COL_GATE_B = 4096
COL_QI = 6144
COL_KA = 7168
COL_VA = 7680
COL_CQ = 8192
COL_CKV = 8704
COL_KIW = 8960
COL_KR = 9088
N_PROJ = 9216


def _dot(a, b):
    return jnp.dot(a, b, preferred_element_type=F32)


def _dot_nt(a, b):
    return lax.dot_general(a, b, (((1,), (1,)), ((), ())), preferred_element_type=F32)


def _cparams(sem):
    return pltpu.CompilerParams(dimension_semantics=sem, vmem_limit_bytes=VMEM_LIMIT)


def _rms_to_bf16(x, g):
    ms = jnp.mean(x * x, axis=-1, keepdims=True)
    return (x * lax.rsqrt(ms + NORM_EPS) * g).astype(BF16)


def _norm_matmul_kernel(x_ref, g_ref, w_ref, o_ref, hn_ref):
    @pl.when(pl.program_id(1) == 0)
    def _():
        hn_ref[...] = _rms_to_bf16(x_ref[...], g_ref[...])

    o_ref[...] = _dot(hn_ref[...], w_ref[...]).astype(o_ref.dtype)


def _norm_matmul(name, x, x_col_block, k, g, w, out_dtype, tm, tn):
    m = x.shape[0]
    n = w.shape[1]
    return pl.pallas_call(
        _norm_matmul_kernel,
        name=name,
        grid=(m // tm, n // tn),
        in_specs=[
            pl.BlockSpec((tm, k), lambda i, j: (i, x_col_block)),
            pl.BlockSpec((1, k), lambda i, j: (0, 0)),
            pl.BlockSpec((k, tn), lambda i, j: (0, j)),
        ],
        out_specs=pl.BlockSpec((tm, tn), lambda i, j: (i, j)),
        out_shape=jax.ShapeDtypeStruct((m, n), out_dtype),
        scratch_shapes=[pltpu.VMEM((tm, k), BF16)],
        compiler_params=_cparams(("parallel", "arbitrary")),
    )(x, g, w)


def _ffn_up_kernel(h_ref, wg_ref, wu_ref, o_ref):
    h = h_ref[...]
    a = _dot(h, wg_ref[...])
    b = _dot(h, wu_ref[...])
    o_ref[...] = (a * jax.nn.sigmoid(a) * b).astype(o_ref.dtype)


def _ffn_up(h, wg, wu, tm, tn):
    m, k = h.shape
    n = wg.shape[1]
    return pl.pallas_call(
        _ffn_up_kernel,
        name="ffn_up",
        grid=(m // tm, n // tn),
        in_specs=[
            pl.BlockSpec((tm, k), lambda i, j: (i, 0)),
            pl.BlockSpec((k, tn), lambda i, j: (0, j)),
            pl.BlockSpec((k, tn), lambda i, j: (0, j)),
        ],
        out_specs=pl.BlockSpec((tm, tn), lambda i, j: (i, j)),
        out_shape=jax.ShapeDtypeStruct((m, n), BF16),
        compiler_params=_cparams(("parallel", "parallel")),
    )(h, wg, wu)


def _w_o_kernel(a_ref, w_ref, r_ref, g_ref, o_ref, hn_ref):
    x1 = r_ref[...] + _dot(a_ref[...], w_ref[...])
    o_ref[...] = x1
    hn_ref[...] = _rms_to_bf16(x1, g_ref[...])


def _w_o(a, w, res, g, tm):
    m, k = a.shape
    n = w.shape[1]
    rows = lambda width: pl.BlockSpec((tm, width), lambda i: (i, 0))
    return pl.pallas_call(
        _w_o_kernel,
        name="w_o",
        grid=(m // tm,),
        in_specs=[rows(k), pl.BlockSpec((k, n), lambda i: (0, 0)), rows(n), pl.BlockSpec((1, n), lambda i: (0, 0))],
        out_specs=[rows(n), rows(n)],
        out_shape=[jax.ShapeDtypeStruct((m, n), F32), jax.ShapeDtypeStruct((m, n), BF16)],
        compiler_params=_cparams(("parallel",)),
    )(a, w, res, g)


def _matmul_res_kernel(a_ref, w_ref, r_ref, o_ref):
    o_ref[...] = r_ref[...] + _dot(a_ref[...], w_ref[...])


def _matmul_res(name, a, w, res, tm, tn):
    m, k = a.shape
    n = w.shape[1]
    return pl.pallas_call(
        _matmul_res_kernel,
        name=name,
        grid=(n // tn, m // tm),
        in_specs=[
            pl.BlockSpec((tm, k), lambda j, i: (i, 0)),
            pl.BlockSpec((k, tn), lambda j, i: (0, j)),
            pl.BlockSpec((tm, tn), lambda j, i: (i, j)),
        ],
        out_specs=pl.BlockSpec((tm, tn), lambda j, i: (i, j)),
        out_shape=jax.ShapeDtypeStruct((m, n), F32),
        compiler_params=_cparams(("parallel", "parallel")),
    )(a, w, res)


def _merge_kernel(ya_ref, yb_ref, wa_ref, wb_ref, ga_ref, gb_ref, o_ref):
    a = _dot(ya_ref[...], wa_ref[...])
    b = _dot(yb_ref[...], wb_ref[...])
    o_ref[...] = (jax.nn.sigmoid(ga_ref[...]) * a + jax.nn.sigmoid(gb_ref[...]) * b).astype(o_ref.dtype)


def _merge(ya, yb, wa, wb, proj, tm, tn):
    m, k = ya.shape
    n = wa.shape[1]
    ga0 = COL_GATE_A // tn
    gb0 = COL_GATE_B // tn
    return pl.pallas_call(
        _merge_kernel,
        name="merge",
        grid=(m // tm, n // tn),
        in_specs=[
            pl.BlockSpec((tm, k), lambda i, j: (i, 0)),
            pl.BlockSpec((tm, k), lambda i, j: (i, 0)),
            pl.BlockSpec((k, tn), lambda i, j: (0, j)),
            pl.BlockSpec((k, tn), lambda i, j: (0, j)),
            pl.BlockSpec((tm, tn), lambda i, j: (i, ga0 + j)),
            pl.BlockSpec((tm, tn), lambda i, j: (i, gb0 + j)),
        ],
        out_specs=pl.BlockSpec((tm, tn), lambda i, j: (i, j)),
        out_shape=jax.ShapeDtypeStruct((m, n), BF16),
        compiler_params=_cparams(("parallel", "parallel")),
    )(ya, yb, wa, wb, proj, proj)


def _ple_kernel(x_ref, g_ref, wg_ref, p_ref, wp_ref, o_ref):
    x = x_ref[...]
    gate = jax.nn.sigmoid(_dot(_rms_to_bf16(x, g_ref[...]), wg_ref[...]))
    emb = _dot(p_ref[...].astype(BF16), wp_ref[...])
    o_ref[...] = x + gate * emb


def _ple(x, g, wg, p, wp, tm):
    m, k = x.shape
    n = wg.shape[1]
    kp = p.shape[1]
    assert n == k
    return pl.pallas_call(
        _ple_kernel,
        name="ple",
        grid=(m // tm,),
        in_specs=[
            pl.BlockSpec((tm, k), lambda i: (i, 0)),
            pl.BlockSpec((1, k), lambda i: (0, 0)),
            pl.BlockSpec((k, n), lambda i: (0, 0)),
            pl.BlockSpec((tm, kp), lambda i: (i, 0)),
            pl.BlockSpec((kp, n), lambda i: (0, 0)),
        ],
        out_specs=pl.BlockSpec((tm, n), lambda i: (i, 0)),
        out_shape=jax.ShapeDtypeStruct((m, n), F32),
        compiler_params=_cparams(("parallel",)),
    )(x, g, wg, p, wp)


def _rope64_pairs(x, c64, s64, first_half):
    partner = jnp.where(first_half, pltpu.roll(x, 96, 1), pltpu.roll(x, 32, 1))
    return x * c64 + partner * s64


def _prep_a_kernel(qa_ref, ka_ref, va_ref, qi_ref, kiw_ref, c128_ref, s128_ref, c64_ref, s64_ref,
                   gq_ref, gk_ref, qa_o, ka_o, va_o, qi_o, klo_o, khi_o, wi_o, *, wi_scale):
    c128 = c128_ref[...]
    s128 = s128_ref[...]

    def norm_rope(x, g):
        ms = jnp.mean(x * x, axis=-1, keepdims=True)
        y = x * lax.rsqrt(ms + NORM_EPS) * g
        return y * c128 + pltpu.roll(y, 64, 1) * s128

    for h in range(A_HEADS):
        hs = slice(h * LANES, (h + 1) * LANES)
        qa_o[:, hs] = norm_rope(qa_ref[:, hs], gq_ref[...]).astype(BF16)
    for h in range(A_KV_HEADS):
        hs = slice(h * LANES, (h + 1) * LANES)
        ka_o[:, hs] = norm_rope(ka_ref[:, hs], gk_ref[...]).astype(BF16)
    va_o[...] = va_ref[...].astype(BF16)

    c64 = c64_ref[...]
    s64 = s64_ref[...]
    lane = lax.broadcasted_iota(jnp.int32, c64.shape, 1)
    first_half = (lane % 64) < 32
    for p in range(IDX_HEADS * IDX_DIM // LANES):
        ps = slice(p * LANES, (p + 1) * LANES)
        qi_o[:, ps] = _rope64_pairs(qi_ref[:, ps], c64, s64, first_half).astype(BF16)
    kiw = kiw_ref[...]
    ki = jnp.where(lane < 64, _rope64_pairs(kiw, c64, s64, first_half), 0.0)
    klo_o[...] = ki.astype(BF16)
    khi_o[...] = pltpu.roll(ki, 64, 1).astype(BF16)
    wi_o[...] = kiw * wi_scale


def _prep_a(proj, c128, s128, c64, s64, gq, gk, tp):
    m = proj.shape[0]
    row = lambda w, cb: pl.BlockSpec((tp, w), lambda i: (i, cb))
    vec = pl.BlockSpec((1, LANES), lambda i: (0, 0))
    out = lambda w: pl.BlockSpec((tp, w), lambda i: (i, 0))
    wi_scale = (IDX_HEADS ** -0.5) * (IDX_DIM ** -0.5)
    return pl.pallas_call(
        functools.partial(_prep_a_kernel, wi_scale=wi_scale),
        name="prep_a",
        grid=(m // tp,),
        in_specs=[
            row(2048, COL_QA // 2048), row(512, COL_KA // 512), row(512, COL_VA // 512),
            row(1024, COL_QI // 1024), row(LANES, COL_KIW // LANES),
            out(LANES), out(LANES), out(LANES), out(LANES), vec, vec,
        ],
        out_specs=[out(2048), out(512), out(512), out(1024), out(LANES), out(LANES), out(LANES)],
        out_shape=[
            jax.ShapeDtypeStruct((m, 2048), BF16), jax.ShapeDtypeStruct((m, 512), BF16),
            jax.ShapeDtypeStruct((m, 512), BF16), jax.ShapeDtypeStruct((m, 1024), BF16),
            jax.ShapeDtypeStruct((m, LANES), BF16), jax.ShapeDtypeStruct((m, LANES), BF16),
            jax.ShapeDtypeStruct((m, LANES), F32),
        ],
        compiler_params=_cparams(("parallel",)),
    )(proj, proj, proj, proj, proj, c128, s128, c64, s64, gq, gk)


def _mla_q_kernel(cq_ref, g_ref, w_ref, c64_ref, s64_ref, gqn_ref, gqp_ref, q_o, *, q_scale):
    qb = _dot(_rms_to_bf16(cq_ref[...], g_ref[...]), w_ref[...])
    c64 = c64_ref[...]
    s64 = s64_ref[...]
    lane = lax.broadcasted_iota(jnp.int32, c64.shape, 1)
    first_half = (lane % 64) < 32
    lo = lane < 64
    nope_w = B_HEADS * B_NOPE
    for pair in range(B_HEADS // 2):
        qp2 = qb[:, nope_w + pair * LANES:nope_w + (pair + 1) * LANES]
        qp2_sq = qp2 * qp2
        rot2 = _rope64_pairs(qp2 * gqp_ref[...], c64, s64, first_half)
        for h in (2 * pair, 2 * pair + 1):
            mine = lo if h % 2 == 0 else jnp.logical_not(lo)
            shift_lane = lane == (64 if h % 2 == 0 else 0)
            qn = qb[:, h * LANES:(h + 1) * LANES]
            ss = jnp.sum(qn * qn + jnp.where(mine, qp2_sq, 0.0), axis=-1, keepdims=True)
            r = lax.rsqrt(ss * (1.0 / B_QK) + NORM_EPS) * q_scale
            q_o[:, 2 * h * LANES:(2 * h + 1) * LANES] = (qn * r * gqn_ref[...]).astype(BF16)
            qpe = jnp.where(mine, rot2 * r, jnp.where(shift_lane, 1.0, 0.0))
            q_o[:, (2 * h + 1) * LANES:(2 * h + 2) * LANES] = qpe.astype(BF16)


def _mla_kv_kernel(ckv_ref, g_ref, w_ref, kr_ref, c64_ref, s64_ref, gkn_ref, gkp_ref, shift_ref, k_o, v_o):
    kv = _dot(_rms_to_bf16(ckv_ref[...], g_ref[...]), w_ref[...])
    c64 = c64_ref[...]
    s64 = s64_ref[...]
    lane = lax.broadcasted_iota(jnp.int32, c64.shape, 1)
    first_half = (lane % 64) < 32
    lo = lane < 64
    kr = jnp.where(lo, kr_ref[...], 0.0)
    kr_ss = jnp.sum(kr * kr, axis=-1, keepdims=True)
    kpe_lo = jnp.where(lo, _rope64_pairs(kr * gkp_ref[...], c64, s64, first_half), 0.0)
    kpe_hi = pltpu.roll(kpe_lo, 64, 1)
    nope_w = B_HEADS * B_NOPE
    for h in range(B_HEADS):
        hs = slice(h * LANES, (h + 1) * LANES)
        shift_lane = lane == (64 if h % 2 == 0 else 0)
        kn = kv[:, hs]
        rk = lax.rsqrt((jnp.sum(kn * kn, axis=-1, keepdims=True) + kr_ss) * (1.0 / B_QK) + NORM_EPS)
        k_o[:, 2 * h * LANES:(2 * h + 1) * LANES] = (kn * rk * gkn_ref[...]).astype(BF16)
        kpe = kpe_lo if h % 2 == 0 else kpe_hi
        k_o[:, (2 * h + 1) * LANES:(2 * h + 2) * LANES] = jnp.where(
            shift_lane, -shift_ref[...], kpe * rk).astype(BF16)
        v_o[:, hs] = kv[:, nope_w + h * LANES:nope_w + (h + 1) * LANES].astype(BF16)


def _mla_q(proj, g_cq, w_uq, c64, s64, gqn, gqp, tm):
    m = proj.shape[0]
    vec = lambda w: pl.BlockSpec((1, w), lambda i: (0, 0))
    rows = lambda w: pl.BlockSpec((tm, w), lambda i: (i, 0))
    return pl.pallas_call(
        functools.partial(_mla_q_kernel, q_scale=B_QK ** -0.5 * LOG2E),
        name="mla_q",
        grid=(m // tm,),
        in_specs=[
            pl.BlockSpec((tm, B_Q_LORA), lambda i: (i, COL_CQ // B_Q_LORA)),
            vec(B_Q_LORA),
            pl.BlockSpec(w_uq.shape, lambda i: (0, 0)),
            rows(LANES), rows(LANES), vec(LANES), vec(LANES),
        ],
        out_specs=rows(B_HEADS * 256),
        out_shape=jax.ShapeDtypeStruct((m, B_HEADS * 256), BF16),
        compiler_params=_cparams(("parallel",)),
    )(proj, g_cq, w_uq, c64, s64, gqn, gqp)


def _mla_kv(proj, g_ckv, w_ukv, c64, s64, gkn, gkp, shift, tm):
    m = proj.shape[0]
    vec = lambda w: pl.BlockSpec((1, w), lambda i: (0, 0))
    rows = lambda w: pl.BlockSpec((tm, w), lambda i: (i, 0))
    return pl.pallas_call(
        _mla_kv_kernel,
        name="mla_kv",
        grid=(m // tm,),
        in_specs=[
            pl.BlockSpec((tm, B_KV_LORA), lambda i: (i, COL_CKV // B_KV_LORA)),
            vec(B_KV_LORA),
            pl.BlockSpec(w_ukv.shape, lambda i: (0, 0)),
            pl.BlockSpec((tm, LANES), lambda i: (i, COL_KR // LANES)),
            rows(LANES), rows(LANES), vec(LANES), vec(LANES), vec(LANES),
        ],
        out_specs=[rows(B_HEADS * 256), rows(B_HEADS * B_V)],
        out_shape=[
            jax.ShapeDtypeStruct((m, B_HEADS * 256), BF16),
            jax.ShapeDtypeStruct((m, B_HEADS * B_V), BF16),
        ],
        compiler_params=_cparams(("parallel",)),
    )(proj, g_ckv, w_ukv, proj, c64, s64, gkn, gkp, shift)


def _indexer_kernel(q_ref, klo_ref, khi_ref, w_ref, sc_ref, th_ref, *, tq, tk, seq, k_sel):
    qi = pl.program_id(1)
    n_vis = (qi * tq) // tk + 1
    n_all = seq // tk
    row = qi * tq + lax.broadcasted_iota(jnp.int32, (tq, 1), 0)
    wcols = [w_ref[0, :, IDX_DIM + h:IDX_DIM + h + 1] for h in range(IDX_HEADS)]

    def gen(c, carry):
        rmax, rmin = carry
        off = pl.multiple_of(c * tk, tk)
        kl = klo_ref[0, pl.ds(off, tk), :]
        kh = khi_ref[0, pl.ds(off, tk), :]
        acc = jnp.zeros((tq, tk), F32)
        for p in range(IDX_HEADS // 2):
            qp = q_ref[0, :, p * LANES:(p + 1) * LANES]
            acc = acc + wcols[2 * p] * jnp.maximum(_dot_nt(qp, kl), 0.0)
            acc = acc + wcols[2 * p + 1] * jnp.maximum(_dot_nt(qp, kh), 0.0)
        col = off + lax.broadcasted_iota(jnp.int32, (tq, tk), 1)
        vis = col <= row
        sc_ref[0, :, pl.ds(off, tk)] = jnp.where(vis, acc, NEG)
        rmax = jnp.maximum(rmax, jnp.max(jnp.where(vis, acc, NEG), axis=-1, keepdims=True))
        rmin = jnp.minimum(rmin, jnp.min(jnp.where(vis, acc, POS_BIG), axis=-1, keepdims=True))
        return rmax, rmin

    rmax, rmin = lax.fori_loop(0, n_vis, gen,
                               (jnp.full((tq, 1), NEG, F32), jnp.full((tq, 1), POS_BIG, F32)))

    def fill(c, carry):
        sc_ref[0, :, pl.ds(pl.multiple_of(c * tk, tk), tk)] = jnp.full((tq, tk), NEG, F32)
        return carry

    lax.fori_loop(n_vis, n_all, fill, 0)

    cw = min(2 * tk, seq)
    n_cnt = (n_vis * tk + cw - 1) // cw

    def count_ge(mid, strict=False, open_groups=None):
        zeros = jnp.zeros((CNT_ROWS, LANES), F32)
        accs = []
        for r in range(tq // CNT_ROWS):
            rs = slice(r * CNT_ROWS, (r + 1) * CNT_ROWS)
            midb = mid[rs]

            def body(c, acc, rs=rs, midb=midb):
                slab = sc_ref[0, rs, pl.ds(pl.multiple_of(c * cw, cw), cw)]
                for j in range(cw // LANES):
                    blk = slab[:, j * LANES:(j + 1) * LANES]
                    acc = acc + jnp.where(blk > midb if strict else blk >= midb, 1.0, 0.0)
                return acc

            if open_groups is None:
                accs.append(lax.fori_loop(0, n_cnt, body, zeros))
            else:
                accs.append(lax.cond(open_groups[r] > 0.0,
                                     functools.partial(lax.fori_loop, 0, n_cnt, body, zeros),
                                     lambda: zeros))
        return jnp.concatenate(
            [jnp.broadcast_to(jnp.sum(a, axis=-1, keepdims=True), (CNT_ROWS, LANES)) for a in accs], axis=0)

    rowb = qi * tq + lax.broadcasted_iota(jnp.int32, (tq, LANES), 0)
    active0 = (rowb + 1) > k_sel
    lo0 = jnp.where(active0, jnp.broadcast_to(rmin, (tq, LANES)), TH_ALL)
    hi0 = jnp.where(active0, jnp.broadcast_to(rmax, (tq, LANES)), TH_ALL)

    def n_open(lo, hi):
        still = jnp.where(lo < hi, 1.0, 0.0)
        return tuple(jnp.sum(still[r * CNT_ROWS:(r + 1) * CNT_ROWS]) for r in range(tq // CNT_ROWS))

    def cond(st):
        it, nopen = st[0], st[1]
        return jnp.logical_and(functools.reduce(jnp.add, nopen) > 0.0, it < MAX_SEARCH_STEPS)

    def body(st):
        it, was_open, lo, hi, cnt_lo = st
        mid = lo + (hi - lo) * 0.5
        cnt = count_ge(mid, open_groups=was_open)
        nopen = n_open(lo, hi)
        ge = cnt >= k_sel
        lo_n = jnp.where(ge, mid, lo)
        hi_n = jnp.where(ge, hi, mid)
        hi_n = jnp.where(cnt == k_sel, lo_n, hi_n)
        hi_n = jnp.where(mid >= hi, lo_n, hi_n)
        hi_n = jnp.where(mid <= lo, lo_n, hi_n)
        return it + 1, nopen, lo_n, hi_n, jnp.where(ge, cnt, cnt_lo)

    st = lax.while_loop(cond, body, (jnp.int32(0), n_open(lo0, hi0), lo0, hi0,
                                     (rowb + 1).astype(F32)))
    th = st[2]
    th_ref[0] = th

    tied = jnp.logical_and(active0, st[4] > k_sel)

    @pl.when(jnp.sum(jnp.where(tied, 1.0, 0.0)) > 0.0)
    def _():
        keep_ties = k_sel - count_ge(th, strict=True)
        reps = tk // LANES
        th_w = jnp.concatenate([th] * reps, axis=1)
        keep_w = jnp.concatenate([keep_ties] * reps, axis=1)
        earlier = (lax.broadcasted_iota(jnp.int32, (tk, tk), 0)
                   < lax.broadcasted_iota(jnp.int32, (tk, tk), 1))
        tri = jnp.where(earlier, 1.0, 0.0).astype(BF16)

        def demote(c, seen):
            off = pl.multiple_of(c * tk, tk)
            blk = sc_ref[0, :, pl.ds(off, tk)]
            eq = blk == th_w
            eqf = jnp.where(eq, 1.0, 0.0)
            rank = _dot(eqf.astype(BF16), tri) + jnp.concatenate([seen] * reps, axis=1)
            sc_ref[0, :, pl.ds(off, tk)] = jnp.where(jnp.logical_and(eq, rank >= keep_w), NEG, blk)
            return seen + jnp.sum(eqf, axis=-1, keepdims=True)

        lax.fori_loop(0, n_vis, demote, jnp.zeros((tq, LANES), F32))


def _indexer(qi_r, klo, khi, wi, tq, tk, k_sel):
    b, s, _ = qi_r.shape
    return pl.pallas_call(
        functools.partial(_indexer_kernel, tq=tq, tk=tk, seq=s, k_sel=float(k_sel)),
        name="indexer",
        grid=(b, s // tq),
        in_specs=[
            pl.BlockSpec((1, tq, IDX_HEADS * IDX_DIM), lambda bi, qi: (bi, qi, 0)),
            pl.BlockSpec((1, s, LANES), lambda bi, qi: (bi, 0, 0)),
            pl.BlockSpec((1, s, LANES), lambda bi, qi: (bi, 0, 0)),
            pl.BlockSpec((1, tq, LANES), lambda bi, qi: (bi, qi, 0)),
        ],
        out_specs=[
            pl.BlockSpec((1, tq, s), lambda bi, qi: (bi, qi, 0)),
            pl.BlockSpec((1, tq, LANES), lambda bi, qi: (bi, qi, 0)),
        ],
        out_shape=[jax.ShapeDtypeStruct((b, s, s), F32), jax.ShapeDtypeStruct((b, s, LANES), F32)],
        compiler_params=_cparams(("parallel", "parallel")),
    )(qi_r, klo, khi, wi)


def _causal_pairs(s, tq, tk):
    qs, ks = [], []
    for qi in range(s // tq):
        for ki in range((qi * tq + tq - 1) // tk + 1):
            qs.append(qi)
            ks.append(ki)
    return jnp.asarray(np.array(qs, np.int32)), jnp.asarray(np.array(ks, np.int32))


def _softmax_step(s, v, rows, m_ref, l_ref, acc_ref, fixed_shift, sums_in_matmul=False):
    blocks = [s[:, j * LANES:(j + 1) * LANES] for j in range(s.shape[1] // LANES)]
    dv = v.shape[1]
    if fixed_shift and sums_in_matmul:
        p = jnp.concatenate([jnp.exp2(blk).astype(BF16) for blk in blocks], axis=1)
        acc_ref[rows] = acc_ref[rows] + _dot(p, jnp.concatenate([v, jnp.ones_like(v)], axis=1))
        return
    if fixed_shift:
        ps = [jnp.exp2(blk) for blk in blocks]
        l_ref[rows] = l_ref[rows] + functools.reduce(jnp.add, ps)
        p = jnp.concatenate([blk.astype(BF16) for blk in ps], axis=1)
        acc_ref[rows, :dv] = acc_ref[rows, :dv] + _dot(p, v)
        return
    mx = functools.reduce(jnp.maximum, blocks)
    m_prev = m_ref[rows]
    m_new = jnp.maximum(m_prev, jnp.max(mx, axis=-1, keepdims=True))
    alpha = jnp.exp2(m_prev - m_new)
    ps = [jnp.exp2(blk - m_new) for blk in blocks]
    l_ref[rows] = alpha * l_ref[rows] + jnp.sum(functools.reduce(jnp.add, ps), axis=-1, keepdims=True)
    p = jnp.concatenate([blk.astype(BF16) for blk in ps], axis=1)
    acc_ref[rows, :dv] = alpha * acc_ref[rows, :dv] + _dot(p, v)
    m_ref[rows] = m_new


def _softmax_finish(rows, l_ref, acc_ref, dv, fixed_shift, sums_in_matmul=False):
    if fixed_shift and sums_in_matmul:
        acc = acc_ref[rows]
        return acc[:, :dv] / acc[:, dv:]
    l = l_ref[rows]
    if fixed_shift:
        l = jnp.sum(l, axis=-1, keepdims=True)
    return acc_ref[rows, :dv] / l


def _dsa_kernel(qt_ref, kt_ref, q_ref, k_ref, v_ref, sc_ref, th_ref, shift_ref, o_ref, m_ref, l_ref,
                acc_ref, bias_ref, qs_ref, *, tq, tk, fixed_shift):
    step = pl.program_id(1)
    qi = qt_ref[step]
    ki = kt_ref[step]

    @pl.when(ki == 0)
    def _():
        m_ref[...] = jnp.full(m_ref.shape, M_INIT, F32)
        l_ref[...] = jnp.zeros(l_ref.shape, F32)
        acc_ref[...] = jnp.zeros(acc_ref.shape, F32)
        for h in range(A_HEADS):
            qs_ref[h * tq:(h + 1) * tq, :] = q_ref[0, :, h * LANES:(h + 1) * LANES]

    bias_ref[...] = jnp.where(sc_ref[0] >= th_ref[0][:, :1], -shift_ref[:, :1], NEG)
    grp = A_HEADS // A_KV_HEADS
    for g in range(A_KV_HEADS):
        rows = slice(g * grp * tq, (g + 1) * grp * tq)
        k = k_ref[0, :, g * LANES:(g + 1) * LANES]
        v = v_ref[0, :, g * LANES:(g + 1) * LANES]
        s = _dot_nt(qs_ref[rows, :], k)
        s = jnp.concatenate([s[j * tq:(j + 1) * tq] + bias_ref[...] for j in range(grp)], axis=0)
        _softmax_step(s, v, rows, m_ref, l_ref, acc_ref, fixed_shift, sums_in_matmul=True)

    @pl.when(ki == (qi * tq + tq - 1) // tk)
    def _():
        for h in range(A_HEADS):
            out = _softmax_finish(slice(h * tq, (h + 1) * tq), l_ref, acc_ref, A_HEAD_DIM, fixed_shift,
                                  sums_in_matmul=True)
            o_ref[0, :, h * LANES:(h + 1) * LANES] = out.astype(o_ref.dtype)


def _dsa_attn(q, k, v, scores, th, shift, tq, tk, fixed_shift):
    b, s, dq = q.shape
    qt, kt = _causal_pairs(s, tq, tk)
    grid_spec = pltpu.PrefetchScalarGridSpec(
        num_scalar_prefetch=2,
        grid=(b, qt.shape[0]),
        in_specs=[
            pl.BlockSpec((1, tq, dq), lambda bi, p, qt, kt: (bi, qt[p], 0)),
            pl.BlockSpec((1, tk, k.shape[2]), lambda bi, p, qt, kt: (bi, kt[p], 0)),
            pl.BlockSpec((1, tk, v.shape[2]), lambda bi, p, qt, kt: (bi, kt[p], 0)),
            pl.BlockSpec((1, tq, tk), lambda bi, p, qt, kt: (bi, qt[p], kt[p])),
            pl.BlockSpec((1, tq, LANES), lambda bi, p, qt, kt: (bi, qt[p], 0)),
            pl.BlockSpec((1, LANES), lambda bi, p, qt, kt: (0, 0)),
        ],
        out_specs=pl.BlockSpec((1, tq, dq), lambda bi, p, qt, kt: (bi, qt[p], 0)),
        scratch_shapes=[
            pltpu.VMEM((A_HEADS * tq, LANES), F32),
            pltpu.VMEM((A_HEADS * tq, LANES), F32),
            pltpu.VMEM((A_HEADS * tq, 2 * A_HEAD_DIM), F32),
            pltpu.VMEM((tq, tk), F32),
            pltpu.VMEM((A_HEADS * tq, A_HEAD_DIM), BF16),
        ],
    )
    return pl.pallas_call(
        functools.partial(_dsa_kernel, tq=tq, tk=tk, fixed_shift=fixed_shift),
        name="dsa_attn_fixed" if fixed_shift else "dsa_attn_online",
        grid_spec=grid_spec,
        out_shape=jax.ShapeDtypeStruct((b, s, dq), BF16),
        compiler_params=_cparams(("parallel", "arbitrary")),
    )(qt, kt, q, k, v, scores, th, shift)


def _mla_kernel(qt_ref, kt_ref, q_ref, k_ref, v_ref, o_ref, m_ref, l_ref, acc_ref, *, tq, tk, hg,
                fixed_shift):
    step = pl.program_id(2)
    qi = qt_ref[step]
    ki = kt_ref[step]

    @pl.when(ki == 0)
    def _():
        m_ref[...] = jnp.full(m_ref.shape, M_INIT, F32)
        l_ref[...] = jnp.zeros(l_ref.shape, F32)
        acc_ref[...] = jnp.zeros(acc_ref.shape, F32)

    def heads(masked):
        if masked:
            keep = (lax.broadcasted_iota(jnp.int32, (tq, tk), 1)
                    <= lax.broadcasted_iota(jnp.int32, (tq, tk), 0))
        for h in range(hg):
            q = q_ref[0, :, h * 256:(h + 1) * 256]
            k = k_ref[0, :, h * 256:(h + 1) * 256]
            v = v_ref[0, :, h * B_V:(h + 1) * B_V]
            s = _dot_nt(q, k)
            if masked:
                s = jnp.where(keep, s, NEG)
            _softmax_step(s, v, slice(h * tq, (h + 1) * tq), m_ref, l_ref, acc_ref, fixed_shift)

    @pl.when(ki < qi)
    def _():
        heads(False)

    @pl.when(ki == qi)
    def _():
        heads(True)
        for h in range(hg):
            out = _softmax_finish(slice(h * tq, (h + 1) * tq), l_ref, acc_ref, B_V, fixed_shift)
            o_ref[0, :, h * B_V:(h + 1) * B_V] = out.astype(o_ref.dtype)


def _mla_attn(q, k, v, tq, tk, hg, fixed_shift):
    b, s, _ = q.shape
    assert tq == tk
    qt, kt = _causal_pairs(s, tq, tk)
    grid_spec = pltpu.PrefetchScalarGridSpec(
        num_scalar_prefetch=2,
        grid=(b, B_HEADS // hg, qt.shape[0]),
        in_specs=[
            pl.BlockSpec((1, tq, hg * 256), lambda bi, g, p, qt, kt: (bi, qt[p], g)),
            pl.BlockSpec((1, tk, hg * 256), lambda bi, g, p, qt, kt: (bi, kt[p], g)),
            pl.BlockSpec((1, tk, hg * B_V), lambda bi, g, p, qt, kt: (bi, kt[p], g)),
        ],
        out_specs=pl.BlockSpec((1, tq, hg * B_V), lambda bi, g, p, qt, kt: (bi, qt[p], g)),
        scratch_shapes=[
            pltpu.VMEM((hg * tq, LANES), F32),
            pltpu.VMEM((hg * tq, LANES), F32),
            pltpu.VMEM((hg * tq, B_V), F32),
        ],
    )
    return pl.pallas_call(
        functools.partial(_mla_kernel, tq=tq, tk=tk, hg=hg, fixed_shift=fixed_shift),
        name="mla_attn_fixed" if fixed_shift else "mla_attn_online",
        grid_spec=grid_spec,
        out_shape=jax.ShapeDtypeStruct((b, s, B_HEADS * B_V), BF16),
        compiler_params=_cparams(("parallel", "parallel", "arbitrary")),
    )(qt, kt, q, k, v)


def _regroup_w_in(w):
    sizes = (A_HEADS * A_HEAD_DIM, A_KV_HEADS * A_HEAD_DIM, A_KV_HEADS * A_HEAD_DIM,
             IDX_HEADS * IDX_DIM, IDX_DIM, IDX_HEADS, B_Q_LORA, B_KV_LORA, B_ROPE)
    offs = np.cumsum((0,) + sizes)
    w = w.astype(BF16)
    qa, ka, va, qi, ki, wi, cq, ckv, kr = (w[:, offs[j]:offs[j + 1]] for j in range(len(sizes)))
    gates = w[:, offs[-1]:]
    d = w.shape[0]
    kiw = jnp.concatenate([ki, wi, jnp.zeros((d, LANES - IDX_DIM - IDX_HEADS), w.dtype)], axis=1)
    krp = jnp.concatenate([kr, jnp.zeros((d, LANES - B_ROPE), w.dtype)], axis=1)
    return jnp.concatenate([qa, gates, qi, ka, va, cq, ckv, kiw, krp], axis=1).astype(BF16)


def _softmax_shift(d, g_q, g_k):
    bound = (d ** 0.5) * LOG2E * 1.02 * jnp.max(jnp.abs(g_q)) * jnp.max(jnp.abs(g_k))
    c = jnp.ceil(bound.astype(F32))
    ok = c <= MAX_FIXED_SHIFT
    return jnp.full((1, LANES), jnp.where(ok, c, 0.0), F32), ok


def _rope_tables(pos, d):
    half = d // 2
    freqs = jnp.power(jnp.float32(ROPE_THETA), -jnp.arange(half, dtype=F32) * (2.0 / d))
    ang = pos.astype(F32)[..., None] * freqs
    cos, sin = jnp.cos(ang), jnp.sin(ang)
    reps = LANES // d
    c = jnp.tile(jnp.concatenate([cos, cos], axis=-1), (1, 1, reps))
    s = jnp.tile(jnp.concatenate([-sin, sin], axis=-1), (1, 1, reps))
    return c.reshape(-1, LANES), s.reshape(-1, LANES)


def _layer(x, p, positions, g_mix_norm, w_in, g_qa, g_ka, g_cq, w_uq, g_ckv, w_ukv, g_qb, g_kb,
           w_out_a, w_out_b, w_o, g_ffn_norm, w_ffn_gate, w_ffn_up, w_ffn_down, g_ple_norm,
           w_ple_gate, w_ple_proj):
    b, s, d = x.shape
    m = b * s
    k_sel = min(TOPK_MAX, s // 4)
    xf = x.reshape(m, d)
    row = lambda g: g.reshape(1, -1).astype(F32)

    c128, s128 = _rope_tables(positions, A_HEAD_DIM)
    c64, s64 = _rope_tables(positions, IDX_DIM)

    proj = _norm_matmul("in_proj", xf, 0, d, row(g_mix_norm), _regroup_w_in(w_in), F32,
                        tm=TILE_IN_PROJ[0], tn=TILE_IN_PROJ[1])

    qa_r, ka_r, va_b, qi_r, klo, khi, wi = _prep_a(
        proj, c128, s128, c64, s64, row(g_qa) * (A_HEAD_DIM ** -0.5 * LOG2E), row(g_ka), tp=TILE_PREP_ROWS)
    r3 = lambda a: a.reshape(b, s, a.shape[-1])
    scores, th = _indexer(r3(qi_r), r3(klo), r3(khi), r3(wi), tq=TILE_ATTN, tk=TILE_ATTN, k_sel=k_sel)
    shift_a, fixed_a = _softmax_shift(A_HEAD_DIM, g_qa, g_ka)
    y_a = lax.cond(
        fixed_a,
        functools.partial(_dsa_attn, tq=TILE_ATTN, tk=TILE_ATTN, fixed_shift=True),
        functools.partial(_dsa_attn, tq=TILE_ATTN, tk=TILE_ATTN, fixed_shift=False),
        r3(qa_r), r3(ka_r), r3(va_b), scores, th, shift_a)

    w_uq_r = w_uq.reshape(B_Q_LORA, B_HEADS, B_QK)
    w_uq_r = jnp.concatenate([w_uq_r[:, :, :B_NOPE].reshape(B_Q_LORA, -1),
                              w_uq_r[:, :, B_NOPE:].reshape(B_Q_LORA, -1)], axis=1).astype(BF16)
    w_ukv_r = w_ukv.reshape(B_KV_LORA, B_HEADS, 2, B_NOPE)
    w_ukv_r = jnp.concatenate([w_ukv_r[:, :, 0, :].reshape(B_KV_LORA, -1),
                               w_ukv_r[:, :, 1, :].reshape(B_KV_LORA, -1)], axis=1).astype(BF16)
    pe2 = lambda g: jnp.tile(g[B_NOPE:], 2).reshape(1, LANES).astype(F32)
    shift_b, fixed_b = _softmax_shift(B_QK, g_qb, g_kb)
    q_m = _mla_q(proj, row(g_cq), w_uq_r, c64, s64, row(g_qb[:B_NOPE]), pe2(g_qb), tm=TILE_MLA_UP_ROWS)
    k_m, v_m = _mla_kv(proj, row(g_ckv), w_ukv_r, c64, s64, row(g_kb[:B_NOPE]), pe2(g_kb), shift_b,
                       tm=TILE_MLA_UP_ROWS)
    y_b = lax.cond(
        fixed_b,
        functools.partial(_mla_attn, tq=TILE_ATTN, tk=TILE_ATTN, hg=B_HEADS, fixed_shift=True),
        functools.partial(_mla_attn, tq=TILE_ATTN, tk=TILE_ATTN, hg=B_HEADS, fixed_shift=False),
        r3(q_m), r3(k_m), r3(v_m))

    mixed = _merge(y_a.reshape(m, -1), y_b.reshape(m, -1), w_out_a.astype(BF16), w_out_b.astype(BF16),
                   proj, tm=TILE_MERGE[0], tn=TILE_MERGE[1])
    x1, h2 = _w_o(mixed, w_o.astype(BF16), xf, row(g_ffn_norm), tm=TILE_W_O_ROWS)

    act = _ffn_up(h2, w_ffn_gate.astype(BF16), w_ffn_up.astype(BF16),
                  tm=min(TILE_FFN_UP[0], m), tn=TILE_FFN_UP[1])
    x2 = _matmul_res("ffn_down", act, w_ffn_down.astype(BF16), x1,
                     tm=TILE_FFN_DOWN[0], tn=TILE_FFN_DOWN[1])

    out = _ple(x2, row(g_ple_norm), w_ple_gate.astype(BF16), p.reshape(m, -1),
               w_ple_proj.astype(BF16), tm=TILE_PLE_ROWS)
    return out.reshape(b, s, d)


def kernel(x, p, positions, g_mix_norm, w_in, g_qa, g_ka, g_cq, w_uq, g_ckv, w_ukv, g_qb, g_kb,
           w_out_a, w_out_b, w_o, g_ffn_norm, w_ffn_gate, w_ffn_up, w_ffn_down, g_ple_norm,
           w_ple_gate, w_ple_proj):
    for i in range(p.shape[0]):
        x = _layer(x, p[i], positions, g_mix_norm[i], w_in[i], g_qa[i], g_ka[i], g_cq[i], w_uq[i],
                   g_ckv[i], w_ukv[i], g_qb[i], g_kb[i], w_out_a[i], w_out_b[i], w_o[i],
                   g_ffn_norm[i], w_ffn_gate[i], w_ffn_up[i], w_ffn_down[i], g_ple_norm[i],
                   w_ple_gate[i], w_ple_proj[i])
    return x
```

```python
import functools

import numpy as np
import jax
import jax.numpy as jnp
from jax import lax
from jax.experimental import pallas as pl
from jax.experimental.pallas import tpu as pltpu

F32 = jnp.float32
BF16 = jnp.bfloat16

ROPE_THETA = 10000.0
NORM_EPS = 1e-6
NEG = -1e30
M_INIT = -1e29
POS_BIG = 3e38
TH_ALL = 0.5 * NEG
LOG2E = 1.4426950408889634
MAX_SEARCH_STEPS = 512
CNT_ROWS = 128
MAX_FIXED_SHIFT = 60.0
A_HEADS = 16
A_KV_HEADS = 4
A_HEAD_DIM = 128
IDX_HEADS = 16
IDX_DIM = 64
TOPK_MAX = 256
B_HEADS = 16
B_Q_LORA = 512
B_KV_LORA = 256
B_NOPE = 128
B_ROPE = 64
B_V = 128
B_QK = B_NOPE + B_ROPE
LANES = 128
VMEM_LIMIT = 56 * 1024 * 1024

TILE_IN_PROJ = (1024, 1536)
TILE_PREP_ROWS = 512
TILE_ATTN = 512
TILE_MLA_UP_ROWS = 512
TILE_MERGE = (1024, 512)
TILE_W_O = (1024, 1024)
TILE_FFN_UP = (1024, 512)
TILE_FFN_DOWN = (512, 1024)
TILE_PLE_ROWS = 512

COL_QA = 0
COL_GATE_A = 2048
COL_GATE_B = 4096
COL_QI = 6144
COL_KA = 7168
COL_VA = 7680
COL_CQ = 8192
COL_CKV = 8704
COL_KIW = 8960
COL_KR = 9088
N_PROJ = 9216


def _dot(a, b):
    return jnp.dot(a, b, preferred_element_type=F32)


def _dot_nt(a, b):
    return lax.dot_general(a, b, (((1,), (1,)), ((), ())), preferred_element_type=F32)


def _cparams(sem):
    return pltpu.CompilerParams(dimension_semantics=sem, vmem_limit_bytes=VMEM_LIMIT)


def _rms_to_bf16(x, g):
    ms = jnp.mean(x * x, axis=-1, keepdims=True)
    return (x * lax.rsqrt(ms + NORM_EPS) * g).astype(BF16)


def _norm_matmul_kernel(x_ref, g_ref, w_ref, o_ref, hn_ref):
    @pl.when(pl.program_id(1) == 0)
    def _():
        hn_ref[...] = _rms_to_bf16(x_ref[...], g_ref[...])

    o_ref[...] = _dot(hn_ref[...], w_ref[...]).astype(o_ref.dtype)


def _norm_matmul(name, x, x_col_block, k, g, w, out_dtype, tm, tn):
    m = x.shape[0]
    n = w.shape[1]
    return pl.pallas_call(
        _norm_matmul_kernel,
        name=name,
        grid=(m // tm, n // tn),
        in_specs=[
            pl.BlockSpec((tm, k), lambda i, j: (i, x_col_block)),
            pl.BlockSpec((1, k), lambda i, j: (0, 0)),
            pl.BlockSpec((k, tn), lambda i, j: (0, j)),
        ],
        out_specs=pl.BlockSpec((tm, tn), lambda i, j: (i, j)),
        out_shape=jax.ShapeDtypeStruct((m, n), out_dtype),
        scratch_shapes=[pltpu.VMEM((tm, k), BF16)],
        compiler_params=_cparams(("parallel", "arbitrary")),
    )(x, g, w)


def _ffn_up_kernel(x_ref, g_ref, wg_ref, wu_ref, o_ref, hn_ref):
    @pl.when(pl.program_id(1) == 0)
    def _():
        hn_ref[...] = _rms_to_bf16(x_ref[...], g_ref[...])

    h = hn_ref[...]
    a = _dot(h, wg_ref[...])
    b = _dot(h, wu_ref[...])
    o_ref[...] = (a * jax.nn.sigmoid(a) * b).astype(o_ref.dtype)


def _ffn_up(x, g, wg, wu, tm, tn):
    m, k = x.shape
    n = wg.shape[1]
    return pl.pallas_call(
        _ffn_up_kernel,
        name="ffn_up",
        grid=(m // tm, n // tn),
        in_specs=[
            pl.BlockSpec((tm, k), lambda i, j: (i, 0)),
            pl.BlockSpec((1, k), lambda i, j: (0, 0)),
            pl.BlockSpec((k, tn), lambda i, j: (0, j)),
            pl.BlockSpec((k, tn), lambda i, j: (0, j)),
        ],
        out_specs=pl.BlockSpec((tm, tn), lambda i, j: (i, j)),
        out_shape=jax.ShapeDtypeStruct((m, n), BF16),
        scratch_shapes=[pltpu.VMEM((tm, k), BF16)],
        compiler_params=_cparams(("parallel", "arbitrary")),
    )(x, g, wg, wu)


def _matmul_res_kernel(a_ref, w_ref, r_ref, o_ref):
    o_ref[...] = r_ref[...] + _dot(a_ref[...], w_ref[...])


def _matmul_res(name, a, w, res, tm, tn):
    m, k = a.shape
    n = w.shape[1]
    return pl.pallas_call(
        _matmul_res_kernel,
        name=name,
        grid=(n // tn, m // tm),
        in_specs=[
            pl.BlockSpec((tm, k), lambda j, i: (i, 0)),
            pl.BlockSpec((k, tn), lambda j, i: (0, j)),
            pl.BlockSpec((tm, tn), lambda j, i: (i, j)),
        ],
        out_specs=pl.BlockSpec((tm, tn), lambda j, i: (i, j)),
        out_shape=jax.ShapeDtypeStruct((m, n), F32),
        compiler_params=_cparams(("parallel", "parallel")),
    )(a, w, res)


def _merge_kernel(ya_ref, yb_ref, wa_ref, wb_ref, ga_ref, gb_ref, o_ref):
    a = _dot(ya_ref[...], wa_ref[...])
    b = _dot(yb_ref[...], wb_ref[...])
    o_ref[...] = (jax.nn.sigmoid(ga_ref[...]) * a + jax.nn.sigmoid(gb_ref[...]) * b).astype(o_ref.dtype)


def _merge(ya, yb, wa, wb, proj, tm, tn):
    m, k = ya.shape
    n = wa.shape[1]
    ga0 = COL_GATE_A // tn
    gb0 = COL_GATE_B // tn
    return pl.pallas_call(
        _merge_kernel,
        name="merge",
        grid=(m // tm, n // tn),
        in_specs=[
            pl.BlockSpec((tm, k), lambda i, j: (i, 0)),
            pl.BlockSpec((tm, k), lambda i, j: (i, 0)),
            pl.BlockSpec((k, tn), lambda i, j: (0, j)),
            pl.BlockSpec((k, tn), lambda i, j: (0, j)),
            pl.BlockSpec((tm, tn), lambda i, j: (i, ga0 + j)),
            pl.BlockSpec((tm, tn), lambda i, j: (i, gb0 + j)),
        ],
        out_specs=pl.BlockSpec((tm, tn), lambda i, j: (i, j)),
        out_shape=jax.ShapeDtypeStruct((m, n), BF16),
        compiler_params=_cparams(("parallel", "parallel")),
    )(ya, yb, wa, wb, proj, proj)


def _ple_kernel(x_ref, g_ref, wg_ref, p_ref, wp_ref, o_ref):
    x = x_ref[...]
    gate = jax.nn.sigmoid(_dot(_rms_to_bf16(x, g_ref[...]), wg_ref[...]))
    emb = _dot(p_ref[...].astype(BF16), wp_ref[...])
    o_ref[...] = x + gate * emb


def _ple(x, g, wg, p, wp, tm):
    m, k = x.shape
    n = wg.shape[1]
    kp = p.shape[1]
    assert n == k
    return pl.pallas_call(
        _ple_kernel,
        name="ple",
        grid=(m // tm,),
        in_specs=[
            pl.BlockSpec((tm, k), lambda i: (i, 0)),
            pl.BlockSpec((1, k), lambda i: (0, 0)),
            pl.BlockSpec((k, n), lambda i: (0, 0)),
            pl.BlockSpec((tm, kp), lambda i: (i, 0)),
            pl.BlockSpec((kp, n), lambda i: (0, 0)),
        ],
        out_specs=pl.BlockSpec((tm, n), lambda i: (i, 0)),
        out_shape=jax.ShapeDtypeStruct((m, n), F32),
        compiler_params=_cparams(("parallel",)),
    )(x, g, wg, p, wp)


def _rope64_pairs(x, c64, s64, first_half):
    partner = jnp.where(first_half, pltpu.roll(x, 96, 1), pltpu.roll(x, 32, 1))
    return x * c64 + partner * s64


def _prep_a_kernel(qa_ref, ka_ref, va_ref, qi_ref, kiw_ref, c128_ref, s128_ref, c64_ref, s64_ref,
                   gq_ref, gk_ref, qa_o, ka_o, va_o, qi_o, klo_o, khi_o, wi_o, *, wi_scale):
    c128 = c128_ref[...]
    s128 = s128_ref[...]

    def norm_rope(x, g):
        ms = jnp.mean(x * x, axis=-1, keepdims=True)
        y = x * lax.rsqrt(ms + NORM_EPS) * g
        return y * c128 + pltpu.roll(y, 64, 1) * s128

    for h in range(A_HEADS):
        hs = slice(h * LANES, (h + 1) * LANES)
        qa_o[:, hs] = norm_rope(qa_ref[:, hs], gq_ref[...]).astype(BF16)
    for h in range(A_KV_HEADS):
        hs = slice(h * LANES, (h + 1) * LANES)
        ka_o[:, hs] = norm_rope(ka_ref[:, hs], gk_ref[...]).astype(BF16)
    va_o[...] = va_ref[...].astype(BF16)

    c64 = c64_ref[...]
    s64 = s64_ref[...]
    lane = lax.broadcasted_iota(jnp.int32, c64.shape, 1)
    first_half = (lane % 64) < 32
    for p in range(IDX_HEADS * IDX_DIM // LANES):
        ps = slice(p * LANES, (p + 1) * LANES)
        qi_o[:, ps] = _rope64_pairs(qi_ref[:, ps], c64, s64, first_half).astype(BF16)
    kiw = kiw_ref[...]
    ki = jnp.where(lane < 64, _rope64_pairs(kiw, c64, s64, first_half), 0.0)
    klo_o[...] = ki.astype(BF16)
    khi_o[...] = pltpu.roll(ki, 64, 1).astype(BF16)
    wi_o[...] = kiw * wi_scale


def _prep_a(proj, c128, s128, c64, s64, gq, gk, tp):
    m = proj.shape[0]
    row = lambda w, cb: pl.BlockSpec((tp, w), lambda i: (i, cb))
    vec = pl.BlockSpec((1, LANES), lambda i: (0, 0))
    out = lambda w: pl.BlockSpec((tp, w), lambda i: (i, 0))
    wi_scale = (IDX_HEADS ** -0.5) * (IDX_DIM ** -0.5)
    return pl.pallas_call(
        functools.partial(_prep_a_kernel, wi_scale=wi_scale),
        name="prep_a",
        grid=(m // tp,),
        in_specs=[
            row(2048, COL_QA // 2048), row(512, COL_KA // 512), row(512, COL_VA // 512),
            row(1024, COL_QI // 1024), row(LANES, COL_KIW // LANES),
            out(LANES), out(LANES), out(LANES), out(LANES), vec, vec,
        ],
        out_specs=[out(2048), out(512), out(512), out(1024), out(LANES), out(LANES), out(LANES)],
        out_shape=[
            jax.ShapeDtypeStruct((m, 2048), BF16), jax.ShapeDtypeStruct((m, 512), BF16),
            jax.ShapeDtypeStruct((m, 512), BF16), jax.ShapeDtypeStruct((m, 1024), BF16),
            jax.ShapeDtypeStruct((m, LANES), BF16), jax.ShapeDtypeStruct((m, LANES), BF16),
            jax.ShapeDtypeStruct((m, LANES), F32),
        ],
        compiler_params=_cparams(("parallel",)),
    )(proj, proj, proj, proj, proj, c128, s128, c64, s64, gq, gk)


def _mla_q_kernel(cq_ref, g_ref, w_ref, c64_ref, s64_ref, gqn_ref, gqp_ref, q_o, *, q_scale):
    qb = _dot(_rms_to_bf16(cq_ref[...], g_ref[...]), w_ref[...])
    c64 = c64_ref[...]
    s64 = s64_ref[...]
    lane = lax.broadcasted_iota(jnp.int32, c64.shape, 1)
    first_half = (lane % 64) < 32
    lo = lane < 64
    nope_w = B_HEADS * B_NOPE
    for pair in range(B_HEADS // 2):
        qp2 = qb[:, nope_w + pair * LANES:nope_w + (pair + 1) * LANES]
        qp2_sq = qp2 * qp2
        rot2 = _rope64_pairs(qp2 * gqp_ref[...], c64, s64, first_half)
        for h in (2 * pair, 2 * pair + 1):
            mine = lo if h % 2 == 0 else jnp.logical_not(lo)
            shift_lane = lane == (64 if h % 2 == 0 else 0)
            qn = qb[:, h * LANES:(h + 1) * LANES]
            ss = jnp.sum(qn * qn + jnp.where(mine, qp2_sq, 0.0), axis=-1, keepdims=True)
            r = lax.rsqrt(ss * (1.0 / B_QK) + NORM_EPS) * q_scale
            q_o[:, 2 * h * LANES:(2 * h + 1) * LANES] = (qn * r * gqn_ref[...]).astype(BF16)
            qpe = jnp.where(mine, rot2 * r, jnp.where(shift_lane, 1.0, 0.0))
            q_o[:, (2 * h + 1) * LANES:(2 * h + 2) * LANES] = qpe.astype(BF16)


def _mla_kv_kernel(ckv_ref, g_ref, w_ref, kr_ref, c64_ref, s64_ref, gkn_ref, gkp_ref, shift_ref, k_o, v_o):
    kv = _dot(_rms_to_bf16(ckv_ref[...], g_ref[...]), w_ref[...])
    c64 = c64_ref[...]
    s64 = s64_ref[...]
    lane = lax.broadcasted_iota(jnp.int32, c64.shape, 1)
    first_half = (lane % 64) < 32
    lo = lane < 64
    kr = jnp.where(lo, kr_ref[...], 0.0)
    kr_ss = jnp.sum(kr * kr, axis=-1, keepdims=True)
    kpe_lo = jnp.where(lo, _rope64_pairs(kr * gkp_ref[...], c64, s64, first_half), 0.0)
    kpe_hi = pltpu.roll(kpe_lo, 64, 1)
    nope_w = B_HEADS * B_NOPE
    for h in range(B_HEADS):
        hs = slice(h * LANES, (h + 1) * LANES)
        shift_lane = lane == (64 if h % 2 == 0 else 0)
        kn = kv[:, hs]
        rk = lax.rsqrt((jnp.sum(kn * kn, axis=-1, keepdims=True) + kr_ss) * (1.0 / B_QK) + NORM_EPS)
        k_o[:, 2 * h * LANES:(2 * h + 1) * LANES] = (kn * rk * gkn_ref[...]).astype(BF16)
        kpe = kpe_lo if h % 2 == 0 else kpe_hi
        k_o[:, (2 * h + 1) * LANES:(2 * h + 2) * LANES] = jnp.where(
            shift_lane, -shift_ref[...], kpe * rk).astype(BF16)
        v_o[:, hs] = kv[:, nope_w + h * LANES:nope_w + (h + 1) * LANES].astype(BF16)


def _mla_q(proj, g_cq, w_uq, c64, s64, gqn, gqp, tm):
    m = proj.shape[0]
    vec = lambda w: pl.BlockSpec((1, w), lambda i: (0, 0))
    rows = lambda w: pl.BlockSpec((tm, w), lambda i: (i, 0))
    return pl.pallas_call(
        functools.partial(_mla_q_kernel, q_scale=B_QK ** -0.5 * LOG2E),
        name="mla_q",
        grid=(m // tm,),
        in_specs=[
            pl.BlockSpec((tm, B_Q_LORA), lambda i: (i, COL_CQ // B_Q_LORA)),
            vec(B_Q_LORA),
            pl.BlockSpec(w_uq.shape, lambda i: (0, 0)),
            rows(LANES), rows(LANES), vec(LANES), vec(LANES),
        ],
        out_specs=rows(B_HEADS * 256),
        out_shape=jax.ShapeDtypeStruct((m, B_HEADS * 256), BF16),
        compiler_params=_cparams(("parallel",)),
    )(proj, g_cq, w_uq, c64, s64, gqn, gqp)


def _mla_kv(proj, g_ckv, w_ukv, c64, s64, gkn, gkp, shift, tm):
    m = proj.shape[0]
    vec = lambda w: pl.BlockSpec((1, w), lambda i: (0, 0))
    rows = lambda w: pl.BlockSpec((tm, w), lambda i: (i, 0))
    return pl.pallas_call(
        _mla_kv_kernel,
        name="mla_kv",
        grid=(m // tm,),
        in_specs=[
            pl.BlockSpec((tm, B_KV_LORA), lambda i: (i, COL_CKV // B_KV_LORA)),
            vec(B_KV_LORA),
            pl.BlockSpec(w_ukv.shape, lambda i: (0, 0)),
            pl.BlockSpec((tm, LANES), lambda i: (i, COL_KR // LANES)),
            rows(LANES), rows(LANES), vec(LANES), vec(LANES), vec(LANES),
        ],
        out_specs=[rows(B_HEADS * 256), rows(B_HEADS * B_V)],
        out_shape=[
            jax.ShapeDtypeStruct((m, B_HEADS * 256), BF16),
            jax.ShapeDtypeStruct((m, B_HEADS * B_V), BF16),
        ],
        compiler_params=_cparams(("parallel",)),
    )(proj, g_ckv, w_ukv, proj, c64, s64, gkn, gkp, shift)


def _indexer_kernel(q_ref, klo_ref, khi_ref, w_ref, sc_ref, th_ref, *, tq, tk, seq, k_sel):
    qi = pl.program_id(1)
    n_vis = (qi * tq) // tk + 1
    n_all = seq // tk
    row = qi * tq + lax.broadcasted_iota(jnp.int32, (tq, 1), 0)
    wcol = lambda h: w_ref[0, :, IDX_DIM + h:IDX_DIM + h + 1]

    def gen(c, carry):
        rmax, rmin = carry
        off = pl.multiple_of(c * tk, tk)
        kl = klo_ref[0, pl.ds(off, tk), :]
        kh = khi_ref[0, pl.ds(off, tk), :]
        acc = jnp.zeros((tq, tk), F32)
        for p in range(IDX_HEADS // 2):
            qp = q_ref[0, :, p * LANES:(p + 1) * LANES]
            acc = acc + wcol(2 * p) * jnp.maximum(_dot_nt(qp, kl), 0.0)
            acc = acc + wcol(2 * p + 1) * jnp.maximum(_dot_nt(qp, kh), 0.0)
        col = off + lax.broadcasted_iota(jnp.int32, (tq, tk), 1)
        vis = col <= row
        sc_ref[0, :, pl.ds(off, tk)] = jnp.where(vis, acc, NEG)
        rmax = jnp.maximum(rmax, jnp.max(jnp.where(vis, acc, NEG), axis=-1, keepdims=True))
        rmin = jnp.minimum(rmin, jnp.min(jnp.where(vis, acc, POS_BIG), axis=-1, keepdims=True))
        return rmax, rmin

    rmax, rmin = lax.fori_loop(0, n_vis, gen,
                               (jnp.full((tq, 1), NEG, F32), jnp.full((tq, 1), POS_BIG, F32)))

    def fill(c, carry):
        sc_ref[0, :, pl.ds(pl.multiple_of(c * tk, tk), tk)] = jnp.full((tq, tk), NEG, F32)
        return carry

    lax.fori_loop(n_vis, n_all, fill, 0)

    cw = min(2 * tk, seq)
    n_cnt = (n_vis * tk + cw - 1) // cw

    def count_ge(mid, strict=False, open_groups=None):
        zeros = jnp.zeros((CNT_ROWS, LANES), F32)
        accs = []
        for r in range(tq // CNT_ROWS):
            rs = slice(r * CNT_ROWS, (r + 1) * CNT_ROWS)
            midb = mid[rs]

            def body(c, acc, rs=rs, midb=midb):
                slab = sc_ref[0, rs, pl.ds(pl.multiple_of(c * cw, cw), cw)]
                for j in range(cw // LANES):
                    blk = slab[:, j * LANES:(j + 1) * LANES]
                    acc = acc + jnp.where(blk > midb if strict else blk >= midb, 1.0, 0.0)
                return acc

            if open_groups is None:
                accs.append(lax.fori_loop(0, n_cnt, body, zeros))
            else:
                accs.append(lax.cond(open_groups[r] > 0.0,
                                     functools.partial(lax.fori_loop, 0, n_cnt, body, zeros),
                                     lambda: zeros))
        return jnp.concatenate(
            [jnp.broadcast_to(jnp.sum(a, axis=-1, keepdims=True), (CNT_ROWS, LANES)) for a in accs], axis=0)

    rowb = qi * tq + lax.broadcasted_iota(jnp.int32, (tq, LANES), 0)
    active0 = (rowb + 1) > k_sel
    lo0 = jnp.where(active0, jnp.broadcast_to(rmin, (tq, LANES)), TH_ALL)
    hi0 = jnp.where(active0, jnp.broadcast_to(rmax, (tq, LANES)), TH_ALL)

    def n_open(lo, hi):
        still = jnp.where(lo < hi, 1.0, 0.0)
        return tuple(jnp.sum(still[r * CNT_ROWS:(r + 1) * CNT_ROWS]) for r in range(tq // CNT_ROWS))

    def cond(st):
        it, nopen = st[0], st[1]
        return jnp.logical_and(functools.reduce(jnp.add, nopen) > 0.0, it < MAX_SEARCH_STEPS)

    def body(st):
        it, was_open, lo, hi, cnt_lo = st
        mid = lo + (hi - lo) * 0.5
        cnt = count_ge(mid, open_groups=was_open)
        nopen = n_open(lo, hi)
        ge = cnt >= k_sel
        lo_n = jnp.where(ge, mid, lo)
        hi_n = jnp.where(ge, hi, mid)
        hi_n = jnp.where(cnt == k_sel, lo_n, hi_n)
        hi_n = jnp.where(mid >= hi, lo_n, hi_n)
        hi_n = jnp.where(mid <= lo, lo_n, hi_n)
        return it + 1, nopen, lo_n, hi_n, jnp.where(ge, cnt, cnt_lo)

    st = lax.while_loop(cond, body, (jnp.int32(0), n_open(lo0, hi0), lo0, hi0,
                                     (rowb + 1).astype(F32)))
    th = st[2]
    th_ref[0] = th

    tied = jnp.logical_and(active0, st[4] > k_sel)

    @pl.when(jnp.sum(jnp.where(tied, 1.0, 0.0)) > 0.0)
    def _():
        keep_ties = k_sel - count_ge(th, strict=True)
        reps = tk // LANES
        th_w = jnp.concatenate([th] * reps, axis=1)
        keep_w = jnp.concatenate([keep_ties] * reps, axis=1)
        earlier = (lax.broadcasted_iota(jnp.int32, (tk, tk), 0)
                   < lax.broadcasted_iota(jnp.int32, (tk, tk), 1))
        tri = jnp.where(earlier, 1.0, 0.0).astype(BF16)

        def demote(c, seen):
            off = pl.multiple_of(c * tk, tk)
            blk = sc_ref[0, :, pl.ds(off, tk)]
            eq = blk == th_w
            eqf = jnp.where(eq, 1.0, 0.0)
            rank = _dot(eqf.astype(BF16), tri) + jnp.concatenate([seen] * reps, axis=1)
            sc_ref[0, :, pl.ds(off, tk)] = jnp.where(jnp.logical_and(eq, rank >= keep_w), NEG, blk)
            return seen + jnp.sum(eqf, axis=-1, keepdims=True)

        lax.fori_loop(0, n_vis, demote, jnp.zeros((tq, LANES), F32))


def _indexer(qi_r, klo, khi, wi, tq, tk, k_sel):
    b, s, _ = qi_r.shape
    return pl.pallas_call(
        functools.partial(_indexer_kernel, tq=tq, tk=tk, seq=s, k_sel=float(k_sel)),
        name="indexer",
        grid=(b, s // tq),
        in_specs=[
            pl.BlockSpec((1, tq, IDX_HEADS * IDX_DIM), lambda bi, qi: (bi, qi, 0)),
            pl.BlockSpec((1, s, LANES), lambda bi, qi: (bi, 0, 0)),
            pl.BlockSpec((1, s, LANES), lambda bi, qi: (bi, 0, 0)),
            pl.BlockSpec((1, tq, LANES), lambda bi, qi: (bi, qi, 0)),
        ],
        out_specs=[
            pl.BlockSpec((1, tq, s), lambda bi, qi: (bi, qi, 0)),
            pl.BlockSpec((1, tq, LANES), lambda bi, qi: (bi, qi, 0)),
        ],
        out_shape=[jax.ShapeDtypeStruct((b, s, s), F32), jax.ShapeDtypeStruct((b, s, LANES), F32)],
        compiler_params=_cparams(("parallel", "parallel")),
    )(qi_r, klo, khi, wi)


def _causal_pairs(s, tq, tk):
    qs, ks = [], []
    for qi in range(s // tq):
        for ki in range((qi * tq + tq - 1) // tk + 1):
            qs.append(qi)
            ks.append(ki)
    return jnp.asarray(np.array(qs, np.int32)), jnp.asarray(np.array(ks, np.int32))


def _softmax_step(s, v, rows, m_ref, l_ref, acc_ref, fixed_shift, sums_in_matmul=False):
    blocks = [s[:, j * LANES:(j + 1) * LANES] for j in range(s.shape[1] // LANES)]
    dv = v.shape[1]
    if fixed_shift and sums_in_matmul:
        p = jnp.concatenate([jnp.exp2(blk).astype(BF16) for blk in blocks], axis=1)
        acc_ref[rows] = acc_ref[rows] + _dot(p, jnp.concatenate([v, jnp.ones_like(v)], axis=1))
        return
    if fixed_shift:
        ps = [jnp.exp2(blk) for blk in blocks]
        l_ref[rows] = l_ref[rows] + functools.reduce(jnp.add, ps)
        p = jnp.concatenate([blk.astype(BF16) for blk in ps], axis=1)
        acc_ref[rows, :dv] = acc_ref[rows, :dv] + _dot(p, v)
        return
    mx = functools.reduce(jnp.maximum, blocks)
    m_prev = m_ref[rows]
    m_new = jnp.maximum(m_prev, jnp.max(mx, axis=-1, keepdims=True))
    alpha = jnp.exp2(m_prev - m_new)
    ps = [jnp.exp2(blk - m_new) for blk in blocks]
    l_ref[rows] = alpha * l_ref[rows] + jnp.sum(functools.reduce(jnp.add, ps), axis=-1, keepdims=True)
    p = jnp.concatenate([blk.astype(BF16) for blk in ps], axis=1)
    acc_ref[rows, :dv] = alpha * acc_ref[rows, :dv] + _dot(p, v)
    m_ref[rows] = m_new


def _softmax_finish(rows, l_ref, acc_ref, dv, fixed_shift, sums_in_matmul=False):
    if fixed_shift and sums_in_matmul:
        acc = acc_ref[rows]
        return acc[:, :dv] / acc[:, dv:]
    l = l_ref[rows]
    if fixed_shift:
        l = jnp.sum(l, axis=-1, keepdims=True)
    return acc_ref[rows, :dv] / l


def _dsa_kernel(qt_ref, kt_ref, q_ref, k_ref, v_ref, sc_ref, th_ref, shift_ref, o_ref, m_ref, l_ref,
                acc_ref, bias_ref, qs_ref, *, tq, tk, fixed_shift):
    step = pl.program_id(1)
    qi = qt_ref[step]
    ki = kt_ref[step]

    @pl.when(ki == 0)
    def _():
        m_ref[...] = jnp.full(m_ref.shape, M_INIT, F32)
        l_ref[...] = jnp.zeros(l_ref.shape, F32)
        acc_ref[...] = jnp.zeros(acc_ref.shape, F32)
        for h in range(A_HEADS):
            qs_ref[h * tq:(h + 1) * tq, :] = q_ref[0, :, h * LANES:(h + 1) * LANES]

    bias_ref[...] = jnp.where(sc_ref[0] >= th_ref[0][:, :1], -shift_ref[:, :1], NEG)
    grp = A_HEADS // A_KV_HEADS
    for g in range(A_KV_HEADS):
        rows = slice(g * grp * tq, (g + 1) * grp * tq)
        k = k_ref[0, :, g * LANES:(g + 1) * LANES]
        v = v_ref[0, :, g * LANES:(g + 1) * LANES]
        s = _dot_nt(qs_ref[rows, :], k)
        s = jnp.concatenate([s[j * tq:(j + 1) * tq] + bias_ref[...] for j in range(grp)], axis=0)
        _softmax_step(s, v, rows, m_ref, l_ref, acc_ref, fixed_shift, sums_in_matmul=True)

    @pl.when(ki == (qi * tq + tq - 1) // tk)
    def _():
        for h in range(A_HEADS):
            out = _softmax_finish(slice(h * tq, (h + 1) * tq), l_ref, acc_ref, A_HEAD_DIM, fixed_shift,
                                  sums_in_matmul=True)
            o_ref[0, :, h * LANES:(h + 1) * LANES] = out.astype(o_ref.dtype)


def _dsa_attn(q, k, v, scores, th, shift, tq, tk, fixed_shift):
    b, s, dq = q.shape
    qt, kt = _causal_pairs(s, tq, tk)
    grid_spec = pltpu.PrefetchScalarGridSpec(
        num_scalar_prefetch=2,
        grid=(b, qt.shape[0]),
        in_specs=[
            pl.BlockSpec((1, tq, dq), lambda bi, p, qt, kt: (bi, qt[p], 0)),
            pl.BlockSpec((1, tk, k.shape[2]), lambda bi, p, qt, kt: (bi, kt[p], 0)),
            pl.BlockSpec((1, tk, v.shape[2]), lambda bi, p, qt, kt: (bi, kt[p], 0)),
            pl.BlockSpec((1, tq, tk), lambda bi, p, qt, kt: (bi, qt[p], kt[p])),
            pl.BlockSpec((1, tq, LANES), lambda bi, p, qt, kt: (bi, qt[p], 0)),
            pl.BlockSpec((1, LANES), lambda bi, p, qt, kt: (0, 0)),
        ],
        out_specs=pl.BlockSpec((1, tq, dq), lambda bi, p, qt, kt: (bi, qt[p], 0)),
        scratch_shapes=[
            pltpu.VMEM((A_HEADS * tq, LANES), F32),
            pltpu.VMEM((A_HEADS * tq, LANES), F32),
            pltpu.VMEM((A_HEADS * tq, 2 * A_HEAD_DIM), F32),
            pltpu.VMEM((tq, tk), F32),
            pltpu.VMEM((A_HEADS * tq, A_HEAD_DIM), BF16),
        ],
    )
    return pl.pallas_call(
        functools.partial(_dsa_kernel, tq=tq, tk=tk, fixed_shift=fixed_shift),
        name="dsa_attn_fixed" if fixed_shift else "dsa_attn_online",
        grid_spec=grid_spec,
        out_shape=jax.ShapeDtypeStruct((b, s, dq), BF16),
        compiler_params=_cparams(("parallel", "arbitrary")),
    )(qt, kt, q, k, v, scores, th, shift)


def _mla_kernel(qt_ref, kt_ref, q_ref, k_ref, v_ref, o_ref, m_ref, l_ref, acc_ref, *, tq, tk, hg,
                fixed_shift):
    step = pl.program_id(2)
    qi = qt_ref[step]
    ki = kt_ref[step]

    @pl.when(ki == 0)
    def _():
        m_ref[...] = jnp.full(m_ref.shape, M_INIT, F32)
        l_ref[...] = jnp.zeros(l_ref.shape, F32)
        acc_ref[...] = jnp.zeros(acc_ref.shape, F32)

    def heads(masked):
        if masked:
            keep = (lax.broadcasted_iota(jnp.int32, (tq, tk), 1)
                    <= lax.broadcasted_iota(jnp.int32, (tq, tk), 0))
        for h in range(hg):
            q = q_ref[0, :, h * 256:(h + 1) * 256]
            k = k_ref[0, :, h * 256:(h + 1) * 256]
            v = v_ref[0, :, h * B_V:(h + 1) * B_V]
            s = _dot_nt(q, k)
            if masked:
                s = jnp.where(keep, s, NEG)
            _softmax_step(s, v, slice(h * tq, (h + 1) * tq), m_ref, l_ref, acc_ref, fixed_shift)

    @pl.when(ki < qi)
    def _():
        heads(False)

    @pl.when(ki == qi)
    def _():
        heads(True)
        for h in range(hg):
            out = _softmax_finish(slice(h * tq, (h + 1) * tq), l_ref, acc_ref, B_V, fixed_shift)
            o_ref[0, :, h * B_V:(h + 1) * B_V] = out.astype(o_ref.dtype)


def _mla_attn(q, k, v, tq, tk, hg, fixed_shift):
    b, s, _ = q.shape
    assert tq == tk
    qt, kt = _causal_pairs(s, tq, tk)
    grid_spec = pltpu.PrefetchScalarGridSpec(
        num_scalar_prefetch=2,
        grid=(b, B_HEADS // hg, qt.shape[0]),
        in_specs=[
            pl.BlockSpec((1, tq, hg * 256), lambda bi, g, p, qt, kt: (bi, qt[p], g)),
            pl.BlockSpec((1, tk, hg * 256), lambda bi, g, p, qt, kt: (bi, kt[p], g)),
            pl.BlockSpec((1, tk, hg * B_V), lambda bi, g, p, qt, kt: (bi, kt[p], g)),
        ],
        out_specs=pl.BlockSpec((1, tq, hg * B_V), lambda bi, g, p, qt, kt: (bi, qt[p], g)),
        scratch_shapes=[
            pltpu.VMEM((hg * tq, LANES), F32),
            pltpu.VMEM((hg * tq, LANES), F32),
            pltpu.VMEM((hg * tq, B_V), F32),
        ],
    )
    return pl.pallas_call(
        functools.partial(_mla_kernel, tq=tq, tk=tk, hg=hg, fixed_shift=fixed_shift),
        name="mla_attn_fixed" if fixed_shift else "mla_attn_online",
        grid_spec=grid_spec,
        out_shape=jax.ShapeDtypeStruct((b, s, B_HEADS * B_V), BF16),
        compiler_params=_cparams(("parallel", "parallel", "arbitrary")),
    )(qt, kt, q, k, v)


def _regroup_w_in(w):
    sizes = (A_HEADS * A_HEAD_DIM, A_KV_HEADS * A_HEAD_DIM, A_KV_HEADS * A_HEAD_DIM,
             IDX_HEADS * IDX_DIM, IDX_DIM, IDX_HEADS, B_Q_LORA, B_KV_LORA, B_ROPE)
    offs = np.cumsum((0,) + sizes)
    w = w.astype(BF16)
    qa, ka, va, qi, ki, wi, cq, ckv, kr = (w[:, offs[j]:offs[j + 1]] for j in range(len(sizes)))
    gates = w[:, offs[-1]:]
    d = w.shape[0]
    kiw = jnp.concatenate([ki, wi, jnp.zeros((d, LANES - IDX_DIM - IDX_HEADS), w.dtype)], axis=1)
    krp = jnp.concatenate([kr, jnp.zeros((d, LANES - B_ROPE), w.dtype)], axis=1)
    return jnp.concatenate([qa, gates, qi, ka, va, cq, ckv, kiw, krp], axis=1).astype(BF16)


def _softmax_shift(d, g_q, g_k):
    bound = (d ** 0.5) * LOG2E * 1.02 * jnp.max(jnp.abs(g_q)) * jnp.max(jnp.abs(g_k))
    c = jnp.ceil(bound.astype(F32))
    ok = c <= MAX_FIXED_SHIFT
    return jnp.full((1, LANES), jnp.where(ok, c, 0.0), F32), ok


def _rope_tables(pos, d):
    half = d // 2
    freqs = jnp.power(jnp.float32(ROPE_THETA), -jnp.arange(half, dtype=F32) * (2.0 / d))
    ang = pos.astype(F32)[..., None] * freqs
    cos, sin = jnp.cos(ang), jnp.sin(ang)
    reps = LANES // d
    c = jnp.tile(jnp.concatenate([cos, cos], axis=-1), (1, 1, reps))
    s = jnp.tile(jnp.concatenate([-sin, sin], axis=-1), (1, 1, reps))
    return c.reshape(-1, LANES), s.reshape(-1, LANES)


def _layer(x, p, positions, g_mix_norm, w_in, g_qa, g_ka, g_cq, w_uq, g_ckv, w_ukv, g_qb, g_kb,
           w_out_a, w_out_b, w_o, g_ffn_norm, w_ffn_gate, w_ffn_up, w_ffn_down, g_ple_norm,
           w_ple_gate, w_ple_proj):
    b, s, d = x.shape
    m = b * s
    k_sel = min(TOPK_MAX, s // 4)
    xf = x.reshape(m, d)
    row = lambda g: g.reshape(1, -1).astype(F32)

    c128, s128 = _rope_tables(positions, A_HEAD_DIM)
    c64, s64 = _rope_tables(positions, IDX_DIM)

    proj = _norm_matmul("in_proj", xf, 0, d, row(g_mix_norm), _regroup_w_in(w_in), F32,
                        tm=TILE_IN_PROJ[0], tn=TILE_IN_PROJ[1])

    qa_r, ka_r, va_b, qi_r, klo, khi, wi = _prep_a(
        proj, c128, s128, c64, s64, row(g_qa) * (A_HEAD_DIM ** -0.5 * LOG2E), row(g_ka), tp=TILE_PREP_ROWS)
    r3 = lambda a: a.reshape(b, s, a.shape[-1])
    scores, th = _indexer(r3(qi_r), r3(klo), r3(khi), r3(wi), tq=TILE_ATTN, tk=TILE_ATTN, k_sel=k_sel)
    shift_a, fixed_a = _softmax_shift(A_HEAD_DIM, g_qa, g_ka)
    y_a = lax.cond(
        fixed_a,
        functools.partial(_dsa_attn, tq=TILE_ATTN, tk=TILE_ATTN, fixed_shift=True),
        functools.partial(_dsa_attn, tq=TILE_ATTN, tk=TILE_ATTN, fixed_shift=False),
        r3(qa_r), r3(ka_r), r3(va_b), scores, th, shift_a)

    w_uq_r = w_uq.reshape(B_Q_LORA, B_HEADS, B_QK)
    w_uq_r = jnp.concatenate([w_uq_r[:, :, :B_NOPE].reshape(B_Q_LORA, -1),
                              w_uq_r[:, :, B_NOPE:].reshape(B_Q_LORA, -1)], axis=1).astype(BF16)
    w_ukv_r = w_ukv.reshape(B_KV_LORA, B_HEADS, 2, B_NOPE)
    w_ukv_r = jnp.concatenate([w_ukv_r[:, :, 0, :].reshape(B_KV_LORA, -1),
                               w_ukv_r[:, :, 1, :].reshape(B_KV_LORA, -1)], axis=1).astype(BF16)
    pe2 = lambda g: jnp.tile(g[B_NOPE:], 2).reshape(1, LANES).astype(F32)
    shift_b, fixed_b = _softmax_shift(B_QK, g_qb, g_kb)
    q_m = _mla_q(proj, row(g_cq), w_uq_r, c64, s64, row(g_qb[:B_NOPE]), pe2(g_qb), tm=TILE_MLA_UP_ROWS)
    k_m, v_m = _mla_kv(proj, row(g_ckv), w_ukv_r, c64, s64, row(g_kb[:B_NOPE]), pe2(g_kb), shift_b,
                       tm=TILE_MLA_UP_ROWS)
    y_b = lax.cond(
        fixed_b,
        functools.partial(_mla_attn, tq=TILE_ATTN, tk=TILE_ATTN, hg=B_HEADS, fixed_shift=True),
        functools.partial(_mla_attn, tq=TILE_ATTN, tk=TILE_ATTN, hg=B_HEADS, fixed_shift=False),
        r3(q_m), r3(k_m), r3(v_m))

    mixed = _merge(y_a.reshape(m, -1), y_b.reshape(m, -1), w_out_a.astype(BF16), w_out_b.astype(BF16),
                   proj, tm=TILE_MERGE[0], tn=TILE_MERGE[1])
    x1 = _matmul_res("w_o", mixed, w_o.astype(BF16), xf, tm=TILE_W_O[0], tn=TILE_W_O[1])

    act = _ffn_up(x1, row(g_ffn_norm), w_ffn_gate.astype(BF16), w_ffn_up.astype(BF16),
                  tm=TILE_FFN_UP[0], tn=TILE_FFN_UP[1])
    x2 = _matmul_res("ffn_down", act, w_ffn_down.astype(BF16), x1,
                     tm=TILE_FFN_DOWN[0], tn=TILE_FFN_DOWN[1])

    out = _ple(x2, row(g_ple_norm), w_ple_gate.astype(BF16), p.reshape(m, -1),
               w_ple_proj.astype(BF16), tm=TILE_PLE_ROWS)
    return out.reshape(b, s, d)


def kernel(x, p, positions, g_mix_norm, w_in, g_qa, g_ka, g_cq, w_uq, g_ckv, w_ukv, g_qb, g_kb,
           w_out_a, w_out_b, w_o, g_ffn_norm, w_ffn_gate, w_ffn_up, w_ffn_down, g_ple_norm,
           w_ple_gate, w_ple_proj):
    for i in range(p.shape[0]):
        x = _layer(x, p[i], positions, g_mix_norm[i], w_in[i], g_qa[i], g_ka[i], g_cq[i], w_uq[i],
                   g_ckv[i], w_ukv[i], g_qb[i], g_kb[i], w_out_a[i], w_out_b[i], w_o[i],
                   g_ffn_norm[i], w_ffn_gate[i], w_ffn_up[i], w_ffn_down[i], g_ple_norm[i],
                   w_ple_gate[i], w_ple_proj[i])
    return x
```
